```python
import math
import jax
import jax.numpy as jnp
from jax import lax
import numpy as np


D_MODEL = 1024
BATCH = 32
SEQ = 2048
DEPTH = 2

HEAD_DIM = 64
N_HEADS_MOBA = 4
N_HEADS_DIFF = 4
DIFF_DIM = HEAD_DIM // 2
N_HEADS_FOX = 4
N_HEADS_DSA = 4
N_IDX_HEADS = 4
IDX_DIM = 64
BRANCH_WIDTH = 4 * HEAD_DIM
N_BRANCHES = 4
MOBA_BLOCK = 256
MOBA_TOPK = 3
MOBA_Q_CHUNK = 16
DSA_TOPK_MAX = 256
DSA_Q_CHUNK = 64
Q_BLOCK = 128
ROPE_THETA = 10000.0
N_EXPERTS = 256
TOP_K = 8
N_GROUPS = 8
TOPK_GROUPS = 4
EXPERT_DIM = 256
SHARED_DIM = 256
ROUTED_SCALE = 2.5
MOE_BLOCK = 128
LN_EPS = 1e-5
DEEPNORM_ALPHA = (2 * DEPTH) ** 0.25
DEEPNORM_BETA = (8 * DEPTH) ** -0.25

IN_SEGMENTS = (
    ('moba_q', BRANCH_WIDTH), ('moba_k', BRANCH_WIDTH), ('moba_v', BRANCH_WIDTH),
    ('diff_q', BRANCH_WIDTH), ('diff_k', BRANCH_WIDTH), ('diff_v', BRANCH_WIDTH),
    ('fox_q', BRANCH_WIDTH), ('fox_k', BRANCH_WIDTH), ('fox_v', BRANCH_WIDTH), ('fox_f', N_HEADS_FOX),
    ('dsa_q', BRANCH_WIDTH), ('dsa_k', BRANCH_WIDTH), ('dsa_v', BRANCH_WIDTH),
    ('idx_q', N_IDX_HEADS * IDX_DIM), ('idx_k', IDX_DIM), ('idx_w', N_IDX_HEADS),
    ('gates', N_BRANCHES * D_MODEL),
)

kernel_name = 'hybrid_moba_diff_fox_dsa_moe_deepnorm'

F32 = jnp.float32


def _normal(key, shape, scale):
    return jax.random.normal(key, shape, F32) * scale


def rope(x, pos):
    half = x.shape[-1] // 2
    inv_freq = ROPE_THETA ** (-jnp.arange(half, dtype=F32) / half)
    ang = pos.astype(F32)[:, None] * inv_freq[None, :]
    cos = jnp.cos(ang)[None, :, None, :].astype(x.dtype)
    sin = jnp.sin(ang)[None, :, None, :].astype(x.dtype)
    x1, x2 = x[..., :half], x[..., half:]
    return jnp.concatenate([x1 * cos - x2 * sin, x1 * sin + x2 * cos], axis=-1)


def layer_norm(x, g, b):
    xf = x.astype(F32)
    mu = jnp.mean(xf, axis=-1, keepdims=True)
    xc = xf - mu
    var = jnp.mean(xc * xc, axis=-1, keepdims=True)
    return (xc * lax.rsqrt(var + LN_EPS) * g.astype(F32) + b.astype(F32)).astype(x.dtype)


def split_projection(proj):
    parts = {}
    off = 0
    for name, width in IN_SEGMENTS:
        parts[name] = proj[..., off:off + width]
        off += width
    return parts


def moba_attention(q, k, v):
    B, T, H, d = q.shape
    nb = -(-T // MOBA_BLOCK)
    pad = nb * MOBA_BLOCK - T
    kp = jnp.pad(k, ((0, 0), (0, pad), (0, 0), (0, 0)))
    vp = jnp.pad(v, ((0, 0), (0, pad), (0, 0), (0, 0)))
    kb = kp.reshape(B, nb, MOBA_BLOCK, H, d).transpose(0, 3, 1, 2, 4)
    vb = vp.reshape(B, nb, MOBA_BLOCK, H, d).transpose(0, 3, 1, 2, 4)
    kmean = jnp.mean(kb.astype(F32), axis=3)
    n_sel = min(MOBA_TOPK, nb - 1)
    scale = d ** -0.5
    bi = jnp.arange(B)[:, None, None, None]
    hi = jnp.arange(H)[None, :, None, None]
    blk_ids = jnp.arange(nb)

    def chunk(i):
        s0 = i * MOBA_Q_CHUNK
        qc = lax.dynamic_slice_in_dim(q, s0, MOBA_Q_CHUNK, axis=1).transpose(0, 2, 1, 3)
        qpos = s0 + jnp.arange(MOBA_Q_CHUNK)
        cur = s0 // MOBA_BLOCK
        ko = lax.dynamic_index_in_dim(kb, cur, axis=2, keepdims=False)
        vo = lax.dynamic_index_in_dim(vb, cur, axis=2, keepdims=False)
        kpos_own = cur * MOBA_BLOCK + jnp.arange(MOBA_BLOCK)
        lo = jnp.einsum('bhqd,bhkd->bhqk', qc, ko, preferred_element_type=F32) * scale
        lo = jnp.where(kpos_own[None, :] <= qpos[:, None], lo, -jnp.inf)
        if n_sel == 0:
            p = jax.nn.softmax(lo, axis=-1)
            out = jnp.einsum('bhqk,bhkd->bhqd', p.astype(v.dtype), vo)
            return out.transpose(0, 2, 1, 3)
        gate = jnp.einsum('bhqd,bhnd->bhqn', qc.astype(F32), kmean)
        gate = jnp.where(blk_ids < cur, gate, -jnp.inf)
        gval, gidx = lax.top_k(gate, n_sel)
        ks = kb[bi, hi, gidx]
        vs = vb[bi, hi, gidx]
        ls = jnp.einsum('bhqd,bhqnkd->bhqnk', qc, ks, preferred_element_type=F32) * scale
        ls = jnp.where((gval > -jnp.inf)[..., None], ls, -jnp.inf)
        logits = jnp.concatenate([lo, ls.reshape(B, H, MOBA_Q_CHUNK, n_sel * MOBA_BLOCK)], axis=-1)
        p = jax.nn.softmax(logits, axis=-1).astype(v.dtype)
        po = p[..., :MOBA_BLOCK]
        ps = p[..., MOBA_BLOCK:].reshape(B, H, MOBA_Q_CHUNK, n_sel, MOBA_BLOCK)
        out = jnp.einsum('bhqk,bhkd->bhqd', po, vo) + jnp.einsum('bhqnk,bhqnkd->bhqd', ps, vs)
        return out.transpose(0, 2, 1, 3)

    o = lax.map(chunk, jnp.arange(T // MOBA_Q_CHUNK))
    return o.transpose(1, 0, 2, 3, 4).reshape(B, T, H * d)


def diff_attention(q, k, v, lam, subln_g, lambda_init):
    B, T, H, _, dd = q.shape
    scale = dd ** -0.5
    kpos = jnp.arange(T)

    def block(i):
        s0 = i * Q_BLOCK
        qb = lax.dynamic_slice_in_dim(q, s0, Q_BLOCK, axis=1)
        logits = jnp.einsum('bqhcd,bkhcd->bhcqk', qb, k, preferred_element_type=F32) * scale
        qpos = s0 + jnp.arange(Q_BLOCK)
        logits = jnp.where(kpos[None, :] <= qpos[:, None], logits, -jnp.inf)
        p = jax.nn.softmax(logits, axis=-1)
        a = p[:, :, 0] - lam * p[:, :, 1]
        return jnp.einsum('bhqk,bkhd->bqhd', a.astype(v.dtype), v)

    o = lax.map(block, jnp.arange(T // Q_BLOCK))
    o = o.transpose(1, 0, 2, 3, 4).reshape(B, T, H, 2 * dd).astype(F32)
    o = o * lax.rsqrt(jnp.mean(o * o, axis=-1, keepdims=True) + LN_EPS) * subln_g.astype(F32)
    return (o * (1.0 - lambda_init)).astype(v.dtype).reshape(B, T, H * 2 * dd)


def fox_attention(q, k, v, logf):
    B, T, H, d = q.shape
    scale = d ** -0.5
    c = jnp.cumsum(logf, axis=1).transpose(0, 2, 1)
    kpos = jnp.arange(T)

    def block(i):
        s0 = i * Q_BLOCK
        qb = lax.dynamic_slice_in_dim(q, s0, Q_BLOCK, axis=1)
        cb = lax.dynamic_slice_in_dim(c, s0, Q_BLOCK, axis=2)
        logits = jnp.einsum('bqhd,bkhd->bhqk', qb, k, preferred_element_type=F32) * scale
        logits = logits + (cb[..., :, None] - c[..., None, :])
        qpos = s0 + jnp.arange(Q_BLOCK)
        logits = jnp.where(kpos[None, :] <= qpos[:, None], logits, -jnp.inf)
        p = jax.nn.softmax(logits, axis=-1)
        return jnp.einsum('bhqk,bkhd->bqhd', p.astype(v.dtype), v)

    o = lax.map(block, jnp.arange(T // Q_BLOCK))
    return o.transpose(1, 0, 2, 3, 4).reshape(B, T, H * d)


def dsa_attention(q, k, v, qi, ki, wi):
    B, T, H, d = q.shape
    n_keep = min(DSA_TOPK_MAX, T // 4)
    scale = d ** -0.5
    kpos = jnp.arange(T)
    bi = jnp.arange(B)[:, None, None]

    def chunk(i):
        s0 = i * DSA_Q_CHUNK
        qc = lax.dynamic_slice_in_dim(q, s0, DSA_Q_CHUNK, axis=1)
        qic = lax.dynamic_slice_in_dim(qi, s0, DSA_Q_CHUNK, axis=1)
        wic = lax.dynamic_slice_in_dim(wi, s0, DSA_Q_CHUNK, axis=1).astype(F32) * N_IDX_HEADS ** -0.5
        qpos = s0 + jnp.arange(DSA_Q_CHUNK)
        dots = jnp.einsum('bqhd,bkd->bqhk', qic, ki, preferred_element_type=F32) * IDX_DIM ** -0.5
        score = jnp.einsum('bqh,bqhk->bqk', wic, jax.nn.relu(dots))
        score = jnp.where(kpos[None, :] <= qpos[:, None], score, -jnp.inf)
        sval, sidx = lax.top_k(score, n_keep)
        ks = k[bi, sidx]
        vs = v[bi, sidx]
        logits = jnp.einsum('bqhd,bqnhd->bhqn', qc, ks, preferred_element_type=F32) * scale
        logits = jnp.where((sval > -jnp.inf)[:, None], logits, -jnp.inf)
        p = jax.nn.softmax(logits, axis=-1)
        return jnp.einsum('bhqn,bqnhd->bqhd', p.astype(v.dtype), vs)

    o = lax.map(chunk, jnp.arange(T // DSA_Q_CHUNK))
    return o.transpose(1, 0, 2, 3, 4).reshape(B, T, H * d)


def token_mixer(x, w_in, b_forget, diff_lambda, diff_subln, w_branch, w_out, lambda_init):
    B, T, D = x.shape
    pos = jnp.arange(T)
    p = split_projection(jnp.einsum('btd,de->bte', x, w_in))

    def heads(a, h):
        return a.reshape(B, T, h, -1)

    o_a = moba_attention(rope(heads(p['moba_q'], N_HEADS_MOBA), pos),
                         rope(heads(p['moba_k'], N_HEADS_MOBA), pos),
                         heads(p['moba_v'], N_HEADS_MOBA))
    dq = rope(p['diff_q'].reshape(B, T, 2 * N_HEADS_DIFF, DIFF_DIM), pos).reshape(B, T, N_HEADS_DIFF, 2, DIFF_DIM)
    dk = rope(p['diff_k'].reshape(B, T, 2 * N_HEADS_DIFF, DIFF_DIM), pos).reshape(B, T, N_HEADS_DIFF, 2, DIFF_DIM)
    dl = diff_lambda.astype(F32)
    lam = jnp.exp(jnp.sum(dl[0] * dl[1])) - jnp.exp(jnp.sum(dl[2] * dl[3])) + lambda_init
    o_b = diff_attention(dq, dk, heads(p['diff_v'], N_HEADS_DIFF), lam, diff_subln, lambda_init)
    logf = jax.nn.log_sigmoid(p['fox_f'].astype(F32) + b_forget.astype(F32))
    o_c = fox_attention(heads(p['fox_q'], N_HEADS_FOX), heads(p['fox_k'], N_HEADS_FOX),
                        heads(p['fox_v'], N_HEADS_FOX), logf)
    qi = rope(p['idx_q'].reshape(B, T, N_IDX_HEADS, IDX_DIM), pos)
    ki = rope(p['idx_k'].reshape(B, T, 1, IDX_DIM), pos)[:, :, 0]
    o_d = dsa_attention(rope(heads(p['dsa_q'], N_HEADS_DSA), pos), rope(heads(p['dsa_k'], N_HEADS_DSA), pos),
                        heads(p['dsa_v'], N_HEADS_DSA), qi, ki, p['idx_w'])
    gates = jax.nn.sigmoid(p['gates']).reshape(B, T, N_BRANCHES, D)
    branches = (o_a, o_b, o_c, o_d)
    merged = gates[:, :, 0] * jnp.einsum('btc,cd->btd', branches[0], w_branch[0])
    for i in range(1, N_BRANCHES):
        merged = merged + gates[:, :, i] * jnp.einsum('btc,cd->btd', branches[i], w_branch[i])
    return jnp.einsum('btd,de->bte', merged, w_out)


def swiglu(x, w_gate, w_up, w_down):
    return (jax.nn.silu(x @ w_gate) * (x @ w_up)) @ w_down


def routed_experts(xf, eidx, wsel, w_gate, w_up, w_down):
    N, D = xf.shape
    E = w_gate.shape[0]
    A = N * TOP_K
    flat_e = eidx.reshape(A)
    flat_tok = jnp.arange(A) // TOP_K
    flat_w = wsel.reshape(A)
    order = jnp.argsort(flat_e)
    se, stok, sw = flat_e[order], flat_tok[order], flat_w[order]
    counts = jnp.bincount(flat_e, length=E)
    start = jnp.cumsum(counts) - counts
    padded = (counts + MOE_BLOCK - 1) // MOE_BLOCK * MOE_BLOCK
    pend = jnp.cumsum(padded)
    pstart = pend - padded
    dest = pstart[se] + (jnp.arange(A) - start[se])
    n_blocks = (A + E * (MOE_BLOCK - 1)) // MOE_BLOCK + 1
    P = n_blocks * MOE_BLOCK
    row_tok = jnp.full((P,), N, jnp.int32).at[dest].set(stok.astype(jnp.int32))
    row_w = jnp.zeros((P,), F32).at[dest].set(sw)
    block_e = jnp.minimum(jnp.searchsorted(pend, jnp.arange(n_blocks) * MOE_BLOCK, side='right'), E - 1)
    x_pad = jnp.concatenate([xf, jnp.zeros((1, D), xf.dtype)], axis=0)

    def body(b, acc):
        e = block_e[b]
        tok = lax.dynamic_slice_in_dim(row_tok, b * MOE_BLOCK, MOE_BLOCK)
        wr = lax.dynamic_slice_in_dim(row_w, b * MOE_BLOCK, MOE_BLOCK).astype(xf.dtype)
        y = swiglu(x_pad[tok], w_gate[e], w_up[e], w_down[e])
        return acc.at[tok].add(y * wr[:, None])

    acc = lax.fori_loop(0, n_blocks, body, jnp.zeros((N + 1, D), xf.dtype))
    return acc[:N]


def moe_ffn(x, w_router, router_bias, w_exp_gate, w_exp_up, w_exp_down, w_sh_gate, w_sh_up, w_sh_down):
    B, T, D = x.shape
    N = B * T
    xf = x.reshape(N, D)
    scores = jax.nn.sigmoid(jnp.einsum('nd,de->ne', xf, w_router, preferred_element_type=F32))
    biased = scores + router_bias.astype(F32)
    per_group = N_EXPERTS // N_GROUPS
    group_score = jnp.sum(lax.top_k(biased.reshape(N, N_GROUPS, per_group), 2)[0], axis=-1)
    _, gsel = lax.top_k(group_score, TOPK_GROUPS)
    gmask = jnp.any(gsel[..., None] == jnp.arange(N_GROUPS), axis=1)
    masked = jnp.where(jnp.repeat(gmask, per_group, axis=1), biased, -jnp.inf)
    _, eidx = lax.top_k(masked, TOP_K)
    wsel = jnp.take_along_axis(scores, eidx, axis=-1)
    wsel = wsel / jnp.sum(wsel, axis=-1, keepdims=True) * ROUTED_SCALE
    routed = routed_experts(xf, eidx, wsel, w_exp_gate, w_exp_up, w_exp_down)
    shared = swiglu(xf, w_sh_gate, w_sh_up, w_sh_down)
    return (routed + shared).reshape(B, T, D)


def setup_inputs(seed: int = 0) -> dict:
    key = jax.random.key(seed)
    ks = jax.random.split(key, 20)
    D, E, F, FS = D_MODEL, N_EXPERTS, EXPERT_DIM, SHARED_DIM
    x = jax.random.normal(ks[0], (BATCH, SEQ, D), F32)
    seg_keys = jax.random.split(ks[1], len(IN_SEGMENTS))
    cols = []
    for (name, width), k in zip(IN_SEGMENTS, seg_keys):
        s = D ** -0.5 * (DEEPNORM_BETA if name.endswith('_v') else 1.0)
        cols.append(_normal(k, (DEPTH, D, width), s))
    w_in = jnp.concatenate(cols, axis=-1)
    b_forget = 2.0 + _normal(ks[2], (DEPTH, N_HEADS_FOX), 0.5)
    diff_lambda = _normal(ks[3], (DEPTH, 4, DIFF_DIM), 0.1)
    diff_subln = 1.0 + _normal(ks[4], (DEPTH, 2 * DIFF_DIM), 0.02)
    w_branch = _normal(ks[5], (DEPTH, N_BRANCHES, BRANCH_WIDTH, D), BRANCH_WIDTH ** -0.5)
    w_out = _normal(ks[6], (DEPTH, D, D), D ** -0.5 * DEEPNORM_BETA)
    ln1_g = 1.0 + _normal(ks[7], (DEPTH, D), 0.02)
    ln1_b = _normal(ks[8], (DEPTH, D), 0.02)
    w_router = _normal(ks[9], (DEPTH, D, E), D ** -0.5)
    router_bias = _normal(ks[10], (DEPTH, E), 0.01)
    w_exp_gate = _normal(ks[11], (DEPTH, E, D, F), D ** -0.5 * DEEPNORM_BETA)
    w_exp_up = _normal(ks[12], (DEPTH, E, D, F), D ** -0.5 * DEEPNORM_BETA)
    w_exp_down = _normal(ks[13], (DEPTH, E, F, D), F ** -0.5 * DEEPNORM_BETA)
    w_sh_gate = _normal(ks[14], (DEPTH, D, FS), D ** -0.5 * DEEPNORM_BETA)
    w_sh_up = _normal(ks[15], (DEPTH, D, FS), D ** -0.5 * DEEPNORM_BETA)
    w_sh_down = _normal(ks[16], (DEPTH, FS, D), FS ** -0.5 * DEEPNORM_BETA)
    ln2_g = 1.0 + _normal(ks[17], (DEPTH, D), 0.02)
    ln2_b = _normal(ks[18], (DEPTH, D), 0.02)
    return {'x': x, 'w_in': w_in, 'b_forget': b_forget, 'diff_lambda': diff_lambda,
            'diff_subln': diff_subln, 'w_branch': w_branch, 'w_out': w_out,
            'ln1_g': ln1_g, 'ln1_b': ln1_b, 'w_router': w_router, 'router_bias': router_bias,
            'w_exp_gate': w_exp_gate, 'w_exp_up': w_exp_up, 'w_exp_down': w_exp_down,
            'w_sh_gate': w_sh_gate, 'w_sh_up': w_sh_up, 'w_sh_down': w_sh_down,
            'ln2_g': ln2_g, 'ln2_b': ln2_b}


def reference(x, w_in, b_forget, diff_lambda, diff_subln, w_branch, w_out, ln1_g, ln1_b,
              w_router, router_bias, w_exp_gate, w_exp_up, w_exp_down,
              w_sh_gate, w_sh_up, w_sh_down, ln2_g, ln2_b):
    for l in range(DEPTH):
        lambda_init = 0.8 - 0.6 * math.exp(-0.3 * l)
        h = token_mixer(x, w_in[l], b_forget[l], diff_lambda[l], diff_subln[l],
                        w_branch[l], w_out[l], lambda_init)
        x = layer_norm(DEEPNORM_ALPHA * x + h, ln1_g[l], ln1_b[l])
        h = moe_ffn(x, w_router[l], router_bias[l], w_exp_gate[l], w_exp_up[l], w_exp_down[l],
                    w_sh_gate[l], w_sh_up[l], w_sh_down[l])
        x = layer_norm(DEEPNORM_ALPHA * x + h, ln2_g[l], ln2_b[l])
    return x
```

```python
import functools
import math

import jax
import jax.numpy as jnp
from jax import lax
from jax.experimental import pallas as pl
from jax.experimental.pallas import tpu as pltpu

F32 = jnp.float32
BF = jnp.bfloat16

D_MODEL = 1024
DEPTH = 2
HEAD_DIM = 64
N_HEADS = 4
DIFF_DIM = HEAD_DIM // 2
N_IDX_HEADS = 4
IDX_DIM = 64
BRANCH_WIDTH = N_HEADS * HEAD_DIM
N_BRANCHES = 4
MOBA_BLOCK = 256
MOBA_TOPK = 3
DSA_TOPK_MAX = 256
ROPE_THETA = 10000.0
N_EXPERTS = 256
TOP_K = 8
N_GROUPS = 8
TOPK_GROUPS = 4
EXPERT_DIM = 256
ROUTED_SCALE = 2.5
LN_EPS = 1e-5
DEEPNORM_ALPHA = (2 * DEPTH) ** 0.25

IN_SEGMENTS = (
    ('moba_q', BRANCH_WIDTH), ('moba_k', BRANCH_WIDTH), ('moba_v', BRANCH_WIDTH),
    ('diff_q', BRANCH_WIDTH), ('diff_k', BRANCH_WIDTH), ('diff_v', BRANCH_WIDTH),
    ('fox_q', BRANCH_WIDTH), ('fox_k', BRANCH_WIDTH), ('fox_v', BRANCH_WIDTH), ('fox_f', N_HEADS),
    ('dsa_q', BRANCH_WIDTH), ('dsa_k', BRANCH_WIDTH), ('dsa_v', BRANCH_WIDTH),
    ('idx_q', N_IDX_HEADS * IDX_DIM), ('idx_k', IDX_DIM), ('idx_w', N_IDX_HEADS),
    ('gates', N_BRANCHES * D_MODEL),
)

LANES = 128
SEG = BRANCH_WIDTH
NEG = -1e30
INT_MIN = -2 ** 31
VMEM_LIMIT = 48 * 1024 * 1024

_QSCALE = HEAD_DIM ** -0.5
PROJ_SEGS = (
    ('moba_q', 64, _QSCALE), ('moba_k', 64, 1.0), ('moba_v', 0, 1.0),
    ('diff_q', 32, DIFF_DIM ** -0.5), ('diff_k', 32, 1.0), ('diff_v', 0, 1.0),
    ('fox_q', 0, _QSCALE), ('fox_k', 0, 1.0), ('fox_v', 0, 1.0),
    ('dsa_q', 64, _QSCALE), ('dsa_k', 64, 1.0), ('dsa_v', 0, 1.0),
    ('idx_q', 64, IDX_DIM ** -0.5), ('idx_k4', 64, 1.0),
)
SEG_ID = {name: i for i, (name, _, _) in enumerate(PROJ_SEGS)}
N_SEG = len(PROJ_SEGS)
MISC_LOGF = 0
MISC_IDXW = 4


def _params(*sem):
    return pltpu.CompilerParams(dimension_semantics=sem, vmem_limit_bytes=VMEM_LIMIT)


def _iota(shape, dim):
    return lax.broadcasted_iota(jnp.int32, shape, dim)


def _dot_nt(a, b):
    return lax.dot_general(a, b, (((1,), (1,)), ((), ())), preferred_element_type=F32)


def _sigmoid(z):
    return 1.0 / (1.0 + jnp.exp(-z))


def _layer_norm(y, g, b):
    mu = jnp.mean(y, axis=-1, keepdims=True)
    yc = y - mu
    var = jnp.mean(yc * yc, axis=-1, keepdims=True)
    return yc * lax.rsqrt(var + LN_EPS) * g + b


def _swap_halves(a, half):
    w = a.shape[-1]
    first = (_iota(a.shape, 1) & (2 * half - 1)) < half
    return jnp.where(first, pltpu.roll(a, w - half, 1), pltpu.roll(a, half, 1))


def _in_proj_kernel(x_ref, w_ref, c64_ref, s64_ref, c32_ref, s32_ref, bf_ref, qkv_ref, misc_ref):
    xb = x_ref[...].astype(BF)
    for s, (_, rot, scale) in enumerate(PROJ_SEGS):
        acc = jnp.dot(xb, w_ref[:, s * SEG:(s + 1) * SEG], preferred_element_type=F32)
        if rot == 64:
            acc = acc * c64_ref[...] + _swap_halves(acc, 32) * s64_ref[...]
        elif rot == 32:
            acc = acc * c32_ref[...] + _swap_halves(acc, 16) * s32_ref[...]
        if scale != 1.0:
            acc = acc * scale
        qkv_ref[:, s * SEG:(s + 1) * SEG] = acc.astype(BF)
    m = jnp.dot(xb, w_ref[:, N_SEG * SEG:N_SEG * SEG + LANES], preferred_element_type=F32)
    z = m + bf_ref[...]
    logf = jnp.minimum(z, 0.0) - jnp.log1p(jnp.exp(-jnp.abs(z)))
    lane = _iota(m.shape, 1)
    misc_ref[...] = jnp.where(lane < MISC_IDXW, logf, m)


def _rope_tables(T):
    pos = jnp.arange(T).astype(F32)

    def tab(group, reps):
        half = group // 2
        inv_freq = ROPE_THETA ** (-jnp.arange(half, dtype=F32) / half)
        ang = pos[:, None] * inv_freq[None, :]
        cos, sin = jnp.cos(ang), jnp.sin(ang)
        return (jnp.tile(jnp.concatenate([cos, cos], -1), (1, reps)),
                jnp.tile(jnp.concatenate([-sin, sin], -1), (1, reps)))

    return tab(64, SEG // 64) + tab(32, SEG // 32)


def _in_proj_weights(w_in, b_forget):
    parts, off = {}, 0
    for name, width in IN_SEGMENTS:
        parts[name] = w_in[:, off:off + width]
        off += width
    parts['idx_k4'] = jnp.tile(parts['idx_k'], (1, N_IDX_HEADS))
    misc = jnp.concatenate([parts['fox_f'], parts['idx_w'],
                            jnp.zeros((D_MODEL, LANES - 2 * N_HEADS), w_in.dtype)], axis=1)
    w = jnp.concatenate([parts[name] for name, _, _ in PROJ_SEGS] + [misc], axis=1).astype(BF)
    bf = jnp.zeros((1, LANES), F32).at[0, MISC_LOGF:MISC_LOGF + N_HEADS].set(b_forget.astype(F32))
    return w, parts['gates'].astype(BF), bf


def _in_proj(x2, w, bf, tables, T, tm):
    N = x2.shape[0]
    nt = T // tm
    tab_spec = pl.BlockSpec((tm, SEG), lambda i: (i % nt, 0))
    return pl.pallas_call(
        _in_proj_kernel,
        grid=(N // tm,),
        in_specs=[pl.BlockSpec((tm, D_MODEL), lambda i: (i, 0)),
                  pl.BlockSpec(w.shape, lambda i: (0, 0)),
                  tab_spec, tab_spec, tab_spec, tab_spec,
                  pl.BlockSpec((1, LANES), lambda i: (0, 0))],
        out_specs=[pl.BlockSpec((tm, N_SEG * SEG), lambda i: (i, 0)),
                   pl.BlockSpec((tm, LANES), lambda i: (i, 0))],
        out_shape=[jax.ShapeDtypeStruct((N, N_SEG * SEG), BF),
                   jax.ShapeDtypeStruct((N, LANES), F32)],
        compiler_params=_params("parallel"),
        name="in_proj",
    )(x2, w, *tables, bf)


def _online_update(s, m, l, acc, v):
    m_new = jnp.maximum(m, jnp.max(s, axis=-1, keepdims=True))
    alpha = jnp.exp(m - m_new)
    p = jnp.exp(s - m_new)
    l = alpha * l + jnp.sum(p, axis=-1, keepdims=True)
    acc = alpha * acc + jnp.dot(p.astype(BF), v, preferred_element_type=F32)
    return m_new, l, acc


def _softmax_init(tq):
    return (jnp.full((tq, 1), NEG, F32), jnp.zeros((tq, 1), F32), jnp.zeros((tq, SEG), F32))


def _group_mask(shape, shift, g):
    return (_iota(shape, 1) >> shift) == g


def _attn_specs(T, tq, q_seg, k_seg, v_seg):
    nq = T // tq
    return [pl.BlockSpec((tq, SEG), lambda b, i: (b * nq + i, q_seg)),
            pl.BlockSpec((T, SEG), lambda b, i: (b, k_seg)),
            pl.BlockSpec((T, SEG), lambda b, i: (b, v_seg))]


def _attn_out(N, T, tq):
    nq = T // tq
    return (pl.BlockSpec((tq, SEG), lambda b, i: (b * nq + i, 0)),
            jax.ShapeDtypeStruct((N, SEG), BF))


def _moba_kernel(q_ref, k_ref, v_ref, o_ref, km_ref, *, tq, nb, n_sel):
    kc = MOBA_BLOCK
    i = pl.program_id(1)

    @pl.when(i == 0)
    def _():
        km_ref[...] = jnp.zeros_like(km_ref)
        for n in range(nb):
            blk = k_ref[n * kc:(n + 1) * kc, :].astype(F32)
            km_ref[n:n + 1, :] = jnp.mean(blk, axis=0, keepdims=True)

    q = q_ref[...]
    q0 = i * tq
    cur = q0 // kc
    qpos = q0 + _iota((tq, 1), 0)
    km = km_ref[...]
    km_hi = km.astype(BF)
    km_lo = (km - km_hi.astype(F32)).astype(BF)
    lane_g = _iota((tq, LANES), 1)
    out = jnp.zeros((tq, SEG), F32)
    for h in range(N_HEADS):
        hm = _group_mask((tq, SEG), 6, h)
        qh = jnp.where(hm, q, jnp.zeros_like(q))
        g = _dot_nt(qh, km_hi) + _dot_nt(qh, km_lo)
        g = jnp.where(lane_g < cur, g, -jnp.inf)
        sel = jnp.zeros((tq, LANES), F32)
        for _ in range(n_sel):
            gmax = jnp.max(g, axis=-1, keepdims=True)
            first = jnp.min(jnp.where(g == gmax, lane_g, LANES), axis=-1, keepdims=True)
            hit = lane_g == first
            sel = jnp.where(hit & (gmax > -jnp.inf), 1.0, sel)
            g = jnp.where(hit, -jnp.inf, g)

        def body(c, carry, qh=qh, sel=sel):
            k0 = pl.multiple_of(c * kc, kc)
            s = _dot_nt(qh, k_ref[pl.ds(k0, kc), :])
            picked = jnp.max(jnp.where(lane_g == c, sel, 0.0), axis=-1, keepdims=True) > 0.5
            limit = jnp.where(c == cur, qpos, jnp.where(picked, jnp.int32(2 ** 30), jnp.int32(-1)))
            kpos = k0 + _iota((tq, kc), 1)
            s = jnp.where(kpos <= limit, s, NEG)
            return _online_update(s, *carry, v_ref[pl.ds(k0, kc), :])

        _, l, acc = lax.fori_loop(0, cur + 1, body, _softmax_init(tq))
        out = jnp.where(hm, acc / l, out)
    o_ref[...] = out.astype(BF)


def _moba(qkv, T, tq):
    N = qkv.shape[0]
    nb = T // MOBA_BLOCK
    assert T % MOBA_BLOCK == 0 and MOBA_BLOCK % tq == 0 and nb <= LANES
    out_spec, out_shape = _attn_out(N, T, tq)
    return pl.pallas_call(
        functools.partial(_moba_kernel, tq=tq, nb=nb, n_sel=min(MOBA_TOPK, nb - 1)),
        grid=(N // T, T // tq),
        in_specs=_attn_specs(T, tq, SEG_ID['moba_q'], SEG_ID['moba_k'], SEG_ID['moba_v']),
        out_specs=out_spec, out_shape=out_shape,
        scratch_shapes=[pltpu.VMEM((LANES, SEG), F32)],
        compiler_params=_params("parallel", "arbitrary"),
        name="moba_attn",
    )(qkv, qkv, qkv)


def _diff_kernel(lam_ref, q_ref, k_ref, v_ref, g_ref, o_ref, *, tq, kc, out_scale):
    i = pl.program_id(1)
    q = q_ref[...]
    q0 = i * tq
    qpos = q0 + _iota((tq, 1), 0)
    n_ch = (q0 + tq + kc - 1) // kc
    lam = lam_ref[0]
    out = jnp.zeros((tq, SEG), F32)
    for h in range(N_HEADS):
        q1 = jnp.where(_group_mask((tq, SEG), 5, 2 * h), q, jnp.zeros_like(q))
        q2 = jnp.where(_group_mask((tq, SEG), 5, 2 * h + 1), q, jnp.zeros_like(q))

        def body(c, carry, q1=q1, q2=q2):
            k0 = pl.multiple_of(c * kc, kc)
            kch = k_ref[pl.ds(k0, kc), :]
            vch = v_ref[pl.ds(k0, kc), :]
            causal = (k0 + _iota((tq, kc), 1)) <= qpos
            s1 = jnp.where(causal, _dot_nt(q1, kch), NEG)
            s2 = jnp.where(causal, _dot_nt(q2, kch), NEG)
            return _online_update(s1, *carry[:3], vch) + _online_update(s2, *carry[3:], vch)

        _, l1, a1, _, l2, a2 = lax.fori_loop(0, n_ch, body, _softmax_init(tq) + _softmax_init(tq))
        out = jnp.where(_group_mask((tq, SEG), 6, h), a1 / l1 - lam * (a2 / l2), out)
    sq = out * out
    rs = jnp.zeros((tq, SEG), F32)
    for h in range(N_HEADS):
        hm = _group_mask((tq, SEG), 6, h)
        ms = jnp.sum(jnp.where(hm, sq, 0.0), axis=-1, keepdims=True) * (1.0 / HEAD_DIM)
        rs = jnp.where(hm, lax.rsqrt(ms + LN_EPS), rs)
    o_ref[...] = (out * rs * g_ref[...] * out_scale).astype(BF)


def _diff(qkv, lam, subln_g, lambda_init, T, tq, kc):
    N = qkv.shape[0]
    out_spec, out_shape = _attn_out(N, T, tq)
    g = jnp.tile(subln_g.astype(F32), N_HEADS)[None, :]
    return pl.pallas_call(
        functools.partial(_diff_kernel, tq=tq, kc=kc, out_scale=1.0 - lambda_init),
        grid=(N // T, T // tq),
        in_specs=[pl.BlockSpec(memory_space=pltpu.SMEM)]
        + _attn_specs(T, tq, SEG_ID['diff_q'], SEG_ID['diff_k'], SEG_ID['diff_v'])
        + [pl.BlockSpec((1, SEG), lambda b, i: (0, 0))],
        out_specs=out_spec, out_shape=out_shape,
        compiler_params=_params("parallel", "arbitrary"),
        name="diff_attn",
    )(lam.reshape(1).astype(F32), qkv, qkv, qkv, g)


def _cumsum_lanes(x):
    lane = _iota(x.shape, 1)
    sh = 1
    while sh < x.shape[-1]:
        x = x + jnp.where(lane >= sh, pltpu.roll(x, sh, 1), 0.0)
        sh *= 2
    return x


def _fox_kernel(q_ref, k_ref, v_ref, misc_ref, o_ref, c_ref, *, tq, kc):
    i = pl.program_id(1)

    @pl.when(i == 0)
    def _():
        mt = misc_ref[...].T
        c_ref[...] = _cumsum_lanes(mt[0:8, :])

    q = q_ref[...]
    q0 = i * tq
    qpos = q0 + _iota((tq, 1), 0)
    n_ch = (q0 + tq + kc - 1) // kc
    out = jnp.zeros((tq, SEG), F32)
    for h in range(N_HEADS):
        hm = _group_mask((tq, SEG), 6, h)
        qh = jnp.where(hm, q, jnp.zeros_like(q))

        def body(c, carry, qh=qh, h=h):
            k0 = pl.multiple_of(c * kc, kc)
            s = _dot_nt(qh, k_ref[pl.ds(k0, kc), :]) - c_ref[h:h + 1, pl.ds(k0, kc)]
            s = jnp.where((k0 + _iota((tq, kc), 1)) <= qpos, s, NEG)
            return _online_update(s, *carry, v_ref[pl.ds(k0, kc), :])

        _, l, acc = lax.fori_loop(0, n_ch, body, _softmax_init(tq))
        out = jnp.where(hm, acc / l, out)
    o_ref[...] = out.astype(BF)


def _fox(qkv, misc, T, tq, kc):
    N = qkv.shape[0]
    out_spec, out_shape = _attn_out(N, T, tq)
    return pl.pallas_call(
        functools.partial(_fox_kernel, tq=tq, kc=kc),
        grid=(N // T, T // tq),
        in_specs=_attn_specs(T, tq, SEG_ID['fox_q'], SEG_ID['fox_k'], SEG_ID['fox_v'])
        + [pl.BlockSpec((T, LANES), lambda b, i: (b, 0))],
        out_specs=out_spec, out_shape=out_shape,
        scratch_shapes=[pltpu.VMEM((8, T), F32)],
        compiler_params=_params("parallel", "arbitrary"),
        name="fox_attn",
    )(qkv, qkv, qkv, misc)


def _dsa_kernel(q_ref, k_ref, v_ref, qi_ref, ki_ref, w_ref, o_ref, sc_ref, key_ref, bias_ref,
                *, tq, kc, T, n_keep):
    i = pl.program_id(1)
    q0 = i * tq
    qpos = q0 + _iota((tq, 1), 0)
    n_ch = (q0 + tq + kc - 1) // kc

    qi = qi_ref[...]
    w = w_ref[...] * (N_IDX_HEADS ** -0.5)
    sc_ref[...] = jnp.full((tq, T), -jnp.inf, F32)

    def score_body(c, carry):
        k0 = pl.multiple_of(c * kc, kc)
        kich = ki_ref[pl.ds(k0, kc), :]
        score = jnp.zeros((tq, kc), F32)
        for h in range(N_IDX_HEADS):
            qh = jnp.where(_group_mask((tq, SEG), 6, h), qi, jnp.zeros_like(qi))
            score = score + w[:, MISC_IDXW + h:MISC_IDXW + h + 1] * jnp.maximum(_dot_nt(qh, kich), 0.0)
        causal = (k0 + _iota((tq, kc), 1)) <= qpos
        sc_ref[:, pl.ds(k0, kc)] = jnp.where(causal, score, -jnp.inf)
        return carry

    lax.fori_loop(0, n_ch, score_body, 0)

    sc = sc_ref[...]
    bits = pltpu.bitcast(sc, jnp.int32)
    key_ref[...] = jnp.where(sc == 0.0, 0, jnp.where(bits < 0, bits ^ jnp.int32(0x7FFFFFFF), bits))

    def thr_body(it, thr):
        cand = thr + lax.shift_left(jnp.int32(1), 31 - it)
        cnt = jnp.sum(jnp.where(key_ref[...] >= cand, 1.0, 0.0), axis=-1, keepdims=True)
        return jnp.where(cnt >= n_keep, cand, thr)

    thr = lax.fori_loop(0, 32, thr_body, jnp.full((tq, 1), INT_MIN, jnp.int32))

    n_gt = jnp.sum(jnp.where(key_ref[...] > thr, 1.0, 0.0), axis=-1, keepdims=True)
    need = n_keep - n_gt
    upper = jnp.where(_iota((LANES, LANES), 0) <= _iota((LANES, LANES), 1), 1.0, 0.0).astype(BF)
    seen = jnp.zeros((tq, 1), F32)
    for cc in range(T // LANES):
        kk = key_ref[:, cc * LANES:(cc + 1) * LANES]
        eq = jnp.where(kk == thr, 1.0, 0.0)
        rank = jnp.dot(eq.astype(BF), upper, preferred_element_type=F32) + seen
        keep = (kk > thr) | ((kk == thr) & (rank <= need))
        causal = (cc * LANES + _iota((tq, LANES), 1)) <= qpos
        bias_ref[:, cc * LANES:(cc + 1) * LANES] = jnp.where(keep & causal, 0.0, NEG)
        seen = seen + jnp.sum(eq, axis=-1, keepdims=True)

    q = q_ref[...]
    out = jnp.zeros((tq, SEG), F32)
    for h in range(N_HEADS):
        hm = _group_mask((tq, SEG), 6, h)
        qh = jnp.where(hm, q, jnp.zeros_like(q))

        def body(c, carry, qh=qh):
            k0 = pl.multiple_of(c * kc, kc)
            s = _dot_nt(qh, k_ref[pl.ds(k0, kc), :]) + bias_ref[:, pl.ds(k0, kc)]
            return _online_update(s, *carry, v_ref[pl.ds(k0, kc), :])

        _, l, acc = lax.fori_loop(0, n_ch, body, _softmax_init(tq))
        out = jnp.where(hm, acc / l, out)
    o_ref[...] = out.astype(BF)


def _dsa(qkv, misc, T, tq, kc):
    N = qkv.shape[0]
    nq = T // tq
    out_spec, out_shape = _attn_out(N, T, tq)
    return pl.pallas_call(
        functools.partial(_dsa_kernel, tq=tq, kc=kc, T=T, n_keep=min(DSA_TOPK_MAX, T // 4)),
        grid=(N // T, nq),
        in_specs=_attn_specs(T, tq, SEG_ID['dsa_q'], SEG_ID['dsa_k'], SEG_ID['dsa_v'])
        + [pl.BlockSpec((tq, SEG), lambda b, i: (b * nq + i, SEG_ID['idx_q'])),
           pl.BlockSpec((T, SEG), lambda b, i: (b, SEG_ID['idx_k4'])),
           pl.BlockSpec((tq, LANES), lambda b, i: (b * nq + i, 0))],
        out_specs=out_spec, out_shape=out_shape,
        scratch_shapes=[pltpu.VMEM((tq, T), F32), pltpu.VMEM((tq, T), jnp.int32),
                        pltpu.VMEM((tq, T), F32)],
        compiler_params=_params("parallel", "arbitrary"),
        name="dsa_attn",
    )(qkv, qkv, qkv, qkv, qkv, misc)


def _mix_out_kernel(x_ref, oa_ref, ob_ref, oc_ref, od_ref, wg_ref, wb_ref, wo_ref, g_ref, b_ref, y_ref):
    x = x_ref[...]
    xb = x.astype(BF)
    merged = jnp.zeros(x.shape, F32)
    for n, o_ref in enumerate((oa_ref, ob_ref, oc_ref, od_ref)):
        gate = _sigmoid(jnp.dot(xb, wg_ref[:, n * D_MODEL:(n + 1) * D_MODEL], preferred_element_type=F32))
        merged = merged + gate * jnp.dot(o_ref[...], wb_ref[n], preferred_element_type=F32)
    h = jnp.dot(merged.astype(BF), wo_ref[...], preferred_element_type=F32)
    y_ref[...] = _layer_norm(DEEPNORM_ALPHA * x + h, g_ref[...], b_ref[...])


def _mix_out(x2, branches, w_gates, w_branch, w_out, ln_g, ln_b, tm):
    N = x2.shape[0]
    row = lambda i: (i, 0)
    fixed2 = lambda i: (0, 0)
    return pl.pallas_call(
        _mix_out_kernel,
        grid=(N // tm,),
        in_specs=[pl.BlockSpec((tm, D_MODEL), row)] + [pl.BlockSpec((tm, SEG), row)] * N_BRANCHES
        + [pl.BlockSpec(w_gates.shape, fixed2),
           pl.BlockSpec(w_branch.shape, lambda i: (0, 0, 0)),
           pl.BlockSpec(w_out.shape, fixed2),
           pl.BlockSpec((1, D_MODEL), fixed2), pl.BlockSpec((1, D_MODEL), fixed2)],
        out_specs=pl.BlockSpec((tm, D_MODEL), row),
        out_shape=jax.ShapeDtypeStruct((N, D_MODEL), F32),
        compiler_params=_params("parallel"),
        name="mix_out",
    )(x2, *branches, w_gates, w_branch.astype(BF), w_out.astype(BF),
      ln_g.astype(F32)[None, :], ln_b.astype(F32)[None, :])


def _split_bf16(a):
    hi = a.astype(BF)
    return hi, (a - hi.astype(F32)).astype(BF)


def _router_kernel(x_ref, whi_ref, wlo_ref, rb_ref, idx_ref, wsel_ref):
    x = x_ref[...]
    xhi, xlo = _split_bf16(x)
    whi = whi_ref[...]
    logits = (jnp.dot(xhi, whi, preferred_element_type=F32)
              + jnp.dot(xlo, whi, preferred_element_type=F32)
              + jnp.dot(xhi, wlo_ref[...], preferred_element_type=F32))
    scores = _sigmoid(logits)
    biased = scores + rb_ref[...]
    tm = x.shape[0]
    lane = _iota((tm, N_EXPERTS), 1)
    per_group = N_EXPERTS // N_GROUPS
    gid = lane // per_group
    gs = []
    for g in range(N_GROUPS):
        mg = jnp.where(gid == g, biased, -jnp.inf)
        m1 = jnp.max(mg, axis=-1, keepdims=True)
        i1 = jnp.min(jnp.where(mg == m1, lane, N_EXPERTS), axis=-1, keepdims=True)
        m2 = jnp.max(jnp.where(lane == i1, -jnp.inf, mg), axis=-1, keepdims=True)
        gs.append(m1 + m2)
    masked = jnp.full((tm, N_EXPERTS), -jnp.inf, F32)
    for g in range(N_GROUPS):
        rank = jnp.zeros((tm, 1), F32)
        for o in range(N_GROUPS):
            if o != g:
                beats = (gs[o] > gs[g]) | (gs[o] == gs[g]) if o < g else (gs[o] > gs[g])
                rank = rank + jnp.where(beats, 1.0, 0.0)
        masked = jnp.where((gid == g) & (rank < TOPK_GROUPS), biased, masked)
    lane_o = _iota((tm, LANES), 1)
    idx_out = jnp.zeros((tm, LANES), jnp.int32)
    w_out = jnp.zeros((tm, LANES), F32)
    wsum = jnp.zeros((tm, 1), F32)
    for k in range(TOP_K):
        mx = jnp.max(masked, axis=-1, keepdims=True)
        pick = jnp.min(jnp.where(masked == mx, lane, N_EXPERTS), axis=-1, keepdims=True)
        hit = lane == pick
        sc = jnp.sum(jnp.where(hit, scores, 0.0), axis=-1, keepdims=True)
        masked = jnp.where(hit, -jnp.inf, masked)
        idx_out = jnp.where(lane_o == k, pick, idx_out)
        w_out = jnp.where(lane_o == k, sc, w_out)
        wsum = wsum + sc
    idx_ref[...] = idx_out
    wsel_ref[...] = w_out / wsum * ROUTED_SCALE


def _router(x1, w_router, router_bias, tm):
    N = x1.shape[0]
    whi, wlo = _split_bf16(w_router.astype(F32))
    row = lambda i: (i, 0)
    fixed = lambda i: (0, 0)
    idx, wsel = pl.pallas_call(
        _router_kernel,
        grid=(N // tm,),
        in_specs=[pl.BlockSpec((tm, D_MODEL), row), pl.BlockSpec(whi.shape, fixed),
                  pl.BlockSpec(wlo.shape, fixed), pl.BlockSpec((1, N_EXPERTS), fixed)],
        out_specs=[pl.BlockSpec((tm, LANES), row), pl.BlockSpec((tm, LANES), row)],
        out_shape=[jax.ShapeDtypeStruct((N, LANES), jnp.int32), jax.ShapeDtypeStruct((N, LANES), F32)],
        compiler_params=_params("parallel"),
        name="router",
    )(x1, whi, wlo, router_bias.astype(F32)[None, :])
    return idx[:, :TOP_K], wsel[:, :TOP_K]


def _expert_kernel(be_ref, nu_ref, xs_ref, rw_ref, wg_ref, wu_ref, wd_ref, y_ref):
    b = pl.program_id(0)

    @pl.when(b < nu_ref[0])
    def _():
        xs = xs_ref[...]
        g = jnp.dot(xs, wg_ref[...], preferred_element_type=F32)
        u = jnp.dot(xs, wu_ref[...], preferred_element_type=F32)
        h = (g * _sigmoid(g) * u).astype(BF)
        y = jnp.dot(h, wd_ref[...], preferred_element_type=F32)
        y_ref[...] = (y * rw_ref[...]).astype(y_ref.dtype)

    @pl.when(b >= nu_ref[0])
    def _():
        y_ref[...] = jnp.zeros_like(y_ref)


def _experts(xs, row_w, block_e, n_used, w_gate, w_up, w_down, bm):
    P = xs.shape[0]
    wspec = lambda shape: pl.BlockSpec((None,) + shape, lambda b, be, nu: (be[b], 0, 0))
    return pl.pallas_call(
        _expert_kernel,
        grid_spec=pltpu.PrefetchScalarGridSpec(
            num_scalar_prefetch=2,
            grid=(P // bm,),
            in_specs=[pl.BlockSpec((bm, D_MODEL), lambda b, be, nu: (b, 0)),
                      pl.BlockSpec((bm, 1), lambda b, be, nu: (b, 0)),
                      wspec((D_MODEL, EXPERT_DIM)), wspec((D_MODEL, EXPERT_DIM)),
                      wspec((EXPERT_DIM, D_MODEL))],
            out_specs=pl.BlockSpec((bm, D_MODEL), lambda b, be, nu: (b, 0)),
        ),
        out_shape=jax.ShapeDtypeStruct((P, D_MODEL), F32),
        compiler_params=_params("arbitrary"),
        name="experts",
    )(block_e, n_used, xs, row_w, w_gate, w_up, w_down)


def _dispatch_plan(eidx, wsel, bm):
    N = eidx.shape[0]
    A, E = N * TOP_K, N_EXPERTS
    flat_e = eidx.reshape(A)
    order = jnp.argsort(flat_e, stable=True).astype(jnp.int32)
    se = flat_e[order]
    counts = jnp.zeros((E,), jnp.int32).at[flat_e].add(1)
    start = jnp.cumsum(counts) - counts
    padded = (counts + bm - 1) // bm * bm
    pend = jnp.cumsum(padded)
    pstart = pend - padded
    dest = pstart[se] + (jnp.arange(A, dtype=jnp.int32) - start[se])
    n_blocks = (A + E * (bm - 1)) // bm + 1
    P = n_blocks * bm
    row_tok = jnp.zeros((P,), jnp.int32).at[dest].set(order // TOP_K)
    row_w = jnp.zeros((P,), F32).at[dest].set(wsel.reshape(A)[order])
    block_e = jnp.minimum(jnp.searchsorted(pend, jnp.arange(n_blocks, dtype=jnp.int32) * bm, side='right'),
                          E - 1).astype(jnp.int32)
    n_used = (pend[-1] // bm).astype(jnp.int32).reshape(1)
    pos = jnp.zeros((A,), jnp.int32).at[order].set(dest).reshape(N, TOP_K)
    return row_tok, row_w, block_e, n_used, pos


def _moe_out_kernel(x_ref, r_ref, wg_ref, wu_ref, wd_ref, g_ref, b_ref, y_ref):
    x = x_ref[...]
    xb = x.astype(BF)
    g = jnp.dot(xb, wg_ref[...], preferred_element_type=F32)
    u = jnp.dot(xb, wu_ref[...], preferred_element_type=F32)
    h = (g * _sigmoid(g) * u).astype(BF)
    shared = jnp.dot(h, wd_ref[...], preferred_element_type=F32)
    y_ref[...] = _layer_norm(DEEPNORM_ALPHA * x + (r_ref[...] + shared), g_ref[...], b_ref[...])


def _moe_out(x1, routed, w_sh_gate, w_sh_up, w_sh_down, ln_g, ln_b, tm):
    N = x1.shape[0]
    row = lambda i: (i, 0)
    fixed = lambda i: (0, 0)
    return pl.pallas_call(
        _moe_out_kernel,
        grid=(N // tm,),
        in_specs=[pl.BlockSpec((tm, D_MODEL), row), pl.BlockSpec((tm, D_MODEL), row),
                  pl.BlockSpec(w_sh_gate.shape, fixed), pl.BlockSpec(w_sh_up.shape, fixed),
                  pl.BlockSpec(w_sh_down.shape, fixed),
                  pl.BlockSpec((1, D_MODEL), fixed), pl.BlockSpec((1, D_MODEL), fixed)],
        out_specs=pl.BlockSpec((tm, D_MODEL), row),
        out_shape=jax.ShapeDtypeStruct((N, D_MODEL), F32),
        compiler_params=_params("parallel"),
        name="moe_out",
    )(x1, routed, w_sh_gate.astype(BF), w_sh_up.astype(BF), w_sh_down.astype(BF),
      ln_g.astype(F32)[None, :], ln_b.astype(F32)[None, :])


def _tiles(N, T):
    tm_proj = min(512, T)
    tm_row = min(256, T)
    tq = min(256, T)
    kc = min(256, T)
    return tm_proj, tm_row, tq, kc


def _mixer_layer(x2, T, w_in, b_forget, diff_lambda, diff_subln, w_branch, w_out, ln_g, ln_b, lambda_init):
    N = x2.shape[0]
    tm_proj, tm_row, tq, kc = _tiles(N, T)
    w, w_gates, bf = _in_proj_weights(w_in, b_forget)
    qkv, misc = _in_proj(x2, w, bf, _rope_tables(T), T, tm_proj)
    dl = diff_lambda.astype(F32)
    lam = jnp.exp(jnp.sum(dl[0] * dl[1])) - jnp.exp(jnp.sum(dl[2] * dl[3])) + lambda_init
    o_a = _moba(qkv, T, tq)
    o_b = _diff(qkv, lam, diff_subln, lambda_init, T, tq, kc)
    o_c = _fox(qkv, misc, T, tq, kc)
    o_d = _dsa(qkv, misc, T, tq, kc)
    return _mix_out(x2, (o_a, o_b, o_c, o_d), w_gates, w_branch, w_out, ln_g, ln_b, tm_row)


def _moe_layer(x1, T, w_router, router_bias, w_exp_gate, w_exp_up, w_exp_down,
               w_sh_gate, w_sh_up, w_sh_down, ln_g, ln_b):
    N = x1.shape[0]
    _, tm_row, _, _ = _tiles(N, T)
    bm = 256
    eidx, wsel = _router(x1, w_router, router_bias, tm_row)
    row_tok, row_w, block_e, n_used, pos = _dispatch_plan(eidx, wsel, bm)
    xs = x1.astype(BF)[row_tok]
    ys = _experts(xs, row_w[:, None], block_e, n_used,
                  w_exp_gate.astype(BF), w_exp_up.astype(BF), w_exp_down.astype(BF), bm)
    routed = jnp.sum(ys[pos], axis=1)
    return _moe_out(x1, routed, w_sh_gate, w_sh_up, w_sh_down, ln_g, ln_b, tm_row)


def kernel(x, w_in, b_forget, diff_lambda, diff_subln, w_branch, w_out, ln1_g, ln1_b, w_router, router_bias,
           w_exp_gate, w_exp_up, w_exp_down, w_sh_gate, w_sh_up, w_sh_down, ln2_g, ln2_b):
    B, T, D = x.shape
    x2 = x.reshape(B * T, D)
    for l in range(DEPTH):
        lambda_init = 0.8 - 0.6 * math.exp(-0.3 * l)
        x2 = _mixer_layer(x2, T, w_in[l], b_forget[l], diff_lambda[l], diff_subln[l], w_branch[l], w_out[l],
                          ln1_g[l], ln1_b[l], lambda_init)
        x2 = _moe_layer(x2, T, w_router[l], router_bias[l], w_exp_gate[l], w_exp_up[l], w_exp_down[l],
                        w_sh_gate[l], w_sh_up[l], w_sh_down[l], ln2_g[l], ln2_b[l])
    return x2.reshape(B, T, D)
```

```python
import functools
import math

import jax
import jax.numpy as jnp
from jax import lax
from jax.experimental import pallas as pl
from jax.experimental.pallas import tpu as pltpu

F32 = jnp.float32
BF = jnp.bfloat16

D_MODEL = 1024
DEPTH = 2
HEAD_DIM = 64
N_HEADS = 4
DIFF_DIM = HEAD_DIM // 2
N_IDX_HEADS = 4
IDX_DIM = 64
BRANCH_WIDTH = N_HEADS * HEAD_DIM
N_BRANCHES = 4
MOBA_BLOCK = 256
MOBA_TOPK = 3
DSA_TOPK_MAX = 256
ROPE_THETA = 10000.0
N_EXPERTS = 256
TOP_K = 8
N_GROUPS = 8
TOPK_GROUPS = 4
EXPERT_DIM = 256
ROUTED_SCALE = 2.5
LN_EPS = 1e-5
DEEPNORM_ALPHA = (2 * DEPTH) ** 0.25

IN_SEGMENTS = (
    ('moba_q', BRANCH_WIDTH), ('moba_k', BRANCH_WIDTH), ('moba_v', BRANCH_WIDTH),
    ('diff_q', BRANCH_WIDTH), ('diff_k', BRANCH_WIDTH), ('diff_v', BRANCH_WIDTH),
    ('fox_q', BRANCH_WIDTH), ('fox_k', BRANCH_WIDTH), ('fox_v', BRANCH_WIDTH), ('fox_f', N_HEADS),
    ('dsa_q', BRANCH_WIDTH), ('dsa_k', BRANCH_WIDTH), ('dsa_v', BRANCH_WIDTH),
    ('idx_q', N_IDX_HEADS * IDX_DIM), ('idx_k', IDX_DIM), ('idx_w', N_IDX_HEADS),
    ('gates', N_BRANCHES * D_MODEL),
)

LANES = 128
SEG = BRANCH_WIDTH
NEG = -1e30
INT_MIN = -2 ** 31
VMEM_LIMIT = 48 * 1024 * 1024

_QSCALE = HEAD_DIM ** -0.5
PROJ_SEGS = (
    ('moba_q', 64, _QSCALE), ('moba_k', 64, 1.0), ('moba_v', 0, 1.0),
    ('diff_q', 32, DIFF_DIM ** -0.5), ('diff_k', 32, 1.0), ('diff_v', 0, 1.0),
    ('fox_q', 0, _QSCALE), ('fox_k', 0, 1.0), ('fox_v', 0, 1.0),
    ('dsa_q', 64, _QSCALE), ('dsa_k', 64, 1.0), ('dsa_v', 0, 1.0),
    ('idx_q', 64, IDX_DIM ** -0.5), ('idx_k4', 64, 1.0),
)
SEG_ID = {name: i for i, (name, _, _) in enumerate(PROJ_SEGS)}
N_SEG = len(PROJ_SEGS)
MISC_LOGF = 0
MISC_IDXW = 4


def _params(*sem):
    return pltpu.CompilerParams(dimension_semantics=sem, vmem_limit_bytes=VMEM_LIMIT)


def _iota(shape, dim):
    return lax.broadcasted_iota(jnp.int32, shape, dim)


def _dot_nt(a, b):
    return lax.dot_general(a, b, (((1,), (1,)), ((), ())), preferred_element_type=F32)


def _sigmoid(z):
    return 1.0 / (1.0 + jnp.exp(-z))


def _layer_norm(y, g, b):
    mu = jnp.mean(y, axis=-1, keepdims=True)
    yc = y - mu
    var = jnp.mean(yc * yc, axis=-1, keepdims=True)
    return yc * lax.rsqrt(var + LN_EPS) * g + b


def _swap_halves(a, half):
    w = a.shape[-1]
    first = (_iota(a.shape, 1) & (2 * half - 1)) < half
    return jnp.where(first, pltpu.roll(a, w - half, 1), pltpu.roll(a, half, 1))


def _in_proj_kernel(x_ref, w_ref, c64_ref, s64_ref, c32_ref, s32_ref, bf_ref, qkv_ref, misc_ref):
    xb = x_ref[...].astype(BF)
    for s, (_, rot, scale) in enumerate(PROJ_SEGS):
        acc = jnp.dot(xb, w_ref[:, s * SEG:(s + 1) * SEG], preferred_element_type=F32)
        if rot == 64:
            acc = acc * c64_ref[...] + _swap_halves(acc, 32) * s64_ref[...]
        elif rot == 32:
            acc = acc * c32_ref[...] + _swap_halves(acc, 16) * s32_ref[...]
        if scale != 1.0:
            acc = acc * scale
        qkv_ref[:, s * SEG:(s + 1) * SEG] = acc.astype(BF)
    m = jnp.dot(xb, w_ref[:, N_SEG * SEG:N_SEG * SEG + LANES], preferred_element_type=F32)
    z = m + bf_ref[...]
    logf = jnp.minimum(z, 0.0) - jnp.log1p(jnp.exp(-jnp.abs(z)))
    lane = _iota(m.shape, 1)
    misc_ref[...] = jnp.where(lane < MISC_IDXW, logf, m)


def _rope_tables(T):
    pos = jnp.arange(T).astype(F32)

    def tab(group, reps):
        half = group // 2
        inv_freq = ROPE_THETA ** (-jnp.arange(half, dtype=F32) / half)
        ang = pos[:, None] * inv_freq[None, :]
        cos, sin = jnp.cos(ang), jnp.sin(ang)
        return (jnp.tile(jnp.concatenate([cos, cos], -1), (1, reps)),
                jnp.tile(jnp.concatenate([-sin, sin], -1), (1, reps)))

    return tab(64, SEG // 64) + tab(32, SEG // 32)


def _in_proj_weights(w_in, b_forget):
    parts, off = {}, 0
    for name, width in IN_SEGMENTS:
        parts[name] = w_in[:, off:off + width]
        off += width
    parts['idx_k4'] = jnp.tile(parts['idx_k'], (1, N_IDX_HEADS))
    misc = jnp.concatenate([parts['fox_f'], parts['idx_w'],
                            jnp.zeros((D_MODEL, LANES - 2 * N_HEADS), w_in.dtype)], axis=1)
    w = jnp.concatenate([parts[name] for name, _, _ in PROJ_SEGS] + [misc], axis=1).astype(BF)
    bf = jnp.zeros((1, LANES), F32).at[0, MISC_LOGF:MISC_LOGF + N_HEADS].set(b_forget.astype(F32))
    return w, parts['gates'].astype(BF), bf


def _in_proj(x2, w, bf, tables, T, tm):
    N = x2.shape[0]
    nt = T // tm
    tab_spec = pl.BlockSpec((tm, SEG), lambda i: (i % nt, 0))
    return pl.pallas_call(
        _in_proj_kernel,
        grid=(N // tm,),
        in_specs=[pl.BlockSpec((tm, D_MODEL), lambda i: (i, 0)),
                  pl.BlockSpec(w.shape, lambda i: (0, 0)),
                  tab_spec, tab_spec, tab_spec, tab_spec,
                  pl.BlockSpec((1, LANES), lambda i: (0, 0))],
        out_specs=[pl.BlockSpec((tm, N_SEG * SEG), lambda i: (i, 0)),
                   pl.BlockSpec((tm, LANES), lambda i: (i, 0))],
        out_shape=[jax.ShapeDtypeStruct((N, N_SEG * SEG), BF),
                   jax.ShapeDtypeStruct((N, LANES), F32)],
        compiler_params=_params("parallel"),
        name="in_proj",
    )(x2, w, *tables, bf)


def _group_mask(shape, shift, g):
    return (_iota(shape, 1) >> shift) == g


def _stack_groups(q, shift, n_groups):
    zero = jnp.zeros_like(q)
    return jnp.concatenate([jnp.where(_group_mask(q.shape, shift, g), q, zero) for g in range(n_groups)],
                           axis=0)


def _stack_values(v):
    zero = jnp.zeros_like(v)
    return jnp.concatenate([jnp.where(_group_mask(v.shape, 6, h), v, zero) for h in range(N_HEADS)], axis=0)


def _rows(a, g, tq):
    return a[g * tq:(g + 1) * tq]


def _spread(cols):
    shape = (cols[0].shape[0], SEG)
    out = jnp.broadcast_to(cols[-1], shape)
    for h in range(N_HEADS - 2, -1, -1):
        out = jnp.where(_group_mask(shape, 6, h), cols[h], out)
    return out


def _flash_init(n_groups, tq, n_sets):
    return (jnp.full((n_groups * tq, 1), NEG, F32), jnp.zeros((n_groups * tq, 1), F32),
            tuple(jnp.zeros((tq, SEG), F32) for _ in range(n_sets)))


def _flash_update(s, carry, v_stack, tq, head_sets):
    m, l, accs = carry
    m_new = jnp.maximum(m, jnp.max(s, axis=-1, keepdims=True))
    alpha = jnp.exp(m - m_new)
    p = jnp.exp(s - m_new)
    l = alpha * l + jnp.sum(p, axis=-1, keepdims=True)
    pb = p.astype(BF)
    new_accs = []
    for acc, groups in zip(accs, head_sets):
        p_cat = jnp.concatenate([_rows(pb, g, tq) for g in groups], axis=1)
        a = _spread([_rows(alpha, g, tq) for g in groups])
        new_accs.append(a * acc + jnp.dot(p_cat, v_stack, preferred_element_type=F32))
    return m_new, l, tuple(new_accs)


def _causal(n_groups, tq):
    shape = (n_groups * tq, tq)
    return _iota(shape, 1) <= (_iota(shape, 0) & (tq - 1))


def _normalised(acc, l, groups, tq):
    return acc * _spread([1.0 / _rows(l, g, tq) for g in groups])


HEADS = tuple(range(N_HEADS))


def _attn_specs(T, tq, q_seg, k_seg, v_seg):
    nq = T // tq
    return [pl.BlockSpec((tq, SEG), lambda b, i: (b * nq + i, q_seg)),
            pl.BlockSpec((T, SEG), lambda b, i: (b, k_seg)),
            pl.BlockSpec((T, SEG), lambda b, i: (b, v_seg))]


def _attn_out(N, T, tq):
    nq = T // tq
    return (pl.BlockSpec((tq, SEG), lambda b, i: (b * nq + i, 0)),
            jax.ShapeDtypeStruct((N, SEG), BF))


def _moba_kernel(q_ref, k_ref, v_ref, o_ref, km_ref, *, tq, nb, n_sel):
    i = pl.program_id(1)

    @pl.when(i == 0)
    def _():
        km_ref[...] = jnp.zeros_like(km_ref)
        for n in range(nb):
            blk = k_ref[n * tq:(n + 1) * tq, :].astype(F32)
            km_ref[n:n + 1, :] = jnp.mean(blk, axis=0, keepdims=True)

    qs = _stack_groups(q_ref[...], 6, N_HEADS)
    km = km_ref[...]
    km_hi = km.astype(BF)
    km_lo = (km - km_hi.astype(F32)).astype(BF)
    lane_g = _iota((N_HEADS * tq, LANES), 1)
    g = _dot_nt(qs, km_hi) + _dot_nt(qs, km_lo)
    g = jnp.where(lane_g < i, g, -jnp.inf)
    sel = jnp.zeros(g.shape, F32)
    for _ in range(n_sel):
        gmax = jnp.max(g, axis=-1, keepdims=True)
        first = jnp.min(jnp.where(g == gmax, lane_g, LANES), axis=-1, keepdims=True)
        hit = lane_g == first
        sel = jnp.where(hit & (gmax > -jnp.inf), 1.0, sel)
        g = jnp.where(hit, -jnp.inf, g)

    def body(c, carry):
        k0 = pl.multiple_of(c * tq, tq)
        picked = jnp.max(jnp.where(lane_g == c, sel, 0.0), axis=-1, keepdims=True) > 0.5
        s = jnp.where(picked, _dot_nt(qs, k_ref[pl.ds(k0, tq), :]), NEG)
        return _flash_update(s, carry, _stack_values(v_ref[pl.ds(k0, tq), :]), tq, (HEADS,))

    carry = lax.fori_loop(0, i, body, _flash_init(N_HEADS, tq, 1))
    k0 = pl.multiple_of(i * tq, tq)
    s = jnp.where(_causal(N_HEADS, tq), _dot_nt(qs, k_ref[pl.ds(k0, tq), :]), NEG)
    _, l, (acc,) = _flash_update(s, carry, _stack_values(v_ref[pl.ds(k0, tq), :]), tq, (HEADS,))
    o_ref[...] = _normalised(acc, l, HEADS, tq).astype(BF)


def _moba(qkv, T, tq):
    N = qkv.shape[0]
    nb = T // MOBA_BLOCK
    assert T % MOBA_BLOCK == 0 and tq == MOBA_BLOCK and nb <= LANES
    out_spec, out_shape = _attn_out(N, T, tq)
    return pl.pallas_call(
        functools.partial(_moba_kernel, tq=tq, nb=nb, n_sel=min(MOBA_TOPK, nb - 1)),
        grid=(N // T, T // tq),
        in_specs=_attn_specs(T, tq, SEG_ID['moba_q'], SEG_ID['moba_k'], SEG_ID['moba_v']),
        out_specs=out_spec, out_shape=out_shape,
        scratch_shapes=[pltpu.VMEM((LANES, SEG), F32)],
        compiler_params=_params("parallel", "arbitrary"),
        name="moba_attn",
    )(qkv, qkv, qkv)


DIFF_SETS = (tuple(2 * h for h in HEADS), tuple(2 * h + 1 for h in HEADS))


def _diff_kernel(lam_ref, q_ref, k_ref, v_ref, g_ref, o_ref, *, tq, out_scale):
    i = pl.program_id(1)
    n_groups = 2 * N_HEADS
    qs = _stack_groups(q_ref[...], 5, n_groups)

    def body(c, carry):
        k0 = pl.multiple_of(c * tq, tq)
        s = _dot_nt(qs, k_ref[pl.ds(k0, tq), :])
        return _flash_update(s, carry, _stack_values(v_ref[pl.ds(k0, tq), :]), tq, DIFF_SETS)

    carry = lax.fori_loop(0, i, body, _flash_init(n_groups, tq, 2))
    k0 = pl.multiple_of(i * tq, tq)
    s = jnp.where(_causal(n_groups, tq), _dot_nt(qs, k_ref[pl.ds(k0, tq), :]), NEG)
    _, l, (a1, a2) = _flash_update(s, carry, _stack_values(v_ref[pl.ds(k0, tq), :]), tq, DIFF_SETS)
    out = _normalised(a1, l, DIFF_SETS[0], tq) - lam_ref[0] * _normalised(a2, l, DIFF_SETS[1], tq)
    sq = out * out
    ms = [jnp.sum(jnp.where(_group_mask(sq.shape, 6, h), sq, 0.0), axis=-1, keepdims=True) for h in HEADS]
    rs = _spread([lax.rsqrt(m * (1.0 / HEAD_DIM) + LN_EPS) for m in ms])
    o_ref[...] = (out * rs * g_ref[...] * out_scale).astype(BF)


def _diff(qkv, lam, subln_g, lambda_init, T, tq):
    N = qkv.shape[0]
    out_spec, out_shape = _attn_out(N, T, tq)
    g = jnp.tile(subln_g.astype(F32), N_HEADS)[None, :]
    return pl.pallas_call(
        functools.partial(_diff_kernel, tq=tq, out_scale=1.0 - lambda_init),
        grid=(N // T, T // tq),
        in_specs=[pl.BlockSpec(memory_space=pltpu.SMEM)]
        + _attn_specs(T, tq, SEG_ID['diff_q'], SEG_ID['diff_k'], SEG_ID['diff_v'])
        + [pl.BlockSpec((1, SEG), lambda b, i: (0, 0))],
        out_specs=out_spec, out_shape=out_shape,
        compiler_params=_params("parallel", "arbitrary"),
        name="diff_attn",
    )(lam.reshape(1).astype(F32), qkv, qkv, qkv, g)


def _cumsum_lanes(x):
    lane = _iota(x.shape, 1)
    sh = 1
    while sh < x.shape[-1]:
        x = x + jnp.where(lane >= sh, pltpu.roll(x, sh, 1), 0.0)
        sh *= 2
    return x


def _fox_kernel(q_ref, k_ref, v_ref, misc_ref, o_ref, c_ref, *, tq):
    i = pl.program_id(1)

    @pl.when(i == 0)
    def _():
        mt = misc_ref[...].T
        c_ref[...] = _cumsum_lanes(mt[0:8, :])

    qs = _stack_groups(q_ref[...], 6, N_HEADS)

    def logits(k0):
        s = _dot_nt(qs, k_ref[pl.ds(k0, tq), :])
        return jnp.concatenate([_rows(s, h, tq) - c_ref[h:h + 1, pl.ds(k0, tq)] for h in HEADS], axis=0)

    def body(c, carry):
        k0 = pl.multiple_of(c * tq, tq)
        return _flash_update(logits(k0), carry, _stack_values(v_ref[pl.ds(k0, tq), :]), tq, (HEADS,))

    carry = lax.fori_loop(0, i, body, _flash_init(N_HEADS, tq, 1))
    k0 = pl.multiple_of(i * tq, tq)
    s = jnp.where(_causal(N_HEADS, tq), logits(k0), NEG)
    _, l, (acc,) = _flash_update(s, carry, _stack_values(v_ref[pl.ds(k0, tq), :]), tq, (HEADS,))
    o_ref[...] = _normalised(acc, l, HEADS, tq).astype(BF)


def _fox(qkv, misc, T, tq):
    N = qkv.shape[0]
    out_spec, out_shape = _attn_out(N, T, tq)
    return pl.pallas_call(
        functools.partial(_fox_kernel, tq=tq),
        grid=(N // T, T // tq),
        in_specs=_attn_specs(T, tq, SEG_ID['fox_q'], SEG_ID['fox_k'], SEG_ID['fox_v'])
        + [pl.BlockSpec((T, LANES), lambda b, i: (b, 0))],
        out_specs=out_spec, out_shape=out_shape,
        scratch_shapes=[pltpu.VMEM((8, T), F32)],
        compiler_params=_params("parallel", "arbitrary"),
        name="fox_attn",
    )(qkv, qkv, qkv, misc)


def _dsa_kernel(q_ref, k_ref, v_ref, qi_ref, ki_ref, w_ref, o_ref, key_ref, bias_ref, *, tq, n_keep):
    i = pl.program_id(1)
    n_ch = i + 1
    qpos = i * tq + _iota((tq, 1), 0)

    qis = _stack_groups(qi_ref[...], 6, N_IDX_HEADS)
    w = w_ref[...] * (N_IDX_HEADS ** -0.5)
    w_cols = [w[:, MISC_IDXW + h:MISC_IDXW + h + 1] for h in range(N_IDX_HEADS)]

    def score_body(c, carry):
        k0 = pl.multiple_of(c * tq, tq)
        d = jnp.maximum(_dot_nt(qis, ki_ref[pl.ds(k0, tq), :]), 0.0)
        score = w_cols[0] * _rows(d, 0, tq)
        for h in range(1, N_IDX_HEADS):
            score = score + w_cols[h] * _rows(d, h, tq)
        sc = jnp.where((k0 + _iota((tq, tq), 1)) <= qpos, score, -jnp.inf)
        bits = pltpu.bitcast(sc, jnp.int32)
        key_ref[:, pl.ds(k0, tq)] = jnp.where(sc == 0.0, 0,
                                              jnp.where(bits < 0, bits ^ jnp.int32(0x7FFFFFFF), bits))
        return carry

    lax.fori_loop(0, n_ch, score_body, 0)

    def count(pred):
        def body(c, acc):
            k0 = pl.multiple_of(c * tq, tq)
            hit = jnp.where(pred(key_ref[:, pl.ds(k0, tq)]), 1.0, 0.0)
            return acc + sum(hit[:, j * LANES:(j + 1) * LANES] for j in range(tq // LANES))
        acc = lax.fori_loop(0, n_ch, body, jnp.zeros((tq, LANES), F32))
        return jnp.sum(acc, axis=-1, keepdims=True)

    def thr_body(it, thr):
        cand = thr + lax.shift_left(jnp.int32(1), 31 - it)
        return jnp.where(count(lambda kk: kk >= cand) >= n_keep, cand, thr)

    thr = lax.fori_loop(0, 32, thr_body, jnp.full((tq, 1), INT_MIN, jnp.int32))

    need = n_keep - count(lambda kk: kk > thr)
    upper = jnp.where(_iota((LANES, LANES), 0) <= _iota((LANES, LANES), 1), 1.0, 0.0).astype(BF)

    def tie_body(c, seen):
        for j in range(tq // LANES):
            k0 = pl.multiple_of(c * tq + j * LANES, LANES)
            kk = key_ref[:, pl.ds(k0, LANES)]
            eq = jnp.where(kk == thr, 1.0, 0.0)
            rank = jnp.dot(eq.astype(BF), upper, preferred_element_type=F32) + seen
            keep = (kk > thr) | ((kk == thr) & (rank <= need))
            causal = (k0 + _iota((tq, LANES), 1)) <= qpos
            bias_ref[:, pl.ds(k0, LANES)] = jnp.where(keep & causal, 0.0, NEG)
            seen = seen + jnp.sum(eq, axis=-1, keepdims=True)
        return seen

    lax.fori_loop(0, n_ch, tie_body, jnp.zeros((tq, 1), F32))

    qs = _stack_groups(q_ref[...], 6, N_HEADS)

    def body(c, carry):
        k0 = pl.multiple_of(c * tq, tq)
        s = _dot_nt(qs, k_ref[pl.ds(k0, tq), :])
        bias = bias_ref[:, pl.ds(k0, tq)]
        s = jnp.concatenate([_rows(s, h, tq) + bias for h in HEADS], axis=0)
        return _flash_update(s, carry, _stack_values(v_ref[pl.ds(k0, tq), :]), tq, (HEADS,))

    _, l, (acc,) = lax.fori_loop(0, n_ch, body, _flash_init(N_HEADS, tq, 1))
    o_ref[...] = _normalised(acc, l, HEADS, tq).astype(BF)


def _dsa(qkv, misc, T, tq):
    N = qkv.shape[0]
    nq = T // tq
    out_spec, out_shape = _attn_out(N, T, tq)
    return pl.pallas_call(
        functools.partial(_dsa_kernel, tq=tq, n_keep=min(DSA_TOPK_MAX, T // 4)),
        grid=(N // T, nq),
        in_specs=_attn_specs(T, tq, SEG_ID['dsa_q'], SEG_ID['dsa_k'], SEG_ID['dsa_v'])
        + [pl.BlockSpec((tq, SEG), lambda b, i: (b * nq + i, SEG_ID['idx_q'])),
           pl.BlockSpec((T, SEG), lambda b, i: (b, SEG_ID['idx_k4'])),
           pl.BlockSpec((tq, LANES), lambda b, i: (b * nq + i, 0))],
        out_specs=out_spec, out_shape=out_shape,
        scratch_shapes=[pltpu.VMEM((tq, T), jnp.int32), pltpu.VMEM((tq, T), F32)],
        compiler_params=_params("parallel", "arbitrary"),
        name="dsa_attn",
    )(qkv, qkv, qkv, qkv, qkv, misc)


def _mix_out_kernel(x_ref, oa_ref, ob_ref, oc_ref, od_ref, wg_ref, wb_ref, wo_ref, g_ref, b_ref,
                    y_ref, yb_ref):
    x = x_ref[...]
    xb = x.astype(BF)
    merged = jnp.zeros(x.shape, F32)
    for n, o_ref in enumerate((oa_ref, ob_ref, oc_ref, od_ref)):
        gate = _sigmoid(jnp.dot(xb, wg_ref[:, n * D_MODEL:(n + 1) * D_MODEL], preferred_element_type=F32))
        merged = merged + gate * jnp.dot(o_ref[...], wb_ref[n], preferred_element_type=F32)
    h = jnp.dot(merged.astype(BF), wo_ref[...], preferred_element_type=F32)
    y = _layer_norm(DEEPNORM_ALPHA * x + h, g_ref[...], b_ref[...])
    y_ref[...] = y
    yb_ref[...] = y.astype(BF)


def _mix_out(x2, branches, w_gates, w_branch, w_out, ln_g, ln_b, tm):
    N = x2.shape[0]
    row = lambda i: (i, 0)
    fixed2 = lambda i: (0, 0)
    return pl.pallas_call(
        _mix_out_kernel,
        grid=(N // tm,),
        in_specs=[pl.BlockSpec((tm, D_MODEL), row)] + [pl.BlockSpec((tm, SEG), row)] * N_BRANCHES
        + [pl.BlockSpec(w_gates.shape, fixed2),
           pl.BlockSpec(w_branch.shape, lambda i: (0, 0, 0)),
           pl.BlockSpec(w_out.shape, fixed2),
           pl.BlockSpec((1, D_MODEL), fixed2), pl.BlockSpec((1, D_MODEL), fixed2)],
        out_specs=[pl.BlockSpec((tm, D_MODEL), row), pl.BlockSpec((tm, D_MODEL), row)],
        out_shape=[jax.ShapeDtypeStruct((N, D_MODEL), F32), jax.ShapeDtypeStruct((N, D_MODEL), BF)],
        compiler_params=_params("parallel"),
        name="mix_out",
    )(x2, *branches, w_gates, w_branch.astype(BF), w_out.astype(BF),
      ln_g.astype(F32)[None, :], ln_b.astype(F32)[None, :])


def _split_bf16(a):
    hi = a.astype(BF)
    return hi, (a - hi.astype(F32)).astype(BF)


def _router_kernel(x_ref, whi_ref, wlo_ref, rb_ref, idx_ref, wsel_ref):
    x = x_ref[...]
    xhi, xlo = _split_bf16(x)
    whi = whi_ref[...]
    logits = (jnp.dot(xhi, whi, preferred_element_type=F32)
              + jnp.dot(xlo, whi, preferred_element_type=F32)
              + jnp.dot(xhi, wlo_ref[...], preferred_element_type=F32))
    scores = _sigmoid(logits)
    biased = scores + rb_ref[...]
    tm = x.shape[0]
    lane = _iota((tm, N_EXPERTS), 1)
    per_group = N_EXPERTS // N_GROUPS
    gid = lane // per_group
    gs = []
    for g in range(N_GROUPS):
        mg = jnp.where(gid == g, biased, -jnp.inf)
        m1 = jnp.max(mg, axis=-1, keepdims=True)
        i1 = jnp.min(jnp.where(mg == m1, lane, N_EXPERTS), axis=-1, keepdims=True)
        m2 = jnp.max(jnp.where(lane == i1, -jnp.inf, mg), axis=-1, keepdims=True)
        gs.append(m1 + m2)
    masked = jnp.full((tm, N_EXPERTS), -jnp.inf, F32)
    for g in range(N_GROUPS):
        rank = jnp.zeros((tm, 1), F32)
        for o in range(N_GROUPS):
            if o != g:
                beats = (gs[o] > gs[g]) | (gs[o] == gs[g]) if o < g else (gs[o] > gs[g])
                rank = rank + jnp.where(beats, 1.0, 0.0)
        masked = jnp.where((gid == g) & (rank < TOPK_GROUPS), biased, masked)
    lane_o = _iota((tm, LANES), 1)
    idx_out = jnp.zeros((tm, LANES), jnp.int32)
    w_out = jnp.zeros((tm, LANES), F32)
    wsum = jnp.zeros((tm, 1), F32)
    for k in range(TOP_K):
        mx = jnp.max(masked, axis=-1, keepdims=True)
        pick = jnp.min(jnp.where(masked == mx, lane, N_EXPERTS), axis=-1, keepdims=True)
        hit = lane == pick
        sc = jnp.sum(jnp.where(hit, scores, 0.0), axis=-1, keepdims=True)
        masked = jnp.where(hit, -jnp.inf, masked)
        idx_out = jnp.where(lane_o == k, pick, idx_out)
        w_out = jnp.where(lane_o == k, sc, w_out)
        wsum = wsum + sc
    idx_ref[...] = idx_out
    wsel_ref[...] = w_out / wsum * ROUTED_SCALE


def _router(x1, w_router, router_bias, tm):
    N = x1.shape[0]
    whi, wlo = _split_bf16(w_router.astype(F32))
    row = lambda i: (i, 0)
    fixed = lambda i: (0, 0)
    idx, wsel = pl.pallas_call(
        _router_kernel,
        grid=(N // tm,),
        in_specs=[pl.BlockSpec((tm, D_MODEL), row), pl.BlockSpec(whi.shape, fixed),
                  pl.BlockSpec(wlo.shape, fixed), pl.BlockSpec((1, N_EXPERTS), fixed)],
        out_specs=[pl.BlockSpec((tm, LANES), row), pl.BlockSpec((tm, LANES), row)],
        out_shape=[jax.ShapeDtypeStruct((N, LANES), jnp.int32), jax.ShapeDtypeStruct((N, LANES), F32)],
        compiler_params=_params("parallel"),
        name="router",
    )(x1, whi, wlo, router_bias.astype(F32)[None, :])
    return idx[:, :TOP_K], wsel[:, :TOP_K]


def _expert_kernel(blk_ref, exp_ref, lo_ref, hi_ref, first_ref, newe_ref,
                   xs_ref, rw_ref, wg_ref, wu_ref, wd_ref, y_ref, wgb_ref, wub_ref, wdb_ref, *, bm):
    t = pl.program_id(0)
    lo, hi = lo_ref[t], hi_ref[t]

    @pl.when(newe_ref[t] == 1)
    def _():
        wgb_ref[...] = wg_ref[...].astype(BF)
        wub_ref[...] = wu_ref[...].astype(BF)
        wdb_ref[...] = wd_ref[...].astype(BF)

    @pl.when(first_ref[t] == 1)
    def _():
        y_ref[...] = jnp.zeros_like(y_ref)

    @pl.when(hi > lo)
    def _():
        xs = xs_ref[...]
        g = jnp.dot(xs, wgb_ref[...], preferred_element_type=F32)
        u = jnp.dot(xs, wub_ref[...], preferred_element_type=F32)
        h = (g * _sigmoid(g) * u).astype(BF)
        y = jnp.dot(h, wdb_ref[...], preferred_element_type=F32) * rw_ref[...]
        row = blk_ref[t] * bm + _iota((bm, 1), 0)
        mine = (row >= lo) & (row < hi)
        y_ref[...] = jnp.where(mine, y.astype(y_ref.dtype), y_ref[...])


def _experts(xs, row_w, plan, layer, w_gate, w_up, w_down, bm):
    A = xs.shape[0]
    n_items = plan[0].shape[0]
    row = lambda t, blk, *_: (blk[t], 0)
    wspec = lambda shape: pl.BlockSpec((None, None) + shape, lambda t, blk, exp, *_: (layer, exp[t], 0, 0))
    return pl.pallas_call(
        functools.partial(_expert_kernel, bm=bm),
        grid_spec=pltpu.PrefetchScalarGridSpec(
            num_scalar_prefetch=len(plan),
            grid=(n_items,),
            in_specs=[pl.BlockSpec((bm, D_MODEL), row), pl.BlockSpec((bm, 1), row),
                      wspec((D_MODEL, EXPERT_DIM)), wspec((D_MODEL, EXPERT_DIM)),
                      wspec((EXPERT_DIM, D_MODEL))],
            out_specs=pl.BlockSpec((bm, D_MODEL), row),
            scratch_shapes=[pltpu.VMEM((D_MODEL, EXPERT_DIM), BF), pltpu.VMEM((D_MODEL, EXPERT_DIM), BF),
                            pltpu.VMEM((EXPERT_DIM, D_MODEL), BF)],
        ),
        out_shape=jax.ShapeDtypeStruct((A, D_MODEL), BF),
        compiler_params=_params("arbitrary"),
        name="experts",
    )(*plan, xs, row_w, w_gate, w_up, w_down)


def _dispatch_plan(eidx, wsel, bm):
    N = eidx.shape[0]
    A, E = N * TOP_K, N_EXPERTS
    assert A % bm == 0
    ids = jnp.arange(A, dtype=jnp.int32)
    se, sid, sw = lax.sort((eidx.reshape(A), ids, wsel.reshape(A)), num_keys=1, is_stable=True)
    _, pos = lax.sort((sid, ids), num_keys=1)
    n_blk = A // bm
    start = jnp.searchsorted(se, jnp.arange(E, dtype=jnp.int32), side='left').astype(jnp.int32)
    lo = jnp.sort(jnp.concatenate([jnp.arange(n_blk, dtype=jnp.int32) * bm, start]))
    hi = jnp.concatenate([lo[1:], jnp.full((1,), A, jnp.int32)])
    blk = jnp.minimum(lo // bm, n_blk - 1)
    exp = se[jnp.minimum(lo, A - 1)]
    changed = lambda a: jnp.concatenate([jnp.ones((1,), jnp.int32), (a[1:] != a[:-1]).astype(jnp.int32)])
    plan = (blk, exp, lo, hi, changed(blk), changed(exp))
    return sid // TOP_K, sw, pos.reshape(N, TOP_K), plan


def _moe_out_kernel(x_ref, r_ref, wg_ref, wu_ref, wd_ref, g_ref, b_ref, y_ref):
    x = x_ref[...]
    xb = x.astype(BF)
    g = jnp.dot(xb, wg_ref[...], preferred_element_type=F32)
    u = jnp.dot(xb, wu_ref[...], preferred_element_type=F32)
    h = (g * _sigmoid(g) * u).astype(BF)
    shared = jnp.dot(h, wd_ref[...], preferred_element_type=F32)
    y_ref[...] = _layer_norm(DEEPNORM_ALPHA * x + (r_ref[...] + shared), g_ref[...], b_ref[...])


def _moe_out(x1, routed, w_sh_gate, w_sh_up, w_sh_down, ln_g, ln_b, tm):
    N = x1.shape[0]
    row = lambda i: (i, 0)
    fixed = lambda i: (0, 0)
    return pl.pallas_call(
        _moe_out_kernel,
        grid=(N // tm,),
        in_specs=[pl.BlockSpec((tm, D_MODEL), row), pl.BlockSpec((tm, D_MODEL), row),
                  pl.BlockSpec(w_sh_gate.shape, fixed), pl.BlockSpec(w_sh_up.shape, fixed),
                  pl.BlockSpec(w_sh_down.shape, fixed),
                  pl.BlockSpec((1, D_MODEL), fixed), pl.BlockSpec((1, D_MODEL), fixed)],
        out_specs=pl.BlockSpec((tm, D_MODEL), row),
        out_shape=jax.ShapeDtypeStruct((N, D_MODEL), F32),
        compiler_params=_params("parallel"),
        name="moe_out",
    )(x1, routed, w_sh_gate.astype(BF), w_sh_up.astype(BF), w_sh_down.astype(BF),
      ln_g.astype(F32)[None, :], ln_b.astype(F32)[None, :])


def _tiles(N, T):
    return min(512, T), min(256, T), MOBA_BLOCK, min(512, N * TOP_K)


def _mixer_layer(x2, T, w_in, b_forget, diff_lambda, diff_subln, w_branch, w_out, ln_g, ln_b, lambda_init):
    N = x2.shape[0]
    tm_proj, tm_row, tq, _ = _tiles(N, T)
    w, w_gates, bf = _in_proj_weights(w_in, b_forget)
    qkv, misc = _in_proj(x2, w, bf, _rope_tables(T), T, tm_proj)
    dl = diff_lambda.astype(F32)
    lam = jnp.exp(jnp.sum(dl[0] * dl[1])) - jnp.exp(jnp.sum(dl[2] * dl[3])) + lambda_init
    o_a = _moba(qkv, T, tq)
    o_b = _diff(qkv, lam, diff_subln, lambda_init, T, tq)
    o_c = _fox(qkv, misc, T, tq)
    o_d = _dsa(qkv, misc, T, tq)
    return _mix_out(x2, (o_a, o_b, o_c, o_d), w_gates, w_branch, w_out, ln_g, ln_b, tm_row)


def _moe_layer(x1, x1b, T, layer, w_router, router_bias, w_exp_gate, w_exp_up, w_exp_down,
               w_sh_gate, w_sh_up, w_sh_down, ln_g, ln_b):
    N = x1.shape[0]
    _, tm_row, _, bm = _tiles(N, T)
    eidx, wsel = _router(x1, w_router, router_bias, tm_row)
    row_tok, row_w, pos, plan = _dispatch_plan(eidx, wsel, bm)
    ys = _experts(x1b[row_tok], row_w[:, None], plan, layer, w_exp_gate, w_exp_up, w_exp_down, bm)
    routed = jnp.sum(ys[pos].astype(F32), axis=1)
    return _moe_out(x1, routed, w_sh_gate, w_sh_up, w_sh_down, ln_g, ln_b, tm_row)


def kernel(x, w_in, b_forget, diff_lambda, diff_subln, w_branch, w_out, ln1_g, ln1_b, w_router, router_bias,
           w_exp_gate, w_exp_up, w_exp_down, w_sh_gate, w_sh_up, w_sh_down, ln2_g, ln2_b):
    B, T, D = x.shape
    x2 = x.reshape(B * T, D)
    for l in range(DEPTH):
        lambda_init = 0.8 - 0.6 * math.exp(-0.3 * l)
        x1, x1b = _mixer_layer(x2, T, w_in[l], b_forget[l], diff_lambda[l], diff_subln[l], w_branch[l],
                               w_out[l], ln1_g[l], ln1_b[l], lambda_init)
        x2 = _moe_layer(x1, x1b, T, l, w_router[l], router_bias[l], w_exp_gate, w_exp_up, w_exp_down,
                        w_sh_gate[l], w_sh_up[l], w_sh_down[l], ln2_g[l], ln2_b[l])
    return x2.reshape(B, T, D)
```

```python
import functools
import math

import jax
import jax.numpy as jnp
from jax import lax
from jax.experimental import pallas as pl
from jax.experimental.pallas import tpu as pltpu

F32 = jnp.float32
BF = jnp.bfloat16

D_MODEL = 1024
DEPTH = 2
HEAD_DIM = 64
N_HEADS = 4
DIFF_DIM = HEAD_DIM // 2
N_IDX_HEADS = 4
IDX_DIM = 64
BRANCH_WIDTH = N_HEADS * HEAD_DIM
N_BRANCHES = 4
MOBA_BLOCK = 256
MOBA_TOPK = 3
DSA_TOPK_MAX = 256
ROPE_THETA = 10000.0
N_EXPERTS = 256
TOP_K = 8
N_GROUPS = 8
TOPK_GROUPS = 4
EXPERT_DIM = 256
ROUTED_SCALE = 2.5
LN_EPS = 1e-5
DEEPNORM_ALPHA = (2 * DEPTH) ** 0.25

IN_SEGMENTS = (
    ('moba_q', BRANCH_WIDTH), ('moba_k', BRANCH_WIDTH), ('moba_v', BRANCH_WIDTH),
    ('diff_q', BRANCH_WIDTH), ('diff_k', BRANCH_WIDTH), ('diff_v', BRANCH_WIDTH),
    ('fox_q', BRANCH_WIDTH), ('fox_k', BRANCH_WIDTH), ('fox_v', BRANCH_WIDTH), ('fox_f', N_HEADS),
    ('dsa_q', BRANCH_WIDTH), ('dsa_k', BRANCH_WIDTH), ('dsa_v', BRANCH_WIDTH),
    ('idx_q', N_IDX_HEADS * IDX_DIM), ('idx_k', IDX_DIM), ('idx_w', N_IDX_HEADS),
    ('gates', N_BRANCHES * D_MODEL),
)

LANES = 128
SUBLANES = 8
SEG = BRANCH_WIDTH
NEG = -1e30
INT_MIN = -2 ** 31
VMEM_LIMIT = 48 * 1024 * 1024

_QSCALE = HEAD_DIM ** -0.5
PROJ_SEGS = (
    ('moba_q', 64, _QSCALE), ('moba_k', 64, 1.0),
    ('diff_q', 32, DIFF_DIM ** -0.5), ('diff_k', 32, 1.0),
    ('fox_q', 0, _QSCALE), ('fox_k', 0, 1.0),
    ('dsa_q', 64, _QSCALE), ('dsa_k', 64, 1.0),
    ('idx_q', 64, IDX_DIM ** -0.5), ('idx_k4', 64, 1.0),
)
SEG_ID = {name: i for i, (name, _, _) in enumerate(PROJ_SEGS)}
N_SEG = len(PROJ_SEGS)
V_SEGS = ('moba_v', 'diff_v', 'fox_v', 'dsa_v')
V_ID = {name: i for i, name in enumerate(V_SEGS)}
MISC_LOGF = 0
MISC_IDXW = 4


def _params(*sem):
    return pltpu.CompilerParams(dimension_semantics=sem, vmem_limit_bytes=VMEM_LIMIT)


def _iota(shape, dim):
    return lax.broadcasted_iota(jnp.int32, shape, dim)


def _dot_nt(a, b):
    return lax.dot_general(a, b, (((1,), (1,)), ((), ())), preferred_element_type=F32)


def _sigmoid(z):
    return 1.0 / (1.0 + jnp.exp(-z))


def _layer_norm(y, g, b):
    mu = jnp.mean(y, axis=-1, keepdims=True)
    yc = y - mu
    var = jnp.mean(yc * yc, axis=-1, keepdims=True)
    return yc * lax.rsqrt(var + LN_EPS) * g + b


def _swap_halves(a, half):
    w = a.shape[-1]
    first = (_iota(a.shape, 1) & (2 * half - 1)) < half
    return jnp.where(first, pltpu.roll(a, w - half, 1), pltpu.roll(a, half, 1))


def _in_proj_kernel(x_ref, w_ref, wvt_ref, c64_ref, s64_ref, c32_ref, s32_ref, bf_ref,
                    qk_ref, vt_ref, misc_ref):
    xb = x_ref[...].astype(BF)
    for s, (_, rot, scale) in enumerate(PROJ_SEGS):
        acc = jnp.dot(xb, w_ref[:, s * SEG:(s + 1) * SEG], preferred_element_type=F32)
        if rot == 64:
            acc = acc * c64_ref[...] + _swap_halves(acc, 32) * s64_ref[...]
        elif rot == 32:
            acc = acc * c32_ref[...] + _swap_halves(acc, 16) * s32_ref[...]
        if scale != 1.0:
            acc = acc * scale
        qk_ref[:, s * SEG:(s + 1) * SEG] = acc.astype(BF)
    vt_ref[...] = _dot_nt(wvt_ref[...], xb).astype(BF)
    m = jnp.dot(xb, w_ref[:, N_SEG * SEG:N_SEG * SEG + LANES], preferred_element_type=F32)
    z = m + bf_ref[...]
    logf = jnp.minimum(z, 0.0) - jnp.log1p(jnp.exp(-jnp.abs(z)))
    lane = _iota(m.shape, 1)
    misc_ref[...] = jnp.where(lane < MISC_IDXW, logf, m)


def _rope_tables(T):
    pos = jnp.arange(T).astype(F32)

    def tab(group, reps):
        half = group // 2
        inv_freq = ROPE_THETA ** (-jnp.arange(half, dtype=F32) / half)
        ang = pos[:, None] * inv_freq[None, :]
        cos, sin = jnp.cos(ang), jnp.sin(ang)
        return (jnp.tile(jnp.concatenate([cos, cos], -1), (1, reps)),
                jnp.tile(jnp.concatenate([-sin, sin], -1), (1, reps)))

    return tab(64, SEG // 64) + tab(32, SEG // 32)


def _in_proj_weights(w_in, b_forget):
    parts, off = {}, 0
    for name, width in IN_SEGMENTS:
        parts[name] = w_in[:, off:off + width]
        off += width
    parts['idx_k4'] = jnp.tile(parts['idx_k'], (1, N_IDX_HEADS))
    misc = jnp.concatenate([parts['fox_f'], parts['idx_w'],
                            jnp.zeros((D_MODEL, LANES - 2 * N_HEADS), w_in.dtype)], axis=1)
    w = jnp.concatenate([parts[name] for name, _, _ in PROJ_SEGS] + [misc], axis=1).astype(BF)
    wvt = jnp.concatenate([parts[name] for name in V_SEGS], axis=1).T.astype(BF)
    bf = jnp.zeros((1, LANES), F32).at[0, MISC_LOGF:MISC_LOGF + N_HEADS].set(b_forget.astype(F32))
    return w, wvt, parts['gates'].astype(BF), bf


def _in_proj(x2, w, wvt, bf, tables, T, tm):
    N = x2.shape[0]
    nt = T // tm
    tab_spec = pl.BlockSpec((tm, SEG), lambda i: (i % nt, 0))
    fixed = lambda i: (0, 0)
    return pl.pallas_call(
        _in_proj_kernel,
        grid=(N // tm,),
        in_specs=[pl.BlockSpec((tm, D_MODEL), lambda i: (i, 0)),
                  pl.BlockSpec(w.shape, fixed), pl.BlockSpec(wvt.shape, fixed),
                  tab_spec, tab_spec, tab_spec, tab_spec,
                  pl.BlockSpec((1, LANES), fixed)],
        out_specs=[pl.BlockSpec((tm, N_SEG * SEG), lambda i: (i, 0)),
                   pl.BlockSpec((len(V_SEGS) * SEG, tm), lambda i: (0, i)),
                   pl.BlockSpec((tm, LANES), lambda i: (i, 0))],
        out_shape=[jax.ShapeDtypeStruct((N, N_SEG * SEG), BF),
                   jax.ShapeDtypeStruct((len(V_SEGS) * SEG, N), BF),
                   jax.ShapeDtypeStruct((N, LANES), F32)],
        compiler_params=_params("parallel"),
        name="in_proj",
    )(x2, w, wvt, *tables, bf)


def _group_mask(shape, shift, g):
    return (_iota(shape, 1) >> shift) == g


def _stack_groups(q, shift, n_groups):
    zero = jnp.zeros_like(q)
    return jnp.concatenate([jnp.where(_group_mask(q.shape, shift, g), q, zero) for g in range(n_groups)],
                           axis=0)


def _cols(a, g, tq):
    return a[:, g * tq:(g + 1) * tq]


def _head_rows(a, h):
    return a[h * HEAD_DIM:(h + 1) * HEAD_DIM]


def _flash_init(n_groups, tq, n_sets):
    return (jnp.full((1, n_groups * tq), NEG, F32), jnp.zeros((1, n_groups * tq), F32),
            tuple(jnp.zeros((SEG, tq), F32) for _ in range(n_sets)))


def _flash_update(s, carry, vt, tq, head_sets):
    m, l, accs = carry
    m_new = jnp.maximum(m, jnp.max(s, axis=0, keepdims=True))
    alpha = jnp.exp(m - m_new)
    p = jnp.exp(s - m_new)
    l = alpha * l + jnp.sum(p, axis=0, keepdims=True)
    pb = p.astype(BF)
    new_accs = []
    for acc, groups in zip(accs, head_sets):
        new_accs.append(jnp.concatenate(
            [_cols(alpha, g, tq) * _head_rows(acc, h)
             + jnp.dot(_head_rows(vt, h), _cols(pb, g, tq), preferred_element_type=F32)
             for h, g in enumerate(groups)], axis=0))
    return m_new, l, tuple(new_accs)


def _causal(n_groups, tq):
    shape = (tq, n_groups * tq)
    return _iota(shape, 0) <= (_iota(shape, 1) & (tq - 1))


def _normalised(acc, l, groups, tq):
    return jnp.concatenate([_head_rows(acc, h) * (1.0 / _cols(l, g, tq)) for h, g in enumerate(groups)],
                           axis=0)


def _tile_lanes(a, n):
    return jnp.concatenate([a] * n, axis=1) if n > 1 else a


HEADS = tuple(range(N_HEADS))


def _attn_specs(T, tq, q_seg, k_seg, v_name):
    nq = T // tq
    v_id = V_ID[v_name]
    return [pl.BlockSpec((tq, SEG), lambda b, i: (b * nq + i, q_seg)),
            pl.BlockSpec((T, SEG), lambda b, i: (b, k_seg)),
            pl.BlockSpec((SEG, T), lambda b, i: (v_id, b))]


def _attn_out(N, T, tq):
    nq = T // tq
    return (pl.BlockSpec((tq, SEG), lambda b, i: (b * nq + i, 0)),
            jax.ShapeDtypeStruct((N, SEG), BF))


def _moba_kernel(q_ref, k_ref, vt_ref, o_ref, km_ref, sel_ref, *, tq, nb, n_sel):
    i = pl.program_id(1)

    @pl.when(i == 0)
    def _():
        km_ref[...] = jnp.zeros_like(km_ref)
        for n in range(nb):
            blk = k_ref[n * tq:(n + 1) * tq, :].astype(F32)
            km_ref[n:n + 1, :] = jnp.mean(blk, axis=0, keepdims=True)

    qs = _stack_groups(q_ref[...], 6, N_HEADS)
    km = km_ref[...]
    km_hi = km.astype(BF)
    km_lo = (km - km_hi.astype(F32)).astype(BF)
    nrow = sel_ref.shape[0]
    g = (_dot_nt(km_hi, qs) + _dot_nt(km_lo, qs))[0:nrow]
    blk_id = _iota(g.shape, 0)
    g = jnp.where(blk_id < i, g, -jnp.inf)
    sel = jnp.zeros(g.shape, F32)
    for _ in range(n_sel):
        gmax = jnp.max(g, axis=0, keepdims=True)
        first = jnp.min(jnp.where(g == gmax, blk_id, nrow), axis=0, keepdims=True)
        hit = blk_id == first
        sel = jnp.where(hit & (gmax > -jnp.inf), 1.0, sel)
        g = jnp.where(hit, -jnp.inf, g)
    sel_ref[...] = sel

    def body(c, carry):
        k0 = pl.multiple_of(c * tq, tq)
        picked = sel_ref[pl.ds(c, 1), :] > 0.5
        s = jnp.where(picked, _dot_nt(k_ref[pl.ds(k0, tq), :], qs), NEG)
        return _flash_update(s, carry, vt_ref[:, pl.ds(k0, tq)], tq, (HEADS,))

    carry = lax.fori_loop(0, i, body, _flash_init(N_HEADS, tq, 1))
    k0 = pl.multiple_of(i * tq, tq)
    s = jnp.where(_causal(N_HEADS, tq), _dot_nt(k_ref[pl.ds(k0, tq), :], qs), NEG)
    _, l, (acc,) = _flash_update(s, carry, vt_ref[:, pl.ds(k0, tq)], tq, (HEADS,))
    o_ref[...] = _normalised(acc, l, HEADS, tq).T.astype(BF)


def _moba(qk, vt, T, tq):
    N = qk.shape[0]
    nb = T // MOBA_BLOCK
    assert T % MOBA_BLOCK == 0 and tq == MOBA_BLOCK and nb <= LANES
    sel_rows = -(-nb // SUBLANES) * SUBLANES
    out_spec, out_shape = _attn_out(N, T, tq)
    return pl.pallas_call(
        functools.partial(_moba_kernel, tq=tq, nb=nb, n_sel=min(MOBA_TOPK, nb - 1)),
        grid=(N // T, T // tq),
        in_specs=_attn_specs(T, tq, SEG_ID['moba_q'], SEG_ID['moba_k'], 'moba_v'),
        out_specs=out_spec, out_shape=out_shape,
        scratch_shapes=[pltpu.VMEM((LANES, SEG), F32), pltpu.VMEM((sel_rows, N_HEADS * tq), F32)],
        compiler_params=_params("parallel", "arbitrary"),
        name="moba_attn",
    )(qk, qk, vt)


DIFF_SETS = (tuple(2 * h for h in HEADS), tuple(2 * h + 1 for h in HEADS))


def _diff_kernel(lam_ref, q_ref, k_ref, vt_ref, g_ref, o_ref, *, tq, out_scale):
    i = pl.program_id(1)
    n_groups = 2 * N_HEADS
    qs = _stack_groups(q_ref[...], 5, n_groups)

    def body(c, carry):
        k0 = pl.multiple_of(c * tq, tq)
        s = _dot_nt(k_ref[pl.ds(k0, tq), :], qs)
        return _flash_update(s, carry, vt_ref[:, pl.ds(k0, tq)], tq, DIFF_SETS)

    carry = lax.fori_loop(0, i, body, _flash_init(n_groups, tq, 2))
    k0 = pl.multiple_of(i * tq, tq)
    s = jnp.where(_causal(n_groups, tq), _dot_nt(k_ref[pl.ds(k0, tq), :], qs), NEG)
    _, l, (a1, a2) = _flash_update(s, carry, vt_ref[:, pl.ds(k0, tq)], tq, DIFF_SETS)
    out = _normalised(a1, l, DIFF_SETS[0], tq) - lam_ref[0] * _normalised(a2, l, DIFF_SETS[1], tq)
    normed = []
    for h in HEADS:
        oh = _head_rows(out, h)
        ms = jnp.mean(oh * oh, axis=0, keepdims=True)
        normed.append(oh * lax.rsqrt(ms + LN_EPS))
    o_ref[...] = (jnp.concatenate(normed, axis=0).T * g_ref[...] * out_scale).astype(BF)


def _diff(qk, vt, lam, subln_g, lambda_init, T, tq):
    N = qk.shape[0]
    out_spec, out_shape = _attn_out(N, T, tq)
    g = jnp.tile(subln_g.astype(F32), N_HEADS)[None, :]
    return pl.pallas_call(
        functools.partial(_diff_kernel, tq=tq, out_scale=1.0 - lambda_init),
        grid=(N // T, T // tq),
        in_specs=[pl.BlockSpec(memory_space=pltpu.SMEM)]
        + _attn_specs(T, tq, SEG_ID['diff_q'], SEG_ID['diff_k'], 'diff_v')
        + [pl.BlockSpec((1, SEG), lambda b, i: (0, 0))],
        out_specs=out_spec, out_shape=out_shape,
        compiler_params=_params("parallel", "arbitrary"),
        name="diff_attn",
    )(lam.reshape(1).astype(F32), qk, qk, vt, g)


def _cumsum_lanes(x):
    lane = _iota(x.shape, 1)
    sh = 1
    while sh < x.shape[-1]:
        x = x + jnp.where(lane >= sh, pltpu.roll(x, sh, 1), 0.0)
        sh *= 2
    return x


def _fox_kernel(q_ref, k_ref, vt_ref, misc_ref, o_ref, c_ref, *, tq):
    i = pl.program_id(1)

    @pl.when(i == 0)
    def _():
        c = _cumsum_lanes(misc_ref[...].T).T
        for h in HEADS:
            c_ref[h] = jnp.broadcast_to(c[:, MISC_LOGF + h:MISC_LOGF + h + 1], c.shape)

    qs = _stack_groups(q_ref[...], 6, N_HEADS)

    def logits(k0):
        s = _dot_nt(k_ref[pl.ds(k0, tq), :], qs)
        return jnp.concatenate(
            [_cols(s, h, tq) - _tile_lanes(c_ref[h, pl.ds(k0, tq), :], tq // LANES) for h in HEADS], axis=1)

    def body(c, carry):
        k0 = pl.multiple_of(c * tq, tq)
        return _flash_update(logits(k0), carry, vt_ref[:, pl.ds(k0, tq)], tq, (HEADS,))

    carry = lax.fori_loop(0, i, body, _flash_init(N_HEADS, tq, 1))
    k0 = pl.multiple_of(i * tq, tq)
    s = jnp.where(_causal(N_HEADS, tq), logits(k0), NEG)
    _, l, (acc,) = _flash_update(s, carry, vt_ref[:, pl.ds(k0, tq)], tq, (HEADS,))
    o_ref[...] = _normalised(acc, l, HEADS, tq).T.astype(BF)


def _fox(qk, vt, misc, T, tq):
    N = qk.shape[0]
    out_spec, out_shape = _attn_out(N, T, tq)
    return pl.pallas_call(
        functools.partial(_fox_kernel, tq=tq),
        grid=(N // T, T // tq),
        in_specs=_attn_specs(T, tq, SEG_ID['fox_q'], SEG_ID['fox_k'], 'fox_v')
        + [pl.BlockSpec((T, LANES), lambda b, i: (b, 0))],
        out_specs=out_spec, out_shape=out_shape,
        scratch_shapes=[pltpu.VMEM((N_HEADS, T, LANES), F32)],
        compiler_params=_params("parallel", "arbitrary"),
        name="fox_attn",
    )(qk, qk, vt, misc)


def _dsa_kernel(q_ref, k_ref, vt_ref, qi_ref, ki_ref, w_ref, o_ref, key_ref, bias_ref, *, tq, n_keep):
    i = pl.program_id(1)
    n_ch = i + 1
    qpos = i * tq + _iota((1, tq), 1)

    qis = _stack_groups(qi_ref[...], 6, N_IDX_HEADS)
    wt = w_ref[...].T * (N_IDX_HEADS ** -0.5)
    w_rows = [wt[MISC_IDXW + h:MISC_IDXW + h + 1] for h in range(N_IDX_HEADS)]

    def score_body(c, carry):
        k0 = pl.multiple_of(c * tq, tq)
        d = jnp.maximum(_dot_nt(ki_ref[pl.ds(k0, tq), :], qis), 0.0)
        score = w_rows[0] * _cols(d, 0, tq)
        for h in range(1, N_IDX_HEADS):
            score = score + w_rows[h] * _cols(d, h, tq)
        sc = jnp.where((k0 + _iota((tq, tq), 0)) <= qpos, score, -jnp.inf)
        bits = pltpu.bitcast(sc, jnp.int32)
        key_ref[pl.ds(k0, tq), :] = jnp.where(sc == 0.0, 0,
                                              jnp.where(bits < 0, bits ^ jnp.int32(0x7FFFFFFF), bits))
        return carry

    lax.fori_loop(0, n_ch, score_body, 0)

    def count(pred):
        def body(c, acc):
            k0 = pl.multiple_of(c * tq, tq)
            hit = jnp.where(pred(key_ref[pl.ds(k0, tq), :]), 1.0, 0.0)
            return acc + jnp.sum(hit, axis=0, keepdims=True)
        return lax.fori_loop(0, n_ch, body, jnp.zeros((1, tq), F32))

    def thr_body(it, thr):
        cand = thr + lax.shift_left(jnp.int32(1), 31 - it)
        return jnp.where(count(lambda kk: kk >= cand) >= n_keep, cand, thr)

    thr = lax.fori_loop(0, 32, thr_body, jnp.full((1, tq), INT_MIN, jnp.int32))

    need = n_keep - count(lambda kk: kk > thr)
    lower = jnp.where(_iota((tq, tq), 1) <= _iota((tq, tq), 0), 1.0, 0.0).astype(BF)

    def tie_body(c, seen):
        k0 = pl.multiple_of(c * tq, tq)
        kk = key_ref[pl.ds(k0, tq), :]
        eq = jnp.where(kk == thr, 1.0, 0.0)
        rank = jnp.dot(lower, eq.astype(BF), preferred_element_type=F32) + seen
        keep = (kk > thr) | ((kk == thr) & (rank <= need))
        causal = (k0 + _iota((tq, tq), 0)) <= qpos
        bias_ref[pl.ds(k0, tq), :] = jnp.where(keep & causal, 0.0, NEG)
        return seen + jnp.sum(eq, axis=0, keepdims=True)

    lax.fori_loop(0, n_ch, tie_body, jnp.zeros((1, tq), F32))

    qs = _stack_groups(q_ref[...], 6, N_HEADS)

    def body(c, carry):
        k0 = pl.multiple_of(c * tq, tq)
        s = _dot_nt(k_ref[pl.ds(k0, tq), :], qs) + _tile_lanes(bias_ref[pl.ds(k0, tq), :], N_HEADS)
        return _flash_update(s, carry, vt_ref[:, pl.ds(k0, tq)], tq, (HEADS,))

    _, l, (acc,) = lax.fori_loop(0, n_ch, body, _flash_init(N_HEADS, tq, 1))
    o_ref[...] = _normalised(acc, l, HEADS, tq).T.astype(BF)


def _dsa(qk, vt, misc, T, tq):
    N = qk.shape[0]
    nq = T // tq
    out_spec, out_shape = _attn_out(N, T, tq)
    return pl.pallas_call(
        functools.partial(_dsa_kernel, tq=tq, n_keep=min(DSA_TOPK_MAX, T // 4)),
        grid=(N // T, nq),
        in_specs=_attn_specs(T, tq, SEG_ID['dsa_q'], SEG_ID['dsa_k'], 'dsa_v')
        + [pl.BlockSpec((tq, SEG), lambda b, i: (b * nq + i, SEG_ID['idx_q'])),
           pl.BlockSpec((T, SEG), lambda b, i: (b, SEG_ID['idx_k4'])),
           pl.BlockSpec((tq, LANES), lambda b, i: (b * nq + i, 0))],
        out_specs=out_spec, out_shape=out_shape,
        scratch_shapes=[pltpu.VMEM((T, tq), jnp.int32), pltpu.VMEM((T, tq), F32)],
        compiler_params=_params("parallel", "arbitrary"),
        name="dsa_attn",
    )(qk, qk, vt, qk, qk, misc)


def _mix_out_kernel(x_ref, oa_ref, ob_ref, oc_ref, od_ref, wg_ref, wb_ref, wo_ref, g_ref, b_ref,
                    y_ref, yb_ref):
    x = x_ref[...]
    xb = x.astype(BF)
    merged = jnp.zeros(x.shape, F32)
    for n, o_ref in enumerate((oa_ref, ob_ref, oc_ref, od_ref)):
        gate = _sigmoid(jnp.dot(xb, wg_ref[:, n * D_MODEL:(n + 1) * D_MODEL], preferred_element_type=F32))
        merged = merged + gate * jnp.dot(o_ref[...], wb_ref[n], preferred_element_type=F32)
    h = jnp.dot(merged.astype(BF), wo_ref[...], preferred_element_type=F32)
    y = _layer_norm(DEEPNORM_ALPHA * x + h, g_ref[...], b_ref[...])
    y_ref[...] = y
    yb_ref[...] = y.astype(BF)


def _mix_out(x2, branches, w_gates, w_branch, w_out, ln_g, ln_b, tm):
    N = x2.shape[0]
    row = lambda i: (i, 0)
    fixed2 = lambda i: (0, 0)
    return pl.pallas_call(
        _mix_out_kernel,
        grid=(N // tm,),
        in_specs=[pl.BlockSpec((tm, D_MODEL), row)] + [pl.BlockSpec((tm, SEG), row)] * N_BRANCHES
        + [pl.BlockSpec(w_gates.shape, fixed2),
           pl.BlockSpec(w_branch.shape, lambda i: (0, 0, 0)),
           pl.BlockSpec(w_out.shape, fixed2),
           pl.BlockSpec((1, D_MODEL), fixed2), pl.BlockSpec((1, D_MODEL), fixed2)],
        out_specs=[pl.BlockSpec((tm, D_MODEL), row), pl.BlockSpec((tm, D_MODEL), row)],
        out_shape=[jax.ShapeDtypeStruct((N, D_MODEL), F32), jax.ShapeDtypeStruct((N, D_MODEL), BF)],
        compiler_params=_params("parallel"),
        name="mix_out",
    )(x2, *branches, w_gates, w_branch.astype(BF), w_out.astype(BF),
      ln_g.astype(F32)[None, :], ln_b.astype(F32)[None, :])


def _split_bf16(a):
    hi = a.astype(BF)
    return hi, (a - hi.astype(F32)).astype(BF)


def _router_kernel(x_ref, whi_ref, wlo_ref, rb_ref, idx_ref, wsel_ref):
    xhi, xlo = _split_bf16(x_ref[...])
    whi = whi_ref[...]
    logits = _dot_nt(whi, xhi) + _dot_nt(whi, xlo) + _dot_nt(wlo_ref[...], xhi)
    tm = logits.shape[1]
    scores = _sigmoid(logits)
    biased = scores + _tile_lanes(rb_ref[...], tm // LANES)
    per_group = N_EXPERTS // N_GROUPS
    gs = []
    for g in range(N_GROUPS):
        bg = biased[g * per_group:(g + 1) * per_group]
        row = _iota(bg.shape, 0)
        m1 = jnp.max(bg, axis=0, keepdims=True)
        i1 = jnp.min(jnp.where(bg == m1, row, per_group), axis=0, keepdims=True)
        m2 = jnp.max(jnp.where(row == i1, -jnp.inf, bg), axis=0, keepdims=True)
        gs.append(m1 + m2)
    kept = []
    for g in range(N_GROUPS):
        rank = jnp.zeros((1, tm), F32)
        for o in range(N_GROUPS):
            if o != g:
                beats = (gs[o] >= gs[g]) if o < g else (gs[o] > gs[g])
                rank = rank + jnp.where(beats, 1.0, 0.0)
        kept.append(jnp.where(rank < TOPK_GROUPS, biased[g * per_group:(g + 1) * per_group], -jnp.inf))
    masked = jnp.concatenate(kept, axis=0)
    eid = _iota(masked.shape, 0)
    picks, weights = [], []
    for _ in range(TOP_K):
        mx = jnp.max(masked, axis=0, keepdims=True)
        pick = jnp.min(jnp.where(masked == mx, eid, N_EXPERTS), axis=0, keepdims=True)
        hit = eid == pick
        weights.append(jnp.sum(jnp.where(hit, scores, 0.0), axis=0, keepdims=True))
        masked = jnp.where(hit, -jnp.inf, masked)
        picks.append(pick)
    wsum = weights[0]
    for wk in weights[1:]:
        wsum = wsum + wk
    idx_ref[...] = jnp.concatenate(picks, axis=0)
    wsel_ref[...] = jnp.concatenate(weights, axis=0) / wsum * ROUTED_SCALE


def _router(x1, w_router, router_bias, tm):
    N = x1.shape[0]
    whi, wlo = _split_bf16(w_router.astype(F32).T)
    rb = jnp.broadcast_to(router_bias.astype(F32)[:, None], (N_EXPERTS, LANES))
    fixed = lambda i: (0, 0)
    col = lambda i: (0, i)
    return pl.pallas_call(
        _router_kernel,
        grid=(N // tm,),
        in_specs=[pl.BlockSpec((tm, D_MODEL), lambda i: (i, 0)), pl.BlockSpec(whi.shape, fixed),
                  pl.BlockSpec(wlo.shape, fixed), pl.BlockSpec(rb.shape, fixed)],
        out_specs=[pl.BlockSpec((TOP_K, tm), col), pl.BlockSpec((TOP_K, tm), col)],
        out_shape=[jax.ShapeDtypeStruct((TOP_K, N), jnp.int32), jax.ShapeDtypeStruct((TOP_K, N), F32)],
        compiler_params=_params("parallel"),
        name="router",
    )(x1, whi, wlo, rb)


def _expert_kernel(blk_ref, exp_ref, lo_ref, hi_ref, first_ref, newe_ref,
                   xs_ref, rw_ref, wg_ref, wu_ref, wd_ref, y_ref, wgb_ref, wub_ref, wdb_ref, *, bm):
    t = pl.program_id(0)
    lo, hi = lo_ref[t], hi_ref[t]

    @pl.when(newe_ref[t] == 1)
    def _():
        wgb_ref[...] = wg_ref[...].astype(BF)
        wub_ref[...] = wu_ref[...].astype(BF)
        wdb_ref[...] = wd_ref[...].astype(BF)

    @pl.when(first_ref[t] == 1)
    def _():
        y_ref[...] = jnp.zeros_like(y_ref)

    @pl.when(hi > lo)
    def _():
        xs = xs_ref[...]
        g = jnp.dot(xs, wgb_ref[...], preferred_element_type=F32)
        u = jnp.dot(xs, wub_ref[...], preferred_element_type=F32)
        h = (g * _sigmoid(g) * u).astype(BF)
        y = jnp.dot(h, wdb_ref[...], preferred_element_type=F32) * rw_ref[...]
        row = blk_ref[t] * bm + _iota((bm, 1), 0)
        mine = (row >= lo) & (row < hi)
        y_ref[...] = jnp.where(mine, y.astype(y_ref.dtype), y_ref[...])


def _experts(xs, row_w, plan, layer, w_gate, w_up, w_down, bm):
    A = xs.shape[0]
    n_items = plan[0].shape[0]
    row = lambda t, blk, *_: (blk[t], 0)
    wspec = lambda shape: pl.BlockSpec((None, None) + shape, lambda t, blk, exp, *_: (layer, exp[t], 0, 0))
    return pl.pallas_call(
        functools.partial(_expert_kernel, bm=bm),
        grid_spec=pltpu.PrefetchScalarGridSpec(
            num_scalar_prefetch=len(plan),
            grid=(n_items,),
            in_specs=[pl.BlockSpec((bm, D_MODEL), row), pl.BlockSpec((bm, 1), row),
                      wspec((D_MODEL, EXPERT_DIM)), wspec((D_MODEL, EXPERT_DIM)),
                      wspec((EXPERT_DIM, D_MODEL))],
            out_specs=pl.BlockSpec((bm, D_MODEL), row),
            scratch_shapes=[pltpu.VMEM((D_MODEL, EXPERT_DIM), BF), pltpu.VMEM((D_MODEL, EXPERT_DIM), BF),
                            pltpu.VMEM((EXPERT_DIM, D_MODEL), BF)],
        ),
        out_shape=jax.ShapeDtypeStruct((A, D_MODEL), BF),
        compiler_params=_params("arbitrary"),
        name="experts",
    )(*plan, xs, row_w, w_gate, w_up, w_down)


def _dispatch_plan(eidx_t, wsel_t, bm):
    N = eidx_t.shape[1]
    A, E = N * TOP_K, N_EXPERTS
    assert A % bm == 0
    ids = jnp.arange(A, dtype=jnp.int32)
    se, sid, sw = lax.sort((eidx_t.reshape(A), ids, wsel_t.reshape(A)), num_keys=1, is_stable=True)
    _, pos = lax.sort((sid, ids), num_keys=1)
    n_blk = A // bm
    start = jnp.searchsorted(se, jnp.arange(E, dtype=jnp.int32), side='left').astype(jnp.int32)
    lo = jnp.sort(jnp.concatenate([jnp.arange(n_blk, dtype=jnp.int32) * bm, start]))
    hi = jnp.concatenate([lo[1:], jnp.full((1,), A, jnp.int32)])
    blk = jnp.minimum(lo // bm, n_blk - 1)
    exp = se[jnp.minimum(lo, A - 1)]
    changed = lambda a: jnp.concatenate([jnp.ones((1,), jnp.int32), (a[1:] != a[:-1]).astype(jnp.int32)])
    plan = (blk, exp, lo, hi, changed(blk), changed(exp))
    return sid % N, sw, pos.reshape(TOP_K, N), plan


def _moe_out_kernel(x_ref, r_ref, wg_ref, wu_ref, wd_ref, g_ref, b_ref, y_ref):
    x = x_ref[...]
    xb = x.astype(BF)
    g = jnp.dot(xb, wg_ref[...], preferred_element_type=F32)
    u = jnp.dot(xb, wu_ref[...], preferred_element_type=F32)
    h = (g * _sigmoid(g) * u).astype(BF)
    shared = jnp.dot(h, wd_ref[...], preferred_element_type=F32)
    y_ref[...] = _layer_norm(DEEPNORM_ALPHA * x + (r_ref[...] + shared), g_ref[...], b_ref[...])


def _moe_out(x1, routed, w_sh_gate, w_sh_up, w_sh_down, ln_g, ln_b, tm):
    N = x1.shape[0]
    row = lambda i: (i, 0)
    fixed = lambda i: (0, 0)
    return pl.pallas_call(
        _moe_out_kernel,
        grid=(N // tm,),
        in_specs=[pl.BlockSpec((tm, D_MODEL), row), pl.BlockSpec((tm, D_MODEL), row),
                  pl.BlockSpec(w_sh_gate.shape, fixed), pl.BlockSpec(w_sh_up.shape, fixed),
                  pl.BlockSpec(w_sh_down.shape, fixed),
                  pl.BlockSpec((1, D_MODEL), fixed), pl.BlockSpec((1, D_MODEL), fixed)],
        out_specs=pl.BlockSpec((tm, D_MODEL), row),
        out_shape=jax.ShapeDtypeStruct((N, D_MODEL), F32),
        compiler_params=_params("parallel"),
        name="moe_out",
    )(x1, routed, w_sh_gate.astype(BF), w_sh_up.astype(BF), w_sh_down.astype(BF),
      ln_g.astype(F32)[None, :], ln_b.astype(F32)[None, :])


def _tiles(N, T):
    return min(512, T), min(256, T), MOBA_BLOCK, min(512, N * TOP_K)


def _mixer_layer(x2, T, w_in, b_forget, diff_lambda, diff_subln, w_branch, w_out, ln_g, ln_b, lambda_init):
    N = x2.shape[0]
    tm_proj, tm_row, tq, _ = _tiles(N, T)
    w, wvt, w_gates, bf = _in_proj_weights(w_in, b_forget)
    qk, vt, misc = _in_proj(x2, w, wvt, bf, _rope_tables(T), T, tm_proj)
    dl = diff_lambda.astype(F32)
    lam = jnp.exp(jnp.sum(dl[0] * dl[1])) - jnp.exp(jnp.sum(dl[2] * dl[3])) + lambda_init
    o_a = _moba(qk, vt, T, tq)
    o_b = _diff(qk, vt, lam, diff_subln, lambda_init, T, tq)
    o_c = _fox(qk, vt, misc, T, tq)
    o_d = _dsa(qk, vt, misc, T, tq)
    return _mix_out(x2, (o_a, o_b, o_c, o_d), w_gates, w_branch, w_out, ln_g, ln_b, tm_row)


def _moe_layer(x1, x1b, T, layer, w_router, router_bias, w_exp_gate, w_exp_up, w_exp_down,
               w_sh_gate, w_sh_up, w_sh_down, ln_g, ln_b):
    N = x1.shape[0]
    _, tm_row, _, bm = _tiles(N, T)
    eidx_t, wsel_t = _router(x1, w_router, router_bias, tm_row)
    row_tok, row_w, pos, plan = _dispatch_plan(eidx_t, wsel_t, bm)
    ys = _experts(x1b[row_tok], row_w[:, None], plan, layer, w_exp_gate, w_exp_up, w_exp_down, bm)
    routed = jnp.sum(ys[pos].astype(F32), axis=0)
    return _moe_out(x1, routed, w_sh_gate, w_sh_up, w_sh_down, ln_g, ln_b, tm_row)


def kernel(x, w_in, b_forget, diff_lambda, diff_subln, w_branch, w_out, ln1_g, ln1_b, w_router, router_bias,
           w_exp_gate, w_exp_up, w_exp_down, w_sh_gate, w_sh_up, w_sh_down, ln2_g, ln2_b):
    B, T, D = x.shape
    x2 = x.reshape(B * T, D)
    for l in range(DEPTH):
        lambda_init = 0.8 - 0.6 * math.exp(-0.3 * l)
        x1, x1b = _mixer_layer(x2, T, w_in[l], b_forget[l], diff_lambda[l], diff_subln[l], w_branch[l],
                               w_out[l], ln1_g[l], ln1_b[l], lambda_init)
        x2 = _moe_layer(x1, x1b, T, l, w_router[l], router_bias[l], w_exp_gate, w_exp_up, w_exp_down,
                        w_sh_gate[l], w_sh_up[l], w_sh_down[l], ln2_g[l], ln2_b[l])
    return x2.reshape(B, T, D)
```

```python
import functools
import math

import jax
import jax.numpy as jnp
from jax import lax
from jax.experimental import pallas as pl
from jax.experimental.pallas import tpu as pltpu

F32 = jnp.float32
BF = jnp.bfloat16

D_MODEL = 1024
DEPTH = 2
HEAD_DIM = 64
N_HEADS = 4
DIFF_DIM = HEAD_DIM // 2
N_IDX_HEADS = 4
IDX_DIM = 64
BRANCH_WIDTH = N_HEADS * HEAD_DIM
N_BRANCHES = 4
MOBA_BLOCK = 256
MOBA_TOPK = 3
DSA_TOPK_MAX = 256
ROPE_THETA = 10000.0
N_EXPERTS = 256
TOP_K = 8
N_GROUPS = 8
TOPK_GROUPS = 4
EXPERT_DIM = 256
ROUTED_SCALE = 2.5
LN_EPS = 1e-5
DEEPNORM_ALPHA = (2 * DEPTH) ** 0.25

IN_SEGMENTS = (
    ('moba_q', BRANCH_WIDTH), ('moba_k', BRANCH_WIDTH), ('moba_v', BRANCH_WIDTH),
    ('diff_q', BRANCH_WIDTH), ('diff_k', BRANCH_WIDTH), ('diff_v', BRANCH_WIDTH),
    ('fox_q', BRANCH_WIDTH), ('fox_k', BRANCH_WIDTH), ('fox_v', BRANCH_WIDTH), ('fox_f', N_HEADS),
    ('dsa_q', BRANCH_WIDTH), ('dsa_k', BRANCH_WIDTH), ('dsa_v', BRANCH_WIDTH),
    ('idx_q', N_IDX_HEADS * IDX_DIM), ('idx_k', IDX_DIM), ('idx_w', N_IDX_HEADS),
    ('gates', N_BRANCHES * D_MODEL),
)

LANES = 128
SUBLANES = 8
SEG = BRANCH_WIDTH
NEG = -1e30
INT_MIN = -2 ** 31
VMEM_LIMIT = 48 * 1024 * 1024

LOG2E = math.log2(math.e)
_QSCALE = HEAD_DIM ** -0.5 * LOG2E
PROJ_SEGS = (
    ('moba_q', 64, _QSCALE), ('moba_k', 64, 1.0),
    ('diff_q', 32, DIFF_DIM ** -0.5 * LOG2E), ('diff_k', 32, 1.0),
    ('fox_q', 0, _QSCALE), ('fox_k', 0, 1.0),
    ('dsa_q', 64, _QSCALE), ('dsa_k', 64, 1.0),
    ('idx_q', 64, IDX_DIM ** -0.5), ('idx_k4', 64, 1.0),
)
SEG_ID = {name: i for i, (name, _, _) in enumerate(PROJ_SEGS)}
N_SEG = len(PROJ_SEGS)
V_SEGS = ('moba_v', 'diff_v', 'fox_v', 'dsa_v')
V_ID = {name: i for i, name in enumerate(V_SEGS)}
MISC_LOGF = 0
MISC_IDXW = 4


def _params(*sem):
    return pltpu.CompilerParams(dimension_semantics=sem, vmem_limit_bytes=VMEM_LIMIT)


def _iota(shape, dim):
    return lax.broadcasted_iota(jnp.int32, shape, dim)


def _dot_nt(a, b):
    return lax.dot_general(a, b, (((1,), (1,)), ((), ())), preferred_element_type=F32)


def _sigmoid(z):
    return 1.0 / (1.0 + jnp.exp(-z))


def _layer_norm(y, g, b):
    mu = jnp.mean(y, axis=-1, keepdims=True)
    yc = y - mu
    var = jnp.mean(yc * yc, axis=-1, keepdims=True)
    return yc * lax.rsqrt(var + LN_EPS) * g + b


def _swap_halves(a, half):
    w = a.shape[-1]
    first = (_iota(a.shape, 1) & (2 * half - 1)) < half
    return jnp.where(first, pltpu.roll(a, w - half, 1), pltpu.roll(a, half, 1))


def _in_proj_kernel(x_ref, w_ref, wvt_ref, c64_ref, s64_ref, c32_ref, s32_ref, bf_ref,
                    qk_ref, vt_ref, misc_ref):
    xb = x_ref[...].astype(BF)
    for s, (_, rot, scale) in enumerate(PROJ_SEGS):
        acc = jnp.dot(xb, w_ref[:, s * SEG:(s + 1) * SEG], preferred_element_type=F32)
        if rot == 64:
            acc = acc * c64_ref[...] + _swap_halves(acc, 32) * s64_ref[...]
        elif rot == 32:
            acc = acc * c32_ref[...] + _swap_halves(acc, 16) * s32_ref[...]
        if scale != 1.0:
            acc = acc * scale
        qk_ref[:, s * SEG:(s + 1) * SEG] = acc.astype(BF)
    vt_ref[...] = _dot_nt(wvt_ref[...], xb).astype(BF)
    m = jnp.dot(xb, w_ref[:, N_SEG * SEG:N_SEG * SEG + LANES], preferred_element_type=F32)
    z = m + bf_ref[...]
    logf = jnp.minimum(z, 0.0) - jnp.log1p(jnp.exp(-jnp.abs(z)))
    lane = _iota(m.shape, 1)
    misc_ref[...] = jnp.where(lane < MISC_IDXW, logf, m)


def _rope_tables(T):
    pos = jnp.arange(T).astype(F32)

    def tab(group, reps):
        half = group // 2
        inv_freq = ROPE_THETA ** (-jnp.arange(half, dtype=F32) / half)
        ang = pos[:, None] * inv_freq[None, :]
        cos, sin = jnp.cos(ang), jnp.sin(ang)
        return (jnp.tile(jnp.concatenate([cos, cos], -1), (1, reps)),
                jnp.tile(jnp.concatenate([-sin, sin], -1), (1, reps)))

    return tab(64, SEG // 64) + tab(32, SEG // 32)


def _in_proj_weights(w_in, b_forget):
    parts, off = {}, 0
    for name, width in IN_SEGMENTS:
        parts[name] = w_in[:, off:off + width]
        off += width
    parts['idx_k4'] = jnp.tile(parts['idx_k'], (1, N_IDX_HEADS))
    misc = jnp.concatenate([parts['fox_f'], parts['idx_w'],
                            jnp.zeros((D_MODEL, LANES - 2 * N_HEADS), w_in.dtype)], axis=1)
    w = jnp.concatenate([parts[name] for name, _, _ in PROJ_SEGS] + [misc], axis=1).astype(BF)
    wvt = jnp.concatenate([parts[name] for name in V_SEGS], axis=1).T.astype(BF)
    bf = jnp.zeros((1, LANES), F32).at[0, MISC_LOGF:MISC_LOGF + N_HEADS].set(b_forget.astype(F32))
    return w, wvt, parts['gates'].astype(BF), bf


def _in_proj(x2, w, wvt, bf, tables, T, tm):
    N = x2.shape[0]
    nt = T // tm
    tab_spec = pl.BlockSpec((tm, SEG), lambda i: (i % nt, 0))
    fixed = lambda i: (0, 0)
    return pl.pallas_call(
        _in_proj_kernel,
        grid=(N // tm,),
        in_specs=[pl.BlockSpec((tm, D_MODEL), lambda i: (i, 0)),
                  pl.BlockSpec(w.shape, fixed), pl.BlockSpec(wvt.shape, fixed),
                  tab_spec, tab_spec, tab_spec, tab_spec,
                  pl.BlockSpec((1, LANES), fixed)],
        out_specs=[pl.BlockSpec((tm, N_SEG * SEG), lambda i: (i, 0)),
                   pl.BlockSpec((len(V_SEGS) * SEG, tm), lambda i: (0, i)),
                   pl.BlockSpec((tm, LANES), lambda i: (i, 0))],
        out_shape=[jax.ShapeDtypeStruct((N, N_SEG * SEG), BF),
                   jax.ShapeDtypeStruct((len(V_SEGS) * SEG, N), BF),
                   jax.ShapeDtypeStruct((N, LANES), F32)],
        compiler_params=_params("parallel"),
        name="in_proj",
    )(x2, w, wvt, *tables, bf)


def _group_mask(shape, shift, g):
    return (_iota(shape, 1) >> shift) == g


def _stack_groups(q, shift, n_groups):
    zero = jnp.zeros_like(q)
    return jnp.concatenate([jnp.where(_group_mask(q.shape, shift, g), q, zero) for g in range(n_groups)],
                           axis=0)


def _cols(a, g, tq):
    return a[:, g * tq:(g + 1) * tq]


def _head_rows(a, h):
    return a[h * HEAD_DIM:(h + 1) * HEAD_DIM]


ONES_ROWS = 16


def _flash_init(n_groups, tq, n_sets):
    return (jnp.full((1, n_groups * tq), NEG, F32), jnp.zeros((1, n_groups * tq), F32),
            tuple(jnp.zeros((SEG, tq), F32) for _ in range(n_sets)))


def _flash_update(s, carry, vt, tq, head_sets):
    m, l, accs = carry
    m_new = jnp.maximum(m, jnp.max(s, axis=0, keepdims=True))
    alpha = jnp.exp2(m - m_new)
    pb = jnp.exp2(s - m_new).astype(BF)
    ones = jnp.ones((ONES_ROWS, vt.shape[1]), BF)
    new_accs, p_sum = [], {}
    for acc, groups in zip(accs, head_sets):
        parts = []
        for h, g in enumerate(groups):
            r = jnp.dot(jnp.concatenate([_head_rows(vt, h), ones], axis=0), _cols(pb, g, tq),
                        preferred_element_type=F32)
            parts.append(_cols(alpha, g, tq) * _head_rows(acc, h) + r[0:HEAD_DIM])
            p_sum[g] = r[HEAD_DIM:HEAD_DIM + 1]
        new_accs.append(jnp.concatenate(parts, axis=0))
    l = alpha * l + jnp.concatenate([p_sum[g] for g in sorted(p_sum)], axis=1)
    return m_new, l, tuple(new_accs)


def _causal(n_groups, tq):
    shape = (tq, n_groups * tq)
    return _iota(shape, 0) <= (_iota(shape, 1) & (tq - 1))


def _normalised(acc, l, groups, tq):
    return jnp.concatenate([_head_rows(acc, h) * (1.0 / _cols(l, g, tq)) for h, g in enumerate(groups)],
                           axis=0)


def _tile_lanes(a, n):
    return jnp.concatenate([a] * n, axis=1) if n > 1 else a


HEADS = tuple(range(N_HEADS))


def _attn_specs(T, tq, q_seg, k_seg, v_name):
    nq = T // tq
    v_id = V_ID[v_name]
    return [pl.BlockSpec((tq, SEG), lambda b, i: (b * nq + i, q_seg)),
            pl.BlockSpec((T, SEG), lambda b, i: (b, k_seg)),
            pl.BlockSpec((SEG, T), lambda b, i: (v_id, b))]


def _attn_out(N, T, tq):
    nq = T // tq
    return (pl.BlockSpec((tq, SEG), lambda b, i: (b * nq + i, 0)),
            jax.ShapeDtypeStruct((N, SEG), BF))


def _moba_kernel(q_ref, k_ref, vt_ref, o_ref, km_ref, sel_ref, *, tq, nb, n_sel):
    i = pl.program_id(1)

    @pl.when(i == 0)
    def _():
        km_ref[...] = jnp.zeros_like(km_ref)
        for n in range(nb):
            blk = k_ref[n * tq:(n + 1) * tq, :].astype(F32)
            km_ref[n:n + 1, :] = jnp.mean(blk, axis=0, keepdims=True)

    qs = _stack_groups(q_ref[...], 6, N_HEADS)
    km = km_ref[...]
    km_hi = km.astype(BF)
    km_lo = (km - km_hi.astype(F32)).astype(BF)
    nrow = sel_ref.shape[0]
    g = (_dot_nt(km_hi, qs) + _dot_nt(km_lo, qs))[0:nrow]
    blk_id = _iota(g.shape, 0)
    g = jnp.where(blk_id < i, g, -jnp.inf)
    sel = jnp.zeros(g.shape, F32)
    for _ in range(n_sel):
        gmax = jnp.max(g, axis=0, keepdims=True)
        first = jnp.min(jnp.where(g == gmax, blk_id, nrow), axis=0, keepdims=True)
        hit = blk_id == first
        sel = jnp.where(hit & (gmax > -jnp.inf), 1.0, sel)
        g = jnp.where(hit, -jnp.inf, g)
    sel_ref[...] = sel

    def body(c, carry):
        k0 = pl.multiple_of(c * tq, tq)
        picked = sel_ref[pl.ds(c, 1), :] > 0.5
        s = jnp.where(picked, _dot_nt(k_ref[pl.ds(k0, tq), :], qs), NEG)
        return _flash_update(s, carry, vt_ref[:, pl.ds(k0, tq)], tq, (HEADS,))

    carry = lax.fori_loop(0, i, body, _flash_init(N_HEADS, tq, 1))
    k0 = pl.multiple_of(i * tq, tq)
    s = jnp.where(_causal(N_HEADS, tq), _dot_nt(k_ref[pl.ds(k0, tq), :], qs), NEG)
    _, l, (acc,) = _flash_update(s, carry, vt_ref[:, pl.ds(k0, tq)], tq, (HEADS,))
    o_ref[...] = _normalised(acc, l, HEADS, tq).T.astype(BF)


def _moba(qk, vt, T, tq):
    N = qk.shape[0]
    nb = T // MOBA_BLOCK
    assert T % MOBA_BLOCK == 0 and tq == MOBA_BLOCK and nb <= LANES
    sel_rows = -(-nb // SUBLANES) * SUBLANES
    out_spec, out_shape = _attn_out(N, T, tq)
    return pl.pallas_call(
        functools.partial(_moba_kernel, tq=tq, nb=nb, n_sel=min(MOBA_TOPK, nb - 1)),
        grid=(N // T, T // tq),
        in_specs=_attn_specs(T, tq, SEG_ID['moba_q'], SEG_ID['moba_k'], 'moba_v'),
        out_specs=out_spec, out_shape=out_shape,
        scratch_shapes=[pltpu.VMEM((LANES, SEG), F32), pltpu.VMEM((sel_rows, N_HEADS * tq), F32)],
        compiler_params=_params("parallel", "arbitrary"),
        name="moba_attn",
    )(qk, qk, vt)


DIFF_SETS = (tuple(2 * h for h in HEADS), tuple(2 * h + 1 for h in HEADS))


def _diff_kernel(lam_ref, q_ref, k_ref, vt_ref, g_ref, o_ref, *, tq, out_scale):
    i = pl.program_id(1)
    n_groups = 2 * N_HEADS
    qs = _stack_groups(q_ref[...], 5, n_groups)

    def body(c, carry):
        k0 = pl.multiple_of(c * tq, tq)
        s = _dot_nt(k_ref[pl.ds(k0, tq), :], qs)
        return _flash_update(s, carry, vt_ref[:, pl.ds(k0, tq)], tq, DIFF_SETS)

    carry = lax.fori_loop(0, i, body, _flash_init(n_groups, tq, 2))
    k0 = pl.multiple_of(i * tq, tq)
    s = jnp.where(_causal(n_groups, tq), _dot_nt(k_ref[pl.ds(k0, tq), :], qs), NEG)
    _, l, (a1, a2) = _flash_update(s, carry, vt_ref[:, pl.ds(k0, tq)], tq, DIFF_SETS)
    out = _normalised(a1, l, DIFF_SETS[0], tq) - lam_ref[0] * _normalised(a2, l, DIFF_SETS[1], tq)
    normed = []
    for h in HEADS:
        oh = _head_rows(out, h)
        ms = jnp.mean(oh * oh, axis=0, keepdims=True)
        normed.append(oh * lax.rsqrt(ms + LN_EPS))
    o_ref[...] = (jnp.concatenate(normed, axis=0).T * g_ref[...] * out_scale).astype(BF)


def _diff(qk, vt, lam, subln_g, lambda_init, T, tq):
    N = qk.shape[0]
    out_spec, out_shape = _attn_out(N, T, tq)
    g = jnp.tile(subln_g.astype(F32), N_HEADS)[None, :]
    return pl.pallas_call(
        functools.partial(_diff_kernel, tq=tq, out_scale=1.0 - lambda_init),
        grid=(N // T, T // tq),
        in_specs=[pl.BlockSpec(memory_space=pltpu.SMEM)]
        + _attn_specs(T, tq, SEG_ID['diff_q'], SEG_ID['diff_k'], 'diff_v')
        + [pl.BlockSpec((1, SEG), lambda b, i: (0, 0))],
        out_specs=out_spec, out_shape=out_shape,
        compiler_params=_params("parallel", "arbitrary"),
        name="diff_attn",
    )(lam.reshape(1).astype(F32), qk, qk, vt, g)


def _fox_kernel(q_ref, k_ref, vt_ref, misc_ref, o_ref, c_ref, *, tq):
    i = pl.program_id(1)

    @pl.when(i == 0)
    def _():
        tri = jnp.where(_iota((tq, tq), 1) <= _iota((tq, tq), 0), 1.0, 0.0).astype(BF)
        carry = [jnp.zeros((1, LANES), F32) for _ in HEADS]
        for n in range(c_ref.shape[1] // tq):
            blk = misc_ref[n * tq:(n + 1) * tq, :] * LOG2E
            for h in HEADS:
                col = jnp.broadcast_to(blk[:, MISC_LOGF + h:MISC_LOGF + h + 1], blk.shape)
                hi = col.astype(BF)
                rest = col - hi.astype(F32)
                mid = rest.astype(BF)
                lo = (rest - mid.astype(F32)).astype(BF)
                cs = carry[h] + (jnp.dot(tri, hi, preferred_element_type=F32)
                                 + jnp.dot(tri, mid, preferred_element_type=F32)
                                 + jnp.dot(tri, lo, preferred_element_type=F32))
                c_ref[h, n * tq:(n + 1) * tq, :] = cs
                carry[h] = cs[tq - 1:tq, :]

    qs = _stack_groups(q_ref[...], 6, N_HEADS)

    def logits(k0):
        s = _dot_nt(k_ref[pl.ds(k0, tq), :], qs)
        return jnp.concatenate(
            [_cols(s, h, tq) - _tile_lanes(c_ref[h, pl.ds(k0, tq), :], tq // LANES) for h in HEADS], axis=1)

    def body(c, carry):
        k0 = pl.multiple_of(c * tq, tq)
        return _flash_update(logits(k0), carry, vt_ref[:, pl.ds(k0, tq)], tq, (HEADS,))

    carry = lax.fori_loop(0, i, body, _flash_init(N_HEADS, tq, 1))
    k0 = pl.multiple_of(i * tq, tq)
    s = jnp.where(_causal(N_HEADS, tq), logits(k0), NEG)
    _, l, (acc,) = _flash_update(s, carry, vt_ref[:, pl.ds(k0, tq)], tq, (HEADS,))
    o_ref[...] = _normalised(acc, l, HEADS, tq).T.astype(BF)


def _fox(qk, vt, misc, T, tq):
    N = qk.shape[0]
    out_spec, out_shape = _attn_out(N, T, tq)
    return pl.pallas_call(
        functools.partial(_fox_kernel, tq=tq),
        grid=(N // T, T // tq),
        in_specs=_attn_specs(T, tq, SEG_ID['fox_q'], SEG_ID['fox_k'], 'fox_v')
        + [pl.BlockSpec((T, LANES), lambda b, i: (b, 0))],
        out_specs=out_spec, out_shape=out_shape,
        scratch_shapes=[pltpu.VMEM((N_HEADS, T, LANES), F32)],
        compiler_params=_params("parallel", "arbitrary"),
        name="fox_attn",
    )(qk, qk, vt, misc)


def _dsa_kernel(q_ref, k_ref, vt_ref, qi_ref, ki_ref, w_ref, o_ref, key_ref, bias_ref, *, tq, n_keep):
    i = pl.program_id(1)
    n_ch = i + 1
    qpos = i * tq + _iota((1, tq), 1)

    qis = _stack_groups(qi_ref[...], 6, N_IDX_HEADS)
    wt = w_ref[...].T * (N_IDX_HEADS ** -0.5)
    w_rows = [wt[MISC_IDXW + h:MISC_IDXW + h + 1] for h in range(N_IDX_HEADS)]

    def score_body(c, carry):
        k0 = pl.multiple_of(c * tq, tq)
        d = jnp.maximum(_dot_nt(ki_ref[pl.ds(k0, tq), :], qis), 0.0)
        score = w_rows[0] * _cols(d, 0, tq)
        for h in range(1, N_IDX_HEADS):
            score = score + w_rows[h] * _cols(d, h, tq)
        sc = jnp.where((k0 + _iota((tq, tq), 0)) <= qpos, score, -jnp.inf)
        bits = pltpu.bitcast(sc, jnp.int32)
        key_ref[pl.ds(k0, tq), :] = jnp.where(sc == 0.0, 0,
                                              jnp.where(bits < 0, bits ^ jnp.int32(0x7FFFFFFF), bits))
        return carry

    lax.fori_loop(0, n_ch, score_body, 0)

    @pl.when(n_ch % 2 == 1)
    def _():
        key_ref[pl.ds(pl.multiple_of(n_ch * tq, tq), tq), :] = jnp.full((tq, tq), INT_MIN, jnp.int32)

    def count(pred):
        def body(c, acc):
            k0 = pl.multiple_of(c * 2 * tq, 2 * tq)
            hit = jnp.where(pred(key_ref[pl.ds(k0, 2 * tq), :]), 1.0, 0.0)
            return acc + jnp.sum(hit, axis=0, keepdims=True)
        return lax.fori_loop(0, (n_ch + 1) // 2, body, jnp.zeros((1, tq), F32))

    def thr_body(it, thr):
        cand = thr + lax.shift_left(jnp.int32(1), 31 - it)
        return jnp.where(count(lambda kk: kk >= cand) >= n_keep, cand, thr)

    thr = lax.fori_loop(0, 32, thr_body, jnp.full((1, tq), INT_MIN, jnp.int32))

    need = n_keep - count(lambda kk: kk > thr)
    lower = jnp.where(_iota((tq, tq), 1) <= _iota((tq, tq), 0), 1.0, 0.0).astype(BF)

    def tie_body(c, seen):
        k0 = pl.multiple_of(c * tq, tq)
        kk = key_ref[pl.ds(k0, tq), :]
        eq = jnp.where(kk == thr, 1.0, 0.0)
        rank = jnp.dot(lower, eq.astype(BF), preferred_element_type=F32) + seen
        keep = (kk > thr) | ((kk == thr) & (rank <= need))
        causal = (k0 + _iota((tq, tq), 0)) <= qpos
        bias_ref[pl.ds(k0, tq), :] = jnp.where(keep & causal, 0.0, NEG)
        return seen + jnp.sum(eq, axis=0, keepdims=True)

    lax.fori_loop(0, n_ch, tie_body, jnp.zeros((1, tq), F32))

    qs = _stack_groups(q_ref[...], 6, N_HEADS)

    def body(c, carry):
        k0 = pl.multiple_of(c * tq, tq)
        s = _dot_nt(k_ref[pl.ds(k0, tq), :], qs) + _tile_lanes(bias_ref[pl.ds(k0, tq), :], N_HEADS)
        return _flash_update(s, carry, vt_ref[:, pl.ds(k0, tq)], tq, (HEADS,))

    _, l, (acc,) = lax.fori_loop(0, n_ch, body, _flash_init(N_HEADS, tq, 1))
    o_ref[...] = _normalised(acc, l, HEADS, tq).T.astype(BF)


def _dsa(qk, vt, misc, T, tq):
    N = qk.shape[0]
    nq = T // tq
    out_spec, out_shape = _attn_out(N, T, tq)
    return pl.pallas_call(
        functools.partial(_dsa_kernel, tq=tq, n_keep=min(DSA_TOPK_MAX, T // 4)),
        grid=(N // T, nq),
        in_specs=_attn_specs(T, tq, SEG_ID['dsa_q'], SEG_ID['dsa_k'], 'dsa_v')
        + [pl.BlockSpec((tq, SEG), lambda b, i: (b * nq + i, SEG_ID['idx_q'])),
           pl.BlockSpec((T, SEG), lambda b, i: (b, SEG_ID['idx_k4'])),
           pl.BlockSpec((tq, LANES), lambda b, i: (b * nq + i, 0))],
        out_specs=out_spec, out_shape=out_shape,
        scratch_shapes=[pltpu.VMEM(((nq + nq % 2) * tq, tq), jnp.int32), pltpu.VMEM((T, tq), F32)],
        compiler_params=_params("parallel", "arbitrary"),
        name="dsa_attn",
    )(qk, qk, vt, qk, qk, misc)


def _mix_out_kernel(x_ref, oa_ref, ob_ref, oc_ref, od_ref, wg_ref, wb_ref, wo_ref, g_ref, b_ref,
                    y_ref, yb_ref):
    x = x_ref[...]
    xb = x.astype(BF)
    merged = jnp.zeros(x.shape, F32)
    for n, o_ref in enumerate((oa_ref, ob_ref, oc_ref, od_ref)):
        gate = _sigmoid(jnp.dot(xb, wg_ref[:, n * D_MODEL:(n + 1) * D_MODEL], preferred_element_type=F32))
        merged = merged + gate * jnp.dot(o_ref[...], wb_ref[n], preferred_element_type=F32)
    h = jnp.dot(merged.astype(BF), wo_ref[...], preferred_element_type=F32)
    y = _layer_norm(DEEPNORM_ALPHA * x + h, g_ref[...], b_ref[...])
    y_ref[...] = y
    yb_ref[...] = y.astype(BF)


def _mix_out(x2, branches, w_gates, w_branch, w_out, ln_g, ln_b, tm):
    N = x2.shape[0]
    row = lambda i: (i, 0)
    fixed2 = lambda i: (0, 0)
    return pl.pallas_call(
        _mix_out_kernel,
        grid=(N // tm,),
        in_specs=[pl.BlockSpec((tm, D_MODEL), row)] + [pl.BlockSpec((tm, SEG), row)] * N_BRANCHES
        + [pl.BlockSpec(w_gates.shape, fixed2),
           pl.BlockSpec(w_branch.shape, lambda i: (0, 0, 0)),
           pl.BlockSpec(w_out.shape, fixed2),
           pl.BlockSpec((1, D_MODEL), fixed2), pl.BlockSpec((1, D_MODEL), fixed2)],
        out_specs=[pl.BlockSpec((tm, D_MODEL), row), pl.BlockSpec((tm, D_MODEL), row)],
        out_shape=[jax.ShapeDtypeStruct((N, D_MODEL), F32), jax.ShapeDtypeStruct((N, D_MODEL), BF)],
        compiler_params=_params("parallel"),
        name="mix_out",
    )(x2, *branches, w_gates, w_branch.astype(BF), w_out.astype(BF),
      ln_g.astype(F32)[None, :], ln_b.astype(F32)[None, :])


def _split_bf16(a):
    hi = a.astype(BF)
    return hi, (a - hi.astype(F32)).astype(BF)


def _router_kernel(x_ref, whi_ref, wlo_ref, rb_ref, idx_ref, wsel_ref):
    xhi, xlo = _split_bf16(x_ref[...])
    whi = whi_ref[...]
    logits = _dot_nt(whi, xhi) + _dot_nt(whi, xlo) + _dot_nt(wlo_ref[...], xhi)
    tm = logits.shape[1]
    scores = _sigmoid(logits)
    biased = scores + _tile_lanes(rb_ref[...], tm // LANES)
    per_group = N_EXPERTS // N_GROUPS
    gs = []
    for g in range(N_GROUPS):
        bg = biased[g * per_group:(g + 1) * per_group]
        row = _iota(bg.shape, 0)
        m1 = jnp.max(bg, axis=0, keepdims=True)
        i1 = jnp.min(jnp.where(bg == m1, row, per_group), axis=0, keepdims=True)
        m2 = jnp.max(jnp.where(row == i1, -jnp.inf, bg), axis=0, keepdims=True)
        gs.append(m1 + m2)
    kept = []
    for g in range(N_GROUPS):
        rank = jnp.zeros((1, tm), F32)
        for o in range(N_GROUPS):
            if o != g:
                beats = (gs[o] >= gs[g]) if o < g else (gs[o] > gs[g])
                rank = rank + jnp.where(beats, 1.0, 0.0)
        kept.append(jnp.where(rank < TOPK_GROUPS, biased[g * per_group:(g + 1) * per_group], -jnp.inf))
    masked = jnp.concatenate(kept, axis=0)
    eid = _iota(masked.shape, 0)
    picks, weights = [], []
    for _ in range(TOP_K):
        mx = jnp.max(masked, axis=0, keepdims=True)
        pick = jnp.min(jnp.where(masked == mx, eid, N_EXPERTS), axis=0, keepdims=True)
        hit = eid == pick
        weights.append(jnp.sum(jnp.where(hit, scores, 0.0), axis=0, keepdims=True))
        masked = jnp.where(hit, -jnp.inf, masked)
        picks.append(pick)
    wsum = weights[0]
    for wk in weights[1:]:
        wsum = wsum + wk
    idx_ref[...] = jnp.concatenate(picks, axis=0)
    wsel_ref[...] = jnp.concatenate(weights, axis=0) / wsum * ROUTED_SCALE


def _router(x1, w_router, router_bias, tm):
    N = x1.shape[0]
    whi, wlo = _split_bf16(w_router.astype(F32).T)
    rb = jnp.broadcast_to(router_bias.astype(F32)[:, None], (N_EXPERTS, LANES))
    fixed = lambda i: (0, 0)
    col = lambda i: (0, i)
    return pl.pallas_call(
        _router_kernel,
        grid=(N // tm,),
        in_specs=[pl.BlockSpec((tm, D_MODEL), lambda i: (i, 0)), pl.BlockSpec(whi.shape, fixed),
                  pl.BlockSpec(wlo.shape, fixed), pl.BlockSpec(rb.shape, fixed)],
        out_specs=[pl.BlockSpec((TOP_K, tm), col), pl.BlockSpec((TOP_K, tm), col)],
        out_shape=[jax.ShapeDtypeStruct((TOP_K, N), jnp.int32), jax.ShapeDtypeStruct((TOP_K, N), F32)],
        compiler_params=_params("parallel"),
        name="router",
    )(x1, whi, wlo, rb)


def _expert_kernel(blk_ref, exp_ref, lo_ref, hi_ref, first_ref, newe_ref,
                   xs_ref, rw_ref, wg_ref, wu_ref, wd_ref, y_ref, wgb_ref, wub_ref, wdb_ref, *, bm):
    t = pl.program_id(0)
    lo, hi = lo_ref[t], hi_ref[t]

    @pl.when(newe_ref[t] == 1)
    def _():
        wgb_ref[...] = wg_ref[...].astype(BF)
        wub_ref[...] = wu_ref[...].astype(BF)
        wdb_ref[...] = wd_ref[...].astype(BF)

    @pl.when(first_ref[t] == 1)
    def _():
        y_ref[...] = jnp.zeros_like(y_ref)

    @pl.when(hi > lo)
    def _():
        xs = xs_ref[...]
        g = jnp.dot(xs, wgb_ref[...], preferred_element_type=F32)
        u = jnp.dot(xs, wub_ref[...], preferred_element_type=F32)
        h = (g * _sigmoid(g) * u).astype(BF)
        y = jnp.dot(h, wdb_ref[...], preferred_element_type=F32) * rw_ref[...]
        row = blk_ref[t] * bm + _iota((bm, 1), 0)
        mine = (row >= lo) & (row < hi)
        y_ref[...] = jnp.where(mine, y.astype(y_ref.dtype), y_ref[...])


def _experts(xs, row_w, plan, layer, w_gate, w_up, w_down, bm):
    A = xs.shape[0]
    n_items = plan[0].shape[0]
    row = lambda t, blk, *_: (blk[t], 0)
    wspec = lambda shape: pl.BlockSpec((None, None) + shape, lambda t, blk, exp, *_: (layer, exp[t], 0, 0))
    return pl.pallas_call(
        functools.partial(_expert_kernel, bm=bm),
        grid_spec=pltpu.PrefetchScalarGridSpec(
            num_scalar_prefetch=len(plan),
            grid=(n_items,),
            in_specs=[pl.BlockSpec((bm, D_MODEL), row), pl.BlockSpec((bm, 1), row),
                      wspec((D_MODEL, EXPERT_DIM)), wspec((D_MODEL, EXPERT_DIM)),
                      wspec((EXPERT_DIM, D_MODEL))],
            out_specs=pl.BlockSpec((bm, D_MODEL), row),
            scratch_shapes=[pltpu.VMEM((D_MODEL, EXPERT_DIM), BF), pltpu.VMEM((D_MODEL, EXPERT_DIM), BF),
                            pltpu.VMEM((EXPERT_DIM, D_MODEL), BF)],
        ),
        out_shape=jax.ShapeDtypeStruct((A, D_MODEL), BF),
        compiler_params=_params("arbitrary"),
        name="experts",
    )(*plan, xs, row_w, w_gate, w_up, w_down)


def _dispatch_plan(eidx_t, wsel_t, bm):
    N = eidx_t.shape[1]
    A, E = N * TOP_K, N_EXPERTS
    assert A % bm == 0
    ids = jnp.arange(A, dtype=jnp.int32)
    se, sid, sw = lax.sort((eidx_t.reshape(A), ids, wsel_t.reshape(A)), num_keys=1, is_stable=True)
    _, pos = lax.sort((sid, ids), num_keys=1)
    n_blk = A // bm
    start = jnp.searchsorted(se, jnp.arange(E, dtype=jnp.int32), side='left').astype(jnp.int32)
    lo = jnp.sort(jnp.concatenate([jnp.arange(n_blk, dtype=jnp.int32) * bm, start]))
    hi = jnp.concatenate([lo[1:], jnp.full((1,), A, jnp.int32)])
    blk = jnp.minimum(lo // bm, n_blk - 1)
    exp = se[jnp.minimum(lo, A - 1)]
    changed = lambda a: jnp.concatenate([jnp.ones((1,), jnp.int32), (a[1:] != a[:-1]).astype(jnp.int32)])
    plan = (blk, exp, lo, hi, changed(blk), changed(exp))
    return sid % N, sw, pos.reshape(TOP_K, N), plan


def _moe_out_kernel(x_ref, r_ref, wg_ref, wu_ref, wd_ref, g_ref, b_ref, y_ref):
    x = x_ref[...]
    xb = x.astype(BF)
    g = jnp.dot(xb, wg_ref[...], preferred_element_type=F32)
    u = jnp.dot(xb, wu_ref[...], preferred_element_type=F32)
    h = (g * _sigmoid(g) * u).astype(BF)
    shared = jnp.dot(h, wd_ref[...], preferred_element_type=F32)
    routed = r_ref[0].astype(F32)
    for k in range(1, TOP_K):
        routed = routed + r_ref[k].astype(F32)
    y_ref[...] = _layer_norm(DEEPNORM_ALPHA * x + (routed + shared), g_ref[...], b_ref[...])


def _moe_out(x1, routed, w_sh_gate, w_sh_up, w_sh_down, ln_g, ln_b, tm):
    N = x1.shape[0]
    row = lambda i: (i, 0)
    fixed = lambda i: (0, 0)
    return pl.pallas_call(
        _moe_out_kernel,
        grid=(N // tm,),
        in_specs=[pl.BlockSpec((tm, D_MODEL), row), pl.BlockSpec((TOP_K, tm, D_MODEL), lambda i: (0, i, 0)),
                  pl.BlockSpec(w_sh_gate.shape, fixed), pl.BlockSpec(w_sh_up.shape, fixed),
                  pl.BlockSpec(w_sh_down.shape, fixed),
                  pl.BlockSpec((1, D_MODEL), fixed), pl.BlockSpec((1, D_MODEL), fixed)],
        out_specs=pl.BlockSpec((tm, D_MODEL), row),
        out_shape=jax.ShapeDtypeStruct((N, D_MODEL), F32),
        compiler_params=_params("parallel"),
        name="moe_out",
    )(x1, routed, w_sh_gate.astype(BF), w_sh_up.astype(BF), w_sh_down.astype(BF),
      ln_g.astype(F32)[None, :], ln_b.astype(F32)[None, :])


def _tiles(N, T):
    return min(512, T), min(256, T), MOBA_BLOCK, min(512, N * TOP_K)


def _mixer_layer(x2, T, w_in, b_forget, diff_lambda, diff_subln, w_branch, w_out, ln_g, ln_b, lambda_init):
    N = x2.shape[0]
    tm_proj, tm_row, tq, _ = _tiles(N, T)
    w, wvt, w_gates, bf = _in_proj_weights(w_in, b_forget)
    qk, vt, misc = _in_proj(x2, w, wvt, bf, _rope_tables(T), T, tm_proj)
    dl = diff_lambda.astype(F32)
    lam = jnp.exp(jnp.sum(dl[0] * dl[1])) - jnp.exp(jnp.sum(dl[2] * dl[3])) + lambda_init
    o_a = _moba(qk, vt, T, tq)
    o_b = _diff(qk, vt, lam, diff_subln, lambda_init, T, tq)
    o_c = _fox(qk, vt, misc, T, tq)
    o_d = _dsa(qk, vt, misc, T, tq)
    return _mix_out(x2, (o_a, o_b, o_c, o_d), w_gates, w_branch, w_out, ln_g, ln_b, tm_row)


def _moe_layer(x1, x1b, T, layer, w_router, router_bias, w_exp_gate, w_exp_up, w_exp_down,
               w_sh_gate, w_sh_up, w_sh_down, ln_g, ln_b):
    N = x1.shape[0]
    _, tm_row, _, bm = _tiles(N, T)
    eidx_t, wsel_t = _router(x1, w_router, router_bias, tm_row)
    row_tok, row_w, pos, plan = _dispatch_plan(eidx_t, wsel_t, bm)
    ys = _experts(x1b[row_tok], row_w[:, None], plan, layer, w_exp_gate, w_exp_up, w_exp_down, bm)
    return _moe_out(x1, ys[pos], w_sh_gate, w_sh_up, w_sh_down, ln_g, ln_b, tm_row)


def kernel(x, w_in, b_forget, diff_lambda, diff_subln, w_branch, w_out, ln1_g, ln1_b, w_router, router_bias,
           w_exp_gate, w_exp_up, w_exp_down, w_sh_gate, w_sh_up, w_sh_down, ln2_g, ln2_b):
    B, T, D = x.shape
    x2 = x.reshape(B * T, D)
    for l in range(DEPTH):
        lambda_init = 0.8 - 0.6 * math.exp(-0.3 * l)
        x1, x1b = _mixer_layer(x2, T, w_in[l], b_forget[l], diff_lambda[l], diff_subln[l], w_branch[l],
                               w_out[l], ln1_g[l], ln1_b[l], lambda_init)
        x2 = _moe_layer(x1, x1b, T, l, w_router[l], router_bias[l], w_exp_gate, w_exp_up, w_exp_down,
                        w_sh_gate[l], w_sh_up[l], w_sh_down[l], ln2_g[l], ln2_b[l])
    return x2.reshape(B, T, D)
```

```python
import functools
import math

import jax
import jax.numpy as jnp
from jax import lax
from jax.experimental import pallas as pl
from jax.experimental.pallas import tpu as pltpu

F32 = jnp.float32
BF = jnp.bfloat16

D_MODEL = 1024
DEPTH = 2
HEAD_DIM = 64
N_HEADS = 4
DIFF_DIM = HEAD_DIM // 2
N_IDX_HEADS = 4
IDX_DIM = 64
BRANCH_WIDTH = N_HEADS * HEAD_DIM
N_BRANCHES = 4
MOBA_BLOCK = 256
MOBA_TOPK = 3
DSA_TOPK_MAX = 256
ROPE_THETA = 10000.0
N_EXPERTS = 256
TOP_K = 8
N_GROUPS = 8
TOPK_GROUPS = 4
EXPERT_DIM = 256
ROUTED_SCALE = 2.5
LN_EPS = 1e-5
DEEPNORM_ALPHA = (2 * DEPTH) ** 0.25

IN_SEGMENTS = (
    ('moba_q', BRANCH_WIDTH), ('moba_k', BRANCH_WIDTH), ('moba_v', BRANCH_WIDTH),
    ('diff_q', BRANCH_WIDTH), ('diff_k', BRANCH_WIDTH), ('diff_v', BRANCH_WIDTH),
    ('fox_q', BRANCH_WIDTH), ('fox_k', BRANCH_WIDTH), ('fox_v', BRANCH_WIDTH), ('fox_f', N_HEADS),
    ('dsa_q', BRANCH_WIDTH), ('dsa_k', BRANCH_WIDTH), ('dsa_v', BRANCH_WIDTH),
    ('idx_q', N_IDX_HEADS * IDX_DIM), ('idx_k', IDX_DIM), ('idx_w', N_IDX_HEADS),
    ('gates', N_BRANCHES * D_MODEL),
)

LANES = 128
SUBLANES = 8
SEG = BRANCH_WIDTH
NEG = -1e30
INT_MIN = -2 ** 31
VMEM_LIMIT = 48 * 1024 * 1024

LOG2E = math.log2(math.e)
_QSCALE = HEAD_DIM ** -0.5 * LOG2E
PROJ_SEGS = (
    ('moba_q', 64, _QSCALE), ('moba_k', 64, 1.0),
    ('diff_q', 32, DIFF_DIM ** -0.5 * LOG2E), ('diff_k', 32, 1.0),
    ('fox_q', 0, _QSCALE), ('fox_k', 0, 1.0),
    ('dsa_q', 64, _QSCALE), ('dsa_k', 64, 1.0),
    ('idx_q', 64, IDX_DIM ** -0.5), ('idx_k4', 64, 1.0),
)
SEG_ID = {name: i for i, (name, _, _) in enumerate(PROJ_SEGS)}
N_SEG = len(PROJ_SEGS)
V_SEGS = ('moba_v', 'diff_v', 'fox_v', 'dsa_v')
V_ID = {name: i for i, name in enumerate(V_SEGS)}
MISC_LOGF = 0
MISC_IDXW = 4


def _params(*sem):
    return pltpu.CompilerParams(dimension_semantics=sem, vmem_limit_bytes=VMEM_LIMIT)


def _iota(shape, dim):
    return lax.broadcasted_iota(jnp.int32, shape, dim)


def _dot_nt(a, b):
    return lax.dot_general(a, b, (((1,), (1,)), ((), ())), preferred_element_type=F32)


def _sigmoid(z):
    return 1.0 / (1.0 + jnp.exp(-z))


def _layer_norm(y, g, b):
    mu = jnp.mean(y, axis=-1, keepdims=True)
    yc = y - mu
    var = jnp.mean(yc * yc, axis=-1, keepdims=True)
    return yc * lax.rsqrt(var + LN_EPS) * g + b


def _swap_halves(a, half):
    w = a.shape[-1]
    first = (_iota(a.shape, 1) & (2 * half - 1)) < half
    return jnp.where(first, pltpu.roll(a, w - half, 1), pltpu.roll(a, half, 1))


def _in_proj_kernel(x_ref, w_ref, wvt_ref, c64_ref, s64_ref, c32_ref, s32_ref, bf_ref,
                    qk_ref, vt_ref, misc_ref):
    xb = x_ref[...].astype(BF)
    for s, (_, rot, scale) in enumerate(PROJ_SEGS):
        acc = jnp.dot(xb, w_ref[:, s * SEG:(s + 1) * SEG], preferred_element_type=F32)
        if rot == 64:
            acc = acc * c64_ref[...] + _swap_halves(acc, 32) * s64_ref[...]
        elif rot == 32:
            acc = acc * c32_ref[...] + _swap_halves(acc, 16) * s32_ref[...]
        if scale != 1.0:
            acc = acc * scale
        qk_ref[:, s * SEG:(s + 1) * SEG] = acc.astype(BF)
    vt_ref[...] = _dot_nt(wvt_ref[...], xb).astype(BF)
    m = jnp.dot(xb, w_ref[:, N_SEG * SEG:N_SEG * SEG + LANES], preferred_element_type=F32)
    z = m + bf_ref[...]
    logf = jnp.minimum(z, 0.0) - jnp.log1p(jnp.exp(-jnp.abs(z)))
    lane = _iota(m.shape, 1)
    misc_ref[...] = jnp.where(lane < MISC_IDXW, logf, m)


def _rope_tables(T):
    pos = jnp.arange(T).astype(F32)

    def tab(group, reps):
        half = group // 2
        inv_freq = ROPE_THETA ** (-jnp.arange(half, dtype=F32) / half)
        ang = pos[:, None] * inv_freq[None, :]
        cos, sin = jnp.cos(ang), jnp.sin(ang)
        return (jnp.tile(jnp.concatenate([cos, cos], -1), (1, reps)),
                jnp.tile(jnp.concatenate([-sin, sin], -1), (1, reps)))

    return tab(64, SEG // 64) + tab(32, SEG // 32)


def _in_proj_weights(w_in, b_forget):
    parts, off = {}, 0
    for name, width in IN_SEGMENTS:
        parts[name] = w_in[:, off:off + width]
        off += width
    parts['idx_k4'] = jnp.tile(parts['idx_k'], (1, N_IDX_HEADS))
    misc = jnp.concatenate([parts['fox_f'], parts['idx_w'],
                            jnp.zeros((D_MODEL, LANES - 2 * N_HEADS), w_in.dtype)], axis=1)
    w = jnp.concatenate([parts[name] for name, _, _ in PROJ_SEGS] + [misc], axis=1).astype(BF)
    wvt = jnp.concatenate([parts[name] for name in V_SEGS], axis=1).T.astype(BF)
    bf = jnp.zeros((1, LANES), F32).at[0, MISC_LOGF:MISC_LOGF + N_HEADS].set(b_forget.astype(F32))
    return w, wvt, parts['gates'].astype(BF), bf


def _in_proj(x2, w, wvt, bf, tables, T, tm):
    N = x2.shape[0]
    nt = T // tm
    tab_spec = pl.BlockSpec((tm, SEG), lambda i: (i % nt, 0))
    fixed = lambda i: (0, 0)
    return pl.pallas_call(
        _in_proj_kernel,
        grid=(N // tm,),
        in_specs=[pl.BlockSpec((tm, D_MODEL), lambda i: (i, 0)),
                  pl.BlockSpec(w.shape, fixed), pl.BlockSpec(wvt.shape, fixed),
                  tab_spec, tab_spec, tab_spec, tab_spec,
                  pl.BlockSpec((1, LANES), fixed)],
        out_specs=[pl.BlockSpec((tm, N_SEG * SEG), lambda i: (i, 0)),
                   pl.BlockSpec((len(V_SEGS) * SEG, tm), lambda i: (0, i)),
                   pl.BlockSpec((tm, LANES), lambda i: (i, 0))],
        out_shape=[jax.ShapeDtypeStruct((N, N_SEG * SEG), BF),
                   jax.ShapeDtypeStruct((len(V_SEGS) * SEG, N), BF),
                   jax.ShapeDtypeStruct((N, LANES), F32)],
        compiler_params=_params("parallel"),
        name="in_proj",
    )(x2, w, wvt, *tables, bf)


def _group_mask(shape, shift, g):
    return (_iota(shape, 1) >> shift) == g


def _stack_groups(q, shift, n_groups):
    zero = jnp.zeros_like(q)
    return jnp.concatenate([jnp.where(_group_mask(q.shape, shift, g), q, zero) for g in range(n_groups)],
                           axis=0)


def _cols(a, g, tq):
    return a[:, g * tq:(g + 1) * tq]


def _head_rows(a, h):
    return a[h * HEAD_DIM:(h + 1) * HEAD_DIM]


ONES_ROWS = 16


def _flash_init(n_groups, tq, n_sets):
    return (jnp.full((1, n_groups * tq), NEG, F32), jnp.zeros((1, n_groups * tq), F32),
            tuple(jnp.zeros((SEG, tq), F32) for _ in range(n_sets)))


def _flash_update(s, carry, vt, tq, head_sets):
    m, l, accs = carry
    m_new = jnp.maximum(m, jnp.max(s, axis=0, keepdims=True))
    alpha = jnp.exp2(m - m_new)
    pb = jnp.exp2(s - m_new).astype(BF)
    ones = jnp.ones((ONES_ROWS, vt.shape[1]), BF)
    new_accs, p_sum = [], {}
    for acc, groups in zip(accs, head_sets):
        parts = []
        for h, g in enumerate(groups):
            r = jnp.dot(jnp.concatenate([_head_rows(vt, h), ones], axis=0), _cols(pb, g, tq),
                        preferred_element_type=F32)
            parts.append(_cols(alpha, g, tq) * _head_rows(acc, h) + r[0:HEAD_DIM])
            p_sum[g] = r[HEAD_DIM:HEAD_DIM + 1]
        new_accs.append(jnp.concatenate(parts, axis=0))
    l = alpha * l + jnp.concatenate([p_sum[g] for g in sorted(p_sum)], axis=1)
    return m_new, l, tuple(new_accs)


def _causal(n_groups, tq):
    shape = (tq, n_groups * tq)
    return _iota(shape, 0) <= (_iota(shape, 1) & (tq - 1))


def _normalised(acc, l, groups, tq):
    return jnp.concatenate([_head_rows(acc, h) * (1.0 / _cols(l, g, tq)) for h, g in enumerate(groups)],
                           axis=0)


def _tile_lanes(a, n):
    return jnp.concatenate([a] * n, axis=1) if n > 1 else a


HEADS = tuple(range(N_HEADS))


def _attn_specs(T, tq, q_seg, k_seg, v_name):
    nq = T // tq
    v_id = V_ID[v_name]
    return [pl.BlockSpec((tq, SEG), lambda b, i: (b * nq + i, q_seg)),
            pl.BlockSpec((T, SEG), lambda b, i: (b, k_seg)),
            pl.BlockSpec((SEG, T), lambda b, i: (v_id, b))]


def _attn_out(N, T, tq):
    nq = T // tq
    return (pl.BlockSpec((tq, SEG), lambda b, i: (b * nq + i, 0)),
            jax.ShapeDtypeStruct((N, SEG), BF))


def _moba_kernel(q_ref, k_ref, vt_ref, o_ref, km_ref, sel_ref, *, tq, nb, n_sel):
    i = pl.program_id(1)

    @pl.when(i == 0)
    def _():
        km_ref[...] = jnp.zeros_like(km_ref)
        for n in range(nb):
            blk = k_ref[n * tq:(n + 1) * tq, :].astype(F32)
            km_ref[n:n + 1, :] = jnp.mean(blk, axis=0, keepdims=True)

    qs = _stack_groups(q_ref[...], 6, N_HEADS)
    km = km_ref[...]
    km_hi = km.astype(BF)
    km_lo = (km - km_hi.astype(F32)).astype(BF)
    nrow = sel_ref.shape[0]
    g = (_dot_nt(km_hi, qs) + _dot_nt(km_lo, qs))[0:nrow]
    blk_id = _iota(g.shape, 0)
    g = jnp.where(blk_id < i, g, -jnp.inf)
    sel = jnp.zeros(g.shape, F32)
    for _ in range(n_sel):
        gmax = jnp.max(g, axis=0, keepdims=True)
        first = jnp.min(jnp.where(g == gmax, blk_id, nrow), axis=0, keepdims=True)
        hit = blk_id == first
        sel = jnp.where(hit & (gmax > -jnp.inf), 1.0, sel)
        g = jnp.where(hit, -jnp.inf, g)
    sel_ref[...] = sel

    def body(c, carry):
        k0 = pl.multiple_of(c * tq, tq)
        picked = sel_ref[pl.ds(c, 1), :] > 0.5
        s = jnp.where(picked, _dot_nt(k_ref[pl.ds(k0, tq), :], qs), NEG)
        return _flash_update(s, carry, vt_ref[:, pl.ds(k0, tq)], tq, (HEADS,))

    carry = lax.fori_loop(0, i, body, _flash_init(N_HEADS, tq, 1))
    k0 = pl.multiple_of(i * tq, tq)
    s = jnp.where(_causal(N_HEADS, tq), _dot_nt(k_ref[pl.ds(k0, tq), :], qs), NEG)
    _, l, (acc,) = _flash_update(s, carry, vt_ref[:, pl.ds(k0, tq)], tq, (HEADS,))
    o_ref[...] = _normalised(acc, l, HEADS, tq).T.astype(BF)


def _moba(qk, vt, T, tq):
    N = qk.shape[0]
    nb = T // MOBA_BLOCK
    assert T % MOBA_BLOCK == 0 and tq == MOBA_BLOCK and nb <= LANES
    sel_rows = -(-nb // SUBLANES) * SUBLANES
    out_spec, out_shape = _attn_out(N, T, tq)
    return pl.pallas_call(
        functools.partial(_moba_kernel, tq=tq, nb=nb, n_sel=min(MOBA_TOPK, nb - 1)),
        grid=(N // T, T // tq),
        in_specs=_attn_specs(T, tq, SEG_ID['moba_q'], SEG_ID['moba_k'], 'moba_v'),
        out_specs=out_spec, out_shape=out_shape,
        scratch_shapes=[pltpu.VMEM((LANES, SEG), F32), pltpu.VMEM((sel_rows, N_HEADS * tq), F32)],
        compiler_params=_params("parallel", "arbitrary"),
        name="moba_attn",
    )(qk, qk, vt)


DIFF_SETS = (tuple(2 * h for h in HEADS), tuple(2 * h + 1 for h in HEADS))


def _diff_kernel(lam_ref, q_ref, k_ref, vt_ref, g_ref, o_ref, *, tq, out_scale):
    i = pl.program_id(1)
    n_groups = 2 * N_HEADS
    qs = _stack_groups(q_ref[...], 5, n_groups)

    def body(c, carry):
        k0 = pl.multiple_of(c * tq, tq)
        s = _dot_nt(k_ref[pl.ds(k0, tq), :], qs)
        return _flash_update(s, carry, vt_ref[:, pl.ds(k0, tq)], tq, DIFF_SETS)

    carry = lax.fori_loop(0, i, body, _flash_init(n_groups, tq, 2))
    k0 = pl.multiple_of(i * tq, tq)
    s = jnp.where(_causal(n_groups, tq), _dot_nt(k_ref[pl.ds(k0, tq), :], qs), NEG)
    _, l, (a1, a2) = _flash_update(s, carry, vt_ref[:, pl.ds(k0, tq)], tq, DIFF_SETS)
    out = _normalised(a1, l, DIFF_SETS[0], tq) - lam_ref[0] * _normalised(a2, l, DIFF_SETS[1], tq)
    normed = []
    for h in HEADS:
        oh = _head_rows(out, h)
        ms = jnp.mean(oh * oh, axis=0, keepdims=True)
        normed.append(oh * lax.rsqrt(ms + LN_EPS))
    o_ref[...] = (jnp.concatenate(normed, axis=0).T * g_ref[...] * out_scale).astype(BF)


def _diff(qk, vt, lam, subln_g, lambda_init, T, tq):
    N = qk.shape[0]
    out_spec, out_shape = _attn_out(N, T, tq)
    g = jnp.tile(subln_g.astype(F32), N_HEADS)[None, :]
    return pl.pallas_call(
        functools.partial(_diff_kernel, tq=tq, out_scale=1.0 - lambda_init),
        grid=(N // T, T // tq),
        in_specs=[pl.BlockSpec(memory_space=pltpu.SMEM)]
        + _attn_specs(T, tq, SEG_ID['diff_q'], SEG_ID['diff_k'], 'diff_v')
        + [pl.BlockSpec((1, SEG), lambda b, i: (0, 0))],
        out_specs=out_spec, out_shape=out_shape,
        compiler_params=_params("parallel", "arbitrary"),
        name="diff_attn",
    )(lam.reshape(1).astype(F32), qk, qk, vt, g)


def _fox_kernel(q_ref, k_ref, vt_ref, misc_ref, o_ref, c_ref, *, tq):
    i = pl.program_id(1)

    @pl.when(i == 0)
    def _():
        tri = jnp.where(_iota((tq, tq), 1) <= _iota((tq, tq), 0), 1.0, 0.0).astype(BF)
        carry = [jnp.zeros((1, LANES), F32) for _ in HEADS]
        for n in range(c_ref.shape[1] // tq):
            blk = misc_ref[n * tq:(n + 1) * tq, :] * LOG2E
            for h in HEADS:
                col = jnp.broadcast_to(blk[:, MISC_LOGF + h:MISC_LOGF + h + 1], blk.shape)
                hi = col.astype(BF)
                rest = col - hi.astype(F32)
                mid = rest.astype(BF)
                lo = (rest - mid.astype(F32)).astype(BF)
                cs = carry[h] + (jnp.dot(tri, hi, preferred_element_type=F32)
                                 + jnp.dot(tri, mid, preferred_element_type=F32)
                                 + jnp.dot(tri, lo, preferred_element_type=F32))
                c_ref[h, n * tq:(n + 1) * tq, :] = cs
                carry[h] = cs[tq - 1:tq, :]

    qs = _stack_groups(q_ref[...], 6, N_HEADS)

    def logits(k0):
        s = _dot_nt(k_ref[pl.ds(k0, tq), :], qs)
        return jnp.concatenate(
            [_cols(s, h, tq) - _tile_lanes(c_ref[h, pl.ds(k0, tq), :], tq // LANES) for h in HEADS], axis=1)

    def body(c, carry):
        k0 = pl.multiple_of(c * tq, tq)
        return _flash_update(logits(k0), carry, vt_ref[:, pl.ds(k0, tq)], tq, (HEADS,))

    carry = lax.fori_loop(0, i, body, _flash_init(N_HEADS, tq, 1))
    k0 = pl.multiple_of(i * tq, tq)
    s = jnp.where(_causal(N_HEADS, tq), logits(k0), NEG)
    _, l, (acc,) = _flash_update(s, carry, vt_ref[:, pl.ds(k0, tq)], tq, (HEADS,))
    o_ref[...] = _normalised(acc, l, HEADS, tq).T.astype(BF)


def _fox(qk, vt, misc, T, tq):
    N = qk.shape[0]
    out_spec, out_shape = _attn_out(N, T, tq)
    return pl.pallas_call(
        functools.partial(_fox_kernel, tq=tq),
        grid=(N // T, T // tq),
        in_specs=_attn_specs(T, tq, SEG_ID['fox_q'], SEG_ID['fox_k'], 'fox_v')
        + [pl.BlockSpec((T, LANES), lambda b, i: (b, 0))],
        out_specs=out_spec, out_shape=out_shape,
        scratch_shapes=[pltpu.VMEM((N_HEADS, T, LANES), F32)],
        compiler_params=_params("parallel", "arbitrary"),
        name="fox_attn",
    )(qk, qk, vt, misc)


COUNT_ROWS = 64

def _dsa_kernel(q_ref, k_ref, vt_ref, qi_ref, ki_ref, w_ref, o_ref, key_ref, bias_ref, *, tq, n_keep):
    i = pl.program_id(1)
    n_ch = i + 1
    qpos = i * tq + _iota((1, tq), 1)

    qis = _stack_groups(qi_ref[...], 6, N_IDX_HEADS)
    wt = w_ref[...].T * (N_IDX_HEADS ** -0.5)
    w_rows = [wt[MISC_IDXW + h:MISC_IDXW + h + 1] for h in range(N_IDX_HEADS)]

    def score_body(c, carry):
        k0 = pl.multiple_of(c * tq, tq)
        d = jnp.maximum(_dot_nt(ki_ref[pl.ds(k0, tq), :], qis), 0.0)
        score = w_rows[0] * _cols(d, 0, tq)
        for h in range(1, N_IDX_HEADS):
            score = score + w_rows[h] * _cols(d, h, tq)
        sc = jnp.where((k0 + _iota((tq, tq), 0)) <= qpos, score, -jnp.inf)
        bits = pltpu.bitcast(sc, jnp.int32)
        key_ref[pl.ds(k0, tq), :] = jnp.where(sc == 0.0, 0,
                                              jnp.where(bits < 0, bits ^ jnp.int32(0x7FFFFFFF), bits))
        return carry

    lax.fori_loop(0, n_ch, score_body, 0)

    @pl.when(n_ch % 2 == 1)
    def _():
        key_ref[pl.ds(pl.multiple_of(n_ch * tq, tq), tq), :] = jnp.full((tq, tq), INT_MIN, jnp.int32)

    def count(pred):
        def body(c, acc):
            k0 = pl.multiple_of(c * 2 * tq, 2 * tq)
            hit = jnp.where(pred(key_ref[pl.ds(k0, 2 * tq), :]), 1.0, 0.0)
            parts = [hit[j * COUNT_ROWS:(j + 1) * COUNT_ROWS] for j in range(2 * tq // COUNT_ROWS)]
            while len(parts) > 1:
                parts = [a + b for a, b in zip(parts[0::2], parts[1::2])]
            return acc + parts[0]
        acc = lax.fori_loop(0, (n_ch + 1) // 2, body, jnp.zeros((COUNT_ROWS, tq), F32))
        return jnp.sum(acc, axis=0, keepdims=True)

    def thr_body(it, thr):
        cand = thr + lax.shift_left(jnp.int32(1), 31 - it)
        return jnp.where(count(lambda kk: kk >= cand) >= n_keep, cand, thr)

    thr = lax.fori_loop(0, 32, thr_body, jnp.full((1, tq), INT_MIN, jnp.int32))

    need = n_keep - count(lambda kk: kk > thr)
    lower = jnp.where(_iota((tq, tq), 1) <= _iota((tq, tq), 0), 1.0, 0.0).astype(BF)

    def tie_body(c, seen):
        k0 = pl.multiple_of(c * tq, tq)
        kk = key_ref[pl.ds(k0, tq), :]
        eq = jnp.where(kk == thr, 1.0, 0.0)
        rank = jnp.dot(lower, eq.astype(BF), preferred_element_type=F32) + seen
        keep = (kk > thr) | ((kk == thr) & (rank <= need))
        causal = (k0 + _iota((tq, tq), 0)) <= qpos
        bias_ref[pl.ds(k0, tq), :] = jnp.where(keep & causal, 0.0, NEG)
        return seen + jnp.sum(eq, axis=0, keepdims=True)

    lax.fori_loop(0, n_ch, tie_body, jnp.zeros((1, tq), F32))

    qs = _stack_groups(q_ref[...], 6, N_HEADS)

    def body(c, carry):
        k0 = pl.multiple_of(c * tq, tq)
        s = _dot_nt(k_ref[pl.ds(k0, tq), :], qs) + _tile_lanes(bias_ref[pl.ds(k0, tq), :], N_HEADS)
        return _flash_update(s, carry, vt_ref[:, pl.ds(k0, tq)], tq, (HEADS,))

    _, l, (acc,) = lax.fori_loop(0, n_ch, body, _flash_init(N_HEADS, tq, 1))
    o_ref[...] = _normalised(acc, l, HEADS, tq).T.astype(BF)


def _dsa(qk, vt, misc, T, tq):
    N = qk.shape[0]
    nq = T // tq
    out_spec, out_shape = _attn_out(N, T, tq)
    return pl.pallas_call(
        functools.partial(_dsa_kernel, tq=tq, n_keep=min(DSA_TOPK_MAX, T // 4)),
        grid=(N // T, nq),
        in_specs=_attn_specs(T, tq, SEG_ID['dsa_q'], SEG_ID['dsa_k'], 'dsa_v')
        + [pl.BlockSpec((tq, SEG), lambda b, i: (b * nq + i, SEG_ID['idx_q'])),
           pl.BlockSpec((T, SEG), lambda b, i: (b, SEG_ID['idx_k4'])),
           pl.BlockSpec((tq, LANES), lambda b, i: (b * nq + i, 0))],
        out_specs=out_spec, out_shape=out_shape,
        scratch_shapes=[pltpu.VMEM(((nq + nq % 2) * tq, tq), jnp.int32), pltpu.VMEM((T, tq), F32)],
        compiler_params=_params("parallel", "arbitrary"),
        name="dsa_attn",
    )(qk, qk, vt, qk, qk, misc)


def _mix_out_kernel(x_ref, oa_ref, ob_ref, oc_ref, od_ref, wg_ref, wb_ref, wo_ref, g_ref, b_ref,
                    y_ref, yb_ref):
    x = x_ref[...]
    xb = x.astype(BF)
    merged = jnp.zeros(x.shape, F32)
    for n, o_ref in enumerate((oa_ref, ob_ref, oc_ref, od_ref)):
        gate = _sigmoid(jnp.dot(xb, wg_ref[:, n * D_MODEL:(n + 1) * D_MODEL], preferred_element_type=F32))
        merged = merged + gate * jnp.dot(o_ref[...], wb_ref[n], preferred_element_type=F32)
    h = jnp.dot(merged.astype(BF), wo_ref[...], preferred_element_type=F32)
    y = _layer_norm(DEEPNORM_ALPHA * x + h, g_ref[...], b_ref[...])
    y_ref[...] = y
    yb_ref[...] = y.astype(BF)


def _mix_out(x2, branches, w_gates, w_branch, w_out, ln_g, ln_b, tm):
    N = x2.shape[0]
    row = lambda i: (i, 0)
    fixed2 = lambda i: (0, 0)
    return pl.pallas_call(
        _mix_out_kernel,
        grid=(N // tm,),
        in_specs=[pl.BlockSpec((tm, D_MODEL), row)] + [pl.BlockSpec((tm, SEG), row)] * N_BRANCHES
        + [pl.BlockSpec(w_gates.shape, fixed2),
           pl.BlockSpec(w_branch.shape, lambda i: (0, 0, 0)),
           pl.BlockSpec(w_out.shape, fixed2),
           pl.BlockSpec((1, D_MODEL), fixed2), pl.BlockSpec((1, D_MODEL), fixed2)],
        out_specs=[pl.BlockSpec((tm, D_MODEL), row), pl.BlockSpec((tm, D_MODEL), row)],
        out_shape=[jax.ShapeDtypeStruct((N, D_MODEL), F32), jax.ShapeDtypeStruct((N, D_MODEL), BF)],
        compiler_params=_params("parallel"),
        name="mix_out",
    )(x2, *branches, w_gates, w_branch.astype(BF), w_out.astype(BF),
      ln_g.astype(F32)[None, :], ln_b.astype(F32)[None, :])


def _split_bf16(a):
    hi = a.astype(BF)
    return hi, (a - hi.astype(F32)).astype(BF)


def _router_kernel(x_ref, whi_ref, wlo_ref, rb_ref, idx_ref, wsel_ref):
    xhi, xlo = _split_bf16(x_ref[...])
    whi = whi_ref[...]
    logits = _dot_nt(whi, xhi) + _dot_nt(whi, xlo) + _dot_nt(wlo_ref[...], xhi)
    tm = logits.shape[1]
    scores = _sigmoid(logits)
    biased = scores + _tile_lanes(rb_ref[...], tm // LANES)
    per_group = N_EXPERTS // N_GROUPS
    gs = []
    for g in range(N_GROUPS):
        bg = biased[g * per_group:(g + 1) * per_group]
        row = _iota(bg.shape, 0)
        m1 = jnp.max(bg, axis=0, keepdims=True)
        i1 = jnp.min(jnp.where(bg == m1, row, per_group), axis=0, keepdims=True)
        m2 = jnp.max(jnp.where(row == i1, -jnp.inf, bg), axis=0, keepdims=True)
        gs.append(m1 + m2)
    kept = []
    for g in range(N_GROUPS):
        rank = jnp.zeros((1, tm), F32)
        for o in range(N_GROUPS):
            if o != g:
                beats = (gs[o] >= gs[g]) if o < g else (gs[o] > gs[g])
                rank = rank + jnp.where(beats, 1.0, 0.0)
        kept.append(jnp.where(rank < TOPK_GROUPS, biased[g * per_group:(g + 1) * per_group], -jnp.inf))
    masked = jnp.concatenate(kept, axis=0)
    eid = _iota(masked.shape, 0)
    picks, weights = [], []
    for _ in range(TOP_K):
        mx = jnp.max(masked, axis=0, keepdims=True)
        pick = jnp.min(jnp.where(masked == mx, eid, N_EXPERTS), axis=0, keepdims=True)
        hit = eid == pick
        weights.append(jnp.sum(jnp.where(hit, scores, 0.0), axis=0, keepdims=True))
        masked = jnp.where(hit, -jnp.inf, masked)
        picks.append(pick)
    wsum = weights[0]
    for wk in weights[1:]:
        wsum = wsum + wk
    idx_ref[...] = jnp.concatenate(picks, axis=0)
    wsel_ref[...] = jnp.concatenate(weights, axis=0) / wsum * ROUTED_SCALE


def _router(x1, w_router, router_bias, tm):
    N = x1.shape[0]
    whi, wlo = _split_bf16(w_router.astype(F32).T)
    rb = jnp.broadcast_to(router_bias.astype(F32)[:, None], (N_EXPERTS, LANES))
    fixed = lambda i: (0, 0)
    col = lambda i: (0, i)
    return pl.pallas_call(
        _router_kernel,
        grid=(N // tm,),
        in_specs=[pl.BlockSpec((tm, D_MODEL), lambda i: (i, 0)), pl.BlockSpec(whi.shape, fixed),
                  pl.BlockSpec(wlo.shape, fixed), pl.BlockSpec(rb.shape, fixed)],
        out_specs=[pl.BlockSpec((TOP_K, tm), col), pl.BlockSpec((TOP_K, tm), col)],
        out_shape=[jax.ShapeDtypeStruct((TOP_K, N), jnp.int32), jax.ShapeDtypeStruct((TOP_K, N), F32)],
        compiler_params=_params("parallel"),
        name="router",
    )(x1, whi, wlo, rb)


def _expert_kernel(blk_ref, exp_ref, lo_ref, hi_ref, first_ref, newe_ref,
                   xs_ref, wg_ref, wu_ref, wd_ref, y_ref, wgb_ref, wub_ref, wdb_ref, *, bm):
    t = pl.program_id(0)
    lo, hi = lo_ref[t], hi_ref[t]

    @pl.when(newe_ref[t] == 1)
    def _():
        wgb_ref[...] = wg_ref[...].astype(BF)
        wub_ref[...] = wu_ref[...].astype(BF)
        wdb_ref[...] = wd_ref[...].astype(BF)

    @pl.when(first_ref[t] == 1)
    def _():
        y_ref[...] = jnp.zeros_like(y_ref)

    @pl.when(hi > lo)
    def _():
        xs = xs_ref[...]
        g = jnp.dot(xs, wgb_ref[...], preferred_element_type=F32)
        u = jnp.dot(xs, wub_ref[...], preferred_element_type=F32)
        h = (g * _sigmoid(g) * u).astype(BF)
        y = jnp.dot(h, wdb_ref[...], preferred_element_type=F32)
        row = blk_ref[t] * bm + _iota((bm, 1), 0)
        mine = (row >= lo) & (row < hi)
        y_ref[...] = jnp.where(mine, y.astype(y_ref.dtype), y_ref[...])


def _experts(xs, plan, layer, w_gate, w_up, w_down, bm):
    A = xs.shape[0]
    n_items = plan[0].shape[0]
    row = lambda t, blk, *_: (blk[t], 0)
    wspec = lambda shape: pl.BlockSpec((None, None) + shape, lambda t, blk, exp, *_: (layer, exp[t], 0, 0))
    return pl.pallas_call(
        functools.partial(_expert_kernel, bm=bm),
        grid_spec=pltpu.PrefetchScalarGridSpec(
            num_scalar_prefetch=len(plan),
            grid=(n_items,),
            in_specs=[pl.BlockSpec((bm, D_MODEL), row),
                      wspec((D_MODEL, EXPERT_DIM)), wspec((D_MODEL, EXPERT_DIM)),
                      wspec((EXPERT_DIM, D_MODEL))],
            out_specs=pl.BlockSpec((bm, D_MODEL), row),
            scratch_shapes=[pltpu.VMEM((D_MODEL, EXPERT_DIM), BF), pltpu.VMEM((D_MODEL, EXPERT_DIM), BF),
                            pltpu.VMEM((EXPERT_DIM, D_MODEL), BF)],
        ),
        out_shape=jax.ShapeDtypeStruct((A, D_MODEL), BF),
        compiler_params=_params("arbitrary"),
        name="experts",
    )(*plan, xs, w_gate, w_up, w_down)


def _dispatch_plan(eidx_t, bm):
    N = eidx_t.shape[1]
    A, E = N * TOP_K, N_EXPERTS
    assert A % bm == 0
    ids = jnp.arange(A, dtype=jnp.int32)
    se, sid = lax.sort((eidx_t.reshape(A), ids), num_keys=1, is_stable=True)
    _, pos = lax.sort((sid, ids), num_keys=1)
    n_blk = A // bm
    start = jnp.searchsorted(se, jnp.arange(E, dtype=jnp.int32), side='left').astype(jnp.int32)
    lo = jnp.sort(jnp.concatenate([jnp.arange(n_blk, dtype=jnp.int32) * bm, start]))
    hi = jnp.concatenate([lo[1:], jnp.full((1,), A, jnp.int32)])
    blk = jnp.minimum(lo // bm, n_blk - 1)
    exp = se[jnp.minimum(lo, A - 1)]
    changed = lambda a: jnp.concatenate([jnp.ones((1,), jnp.int32), (a[1:] != a[:-1]).astype(jnp.int32)])
    plan = (blk, exp, lo, hi, changed(blk), changed(exp))
    return sid % N, pos.reshape(TOP_K, N), plan


def _moe_out_kernel(x_ref, r_ref, rw_ref, wg_ref, wu_ref, wd_ref, g_ref, b_ref, y_ref):
    x = x_ref[...]
    tm = x.shape[0]
    xb = x.astype(BF)
    g = jnp.dot(xb, wg_ref[...], preferred_element_type=F32)
    u = jnp.dot(xb, wu_ref[...], preferred_element_type=F32)
    h = (g * _sigmoid(g) * u).astype(BF)
    shared = jnp.dot(h, wd_ref[...], preferred_element_type=F32)
    rw = jnp.concatenate([rw_ref[...], jnp.zeros((LANES - TOP_K, tm), F32)], axis=0).T
    routed = rw[:, 0:1] * r_ref[0].astype(F32)
    for k in range(1, TOP_K):
        routed = routed + rw[:, k:k + 1] * r_ref[k].astype(F32)
    y_ref[...] = _layer_norm(DEEPNORM_ALPHA * x + (routed + shared), g_ref[...], b_ref[...])


def _moe_out(x1, routed, route_w, w_sh_gate, w_sh_up, w_sh_down, ln_g, ln_b, tm):
    N = x1.shape[0]
    row = lambda i: (i, 0)
    fixed = lambda i: (0, 0)
    return pl.pallas_call(
        _moe_out_kernel,
        grid=(N // tm,),
        in_specs=[pl.BlockSpec((tm, D_MODEL), row), pl.BlockSpec((TOP_K, tm, D_MODEL), lambda i: (0, i, 0)),
                  pl.BlockSpec((TOP_K, tm), lambda i: (0, i)),
                  pl.BlockSpec(w_sh_gate.shape, fixed), pl.BlockSpec(w_sh_up.shape, fixed),
                  pl.BlockSpec(w_sh_down.shape, fixed),
                  pl.BlockSpec((1, D_MODEL), fixed), pl.BlockSpec((1, D_MODEL), fixed)],
        out_specs=pl.BlockSpec((tm, D_MODEL), row),
        out_shape=jax.ShapeDtypeStruct((N, D_MODEL), F32),
        compiler_params=_params("parallel"),
        name="moe_out",
    )(x1, routed, route_w, w_sh_gate.astype(BF), w_sh_up.astype(BF), w_sh_down.astype(BF),
      ln_g.astype(F32)[None, :], ln_b.astype(F32)[None, :])


def _tiles(N, T):
    return min(512, T), min(256, T), MOBA_BLOCK, min(512, N * TOP_K)


def _mixer_layer(x2, T, w_in, b_forget, diff_lambda, diff_subln, w_branch, w_out, ln_g, ln_b, lambda_init):
    N = x2.shape[0]
    tm_proj, tm_row, tq, _ = _tiles(N, T)
    w, wvt, w_gates, bf = _in_proj_weights(w_in, b_forget)
    qk, vt, misc = _in_proj(x2, w, wvt, bf, _rope_tables(T), T, tm_proj)
    dl = diff_lambda.astype(F32)
    lam = jnp.exp(jnp.sum(dl[0] * dl[1])) - jnp.exp(jnp.sum(dl[2] * dl[3])) + lambda_init
    o_a = _moba(qk, vt, T, tq)
    o_b = _diff(qk, vt, lam, diff_subln, lambda_init, T, tq)
    o_c = _fox(qk, vt, misc, T, tq)
    o_d = _dsa(qk, vt, misc, T, tq)
    return _mix_out(x2, (o_a, o_b, o_c, o_d), w_gates, w_branch, w_out, ln_g, ln_b, tm_row)


def _moe_layer(x1, x1b, T, layer, w_router, router_bias, w_exp_gate, w_exp_up, w_exp_down,
               w_sh_gate, w_sh_up, w_sh_down, ln_g, ln_b):
    N = x1.shape[0]
    _, tm_row, _, bm = _tiles(N, T)
    eidx_t, wsel_t = _router(x1, w_router, router_bias, tm_row)
    row_tok, pos, plan = _dispatch_plan(eidx_t, bm)
    ys = _experts(x1b[row_tok], plan, layer, w_exp_gate, w_exp_up, w_exp_down, bm)
    return _moe_out(x1, ys[pos], wsel_t, w_sh_gate, w_sh_up, w_sh_down, ln_g, ln_b, tm_row)


def kernel(x, w_in, b_forget, diff_lambda, diff_subln, w_branch, w_out, ln1_g, ln1_b, w_router, router_bias,
           w_exp_gate, w_exp_up, w_exp_down, w_sh_gate, w_sh_up, w_sh_down, ln2_g, ln2_b):
    B, T, D = x.shape
    x2 = x.reshape(B * T, D)
    for l in range(DEPTH):
        lambda_init = 0.8 - 0.6 * math.exp(-0.3 * l)
        x1, x1b = _mixer_layer(x2, T, w_in[l], b_forget[l], diff_lambda[l], diff_subln[l], w_branch[l],
                               w_out[l], ln1_g[l], ln1_b[l], lambda_init)
        x2 = _moe_layer(x1, x1b, T, l, w_router[l], router_bias[l], w_exp_gate, w_exp_up, w_exp_down,
                        w_sh_gate[l], w_sh_up[l], w_sh_down[l], ln2_g[l], ln2_b[l])
    return x2.reshape(B, T, D)
```

```python
import functools
import math

import jax
import jax.numpy as jnp
from jax import lax
from jax.experimental import pallas as pl
from jax.experimental.pallas import tpu as pltpu

F32 = jnp.float32
BF = jnp.bfloat16

D_MODEL = 1024
DEPTH = 2
HEAD_DIM = 64
N_HEADS = 4
DIFF_DIM = HEAD_DIM // 2
N_IDX_HEADS = 4
IDX_DIM = 64
BRANCH_WIDTH = N_HEADS * HEAD_DIM
N_BRANCHES = 4
MOBA_BLOCK = 256
MOBA_TOPK = 3
DSA_TOPK_MAX = 256
ROPE_THETA = 10000.0
N_EXPERTS = 256
TOP_K = 8
N_GROUPS = 8
TOPK_GROUPS = 4
EXPERT_DIM = 256
ROUTED_SCALE = 2.5
LN_EPS = 1e-5
DEEPNORM_ALPHA = (2 * DEPTH) ** 0.25

IN_SEGMENTS = (
    ('moba_q', BRANCH_WIDTH), ('moba_k', BRANCH_WIDTH), ('moba_v', BRANCH_WIDTH),
    ('diff_q', BRANCH_WIDTH), ('diff_k', BRANCH_WIDTH), ('diff_v', BRANCH_WIDTH),
    ('fox_q', BRANCH_WIDTH), ('fox_k', BRANCH_WIDTH), ('fox_v', BRANCH_WIDTH), ('fox_f', N_HEADS),
    ('dsa_q', BRANCH_WIDTH), ('dsa_k', BRANCH_WIDTH), ('dsa_v', BRANCH_WIDTH),
    ('idx_q', N_IDX_HEADS * IDX_DIM), ('idx_k', IDX_DIM), ('idx_w', N_IDX_HEADS),
    ('gates', N_BRANCHES * D_MODEL),
)

LANES = 128
SUBLANES = 8
SEG = BRANCH_WIDTH
NEG = -1e30
INT_MIN = -2 ** 31
VMEM_LIMIT = 48 * 1024 * 1024

LOG2E = math.log2(math.e)
_QSCALE = HEAD_DIM ** -0.5 * LOG2E
PROJ_SEGS = (
    ('moba_q', 64, _QSCALE), ('moba_k', 64, 1.0),
    ('diff_q', 32, DIFF_DIM ** -0.5 * LOG2E), ('diff_k', 32, 1.0),
    ('fox_q', 0, _QSCALE), ('fox_k', 0, 1.0),
    ('dsa_q', 64, _QSCALE), ('dsa_k', 64, 1.0),
    ('idx_q', 64, IDX_DIM ** -0.5), ('idx_k4', 64, 1.0),
)
SEG_ID = {name: i for i, (name, _, _) in enumerate(PROJ_SEGS)}
N_SEG = len(PROJ_SEGS)
V_SEGS = ('moba_v', 'diff_v', 'fox_v', 'dsa_v')
V_ID = {name: i for i, name in enumerate(V_SEGS)}
MISC_LOGF = 0
MISC_IDXW = 4


def _params(*sem):
    return pltpu.CompilerParams(dimension_semantics=sem, vmem_limit_bytes=VMEM_LIMIT)


def _iota(shape, dim):
    return lax.broadcasted_iota(jnp.int32, shape, dim)


def _dot_nt(a, b):
    return lax.dot_general(a, b, (((1,), (1,)), ((), ())), preferred_element_type=F32)


def _sigmoid(z):
    return 1.0 / (1.0 + jnp.exp(-z))


def _layer_norm(y, g, b):
    mu = jnp.mean(y, axis=-1, keepdims=True)
    yc = y - mu
    var = jnp.mean(yc * yc, axis=-1, keepdims=True)
    return yc * lax.rsqrt(var + LN_EPS) * g + b


def _swap_halves(a, half):
    w = a.shape[-1]
    first = (_iota(a.shape, 1) & (2 * half - 1)) < half
    return jnp.where(first, pltpu.roll(a, w - half, 1), pltpu.roll(a, half, 1))


def _in_proj_kernel(x_ref, w_ref, wvt_ref, c64_ref, s64_ref, c32_ref, s32_ref, bf_ref,
                    qk_ref, vt_ref, misc_ref):
    xb = x_ref[...].astype(BF)
    for s, (_, rot, scale) in enumerate(PROJ_SEGS):
        acc = jnp.dot(xb, w_ref[:, s * SEG:(s + 1) * SEG], preferred_element_type=F32)
        if rot == 64:
            acc = acc * c64_ref[...] + _swap_halves(acc, 32) * s64_ref[...]
        elif rot == 32:
            acc = acc * c32_ref[...] + _swap_halves(acc, 16) * s32_ref[...]
        if scale != 1.0:
            acc = acc * scale
        qk_ref[:, s * SEG:(s + 1) * SEG] = acc.astype(BF)
    vt_ref[...] = _dot_nt(wvt_ref[...], xb).astype(BF)
    m = jnp.dot(xb, w_ref[:, N_SEG * SEG:N_SEG * SEG + LANES], preferred_element_type=F32)
    z = m + bf_ref[...]
    logf = jnp.minimum(z, 0.0) - jnp.log1p(jnp.exp(-jnp.abs(z)))
    lane = _iota(m.shape, 1)
    misc_ref[...] = jnp.where(lane < MISC_IDXW, logf, m)


def _rope_tables(T):
    pos = jnp.arange(T).astype(F32)

    def tab(group, reps):
        half = group // 2
        inv_freq = ROPE_THETA ** (-jnp.arange(half, dtype=F32) / half)
        ang = pos[:, None] * inv_freq[None, :]
        cos, sin = jnp.cos(ang), jnp.sin(ang)
        return (jnp.tile(jnp.concatenate([cos, cos], -1), (1, reps)),
                jnp.tile(jnp.concatenate([-sin, sin], -1), (1, reps)))

    return tab(64, SEG // 64) + tab(32, SEG // 32)


def _in_proj_weights(w_in, b_forget):
    parts, off = {}, 0
    for name, width in IN_SEGMENTS:
        parts[name] = w_in[:, off:off + width]
        off += width
    parts['idx_k4'] = jnp.tile(parts['idx_k'], (1, N_IDX_HEADS))
    misc = jnp.concatenate([parts['fox_f'], parts['idx_w'],
                            jnp.zeros((D_MODEL, LANES - 2 * N_HEADS), w_in.dtype)], axis=1)
    w = jnp.concatenate([parts[name] for name, _, _ in PROJ_SEGS] + [misc], axis=1).astype(BF)
    wvt = jnp.concatenate([parts[name] for name in V_SEGS], axis=1).T.astype(BF)
    bf = jnp.zeros((1, LANES), F32).at[0, MISC_LOGF:MISC_LOGF + N_HEADS].set(b_forget.astype(F32))
    return w, wvt, parts['gates'].astype(BF), bf


def _in_proj(x2, w, wvt, bf, tables, T, tm):
    N = x2.shape[0]
    nt = T // tm
    tab_spec = pl.BlockSpec((tm, SEG), lambda i: (i % nt, 0))
    fixed = lambda i: (0, 0)
    return pl.pallas_call(
        _in_proj_kernel,
        grid=(N // tm,),
        in_specs=[pl.BlockSpec((tm, D_MODEL), lambda i: (i, 0)),
                  pl.BlockSpec(w.shape, fixed), pl.BlockSpec(wvt.shape, fixed),
                  tab_spec, tab_spec, tab_spec, tab_spec,
                  pl.BlockSpec((1, LANES), fixed)],
        out_specs=[pl.BlockSpec((tm, N_SEG * SEG), lambda i: (i, 0)),
                   pl.BlockSpec((len(V_SEGS) * SEG, tm), lambda i: (0, i)),
                   pl.BlockSpec((tm, LANES), lambda i: (i, 0))],
        out_shape=[jax.ShapeDtypeStruct((N, N_SEG * SEG), BF),
                   jax.ShapeDtypeStruct((len(V_SEGS) * SEG, N), BF),
                   jax.ShapeDtypeStruct((N, LANES), F32)],
        compiler_params=_params("parallel"),
        name="in_proj",
    )(x2, w, wvt, *tables, bf)


def _group_mask(shape, shift, g):
    return (_iota(shape, 1) >> shift) == g


def _stack_groups(q, shift, n_groups):
    zero = jnp.zeros_like(q)
    return jnp.concatenate([jnp.where(_group_mask(q.shape, shift, g), q, zero) for g in range(n_groups)],
                           axis=0)


def _cols(a, g, tq):
    return a[:, g * tq:(g + 1) * tq]


def _head_rows(a, h):
    return a[h * HEAD_DIM:(h + 1) * HEAD_DIM]


ONES_ROWS = 16


def _flash_init(n_groups, tq, n_sets):
    return (jnp.full((1, n_groups * tq), NEG, F32), jnp.zeros((1, n_groups * tq), F32),
            tuple(jnp.zeros((SEG, tq), F32) for _ in range(n_sets)))


def _flash_update(s, carry, vt, tq, head_sets):
    m, l, accs = carry
    m_new = jnp.maximum(m, jnp.max(s, axis=0, keepdims=True))
    alpha = jnp.exp2(m - m_new)
    pb = jnp.exp2(s - m_new).astype(BF)
    ones = jnp.ones((ONES_ROWS, vt.shape[1]), BF)
    new_accs, p_sum = [], {}
    for acc, groups in zip(accs, head_sets):
        parts = []
        for h, g in enumerate(groups):
            r = jnp.dot(jnp.concatenate([_head_rows(vt, h), ones], axis=0), _cols(pb, g, tq),
                        preferred_element_type=F32)
            parts.append(_cols(alpha, g, tq) * _head_rows(acc, h) + r[0:HEAD_DIM])
            p_sum[g] = r[HEAD_DIM:HEAD_DIM + 1]
        new_accs.append(jnp.concatenate(parts, axis=0))
    l = alpha * l + jnp.concatenate([p_sum[g] for g in sorted(p_sum)], axis=1)
    return m_new, l, tuple(new_accs)


def _causal(n_groups, tq):
    shape = (tq, n_groups * tq)
    return _iota(shape, 0) <= (_iota(shape, 1) & (tq - 1))


def _normalised(acc, l, groups, tq):
    return jnp.concatenate([_head_rows(acc, h) * (1.0 / _cols(l, g, tq)) for h, g in enumerate(groups)],
                           axis=0)


def _tile_lanes(a, n):
    return jnp.concatenate([a] * n, axis=1) if n > 1 else a


HEADS = tuple(range(N_HEADS))


def _flash_attend(i, tq, n_groups, head_sets, qs, k_ref, vt_ref, s_ref, post, last):
    def raw(k0):
        return _dot_nt(k_ref[pl.ds(k0, tq), :], qs)

    s_ref[0] = raw(0)

    def body(c, carry):
        k0 = pl.multiple_of(c * tq, tq)
        s = post(s_ref[c & 1], c, k0)
        s_ref[(c + 1) & 1] = raw(pl.multiple_of(k0 + tq, tq))
        return _flash_update(s, carry, vt_ref[:, pl.ds(k0, tq)], tq, head_sets)

    carry = lax.fori_loop(0, i, body, _flash_init(n_groups, tq, len(head_sets)))
    k0 = pl.multiple_of(i * tq, tq)
    _, l, accs = _flash_update(last(s_ref[i & 1], k0), carry, vt_ref[:, pl.ds(k0, tq)], tq, head_sets)
    return l, accs


def _logit_scratch(n_groups, tq):
    return pltpu.VMEM((2, tq, n_groups * tq), F32)


def _attn_specs(T, tq, q_seg, k_seg, v_name):
    nq = T // tq
    v_id = V_ID[v_name]
    return [pl.BlockSpec((tq, SEG), lambda b, i: (b * nq + i, q_seg)),
            pl.BlockSpec((T, SEG), lambda b, i: (b, k_seg)),
            pl.BlockSpec((SEG, T), lambda b, i: (v_id, b))]


def _attn_out(N, T, tq):
    nq = T // tq
    return (pl.BlockSpec((tq, SEG), lambda b, i: (b * nq + i, 0)),
            jax.ShapeDtypeStruct((N, SEG), BF))


def _moba_kernel(q_ref, k_ref, vt_ref, o_ref, km_ref, sel_ref, s_ref, *, tq, nb, n_sel):
    i = pl.program_id(1)

    @pl.when(i == 0)
    def _():
        km_ref[...] = jnp.zeros_like(km_ref)
        for n in range(nb):
            blk = k_ref[n * tq:(n + 1) * tq, :].astype(F32)
            km_ref[n:n + 1, :] = jnp.mean(blk, axis=0, keepdims=True)

    qs = _stack_groups(q_ref[...], 6, N_HEADS)
    km = km_ref[...]
    km_hi = km.astype(BF)
    km_lo = (km - km_hi.astype(F32)).astype(BF)
    nrow = sel_ref.shape[0]
    g = (_dot_nt(km_hi, qs) + _dot_nt(km_lo, qs))[0:nrow]
    blk_id = _iota(g.shape, 0)
    g = jnp.where(blk_id < i, g, -jnp.inf)
    sel = jnp.zeros(g.shape, F32)
    for _ in range(n_sel):
        gmax = jnp.max(g, axis=0, keepdims=True)
        first = jnp.min(jnp.where(g == gmax, blk_id, nrow), axis=0, keepdims=True)
        hit = blk_id == first
        sel = jnp.where(hit & (gmax > -jnp.inf), 1.0, sel)
        g = jnp.where(hit, -jnp.inf, g)
    sel_ref[...] = sel

    l, (acc,) = _flash_attend(
        i, tq, N_HEADS, (HEADS,), qs, k_ref, vt_ref, s_ref,
        post=lambda s, c, k0: jnp.where(sel_ref[pl.ds(c, 1), :] > 0.5, s, NEG),
        last=lambda s, k0: jnp.where(_causal(N_HEADS, tq), s, NEG))
    o_ref[...] = _normalised(acc, l, HEADS, tq).T.astype(BF)


def _moba(qk, vt, T, tq):
    N = qk.shape[0]
    nb = T // MOBA_BLOCK
    assert T % MOBA_BLOCK == 0 and tq == MOBA_BLOCK and nb <= LANES
    sel_rows = -(-nb // SUBLANES) * SUBLANES
    out_spec, out_shape = _attn_out(N, T, tq)
    return pl.pallas_call(
        functools.partial(_moba_kernel, tq=tq, nb=nb, n_sel=min(MOBA_TOPK, nb - 1)),
        grid=(N // T, T // tq),
        in_specs=_attn_specs(T, tq, SEG_ID['moba_q'], SEG_ID['moba_k'], 'moba_v'),
        out_specs=out_spec, out_shape=out_shape,
        scratch_shapes=[pltpu.VMEM((LANES, SEG), F32), pltpu.VMEM((sel_rows, N_HEADS * tq), F32),
                        _logit_scratch(N_HEADS, tq)],
        compiler_params=_params("parallel", "arbitrary"),
        name="moba_attn",
    )(qk, qk, vt)


DIFF_SETS = (tuple(2 * h for h in HEADS), tuple(2 * h + 1 for h in HEADS))


def _diff_kernel(lam_ref, q_ref, k_ref, vt_ref, g_ref, o_ref, s_ref, *, tq, out_scale):
    i = pl.program_id(1)
    n_groups = 2 * N_HEADS
    qs = _stack_groups(q_ref[...], 5, n_groups)
    l, (a1, a2) = _flash_attend(
        i, tq, n_groups, DIFF_SETS, qs, k_ref, vt_ref, s_ref,
        post=lambda s, c, k0: s,
        last=lambda s, k0: jnp.where(_causal(n_groups, tq), s, NEG))
    out = _normalised(a1, l, DIFF_SETS[0], tq) - lam_ref[0] * _normalised(a2, l, DIFF_SETS[1], tq)
    normed = []
    for h in HEADS:
        oh = _head_rows(out, h)
        ms = jnp.mean(oh * oh, axis=0, keepdims=True)
        normed.append(oh * lax.rsqrt(ms + LN_EPS))
    o_ref[...] = (jnp.concatenate(normed, axis=0).T * g_ref[...] * out_scale).astype(BF)


def _diff(qk, vt, lam, subln_g, lambda_init, T, tq):
    N = qk.shape[0]
    out_spec, out_shape = _attn_out(N, T, tq)
    g = jnp.tile(subln_g.astype(F32), N_HEADS)[None, :]
    return pl.pallas_call(
        functools.partial(_diff_kernel, tq=tq, out_scale=1.0 - lambda_init),
        grid=(N // T, T // tq),
        in_specs=[pl.BlockSpec(memory_space=pltpu.SMEM)]
        + _attn_specs(T, tq, SEG_ID['diff_q'], SEG_ID['diff_k'], 'diff_v')
        + [pl.BlockSpec((1, SEG), lambda b, i: (0, 0))],
        out_specs=out_spec, out_shape=out_shape,
        scratch_shapes=[_logit_scratch(2 * N_HEADS, tq)],
        compiler_params=_params("parallel", "arbitrary"),
        name="diff_attn",
    )(lam.reshape(1).astype(F32), qk, qk, vt, g)


def _fox_kernel(q_ref, k_ref, vt_ref, misc_ref, o_ref, c_ref, s_ref, *, tq):
    i = pl.program_id(1)

    @pl.when(i == 0)
    def _():
        tri = jnp.where(_iota((tq, tq), 1) <= _iota((tq, tq), 0), 1.0, 0.0).astype(BF)
        carry = [jnp.zeros((1, LANES), F32) for _ in HEADS]
        for n in range(c_ref.shape[1] // tq):
            blk = misc_ref[n * tq:(n + 1) * tq, :] * LOG2E
            for h in HEADS:
                col = jnp.broadcast_to(blk[:, MISC_LOGF + h:MISC_LOGF + h + 1], blk.shape)
                hi = col.astype(BF)
                rest = col - hi.astype(F32)
                mid = rest.astype(BF)
                lo = (rest - mid.astype(F32)).astype(BF)
                cs = carry[h] + (jnp.dot(tri, hi, preferred_element_type=F32)
                                 + jnp.dot(tri, mid, preferred_element_type=F32)
                                 + jnp.dot(tri, lo, preferred_element_type=F32))
                c_ref[h, n * tq:(n + 1) * tq, :] = cs
                carry[h] = cs[tq - 1:tq, :]

    qs = _stack_groups(q_ref[...], 6, N_HEADS)

    def decayed(s, k0):
        return jnp.concatenate(
            [_cols(s, h, tq) - _tile_lanes(c_ref[h, pl.ds(k0, tq), :], tq // LANES) for h in HEADS], axis=1)

    l, (acc,) = _flash_attend(
        i, tq, N_HEADS, (HEADS,), qs, k_ref, vt_ref, s_ref,
        post=lambda s, c, k0: decayed(s, k0),
        last=lambda s, k0: jnp.where(_causal(N_HEADS, tq), decayed(s, k0), NEG))
    o_ref[...] = _normalised(acc, l, HEADS, tq).T.astype(BF)


def _fox(qk, vt, misc, T, tq):
    N = qk.shape[0]
    out_spec, out_shape = _attn_out(N, T, tq)
    return pl.pallas_call(
        functools.partial(_fox_kernel, tq=tq),
        grid=(N // T, T // tq),
        in_specs=_attn_specs(T, tq, SEG_ID['fox_q'], SEG_ID['fox_k'], 'fox_v')
        + [pl.BlockSpec((T, LANES), lambda b, i: (b, 0))],
        out_specs=out_spec, out_shape=out_shape,
        scratch_shapes=[pltpu.VMEM((N_HEADS, T, LANES), F32), _logit_scratch(N_HEADS, tq)],
        compiler_params=_params("parallel", "arbitrary"),
        name="fox_attn",
    )(qk, qk, vt, misc)


COUNT_ROWS = 64

def _dsa_kernel(q_ref, k_ref, vt_ref, qi_ref, ki_ref, w_ref, o_ref, key_ref, bias_ref, s_ref,
                *, tq, n_keep):
    i = pl.program_id(1)
    n_ch = i + 1
    qpos = i * tq + _iota((1, tq), 1)

    qis = _stack_groups(qi_ref[...], 6, N_IDX_HEADS)
    wt = w_ref[...].T * (N_IDX_HEADS ** -0.5)
    w_rows = [wt[MISC_IDXW + h:MISC_IDXW + h + 1] for h in range(N_IDX_HEADS)]

    def score_body(c, carry):
        k0 = pl.multiple_of(c * tq, tq)
        d = jnp.maximum(_dot_nt(ki_ref[pl.ds(k0, tq), :], qis), 0.0)
        score = w_rows[0] * _cols(d, 0, tq)
        for h in range(1, N_IDX_HEADS):
            score = score + w_rows[h] * _cols(d, h, tq)
        sc = jnp.where((k0 + _iota((tq, tq), 0)) <= qpos, score, -jnp.inf)
        bits = pltpu.bitcast(sc, jnp.int32)
        key_ref[pl.ds(k0, tq), :] = jnp.where(sc == 0.0, 0,
                                              jnp.where(bits < 0, bits ^ jnp.int32(0x7FFFFFFF), bits))
        return carry

    lax.fori_loop(0, n_ch, score_body, 0)

    @pl.when(n_ch % 2 == 1)
    def _():
        key_ref[pl.ds(pl.multiple_of(n_ch * tq, tq), tq), :] = jnp.full((tq, tq), INT_MIN, jnp.int32)

    def count(pred):
        def body(c, acc):
            k0 = pl.multiple_of(c * 2 * tq, 2 * tq)
            hit = jnp.where(pred(key_ref[pl.ds(k0, 2 * tq), :]), 1.0, 0.0)
            parts = [hit[j * COUNT_ROWS:(j + 1) * COUNT_ROWS] for j in range(2 * tq // COUNT_ROWS)]
            while len(parts) > 1:
                parts = [a + b for a, b in zip(parts[0::2], parts[1::2])]
            return acc + parts[0]
        acc = lax.fori_loop(0, (n_ch + 1) // 2, body, jnp.zeros((COUNT_ROWS, tq), F32))
        return jnp.sum(acc, axis=0, keepdims=True)

    def thr_body(it, thr):
        cand = thr + lax.shift_left(jnp.int32(1), 31 - it)
        return jnp.where(count(lambda kk: kk >= cand) >= n_keep, cand, thr)

    thr = lax.fori_loop(0, 32, thr_body, jnp.full((1, tq), INT_MIN, jnp.int32))

    need = n_keep - count(lambda kk: kk > thr)
    lower = jnp.where(_iota((tq, tq), 1) <= _iota((tq, tq), 0), 1.0, 0.0).astype(BF)

    def tie_body(c, seen):
        k0 = pl.multiple_of(c * tq, tq)
        kk = key_ref[pl.ds(k0, tq), :]
        eq = jnp.where(kk == thr, 1.0, 0.0)
        rank = jnp.dot(lower, eq.astype(BF), preferred_element_type=F32) + seen
        keep = (kk > thr) | ((kk == thr) & (rank <= need))
        causal = (k0 + _iota((tq, tq), 0)) <= qpos
        bias_ref[pl.ds(k0, tq), :] = jnp.where(keep & causal, 0.0, NEG)
        return seen + jnp.sum(eq, axis=0, keepdims=True)

    lax.fori_loop(0, n_ch, tie_body, jnp.zeros((1, tq), F32))

    qs = _stack_groups(q_ref[...], 6, N_HEADS)
    biased = lambda s, k0: s + _tile_lanes(bias_ref[pl.ds(k0, tq), :], N_HEADS)
    l, (acc,) = _flash_attend(i, tq, N_HEADS, (HEADS,), qs, k_ref, vt_ref, s_ref,
                              post=lambda s, c, k0: biased(s, k0), last=biased)
    o_ref[...] = _normalised(acc, l, HEADS, tq).T.astype(BF)


def _dsa(qk, vt, misc, T, tq):
    N = qk.shape[0]
    nq = T // tq
    out_spec, out_shape = _attn_out(N, T, tq)
    return pl.pallas_call(
        functools.partial(_dsa_kernel, tq=tq, n_keep=min(DSA_TOPK_MAX, T // 4)),
        grid=(N // T, nq),
        in_specs=_attn_specs(T, tq, SEG_ID['dsa_q'], SEG_ID['dsa_k'], 'dsa_v')
        + [pl.BlockSpec((tq, SEG), lambda b, i: (b * nq + i, SEG_ID['idx_q'])),
           pl.BlockSpec((T, SEG), lambda b, i: (b, SEG_ID['idx_k4'])),
           pl.BlockSpec((tq, LANES), lambda b, i: (b * nq + i, 0))],
        out_specs=out_spec, out_shape=out_shape,
        scratch_shapes=[pltpu.VMEM(((nq + nq % 2) * tq, tq), jnp.int32), pltpu.VMEM((T, tq), F32),
                        _logit_scratch(N_HEADS, tq)],
        compiler_params=_params("parallel", "arbitrary"),
        name="dsa_attn",
    )(qk, qk, vt, qk, qk, misc)


def _mix_out_kernel(x_ref, oa_ref, ob_ref, oc_ref, od_ref, wg_ref, wb_ref, wo_ref, g_ref, b_ref,
                    y_ref, yb_ref):
    x = x_ref[...]
    xb = x.astype(BF)
    merged = jnp.zeros(x.shape, F32)
    for n, o_ref in enumerate((oa_ref, ob_ref, oc_ref, od_ref)):
        gate = _sigmoid(jnp.dot(xb, wg_ref[:, n * D_MODEL:(n + 1) * D_MODEL], preferred_element_type=F32))
        merged = merged + gate * jnp.dot(o_ref[...], wb_ref[n], preferred_element_type=F32)
    h = jnp.dot(merged.astype(BF), wo_ref[...], preferred_element_type=F32)
    y = _layer_norm(DEEPNORM_ALPHA * x + h, g_ref[...], b_ref[...])
    y_ref[...] = y
    yb_ref[...] = y.astype(BF)


def _mix_out(x2, branches, w_gates, w_branch, w_out, ln_g, ln_b, tm):
    N = x2.shape[0]
    row = lambda i: (i, 0)
    fixed2 = lambda i: (0, 0)
    return pl.pallas_call(
        _mix_out_kernel,
        grid=(N // tm,),
        in_specs=[pl.BlockSpec((tm, D_MODEL), row)] + [pl.BlockSpec((tm, SEG), row)] * N_BRANCHES
        + [pl.BlockSpec(w_gates.shape, fixed2),
           pl.BlockSpec(w_branch.shape, lambda i: (0, 0, 0)),
           pl.BlockSpec(w_out.shape, fixed2),
           pl.BlockSpec((1, D_MODEL), fixed2), pl.BlockSpec((1, D_MODEL), fixed2)],
        out_specs=[pl.BlockSpec((tm, D_MODEL), row), pl.BlockSpec((tm, D_MODEL), row)],
        out_shape=[jax.ShapeDtypeStruct((N, D_MODEL), F32), jax.ShapeDtypeStruct((N, D_MODEL), BF)],
        compiler_params=_params("parallel"),
        name="mix_out",
    )(x2, *branches, w_gates, w_branch.astype(BF), w_out.astype(BF),
      ln_g.astype(F32)[None, :], ln_b.astype(F32)[None, :])


def _split_bf16(a):
    hi = a.astype(BF)
    return hi, (a - hi.astype(F32)).astype(BF)


def _router_kernel(x_ref, whi_ref, wlo_ref, rb_ref, idx_ref, wsel_ref):
    xhi, xlo = _split_bf16(x_ref[...])
    whi = whi_ref[...]
    logits = _dot_nt(whi, xhi) + _dot_nt(whi, xlo) + _dot_nt(wlo_ref[...], xhi)
    tm = logits.shape[1]
    scores = _sigmoid(logits)
    biased = scores + _tile_lanes(rb_ref[...], tm // LANES)
    per_group = N_EXPERTS // N_GROUPS
    gs = []
    for g in range(N_GROUPS):
        bg = biased[g * per_group:(g + 1) * per_group]
        row = _iota(bg.shape, 0)
        m1 = jnp.max(bg, axis=0, keepdims=True)
        i1 = jnp.min(jnp.where(bg == m1, row, per_group), axis=0, keepdims=True)
        m2 = jnp.max(jnp.where(row == i1, -jnp.inf, bg), axis=0, keepdims=True)
        gs.append(m1 + m2)
    kept = []
    for g in range(N_GROUPS):
        rank = jnp.zeros((1, tm), F32)
        for o in range(N_GROUPS):
            if o != g:
                beats = (gs[o] >= gs[g]) if o < g else (gs[o] > gs[g])
                rank = rank + jnp.where(beats, 1.0, 0.0)
        kept.append(jnp.where(rank < TOPK_GROUPS, biased[g * per_group:(g + 1) * per_group], -jnp.inf))
    masked = jnp.concatenate(kept, axis=0)
    eid = _iota(masked.shape, 0)
    picks, weights = [], []
    for _ in range(TOP_K):
        mx = jnp.max(masked, axis=0, keepdims=True)
        pick = jnp.min(jnp.where(masked == mx, eid, N_EXPERTS), axis=0, keepdims=True)
        hit = eid == pick
        weights.append(jnp.sum(jnp.where(hit, scores, 0.0), axis=0, keepdims=True))
        masked = jnp.where(hit, -jnp.inf, masked)
        picks.append(pick)
    wsum = weights[0]
    for wk in weights[1:]:
        wsum = wsum + wk
    idx_ref[...] = jnp.concatenate(picks, axis=0)
    wsel_ref[...] = jnp.concatenate(weights, axis=0) / wsum * ROUTED_SCALE


def _router(x1, w_router, router_bias, tm):
    N = x1.shape[0]
    whi, wlo = _split_bf16(w_router.astype(F32).T)
    rb = jnp.broadcast_to(router_bias.astype(F32)[:, None], (N_EXPERTS, LANES))
    fixed = lambda i: (0, 0)
    col = lambda i: (0, i)
    return pl.pallas_call(
        _router_kernel,
        grid=(N // tm,),
        in_specs=[pl.BlockSpec((tm, D_MODEL), lambda i: (i, 0)), pl.BlockSpec(whi.shape, fixed),
                  pl.BlockSpec(wlo.shape, fixed), pl.BlockSpec(rb.shape, fixed)],
        out_specs=[pl.BlockSpec((TOP_K, tm), col), pl.BlockSpec((TOP_K, tm), col)],
        out_shape=[jax.ShapeDtypeStruct((TOP_K, N), jnp.int32), jax.ShapeDtypeStruct((TOP_K, N), F32)],
        compiler_params=_params("parallel"),
        name="router",
    )(x1, whi, wlo, rb)


def _expert_kernel(blk_ref, exp_ref, lo_ref, hi_ref, first_ref, newe_ref,
                   xs_ref, wg_ref, wu_ref, wd_ref, y_ref, wgb_ref, wub_ref, wdb_ref, *, bm):
    t = pl.program_id(0)
    lo, hi = lo_ref[t], hi_ref[t]

    @pl.when(newe_ref[t] == 1)
    def _():
        wgb_ref[...] = wg_ref[...].astype(BF)
        wub_ref[...] = wu_ref[...].astype(BF)
        wdb_ref[...] = wd_ref[...].astype(BF)

    @pl.when(first_ref[t] == 1)
    def _():
        y_ref[...] = jnp.zeros_like(y_ref)

    sub = bm // 2
    for j in range(2):
        r0 = blk_ref[t] * bm + j * sub

        @pl.when((hi > lo) & (lo < r0 + sub) & (hi > r0))
        def _(j=j, r0=r0):
            rows = slice(j * sub, (j + 1) * sub)
            xs = xs_ref[rows, :]
            g = jnp.dot(xs, wgb_ref[...], preferred_element_type=F32)
            u = jnp.dot(xs, wub_ref[...], preferred_element_type=F32)
            h = (g * _sigmoid(g) * u).astype(BF)
            y = jnp.dot(h, wdb_ref[...], preferred_element_type=F32)
            row = r0 + _iota((sub, 1), 0)
            mine = (row >= lo) & (row < hi)
            y_ref[rows, :] = jnp.where(mine, y.astype(y_ref.dtype), y_ref[rows, :])


def _experts(xs, plan, layer, w_gate, w_up, w_down, bm):
    A = xs.shape[0]
    n_items = plan[0].shape[0]
    row = lambda t, blk, *_: (blk[t], 0)
    wspec = lambda shape: pl.BlockSpec((None, None) + shape, lambda t, blk, exp, *_: (layer, exp[t], 0, 0))
    return pl.pallas_call(
        functools.partial(_expert_kernel, bm=bm),
        grid_spec=pltpu.PrefetchScalarGridSpec(
            num_scalar_prefetch=len(plan),
            grid=(n_items,),
            in_specs=[pl.BlockSpec((bm, D_MODEL), row),
                      wspec((D_MODEL, EXPERT_DIM)), wspec((D_MODEL, EXPERT_DIM)),
                      wspec((EXPERT_DIM, D_MODEL))],
            out_specs=pl.BlockSpec((bm, D_MODEL), row),
            scratch_shapes=[pltpu.VMEM((D_MODEL, EXPERT_DIM), BF), pltpu.VMEM((D_MODEL, EXPERT_DIM), BF),
                            pltpu.VMEM((EXPERT_DIM, D_MODEL), BF)],
        ),
        out_shape=jax.ShapeDtypeStruct((A, D_MODEL), BF),
        compiler_params=_params("arbitrary"),
        name="experts",
    )(*plan, xs, w_gate, w_up, w_down)


def _dispatch_plan(eidx_t, bm):
    N = eidx_t.shape[1]
    A, E = N * TOP_K, N_EXPERTS
    assert A % bm == 0
    ids = jnp.arange(A, dtype=jnp.int32)
    se, sid = lax.sort((eidx_t.reshape(A), ids), num_keys=1, is_stable=True)
    _, pos = lax.sort((sid, ids), num_keys=1)
    n_blk = A // bm
    start = jnp.searchsorted(se, jnp.arange(E, dtype=jnp.int32), side='left').astype(jnp.int32)
    lo = jnp.sort(jnp.concatenate([jnp.arange(n_blk, dtype=jnp.int32) * bm, start]))
    hi = jnp.concatenate([lo[1:], jnp.full((1,), A, jnp.int32)])
    blk = jnp.minimum(lo // bm, n_blk - 1)
    exp = se[jnp.minimum(lo, A - 1)]
    changed = lambda a: jnp.concatenate([jnp.ones((1,), jnp.int32), (a[1:] != a[:-1]).astype(jnp.int32)])
    plan = (blk, exp, lo, hi, changed(blk), changed(exp))
    return sid % N, pos.reshape(TOP_K, N), plan


def _moe_out_kernel(x_ref, r_ref, rw_ref, wg_ref, wu_ref, wd_ref, g_ref, b_ref, y_ref):
    x = x_ref[...]
    tm = x.shape[0]
    xb = x.astype(BF)
    g = jnp.dot(xb, wg_ref[...], preferred_element_type=F32)
    u = jnp.dot(xb, wu_ref[...], preferred_element_type=F32)
    h = (g * _sigmoid(g) * u).astype(BF)
    shared = jnp.dot(h, wd_ref[...], preferred_element_type=F32)
    rw = jnp.concatenate([rw_ref[...], jnp.zeros((LANES - TOP_K, tm), F32)], axis=0).T
    routed = rw[:, 0:1] * r_ref[0].astype(F32)
    for k in range(1, TOP_K):
        routed = routed + rw[:, k:k + 1] * r_ref[k].astype(F32)
    y_ref[...] = _layer_norm(DEEPNORM_ALPHA * x + (routed + shared), g_ref[...], b_ref[...])


def _moe_out(x1, routed, route_w, w_sh_gate, w_sh_up, w_sh_down, ln_g, ln_b, tm):
    N = x1.shape[0]
    row = lambda i: (i, 0)
    fixed = lambda i: (0, 0)
    return pl.pallas_call(
        _moe_out_kernel,
        grid=(N // tm,),
        in_specs=[pl.BlockSpec((tm, D_MODEL), row), pl.BlockSpec((TOP_K, tm, D_MODEL), lambda i: (0, i, 0)),
                  pl.BlockSpec((TOP_K, tm), lambda i: (0, i)),
                  pl.BlockSpec(w_sh_gate.shape, fixed), pl.BlockSpec(w_sh_up.shape, fixed),
                  pl.BlockSpec(w_sh_down.shape, fixed),
                  pl.BlockSpec((1, D_MODEL), fixed), pl.BlockSpec((1, D_MODEL), fixed)],
        out_specs=pl.BlockSpec((tm, D_MODEL), row),
        out_shape=jax.ShapeDtypeStruct((N, D_MODEL), F32),
        compiler_params=_params("parallel"),
        name="moe_out",
    )(x1, routed, route_w, w_sh_gate.astype(BF), w_sh_up.astype(BF), w_sh_down.astype(BF),
      ln_g.astype(F32)[None, :], ln_b.astype(F32)[None, :])


def _tiles(N, T):
    return min(512, T), min(256, T), MOBA_BLOCK, min(1024, N * TOP_K)


def _mixer_layer(x2, T, w_in, b_forget, diff_lambda, diff_subln, w_branch, w_out, ln_g, ln_b, lambda_init):
    N = x2.shape[0]
    tm_proj, tm_row, tq, _ = _tiles(N, T)
    w, wvt, w_gates, bf = _in_proj_weights(w_in, b_forget)
    qk, vt, misc = _in_proj(x2, w, wvt, bf, _rope_tables(T), T, tm_proj)
    dl = diff_lambda.astype(F32)
    lam = jnp.exp(jnp.sum(dl[0] * dl[1])) - jnp.exp(jnp.sum(dl[2] * dl[3])) + lambda_init
    o_a = _moba(qk, vt, T, tq)
    o_b = _diff(qk, vt, lam, diff_subln, lambda_init, T, tq)
    o_c = _fox(qk, vt, misc, T, tq)
    o_d = _dsa(qk, vt, misc, T, tq)
    return _mix_out(x2, (o_a, o_b, o_c, o_d), w_gates, w_branch, w_out, ln_g, ln_b, tm_row)


def _moe_layer(x1, x1b, T, layer, w_router, router_bias, w_exp_gate, w_exp_up, w_exp_down,
               w_sh_gate, w_sh_up, w_sh_down, ln_g, ln_b):
    N = x1.shape[0]
    _, tm_row, _, bm = _tiles(N, T)
    eidx_t, wsel_t = _router(x1, w_router, router_bias, tm_row)
    row_tok, pos, plan = _dispatch_plan(eidx_t, bm)
    ys = _experts(x1b[row_tok], plan, layer, w_exp_gate, w_exp_up, w_exp_down, bm)
    return _moe_out(x1, ys[pos], wsel_t, w_sh_gate, w_sh_up, w_sh_down, ln_g, ln_b, tm_row)


def kernel(x, w_in, b_forget, diff_lambda, diff_subln, w_branch, w_out, ln1_g, ln1_b, w_router, router_bias,
           w_exp_gate, w_exp_up, w_exp_down, w_sh_gate, w_sh_up, w_sh_down, ln2_g, ln2_b):
    B, T, D = x.shape
    x2 = x.reshape(B * T, D)
    for l in range(DEPTH):
        lambda_init = 0.8 - 0.6 * math.exp(-0.3 * l)
        x1, x1b = _mixer_layer(x2, T, w_in[l], b_forget[l], diff_lambda[l], diff_subln[l], w_branch[l],
                               w_out[l], ln1_g[l], ln1_b[l], lambda_init)
        x2 = _moe_layer(x1, x1b, T, l, w_router[l], router_bias[l], w_exp_gate, w_exp_up, w_exp_down,
                        w_sh_gate[l], w_sh_up[l], w_sh_down[l], ln2_g[l], ln2_b[l])
    return x2.reshape(B, T, D)
```

```python
import functools
import math

import jax
import jax.numpy as jnp
from jax import lax
from jax.experimental import pallas as pl
from jax.experimental.pallas import tpu as pltpu

F32 = jnp.float32
BF = jnp.bfloat16

D_MODEL = 1024
DEPTH = 2
HEAD_DIM = 64
N_HEADS = 4
DIFF_DIM = HEAD_DIM // 2
N_IDX_HEADS = 4
IDX_DIM = 64
BRANCH_WIDTH = N_HEADS * HEAD_DIM
N_BRANCHES = 4
MOBA_BLOCK = 256
MOBA_TOPK = 3
DSA_TOPK_MAX = 256
ROPE_THETA = 10000.0
N_EXPERTS = 256
TOP_K = 8
N_GROUPS = 8
TOPK_GROUPS = 4
EXPERT_DIM = 256
ROUTED_SCALE = 2.5
LN_EPS = 1e-5
DEEPNORM_ALPHA = (2 * DEPTH) ** 0.25

IN_SEGMENTS = (
    ('moba_q', BRANCH_WIDTH), ('moba_k', BRANCH_WIDTH), ('moba_v', BRANCH_WIDTH),
    ('diff_q', BRANCH_WIDTH), ('diff_k', BRANCH_WIDTH), ('diff_v', BRANCH_WIDTH),
    ('fox_q', BRANCH_WIDTH), ('fox_k', BRANCH_WIDTH), ('fox_v', BRANCH_WIDTH), ('fox_f', N_HEADS),
    ('dsa_q', BRANCH_WIDTH), ('dsa_k', BRANCH_WIDTH), ('dsa_v', BRANCH_WIDTH),
    ('idx_q', N_IDX_HEADS * IDX_DIM), ('idx_k', IDX_DIM), ('idx_w', N_IDX_HEADS),
    ('gates', N_BRANCHES * D_MODEL),
)

LANES = 128
SUBLANES = 8
SEG = BRANCH_WIDTH
NEG = -1e30
INT_MIN = -2 ** 31
VMEM_LIMIT = 48 * 1024 * 1024

LOG2E = math.log2(math.e)
_QSCALE = HEAD_DIM ** -0.5 * LOG2E
PROJ_SEGS = (
    ('moba_q', 64, _QSCALE), ('moba_k', 64, 1.0),
    ('diff_q', 32, DIFF_DIM ** -0.5 * LOG2E), ('diff_k', 32, 1.0),
    ('fox_q', 0, _QSCALE), ('fox_k', 0, 1.0),
    ('dsa_q', 64, _QSCALE), ('dsa_k', 64, 1.0),
    ('idx_q', 64, IDX_DIM ** -0.5), ('idx_k4', 64, 1.0),
)
SEG_ID = {name: i for i, (name, _, _) in enumerate(PROJ_SEGS)}
N_SEG = len(PROJ_SEGS)
V_SEGS = ('moba_v', 'diff_v', 'fox_v', 'dsa_v')
V_ID = {name: i for i, name in enumerate(V_SEGS)}
MISC_LOGF = 0
MISC_IDXW = 4


def _params(*sem):
    return pltpu.CompilerParams(dimension_semantics=sem, vmem_limit_bytes=VMEM_LIMIT)


def _iota(shape, dim):
    return lax.broadcasted_iota(jnp.int32, shape, dim)


def _dot_nt(a, b):
    return lax.dot_general(a, b, (((1,), (1,)), ((), ())), preferred_element_type=F32)


def _sigmoid(z):
    return 1.0 / (1.0 + jnp.exp(-z))


def _layer_norm(y, g, b):
    mu = jnp.mean(y, axis=-1, keepdims=True)
    yc = y - mu
    var = jnp.mean(yc * yc, axis=-1, keepdims=True)
    return yc * lax.rsqrt(var + LN_EPS) * g + b


def _swap_halves(a, half):
    w = a.shape[-1]
    first = (_iota(a.shape, 1) & (2 * half - 1)) < half
    return jnp.where(first, pltpu.roll(a, w - half, 1), pltpu.roll(a, half, 1))


def _in_proj_kernel(x_ref, w_ref, wvt_ref, c64_ref, s64_ref, c32_ref, s32_ref, bf_ref,
                    qk_ref, vt_ref, misc_ref):
    xb = x_ref[...].astype(BF)
    for s, (_, rot, scale) in enumerate(PROJ_SEGS):
        acc = jnp.dot(xb, w_ref[:, s * SEG:(s + 1) * SEG], preferred_element_type=F32)
        if rot == 64:
            acc = acc * c64_ref[...] + _swap_halves(acc, 32) * s64_ref[...]
        elif rot == 32:
            acc = acc * c32_ref[...] + _swap_halves(acc, 16) * s32_ref[...]
        if scale != 1.0:
            acc = acc * scale
        qk_ref[:, s * SEG:(s + 1) * SEG] = acc.astype(BF)
    vt_ref[...] = _dot_nt(wvt_ref[...], xb).astype(BF)
    m = jnp.dot(xb, w_ref[:, N_SEG * SEG:N_SEG * SEG + LANES], preferred_element_type=F32)
    z = m + bf_ref[...]
    logf = jnp.minimum(z, 0.0) - jnp.log1p(jnp.exp(-jnp.abs(z)))
    lane = _iota(m.shape, 1)
    misc_ref[...] = jnp.where(lane < MISC_IDXW, logf, m)


def _rope_tables(T):
    pos = jnp.arange(T).astype(F32)

    def tab(group, reps):
        half = group // 2
        inv_freq = ROPE_THETA ** (-jnp.arange(half, dtype=F32) / half)
        ang = pos[:, None] * inv_freq[None, :]
        cos, sin = jnp.cos(ang), jnp.sin(ang)
        return (jnp.tile(jnp.concatenate([cos, cos], -1), (1, reps)),
                jnp.tile(jnp.concatenate([-sin, sin], -1), (1, reps)))

    return tab(64, SEG // 64) + tab(32, SEG // 32)


def _in_proj_weights(w_in, b_forget):
    parts, off = {}, 0
    for name, width in IN_SEGMENTS:
        parts[name] = w_in[:, off:off + width]
        off += width
    parts['idx_k4'] = jnp.tile(parts['idx_k'], (1, N_IDX_HEADS))
    misc = jnp.concatenate([parts['fox_f'], parts['idx_w'],
                            jnp.zeros((D_MODEL, LANES - 2 * N_HEADS), w_in.dtype)], axis=1)
    w = jnp.concatenate([parts[name] for name, _, _ in PROJ_SEGS] + [misc], axis=1).astype(BF)
    wvt = jnp.concatenate([parts[name] for name in V_SEGS], axis=1).T.astype(BF)
    bf = jnp.zeros((1, LANES), F32).at[0, MISC_LOGF:MISC_LOGF + N_HEADS].set(b_forget.astype(F32))
    return w, wvt, parts['gates'].astype(BF), bf


def _in_proj(x2, w, wvt, bf, tables, T, tm):
    N = x2.shape[0]
    nt = T // tm
    tab_spec = pl.BlockSpec((tm, SEG), lambda i: (i % nt, 0))
    fixed = lambda i: (0, 0)
    return pl.pallas_call(
        _in_proj_kernel,
        grid=(N // tm,),
        in_specs=[pl.BlockSpec((tm, D_MODEL), lambda i: (i, 0)),
                  pl.BlockSpec(w.shape, fixed), pl.BlockSpec(wvt.shape, fixed),
                  tab_spec, tab_spec, tab_spec, tab_spec,
                  pl.BlockSpec((1, LANES), fixed)],
        out_specs=[pl.BlockSpec((tm, N_SEG * SEG), lambda i: (i, 0)),
                   pl.BlockSpec((len(V_SEGS) * SEG, tm), lambda i: (0, i)),
                   pl.BlockSpec((tm, LANES), lambda i: (i, 0))],
        out_shape=[jax.ShapeDtypeStruct((N, N_SEG * SEG), BF),
                   jax.ShapeDtypeStruct((len(V_SEGS) * SEG, N), BF),
                   jax.ShapeDtypeStruct((N, LANES), F32)],
        compiler_params=_params("parallel"),
        name="in_proj",
    )(x2, w, wvt, *tables, bf)


def _group_mask(shape, shift, g):
    return (_iota(shape, 1) >> shift) == g


def _stack_groups(q, shift, n_groups):
    zero = jnp.zeros_like(q)
    return jnp.concatenate([jnp.where(_group_mask(q.shape, shift, g), q, zero) for g in range(n_groups)],
                           axis=0)


def _cols(a, g, tq):
    return a[:, g * tq:(g + 1) * tq]


def _head_rows(a, h):
    return a[h * HEAD_DIM:(h + 1) * HEAD_DIM]


ONES_ROWS = 16


def _flash_init(n_groups, tq, n_sets):
    return (jnp.full((1, n_groups * tq), NEG, F32), jnp.zeros((1, n_groups * tq), F32),
            tuple(jnp.zeros((SEG, tq), F32) for _ in range(n_sets)))


def _flash_update(s, carry, vt, tq, head_sets):
    m, l, accs = carry
    m_new = jnp.maximum(m, jnp.max(s, axis=0, keepdims=True))
    alpha = jnp.exp2(m - m_new)
    pb = jnp.exp2(s - m_new).astype(BF)
    ones = jnp.ones((ONES_ROWS, vt.shape[1]), BF)
    new_accs, p_sum = [], {}
    for acc, groups in zip(accs, head_sets):
        parts = []
        for h, g in enumerate(groups):
            r = jnp.dot(jnp.concatenate([_head_rows(vt, h), ones], axis=0), _cols(pb, g, tq),
                        preferred_element_type=F32)
            parts.append(_cols(alpha, g, tq) * _head_rows(acc, h) + r[0:HEAD_DIM])
            p_sum[g] = r[HEAD_DIM:HEAD_DIM + 1]
        new_accs.append(jnp.concatenate(parts, axis=0))
    l = alpha * l + jnp.concatenate([p_sum[g] for g in sorted(p_sum)], axis=1)
    return m_new, l, tuple(new_accs)


def _causal(n_groups, tq):
    shape = (tq, n_groups * tq)
    return _iota(shape, 0) <= (_iota(shape, 1) & (tq - 1))


def _normalised(acc, l, groups, tq):
    return jnp.concatenate([_head_rows(acc, h) * (1.0 / _cols(l, g, tq)) for h, g in enumerate(groups)],
                           axis=0)


def _tile_lanes(a, n):
    return jnp.concatenate([a] * n, axis=1) if n > 1 else a


HEADS = tuple(range(N_HEADS))


def _flash_attend(i, tq, n_groups, head_sets, qs, k_ref, vt_ref, s_ref, post, last):
    def raw(k0):
        return _dot_nt(k_ref[pl.ds(k0, tq), :], qs)

    if s_ref is not None:
        s_ref[0] = raw(0)

    def body(c, carry):
        k0 = pl.multiple_of(c * tq, tq)
        if s_ref is None:
            s = post(raw(k0), c, k0)
        else:
            s = post(s_ref[c & 1], c, k0)
            s_ref[(c + 1) & 1] = raw(pl.multiple_of(k0 + tq, tq))
        return _flash_update(s, carry, vt_ref[:, pl.ds(k0, tq)], tq, head_sets)

    carry = lax.fori_loop(0, i, body, _flash_init(n_groups, tq, len(head_sets)))
    k0 = pl.multiple_of(i * tq, tq)
    s = last(raw(k0) if s_ref is None else s_ref[i & 1], k0)
    _, l, accs = _flash_update(s, carry, vt_ref[:, pl.ds(k0, tq)], tq, head_sets)
    return l, accs


def _logit_scratch(n_groups, tq):
    return pltpu.VMEM((2, tq, n_groups * tq), F32)


def _attn_specs(T, tq, q_seg, k_seg, v_name):
    nq = T // tq
    v_id = V_ID[v_name]
    return [pl.BlockSpec((tq, SEG), lambda b, i: (b * nq + i, q_seg)),
            pl.BlockSpec((T, SEG), lambda b, i: (b, k_seg)),
            pl.BlockSpec((SEG, T), lambda b, i: (v_id, b))]


def _attn_out(N, T, tq):
    nq = T // tq
    return (pl.BlockSpec((tq, SEG), lambda b, i: (b * nq + i, 0)),
            jax.ShapeDtypeStruct((N, SEG), BF))


def _moba_kernel(q_ref, k_ref, vt_ref, o_ref, km_ref, sel_ref, s_ref, *, tq, nb, n_sel):
    i = pl.program_id(1)

    @pl.when(i == 0)
    def _():
        km_ref[...] = jnp.zeros_like(km_ref)
        for n in range(nb):
            blk = k_ref[n * tq:(n + 1) * tq, :].astype(F32)
            km_ref[n:n + 1, :] = jnp.mean(blk, axis=0, keepdims=True)

    qs = _stack_groups(q_ref[...], 6, N_HEADS)
    km = km_ref[...]
    km_hi = km.astype(BF)
    km_lo = (km - km_hi.astype(F32)).astype(BF)
    nrow = sel_ref.shape[0]
    g = (_dot_nt(km_hi, qs) + _dot_nt(km_lo, qs))[0:nrow]
    blk_id = _iota(g.shape, 0)
    g = jnp.where(blk_id < i, g, -jnp.inf)
    sel = jnp.zeros(g.shape, F32)
    for _ in range(n_sel):
        gmax = jnp.max(g, axis=0, keepdims=True)
        first = jnp.min(jnp.where(g == gmax, blk_id, nrow), axis=0, keepdims=True)
        hit = blk_id == first
        sel = jnp.where(hit & (gmax > -jnp.inf), 1.0, sel)
        g = jnp.where(hit, -jnp.inf, g)
    sel_ref[...] = sel

    l, (acc,) = _flash_attend(
        i, tq, N_HEADS, (HEADS,), qs, k_ref, vt_ref, s_ref,
        post=lambda s, c, k0: jnp.where(sel_ref[pl.ds(c, 1), :] > 0.5, s, NEG),
        last=lambda s, k0: jnp.where(_causal(N_HEADS, tq), s, NEG))
    o_ref[...] = _normalised(acc, l, HEADS, tq).T.astype(BF)


def _moba(qk, vt, T, tq):
    N = qk.shape[0]
    nb = T // MOBA_BLOCK
    assert T % MOBA_BLOCK == 0 and tq == MOBA_BLOCK and nb <= LANES
    sel_rows = -(-nb // SUBLANES) * SUBLANES
    out_spec, out_shape = _attn_out(N, T, tq)
    return pl.pallas_call(
        functools.partial(_moba_kernel, tq=tq, nb=nb, n_sel=min(MOBA_TOPK, nb - 1)),
        grid=(N // T, T // tq),
        in_specs=_attn_specs(T, tq, SEG_ID['moba_q'], SEG_ID['moba_k'], 'moba_v'),
        out_specs=out_spec, out_shape=out_shape,
        scratch_shapes=[pltpu.VMEM((LANES, SEG), F32), pltpu.VMEM((sel_rows, N_HEADS * tq), F32),
                        _logit_scratch(N_HEADS, tq)],
        compiler_params=_params("parallel", "arbitrary"),
        name="moba_attn",
    )(qk, qk, vt)


DIFF_SETS = (tuple(2 * h for h in HEADS), tuple(2 * h + 1 for h in HEADS))


def _diff_kernel(lam_ref, q_ref, k_ref, vt_ref, g_ref, o_ref, *, tq, out_scale):
    i = pl.program_id(1)
    n_groups = 2 * N_HEADS
    qs = _stack_groups(q_ref[...], 5, n_groups)
    l, (a1, a2) = _flash_attend(
        i, tq, n_groups, DIFF_SETS, qs, k_ref, vt_ref, None,
        post=lambda s, c, k0: s,
        last=lambda s, k0: jnp.where(_causal(n_groups, tq), s, NEG))
    out = _normalised(a1, l, DIFF_SETS[0], tq) - lam_ref[0] * _normalised(a2, l, DIFF_SETS[1], tq)
    normed = []
    for h in HEADS:
        oh = _head_rows(out, h)
        ms = jnp.mean(oh * oh, axis=0, keepdims=True)
        normed.append(oh * lax.rsqrt(ms + LN_EPS))
    o_ref[...] = (jnp.concatenate(normed, axis=0).T * g_ref[...] * out_scale).astype(BF)


def _diff(qk, vt, lam, subln_g, lambda_init, T, tq):
    N = qk.shape[0]
    out_spec, out_shape = _attn_out(N, T, tq)
    g = jnp.tile(subln_g.astype(F32), N_HEADS)[None, :]
    return pl.pallas_call(
        functools.partial(_diff_kernel, tq=tq, out_scale=1.0 - lambda_init),
        grid=(N // T, T // tq),
        in_specs=[pl.BlockSpec(memory_space=pltpu.SMEM)]
        + _attn_specs(T, tq, SEG_ID['diff_q'], SEG_ID['diff_k'], 'diff_v')
        + [pl.BlockSpec((1, SEG), lambda b, i: (0, 0))],
        out_specs=out_spec, out_shape=out_shape,
        compiler_params=_params("parallel", "arbitrary"),
        name="diff_attn",
    )(lam.reshape(1).astype(F32), qk, qk, vt, g)


def _fox_kernel(q_ref, k_ref, vt_ref, misc_ref, o_ref, c_ref, s_ref, *, tq):
    i = pl.program_id(1)

    @pl.when(i == 0)
    def _():
        tri = jnp.where(_iota((tq, tq), 1) <= _iota((tq, tq), 0), 1.0, 0.0).astype(BF)
        carry = [jnp.zeros((1, LANES), F32) for _ in HEADS]
        for n in range(c_ref.shape[1] // tq):
            blk = misc_ref[n * tq:(n + 1) * tq, :] * LOG2E
            for h in HEADS:
                col = jnp.broadcast_to(blk[:, MISC_LOGF + h:MISC_LOGF + h + 1], blk.shape)
                hi = col.astype(BF)
                rest = col - hi.astype(F32)
                mid = rest.astype(BF)
                lo = (rest - mid.astype(F32)).astype(BF)
                cs = carry[h] + (jnp.dot(tri, hi, preferred_element_type=F32)
                                 + jnp.dot(tri, mid, preferred_element_type=F32)
                                 + jnp.dot(tri, lo, preferred_element_type=F32))
                c_ref[h, n * tq:(n + 1) * tq, :] = cs
                carry[h] = cs[tq - 1:tq, :]

    qs = _stack_groups(q_ref[...], 6, N_HEADS)

    def decayed(s, k0):
        return jnp.concatenate(
            [_cols(s, h, tq) - _tile_lanes(c_ref[h, pl.ds(k0, tq), :], tq // LANES) for h in HEADS], axis=1)

    l, (acc,) = _flash_attend(
        i, tq, N_HEADS, (HEADS,), qs, k_ref, vt_ref, s_ref,
        post=lambda s, c, k0: decayed(s, k0),
        last=lambda s, k0: jnp.where(_causal(N_HEADS, tq), decayed(s, k0), NEG))
    o_ref[...] = _normalised(acc, l, HEADS, tq).T.astype(BF)


def _fox(qk, vt, misc, T, tq):
    N = qk.shape[0]
    out_spec, out_shape = _attn_out(N, T, tq)
    return pl.pallas_call(
        functools.partial(_fox_kernel, tq=tq),
        grid=(N // T, T // tq),
        in_specs=_attn_specs(T, tq, SEG_ID['fox_q'], SEG_ID['fox_k'], 'fox_v')
        + [pl.BlockSpec((T, LANES), lambda b, i: (b, 0))],
        out_specs=out_spec, out_shape=out_shape,
        scratch_shapes=[pltpu.VMEM((N_HEADS, T, LANES), F32), _logit_scratch(N_HEADS, tq)],
        compiler_params=_params("parallel", "arbitrary"),
        name="fox_attn",
    )(qk, qk, vt, misc)


COUNT_ROWS = 64

def _dsa_kernel(q_ref, k_ref, vt_ref, qi_ref, ki_ref, w_ref, o_ref, key_ref, bias_ref, s_ref,
                *, tq, n_keep):
    i = pl.program_id(1)
    n_ch = i + 1
    qpos = i * tq + _iota((1, tq), 1)

    qis = _stack_groups(qi_ref[...], 6, N_IDX_HEADS)
    wt = w_ref[...].T * (N_IDX_HEADS ** -0.5)
    w_rows = [wt[MISC_IDXW + h:MISC_IDXW + h + 1] for h in range(N_IDX_HEADS)]

    def score_body(c, carry):
        k0 = pl.multiple_of(c * tq, tq)
        d = jnp.maximum(_dot_nt(ki_ref[pl.ds(k0, tq), :], qis), 0.0)
        score = w_rows[0] * _cols(d, 0, tq)
        for h in range(1, N_IDX_HEADS):
            score = score + w_rows[h] * _cols(d, h, tq)
        sc = jnp.where((k0 + _iota((tq, tq), 0)) <= qpos, score, -jnp.inf)
        bits = pltpu.bitcast(sc, jnp.int32)
        key_ref[pl.ds(k0, tq), :] = jnp.where(sc == 0.0, 0,
                                              jnp.where(bits < 0, bits ^ jnp.int32(0x7FFFFFFF), bits))
        return carry

    lax.fori_loop(0, n_ch, score_body, 0)

    @pl.when(n_ch % 2 == 1)
    def _():
        key_ref[pl.ds(pl.multiple_of(n_ch * tq, tq), tq), :] = jnp.full((tq, tq), INT_MIN, jnp.int32)

    def count(pred):
        def body(c, acc):
            k0 = pl.multiple_of(c * 2 * tq, 2 * tq)
            hit = jnp.where(pred(key_ref[pl.ds(k0, 2 * tq), :]), 1.0, 0.0)
            parts = [hit[j * COUNT_ROWS:(j + 1) * COUNT_ROWS] for j in range(2 * tq // COUNT_ROWS)]
            while len(parts) > 1:
                parts = [a + b for a, b in zip(parts[0::2], parts[1::2])]
            return acc + parts[0]
        acc = lax.fori_loop(0, (n_ch + 1) // 2, body, jnp.zeros((COUNT_ROWS, tq), F32))
        return jnp.sum(acc, axis=0, keepdims=True)

    def thr_body(it, thr):
        cand = thr + lax.shift_left(jnp.int32(1), 31 - it)
        return jnp.where(count(lambda kk: kk >= cand) >= n_keep, cand, thr)

    thr = lax.fori_loop(0, 32, thr_body, jnp.full((1, tq), INT_MIN, jnp.int32))

    need = n_keep - count(lambda kk: kk > thr)
    lower = jnp.where(_iota((tq, tq), 1) <= _iota((tq, tq), 0), 1.0, 0.0).astype(BF)

    def tie_body(c, seen):
        k0 = pl.multiple_of(c * tq, tq)
        kk = key_ref[pl.ds(k0, tq), :]
        eq = jnp.where(kk == thr, 1.0, 0.0)
        rank = jnp.dot(lower, eq.astype(BF), preferred_element_type=F32) + seen
        keep = (kk > thr) | ((kk == thr) & (rank <= need))
        causal = (k0 + _iota((tq, tq), 0)) <= qpos
        bias_ref[pl.ds(k0, tq), :] = jnp.where(keep & causal, 0.0, NEG)
        return seen + jnp.sum(eq, axis=0, keepdims=True)

    lax.fori_loop(0, n_ch, tie_body, jnp.zeros((1, tq), F32))

    qs = _stack_groups(q_ref[...], 6, N_HEADS)
    biased = lambda s, k0: s + _tile_lanes(bias_ref[pl.ds(k0, tq), :], N_HEADS)
    l, (acc,) = _flash_attend(i, tq, N_HEADS, (HEADS,), qs, k_ref, vt_ref, s_ref,
                              post=lambda s, c, k0: biased(s, k0), last=biased)
    o_ref[...] = _normalised(acc, l, HEADS, tq).T.astype(BF)


def _dsa(qk, vt, misc, T, tq):
    N = qk.shape[0]
    nq = T // tq
    out_spec, out_shape = _attn_out(N, T, tq)
    return pl.pallas_call(
        functools.partial(_dsa_kernel, tq=tq, n_keep=min(DSA_TOPK_MAX, T // 4)),
        grid=(N // T, nq),
        in_specs=_attn_specs(T, tq, SEG_ID['dsa_q'], SEG_ID['dsa_k'], 'dsa_v')
        + [pl.BlockSpec((tq, SEG), lambda b, i: (b * nq + i, SEG_ID['idx_q'])),
           pl.BlockSpec((T, SEG), lambda b, i: (b, SEG_ID['idx_k4'])),
           pl.BlockSpec((tq, LANES), lambda b, i: (b * nq + i, 0))],
        out_specs=out_spec, out_shape=out_shape,
        scratch_shapes=[pltpu.VMEM(((nq + nq % 2) * tq, tq), jnp.int32), pltpu.VMEM((T, tq), F32),
                        _logit_scratch(N_HEADS, tq)],
        compiler_params=_params("parallel", "arbitrary"),
        name="dsa_attn",
    )(qk, qk, vt, qk, qk, misc)


def _mix_out_kernel(x_ref, oa_ref, ob_ref, oc_ref, od_ref, wg_ref, wb_ref, wo_ref, g_ref, b_ref,
                    y_ref, yb_ref):
    x = x_ref[...]
    xb = x.astype(BF)
    merged = jnp.zeros(x.shape, F32)
    for n, o_ref in enumerate((oa_ref, ob_ref, oc_ref, od_ref)):
        gate = _sigmoid(jnp.dot(xb, wg_ref[:, n * D_MODEL:(n + 1) * D_MODEL], preferred_element_type=F32))
        merged = merged + gate * jnp.dot(o_ref[...], wb_ref[n], preferred_element_type=F32)
    h = jnp.dot(merged.astype(BF), wo_ref[...], preferred_element_type=F32)
    y = _layer_norm(DEEPNORM_ALPHA * x + h, g_ref[...], b_ref[...])
    y_ref[...] = y
    yb_ref[...] = y.astype(BF)


def _mix_out(x2, branches, w_gates, w_branch, w_out, ln_g, ln_b, tm):
    N = x2.shape[0]
    row = lambda i: (i, 0)
    fixed2 = lambda i: (0, 0)
    return pl.pallas_call(
        _mix_out_kernel,
        grid=(N // tm,),
        in_specs=[pl.BlockSpec((tm, D_MODEL), row)] + [pl.BlockSpec((tm, SEG), row)] * N_BRANCHES
        + [pl.BlockSpec(w_gates.shape, fixed2),
           pl.BlockSpec(w_branch.shape, lambda i: (0, 0, 0)),
           pl.BlockSpec(w_out.shape, fixed2),
           pl.BlockSpec((1, D_MODEL), fixed2), pl.BlockSpec((1, D_MODEL), fixed2)],
        out_specs=[pl.BlockSpec((tm, D_MODEL), row), pl.BlockSpec((tm, D_MODEL), row)],
        out_shape=[jax.ShapeDtypeStruct((N, D_MODEL), F32), jax.ShapeDtypeStruct((N, D_MODEL), BF)],
        compiler_params=_params("parallel"),
        name="mix_out",
    )(x2, *branches, w_gates, w_branch.astype(BF), w_out.astype(BF),
      ln_g.astype(F32)[None, :], ln_b.astype(F32)[None, :])


def _split_bf16(a):
    hi = a.astype(BF)
    return hi, (a - hi.astype(F32)).astype(BF)


def _router_kernel(x_ref, whi_ref, wlo_ref, rb_ref, idx_ref, wsel_ref):
    xhi, xlo = _split_bf16(x_ref[...])
    whi = whi_ref[...]
    logits = _dot_nt(whi, xhi) + _dot_nt(whi, xlo) + _dot_nt(wlo_ref[...], xhi)
    tm = logits.shape[1]
    scores = _sigmoid(logits)
    biased = scores + _tile_lanes(rb_ref[...], tm // LANES)
    per_group = N_EXPERTS // N_GROUPS
    gs = []
    for g in range(N_GROUPS):
        bg = biased[g * per_group:(g + 1) * per_group]
        row = _iota(bg.shape, 0)
        m1 = jnp.max(bg, axis=0, keepdims=True)
        i1 = jnp.min(jnp.where(bg == m1, row, per_group), axis=0, keepdims=True)
        m2 = jnp.max(jnp.where(row == i1, -jnp.inf, bg), axis=0, keepdims=True)
        gs.append(m1 + m2)
    kept = []
    for g in range(N_GROUPS):
        rank = jnp.zeros((1, tm), F32)
        for o in range(N_GROUPS):
            if o != g:
                beats = (gs[o] >= gs[g]) if o < g else (gs[o] > gs[g])
                rank = rank + jnp.where(beats, 1.0, 0.0)
        kept.append(jnp.where(rank < TOPK_GROUPS, biased[g * per_group:(g + 1) * per_group], -jnp.inf))
    masked = jnp.concatenate(kept, axis=0)
    eid = _iota(masked.shape, 0)
    picks, weights = [], []
    for _ in range(TOP_K):
        mx = jnp.max(masked, axis=0, keepdims=True)
        pick = jnp.min(jnp.where(masked == mx, eid, N_EXPERTS), axis=0, keepdims=True)
        hit = eid == pick
        weights.append(jnp.sum(jnp.where(hit, scores, 0.0), axis=0, keepdims=True))
        masked = jnp.where(hit, -jnp.inf, masked)
        picks.append(pick)
    wsum = weights[0]
    for wk in weights[1:]:
        wsum = wsum + wk
    idx_ref[...] = jnp.concatenate(picks, axis=0)
    wsel_ref[...] = jnp.concatenate(weights, axis=0) / wsum * ROUTED_SCALE


def _router(x1, w_router, router_bias, tm):
    N = x1.shape[0]
    whi, wlo = _split_bf16(w_router.astype(F32).T)
    rb = jnp.broadcast_to(router_bias.astype(F32)[:, None], (N_EXPERTS, LANES))
    fixed = lambda i: (0, 0)
    col = lambda i: (0, i)
    return pl.pallas_call(
        _router_kernel,
        grid=(N // tm,),
        in_specs=[pl.BlockSpec((tm, D_MODEL), lambda i: (i, 0)), pl.BlockSpec(whi.shape, fixed),
                  pl.BlockSpec(wlo.shape, fixed), pl.BlockSpec(rb.shape, fixed)],
        out_specs=[pl.BlockSpec((TOP_K, tm), col), pl.BlockSpec((TOP_K, tm), col)],
        out_shape=[jax.ShapeDtypeStruct((TOP_K, N), jnp.int32), jax.ShapeDtypeStruct((TOP_K, N), F32)],
        compiler_params=_params("parallel"),
        name="router",
    )(x1, whi, wlo, rb)


def _expert_kernel(blk_ref, exp_ref, lo_ref, hi_ref, first_ref, newe_ref,
                   xs_ref, wg_ref, wu_ref, wd_ref, y_ref, wgb_ref, wub_ref, wdb_ref, *, bm):
    t = pl.program_id(0)
    lo, hi = lo_ref[t], hi_ref[t]

    @pl.when(newe_ref[t] == 1)
    def _():
        wgb_ref[...] = wg_ref[...].astype(BF)
        wub_ref[...] = wu_ref[...].astype(BF)
        wdb_ref[...] = wd_ref[...].astype(BF)

    @pl.when(first_ref[t] == 1)
    def _():
        y_ref[...] = jnp.zeros_like(y_ref)

    sub = bm // 2
    base = blk_ref[t] * bm

    def work(j0, n_sub):
        rows = slice(j0 * sub, (j0 + n_sub) * sub)
        xs = xs_ref[rows, :]
        g = jnp.dot(xs, wgb_ref[...], preferred_element_type=F32)
        u = jnp.dot(xs, wub_ref[...], preferred_element_type=F32)
        h = (g * _sigmoid(g) * u).astype(BF)
        y = jnp.dot(h, wdb_ref[...], preferred_element_type=F32)
        row = base + j0 * sub + _iota((n_sub * sub, 1), 0)
        mine = (row >= lo) & (row < hi)
        y_ref[rows, :] = jnp.where(mine, y.astype(y_ref.dtype), y_ref[rows, :])

    need = [(hi > lo) & (lo < base + (j + 1) * sub) & (hi > base + j * sub) for j in range(2)]
    pl.when(need[0] & need[1])(lambda: work(0, 2))
    pl.when(need[0] & jnp.logical_not(need[1]))(lambda: work(0, 1))
    pl.when(jnp.logical_not(need[0]) & need[1])(lambda: work(1, 1))


def _expert_kernel_onto(blk_ref, exp_ref, lo_ref, hi_ref, first_ref, newe_ref,
                        xs_ref, wg_ref, wu_ref, wd_ref, prev_ref, y_ref, *scratch, bm):
    del prev_ref
    _expert_kernel(blk_ref, exp_ref, lo_ref, hi_ref, first_ref, newe_ref,
                   xs_ref, wg_ref, wu_ref, wd_ref, y_ref, *scratch, bm=bm)


def _experts(xs_parts, plans, layer, w_gate, w_up, w_down, bm):
    rows_part = xs_parts[0].shape[0]
    A = rows_part * len(xs_parts)
    wspec = lambda shape: pl.BlockSpec((None, None) + shape, lambda t, blk, exp, *_: (layer, exp[t], 0, 0))
    ys = None
    for p, (xs, plan) in enumerate(zip(xs_parts, plans)):
        blk0 = p * rows_part // bm
        in_specs = [pl.BlockSpec((bm, D_MODEL), lambda t, blk, *_, blk0=blk0: (blk[t] - blk0, 0)),
                    wspec((D_MODEL, EXPERT_DIM)), wspec((D_MODEL, EXPERT_DIM)), wspec((EXPERT_DIM, D_MODEL))]
        args = (*plan, xs, w_gate, w_up, w_down)
        if ys is not None:
            in_specs.append(pl.BlockSpec(memory_space=pl.ANY))
            args += (ys,)
        ys = pl.pallas_call(
            functools.partial(_expert_kernel if ys is None else _expert_kernel_onto, bm=bm),
            grid_spec=pltpu.PrefetchScalarGridSpec(
                num_scalar_prefetch=len(plan),
                grid=(plan[0].shape[0],),
                in_specs=in_specs,
                out_specs=pl.BlockSpec((bm, D_MODEL), lambda t, blk, *_: (blk[t], 0)),
                scratch_shapes=[pltpu.VMEM((D_MODEL, EXPERT_DIM), BF), pltpu.VMEM((D_MODEL, EXPERT_DIM), BF),
                                pltpu.VMEM((EXPERT_DIM, D_MODEL), BF)],
            ),
            out_shape=jax.ShapeDtypeStruct((A, D_MODEL), BF),
            input_output_aliases={} if ys is None else {len(args) - 1: 0},
            compiler_params=_params("arbitrary"),
            name="experts",
        )(*args)
    return ys


def _dispatch_plan(eidx_t, bm, n_parts):
    N = eidx_t.shape[1]
    A, E = N * TOP_K, N_EXPERTS
    assert A % (bm * n_parts) == 0
    ids = jnp.arange(A, dtype=jnp.int32)
    se, sid = lax.sort((eidx_t.reshape(A), ids), num_keys=1, is_stable=True)
    _, pos = lax.sort((sid, ids), num_keys=1)
    start = jnp.searchsorted(se, jnp.arange(E, dtype=jnp.int32), side='left').astype(jnp.int32)
    changed = lambda a: jnp.concatenate([jnp.ones((1,), jnp.int32), (a[1:] != a[:-1]).astype(jnp.int32)])
    plans = []
    for p in range(n_parts):
        a0, a1 = p * (A // n_parts), (p + 1) * (A // n_parts)
        lo = jnp.sort(jnp.concatenate([jnp.arange(a0, a1, bm, dtype=jnp.int32), jnp.clip(start, a0, a1)]))
        hi = jnp.concatenate([lo[1:], jnp.full((1,), a1, jnp.int32)])
        blk = jnp.minimum(lo // bm, a1 // bm - 1)
        exp = se[jnp.minimum(lo, A - 1)]
        plans.append((blk, exp, lo, hi, changed(blk), changed(exp)))
    return sid % N, pos.reshape(TOP_K, N), plans


def _moe_out_kernel(x_ref, r_ref, rw_ref, wg_ref, wu_ref, wd_ref, g_ref, b_ref, y_ref):
    x = x_ref[...]
    tm = x.shape[0]
    xb = x.astype(BF)
    g = jnp.dot(xb, wg_ref[...], preferred_element_type=F32)
    u = jnp.dot(xb, wu_ref[...], preferred_element_type=F32)
    h = (g * _sigmoid(g) * u).astype(BF)
    shared = jnp.dot(h, wd_ref[...], preferred_element_type=F32)
    rw = jnp.concatenate([rw_ref[...], jnp.zeros((LANES - TOP_K, tm), F32)], axis=0).T
    routed = rw[:, 0:1] * r_ref[0].astype(F32)
    for k in range(1, TOP_K):
        routed = routed + rw[:, k:k + 1] * r_ref[k].astype(F32)
    y_ref[...] = _layer_norm(DEEPNORM_ALPHA * x + (routed + shared), g_ref[...], b_ref[...])


def _moe_out(x1, routed, route_w, w_sh_gate, w_sh_up, w_sh_down, ln_g, ln_b, tm):
    N = x1.shape[0]
    row = lambda i: (i, 0)
    fixed = lambda i: (0, 0)
    return pl.pallas_call(
        _moe_out_kernel,
        grid=(N // tm,),
        in_specs=[pl.BlockSpec((tm, D_MODEL), row), pl.BlockSpec((TOP_K, tm, D_MODEL), lambda i: (0, i, 0)),
                  pl.BlockSpec((TOP_K, tm), lambda i: (0, i)),
                  pl.BlockSpec(w_sh_gate.shape, fixed), pl.BlockSpec(w_sh_up.shape, fixed),
                  pl.BlockSpec(w_sh_down.shape, fixed),
                  pl.BlockSpec((1, D_MODEL), fixed), pl.BlockSpec((1, D_MODEL), fixed)],
        out_specs=pl.BlockSpec((tm, D_MODEL), row),
        out_shape=jax.ShapeDtypeStruct((N, D_MODEL), F32),
        compiler_params=_params("parallel"),
        name="moe_out",
    )(x1, routed, route_w, w_sh_gate.astype(BF), w_sh_up.astype(BF), w_sh_down.astype(BF),
      ln_g.astype(F32)[None, :], ln_b.astype(F32)[None, :])


EXPERT_PARTS = 2

def _tiles(N, T):
    return min(512, T), min(256, T), MOBA_BLOCK, min(1024, N * TOP_K)


def _mixer_layer(x2, T, w_in, b_forget, diff_lambda, diff_subln, w_branch, w_out, ln_g, ln_b, lambda_init):
    N = x2.shape[0]
    tm_proj, tm_row, tq, _ = _tiles(N, T)
    w, wvt, w_gates, bf = _in_proj_weights(w_in, b_forget)
    qk, vt, misc = _in_proj(x2, w, wvt, bf, _rope_tables(T), T, tm_proj)
    dl = diff_lambda.astype(F32)
    lam = jnp.exp(jnp.sum(dl[0] * dl[1])) - jnp.exp(jnp.sum(dl[2] * dl[3])) + lambda_init
    o_a = _moba(qk, vt, T, tq)
    o_b = _diff(qk, vt, lam, diff_subln, lambda_init, T, tq)
    o_c = _fox(qk, vt, misc, T, tq)
    o_d = _dsa(qk, vt, misc, T, tq)
    return _mix_out(x2, (o_a, o_b, o_c, o_d), w_gates, w_branch, w_out, ln_g, ln_b, tm_row)


def _moe_layer(x1, x1b, T, layer, w_router, router_bias, w_exp_gate, w_exp_up, w_exp_down,
               w_sh_gate, w_sh_up, w_sh_down, ln_g, ln_b):
    N = x1.shape[0]
    _, tm_row, _, bm = _tiles(N, T)
    eidx_t, wsel_t = _router(x1, w_router, router_bias, tm_row)
    row_tok, pos, plans = _dispatch_plan(eidx_t, bm, EXPERT_PARTS)
    rows_part = row_tok.shape[0] // EXPERT_PARTS
    xs_parts = [x1b[row_tok[p * rows_part:(p + 1) * rows_part]] for p in range(EXPERT_PARTS)]
    ys = _experts(xs_parts, plans, layer, w_exp_gate, w_exp_up, w_exp_down, bm)
    return _moe_out(x1, ys[pos], wsel_t, w_sh_gate, w_sh_up, w_sh_down, ln_g, ln_b, tm_row)


def kernel(x, w_in, b_forget, diff_lambda, diff_subln, w_branch, w_out, ln1_g, ln1_b, w_router, router_bias,
           w_exp_gate, w_exp_up, w_exp_down, w_sh_gate, w_sh_up, w_sh_down, ln2_g, ln2_b):
    B, T, D = x.shape
    x2 = x.reshape(B * T, D)
    for l in range(DEPTH):
        lambda_init = 0.8 - 0.6 * math.exp(-0.3 * l)
        x1, x1b = _mixer_layer(x2, T, w_in[l], b_forget[l], diff_lambda[l], diff_subln[l], w_branch[l],
                               w_out[l], ln1_g[l], ln1_b[l], lambda_init)
        x2 = _moe_layer(x1, x1b, T, l, w_router[l], router_bias[l], w_exp_gate, w_exp_up, w_exp_down,
                        w_sh_gate[l], w_sh_up[l], w_sh_down[l], ln2_g[l], ln2_b[l])
    return x2.reshape(B, T, D)
```

```python
import functools
import math

import jax
import jax.numpy as jnp
from jax import lax
from jax.experimental import pallas as pl
from jax.experimental.pallas import tpu as pltpu

F32 = jnp.float32
BF = jnp.bfloat16

D_MODEL = 1024
DEPTH = 2
HEAD_DIM = 64
N_HEADS = 4
DIFF_DIM = HEAD_DIM // 2
N_IDX_HEADS = 4
IDX_DIM = 64
BRANCH_WIDTH = N_HEADS * HEAD_DIM
N_BRANCHES = 4
MOBA_BLOCK = 256
MOBA_TOPK = 3
DSA_TOPK_MAX = 256
ROPE_THETA = 10000.0
N_EXPERTS = 256
TOP_K = 8
N_GROUPS = 8
TOPK_GROUPS = 4
EXPERT_DIM = 256
ROUTED_SCALE = 2.5
LN_EPS = 1e-5
DEEPNORM_ALPHA = (2 * DEPTH) ** 0.25

IN_SEGMENTS = (
    ('moba_q', BRANCH_WIDTH), ('moba_k', BRANCH_WIDTH), ('moba_v', BRANCH_WIDTH),
    ('diff_q', BRANCH_WIDTH), ('diff_k', BRANCH_WIDTH), ('diff_v', BRANCH_WIDTH),
    ('fox_q', BRANCH_WIDTH), ('fox_k', BRANCH_WIDTH), ('fox_v', BRANCH_WIDTH), ('fox_f', N_HEADS),
    ('dsa_q', BRANCH_WIDTH), ('dsa_k', BRANCH_WIDTH), ('dsa_v', BRANCH_WIDTH),
    ('idx_q', N_IDX_HEADS * IDX_DIM), ('idx_k', IDX_DIM), ('idx_w', N_IDX_HEADS),
    ('gates', N_BRANCHES * D_MODEL),
)

LANES = 128
SUBLANES = 8
SEG = BRANCH_WIDTH
NEG = -1e30
INT_MIN = -2 ** 31
VMEM_LIMIT = 48 * 1024 * 1024

LOG2E = math.log2(math.e)
_QSCALE = HEAD_DIM ** -0.5 * LOG2E
PROJ_SEGS = (
    ('moba_q', 64, _QSCALE), ('moba_k', 64, 1.0),
    ('diff_q', 32, DIFF_DIM ** -0.5 * LOG2E), ('diff_k', 32, 1.0),
    ('fox_q', 0, _QSCALE), ('fox_k', 0, 1.0),
    ('dsa_q', 64, _QSCALE), ('dsa_k', 64, 1.0),
    ('idx_q', 64, IDX_DIM ** -0.5), ('idx_k4', 64, 1.0),
)
SEG_ID = {name: i for i, (name, _, _) in enumerate(PROJ_SEGS)}
N_SEG = len(PROJ_SEGS)
V_SEGS = ('moba_v', 'diff_v', 'fox_v', 'dsa_v')
V_ID = {name: i for i, name in enumerate(V_SEGS)}
MISC_LOGF = 0
MISC_IDXW = 4


def _params(*sem):
    return pltpu.CompilerParams(dimension_semantics=sem, vmem_limit_bytes=VMEM_LIMIT)


def _iota(shape, dim):
    return lax.broadcasted_iota(jnp.int32, shape, dim)


def _dot_nt(a, b):
    return lax.dot_general(a, b, (((1,), (1,)), ((), ())), preferred_element_type=F32)


def _sigmoid(z):
    return 1.0 / (1.0 + jnp.exp(-z))


def _layer_norm(y, g, b):
    mu = jnp.mean(y, axis=-1, keepdims=True)
    yc = y - mu
    var = jnp.mean(yc * yc, axis=-1, keepdims=True)
    return yc * lax.rsqrt(var + LN_EPS) * g + b


def _swap_halves(a, half):
    w = a.shape[-1]
    first = (_iota(a.shape, 1) & (2 * half - 1)) < half
    return jnp.where(first, pltpu.roll(a, w - half, 1), pltpu.roll(a, half, 1))


def _in_proj_kernel(x_ref, w_ref, wvt_ref, c64_ref, s64_ref, c32_ref, s32_ref, bf_ref,
                    qk_ref, vt_ref, misc_ref):
    xb = x_ref[...].astype(BF)
    for s, (_, rot, scale) in enumerate(PROJ_SEGS):
        acc = jnp.dot(xb, w_ref[:, s * SEG:(s + 1) * SEG], preferred_element_type=F32)
        if rot == 64:
            acc = acc * c64_ref[...] + _swap_halves(acc, 32) * s64_ref[...]
        elif rot == 32:
            acc = acc * c32_ref[...] + _swap_halves(acc, 16) * s32_ref[...]
        if scale != 1.0:
            acc = acc * scale
        qk_ref[:, s * SEG:(s + 1) * SEG] = acc.astype(BF)
    vt_ref[...] = _dot_nt(wvt_ref[...], xb).astype(BF)
    m = jnp.dot(xb, w_ref[:, N_SEG * SEG:N_SEG * SEG + LANES], preferred_element_type=F32)
    z = m + bf_ref[...]
    logf = jnp.minimum(z, 0.0) - jnp.log1p(jnp.exp(-jnp.abs(z)))
    lane = _iota(m.shape, 1)
    misc_ref[...] = jnp.where(lane < MISC_IDXW, logf, m)


def _rope_tables(T):
    pos = jnp.arange(T).astype(F32)

    def tab(group, reps):
        half = group // 2
        inv_freq = ROPE_THETA ** (-jnp.arange(half, dtype=F32) / half)
        ang = pos[:, None] * inv_freq[None, :]
        cos, sin = jnp.cos(ang), jnp.sin(ang)
        return (jnp.tile(jnp.concatenate([cos, cos], -1), (1, reps)),
                jnp.tile(jnp.concatenate([-sin, sin], -1), (1, reps)))

    return tab(64, SEG // 64) + tab(32, SEG // 32)


def _in_proj_weights(w_in, b_forget):
    parts, off = {}, 0
    for name, width in IN_SEGMENTS:
        parts[name] = w_in[:, off:off + width]
        off += width
    parts['idx_k4'] = jnp.tile(parts['idx_k'], (1, N_IDX_HEADS))
    misc = jnp.concatenate([parts['fox_f'], parts['idx_w'],
                            jnp.zeros((D_MODEL, LANES - 2 * N_HEADS), w_in.dtype)], axis=1)
    w = jnp.concatenate([parts[name] for name, _, _ in PROJ_SEGS] + [misc], axis=1).astype(BF)
    wvt = jnp.concatenate([parts[name] for name in V_SEGS], axis=1).T.astype(BF)
    bf = jnp.zeros((1, LANES), F32).at[0, MISC_LOGF:MISC_LOGF + N_HEADS].set(b_forget.astype(F32))
    return w, wvt, parts['gates'].astype(BF), bf


def _in_proj(x2, w, wvt, bf, tables, T, tm):
    N = x2.shape[0]
    nt = T // tm
    tab_spec = pl.BlockSpec((tm, SEG), lambda i: (i % nt, 0))
    fixed = lambda i: (0, 0)
    return pl.pallas_call(
        _in_proj_kernel,
        grid=(N // tm,),
        in_specs=[pl.BlockSpec((tm, D_MODEL), lambda i: (i, 0)),
                  pl.BlockSpec(w.shape, fixed), pl.BlockSpec(wvt.shape, fixed),
                  tab_spec, tab_spec, tab_spec, tab_spec,
                  pl.BlockSpec((1, LANES), fixed)],
        out_specs=[pl.BlockSpec((tm, N_SEG * SEG), lambda i: (i, 0)),
                   pl.BlockSpec((len(V_SEGS) * SEG, tm), lambda i: (0, i)),
                   pl.BlockSpec((tm, LANES), lambda i: (i, 0))],
        out_shape=[jax.ShapeDtypeStruct((N, N_SEG * SEG), BF),
                   jax.ShapeDtypeStruct((len(V_SEGS) * SEG, N), BF),
                   jax.ShapeDtypeStruct((N, LANES), F32)],
        compiler_params=_params("parallel"),
        name="in_proj",
    )(x2, w, wvt, *tables, bf)


def _group_mask(shape, shift, g):
    return (_iota(shape, 1) >> shift) == g


def _stack_groups(q, shift, n_groups):
    zero = jnp.zeros_like(q)
    return jnp.concatenate([jnp.where(_group_mask(q.shape, shift, g), q, zero) for g in range(n_groups)],
                           axis=0)


def _cols(a, g, tq):
    return a[:, g * tq:(g + 1) * tq]


def _head_rows(a, h):
    return a[h * HEAD_DIM:(h + 1) * HEAD_DIM]


ONES_ROWS = 16


def _flash_init(n_groups, tq, n_sets):
    return (jnp.full((1, n_groups * tq), NEG, F32), jnp.zeros((1, n_groups * tq), F32),
            tuple(jnp.zeros((SEG, tq), F32) for _ in range(n_sets)))


def _flash_update(s, carry, vt, tq, head_sets):
    m, l, accs = carry
    m_new = jnp.maximum(m, jnp.max(s, axis=0, keepdims=True))
    alpha = jnp.exp2(m - m_new)
    pb = jnp.exp2(s - m_new).astype(BF)
    ones = jnp.ones((ONES_ROWS, vt.shape[1]), BF)
    new_accs, p_sum = [], {}
    for acc, groups in zip(accs, head_sets):
        parts = []
        for h, g in enumerate(groups):
            r = jnp.dot(jnp.concatenate([_head_rows(vt, h), ones], axis=0), _cols(pb, g, tq),
                        preferred_element_type=F32)
            parts.append(_cols(alpha, g, tq) * _head_rows(acc, h) + r[0:HEAD_DIM])
            p_sum[g] = r[HEAD_DIM:HEAD_DIM + 1]
        new_accs.append(jnp.concatenate(parts, axis=0))
    l = alpha * l + jnp.concatenate([p_sum[g] for g in sorted(p_sum)], axis=1)
    return m_new, l, tuple(new_accs)


def _causal(n_groups, tq):
    shape = (tq, n_groups * tq)
    return _iota(shape, 0) <= (_iota(shape, 1) & (tq - 1))


def _normalised(acc, l, groups, tq):
    return jnp.concatenate([_head_rows(acc, h) * (1.0 / _cols(l, g, tq)) for h, g in enumerate(groups)],
                           axis=0)


def _tile_lanes(a, n):
    return jnp.concatenate([a] * n, axis=1) if n > 1 else a


HEADS = tuple(range(N_HEADS))


def _flash_attend(i, tq, n_groups, head_sets, qs, k_ref, vt_ref, s_ref, post, last):
    def raw(k0):
        return _dot_nt(k_ref[pl.ds(k0, tq), :], qs)

    if s_ref is not None:
        s_ref[0] = raw(0)

    def body(c, carry):
        k0 = pl.multiple_of(c * tq, tq)
        if s_ref is None:
            s = post(raw(k0), c, k0)
        else:
            s = post(s_ref[c & 1], c, k0)
            s_ref[(c + 1) & 1] = raw(pl.multiple_of(k0 + tq, tq))
        return _flash_update(s, carry, vt_ref[:, pl.ds(k0, tq)], tq, head_sets)

    carry = lax.fori_loop(0, i, body, _flash_init(n_groups, tq, len(head_sets)))
    k0 = pl.multiple_of(i * tq, tq)
    s = last(raw(k0) if s_ref is None else s_ref[i & 1], k0)
    _, l, accs = _flash_update(s, carry, vt_ref[:, pl.ds(k0, tq)], tq, head_sets)
    return l, accs


def _logit_scratch(n_groups, tq):
    return pltpu.VMEM((2, tq, n_groups * tq), F32)


def _attn_specs(T, tq, q_seg, k_seg, v_name):
    nq = T // tq
    v_id = V_ID[v_name]
    return [pl.BlockSpec((tq, SEG), lambda b, i: (b * nq + i, q_seg)),
            pl.BlockSpec((T, SEG), lambda b, i: (b, k_seg)),
            pl.BlockSpec((SEG, T), lambda b, i: (v_id, b))]


def _attn_out(N, T, tq):
    nq = T // tq
    return (pl.BlockSpec((tq, SEG), lambda b, i: (b * nq + i, 0)),
            jax.ShapeDtypeStruct((N, SEG), BF))


def _moba_kernel(q_ref, k_ref, vt_ref, o_ref, km_ref, sel_ref, s_ref, *, tq, nb, n_sel):
    i = pl.program_id(1)

    @pl.when(i == 0)
    def _():
        km_ref[...] = jnp.zeros_like(km_ref)
        for n in range(nb):
            blk = k_ref[n * tq:(n + 1) * tq, :].astype(F32)
            km_ref[n:n + 1, :] = jnp.mean(blk, axis=0, keepdims=True)

    qs = _stack_groups(q_ref[...], 6, N_HEADS)
    km = km_ref[...]
    km_hi = km.astype(BF)
    km_lo = (km - km_hi.astype(F32)).astype(BF)
    nrow = sel_ref.shape[0]
    g = (_dot_nt(km_hi, qs) + _dot_nt(km_lo, qs))[0:nrow]
    blk_id = _iota(g.shape, 0)
    g = jnp.where(blk_id < i, g, -jnp.inf)
    sel = jnp.zeros(g.shape, F32)
    for _ in range(n_sel):
        gmax = jnp.max(g, axis=0, keepdims=True)
        first = jnp.min(jnp.where(g == gmax, blk_id, nrow), axis=0, keepdims=True)
        hit = blk_id == first
        sel = jnp.where(hit & (gmax > -jnp.inf), 1.0, sel)
        g = jnp.where(hit, -jnp.inf, g)
    sel_ref[...] = sel

    l, (acc,) = _flash_attend(
        i, tq, N_HEADS, (HEADS,), qs, k_ref, vt_ref, s_ref,
        post=lambda s, c, k0: jnp.where(sel_ref[pl.ds(c, 1), :] > 0.5, s, NEG),
        last=lambda s, k0: jnp.where(_causal(N_HEADS, tq), s, NEG))
    o_ref[...] = _normalised(acc, l, HEADS, tq).T.astype(BF)


def _moba(qk, vt, T, tq):
    N = qk.shape[0]
    nb = T // MOBA_BLOCK
    assert T % MOBA_BLOCK == 0 and tq == MOBA_BLOCK and nb <= LANES
    sel_rows = -(-nb // SUBLANES) * SUBLANES
    out_spec, out_shape = _attn_out(N, T, tq)
    return pl.pallas_call(
        functools.partial(_moba_kernel, tq=tq, nb=nb, n_sel=min(MOBA_TOPK, nb - 1)),
        grid=(N // T, T // tq),
        in_specs=_attn_specs(T, tq, SEG_ID['moba_q'], SEG_ID['moba_k'], 'moba_v'),
        out_specs=out_spec, out_shape=out_shape,
        scratch_shapes=[pltpu.VMEM((LANES, SEG), F32), pltpu.VMEM((sel_rows, N_HEADS * tq), F32),
                        _logit_scratch(N_HEADS, tq)],
        compiler_params=_params("parallel", "arbitrary"),
        name="moba_attn",
    )(qk, qk, vt)


DIFF_SETS = (tuple(2 * h for h in HEADS), tuple(2 * h + 1 for h in HEADS))


def _diff_kernel(lam_ref, q_ref, k_ref, vt_ref, g_ref, o_ref, *, tq, out_scale):
    i = pl.program_id(1)
    n_groups = 2 * N_HEADS
    qs = _stack_groups(q_ref[...], 5, n_groups)
    l, (a1, a2) = _flash_attend(
        i, tq, n_groups, DIFF_SETS, qs, k_ref, vt_ref, None,
        post=lambda s, c, k0: s,
        last=lambda s, k0: jnp.where(_causal(n_groups, tq), s, NEG))
    out = _normalised(a1, l, DIFF_SETS[0], tq) - lam_ref[0] * _normalised(a2, l, DIFF_SETS[1], tq)
    normed = []
    for h in HEADS:
        oh = _head_rows(out, h)
        ms = jnp.mean(oh * oh, axis=0, keepdims=True)
        normed.append(oh * lax.rsqrt(ms + LN_EPS))
    o_ref[...] = (jnp.concatenate(normed, axis=0).T * g_ref[...] * out_scale).astype(BF)


def _diff(qk, vt, lam, subln_g, lambda_init, T, tq):
    N = qk.shape[0]
    out_spec, out_shape = _attn_out(N, T, tq)
    g = jnp.tile(subln_g.astype(F32), N_HEADS)[None, :]
    return pl.pallas_call(
        functools.partial(_diff_kernel, tq=tq, out_scale=1.0 - lambda_init),
        grid=(N // T, T // tq),
        in_specs=[pl.BlockSpec(memory_space=pltpu.SMEM)]
        + _attn_specs(T, tq, SEG_ID['diff_q'], SEG_ID['diff_k'], 'diff_v')
        + [pl.BlockSpec((1, SEG), lambda b, i: (0, 0))],
        out_specs=out_spec, out_shape=out_shape,
        compiler_params=_params("parallel", "arbitrary"),
        name="diff_attn",
    )(lam.reshape(1).astype(F32), qk, qk, vt, g)


def _fox_kernel(q_ref, k_ref, vt_ref, misc_ref, o_ref, c_ref, s_ref, *, tq):
    i = pl.program_id(1)

    @pl.when(i == 0)
    def _():
        tri = jnp.where(_iota((tq, tq), 1) <= _iota((tq, tq), 0), 1.0, 0.0).astype(BF)
        carry = [jnp.zeros((1, LANES), F32) for _ in HEADS]
        for n in range(c_ref.shape[1] // tq):
            blk = misc_ref[n * tq:(n + 1) * tq, :] * LOG2E
            for h in HEADS:
                col = jnp.broadcast_to(blk[:, MISC_LOGF + h:MISC_LOGF + h + 1], blk.shape)
                hi = col.astype(BF)
                rest = col - hi.astype(F32)
                mid = rest.astype(BF)
                lo = (rest - mid.astype(F32)).astype(BF)
                cs = carry[h] + (jnp.dot(tri, hi, preferred_element_type=F32)
                                 + jnp.dot(tri, mid, preferred_element_type=F32)
                                 + jnp.dot(tri, lo, preferred_element_type=F32))
                c_ref[h, n * tq:(n + 1) * tq, :] = cs
                carry[h] = cs[tq - 1:tq, :]

    qs = _stack_groups(q_ref[...], 6, N_HEADS)

    def decayed(s, k0):
        return jnp.concatenate(
            [_cols(s, h, tq) - _tile_lanes(c_ref[h, pl.ds(k0, tq), :], tq // LANES) for h in HEADS], axis=1)

    l, (acc,) = _flash_attend(
        i, tq, N_HEADS, (HEADS,), qs, k_ref, vt_ref, s_ref,
        post=lambda s, c, k0: decayed(s, k0),
        last=lambda s, k0: jnp.where(_causal(N_HEADS, tq), decayed(s, k0), NEG))
    o_ref[...] = _normalised(acc, l, HEADS, tq).T.astype(BF)


def _fox(qk, vt, misc, T, tq):
    N = qk.shape[0]
    out_spec, out_shape = _attn_out(N, T, tq)
    return pl.pallas_call(
        functools.partial(_fox_kernel, tq=tq),
        grid=(N // T, T // tq),
        in_specs=_attn_specs(T, tq, SEG_ID['fox_q'], SEG_ID['fox_k'], 'fox_v')
        + [pl.BlockSpec((T, LANES), lambda b, i: (b, 0))],
        out_specs=out_spec, out_shape=out_shape,
        scratch_shapes=[pltpu.VMEM((N_HEADS, T, LANES), F32), _logit_scratch(N_HEADS, tq)],
        compiler_params=_params("parallel", "arbitrary"),
        name="fox_attn",
    )(qk, qk, vt, misc)


COUNT_ROWS = 64
F32_TINY = 2.0 ** -126

def _dsa_kernel(q_ref, k_ref, vt_ref, qi_ref, ki_ref, w_ref, o_ref, key_ref, top_ref, bias_ref, s_ref,
                *, tq, n_keep):
    i = pl.program_id(1)
    n_ch = i + 1
    qpos = i * tq + _iota((1, tq), 1)

    qis = _stack_groups(qi_ref[...], 6, N_IDX_HEADS)
    wt = w_ref[...].T * (N_IDX_HEADS ** -0.5)
    w_rows = [wt[MISC_IDXW + h:MISC_IDXW + h + 1] for h in range(N_IDX_HEADS)]

    def score_body(c, carry):
        k0 = pl.multiple_of(c * tq, tq)
        d = jnp.maximum(_dot_nt(ki_ref[pl.ds(k0, tq), :], qis), 0.0)
        score = w_rows[0] * _cols(d, 0, tq)
        for h in range(1, N_IDX_HEADS):
            score = score + w_rows[h] * _cols(d, h, tq)
        sc = jnp.where((k0 + _iota((tq, tq), 0)) <= qpos, score, -jnp.inf)
        sc = jnp.where(jnp.abs(sc) < F32_TINY, 0.0, sc)
        bits = pltpu.bitcast(sc, jnp.int32)
        key_ref[pl.ds(k0, tq), :] = jnp.where(bits < 0, bits ^ jnp.int32(0x7FFFFFFF), bits)
        top_ref[pl.ds(k0, tq), :] = pltpu.bitcast(bits & jnp.int32(-65536), F32).astype(BF)
        return carry

    lax.fori_loop(0, n_ch, score_body, 0)

    @pl.when(n_ch % 2 == 1)
    def _():
        pad = pl.ds(pl.multiple_of(n_ch * tq, tq), tq)
        key_ref[pad, :] = jnp.full((tq, tq), INT_MIN, jnp.int32)
        top_ref[pad, :] = jnp.full((tq, tq), jnp.nan, BF)

    def count(src_ref, pred, dtype):
        one, zero = jnp.ones((), dtype), jnp.zeros((), dtype)

        def body(c, acc):
            k0 = pl.multiple_of(c * 2 * tq, 2 * tq)
            hit = jnp.where(pred(src_ref[pl.ds(k0, 2 * tq), :]), one, zero)
            parts = [hit[j * COUNT_ROWS:(j + 1) * COUNT_ROWS] for j in range(2 * tq // COUNT_ROWS)]
            while len(parts) > 1:
                parts = [a + b for a, b in zip(parts[0::2], parts[1::2])]
            return acc + parts[0]
        acc = lax.fori_loop(0, (n_ch + 1) // 2, body, jnp.zeros((COUNT_ROWS, tq), dtype))
        return jnp.sum(acc.astype(F32), axis=0, keepdims=True)

    def top_body(it, thr):
        cand = thr + lax.shift_left(jnp.int32(1), 31 - it)
        fbits = jnp.where(cand < 0, cand ^ jnp.int32(0x7FFFFFFF), cand) & jnp.int32(-65536)
        subnormal = ((fbits & jnp.int32(0x7F800000)) == 0) & ((fbits & jnp.int32(0x007F0000)) != 0)
        fbits = jnp.where(subnormal, jnp.where(fbits < 0, 0, jnp.int32(0x00800000)), fbits)
        cand_f = pltpu.bitcast(fbits, F32).astype(BF)
        return jnp.where(count(top_ref, lambda tt: tt >= cand_f, BF) >= n_keep, cand, thr)

    def low_body(it, thr):
        cand = thr + lax.shift_left(jnp.int32(1), 31 - it)
        return jnp.where(count(key_ref, lambda kk: kk >= cand, F32) >= n_keep, cand, thr)

    thr = lax.fori_loop(0, 16, top_body, jnp.full((1, tq), INT_MIN, jnp.int32))
    thr = lax.fori_loop(16, 32, low_body, thr)

    need = n_keep - count(key_ref, lambda kk: kk > thr, F32)
    lower = jnp.where(_iota((tq, tq), 1) <= _iota((tq, tq), 0), 1.0, 0.0).astype(BF)

    def tie_body(c, seen):
        k0 = pl.multiple_of(c * tq, tq)
        kk = key_ref[pl.ds(k0, tq), :]
        eq = jnp.where(kk == thr, 1.0, 0.0)
        rank = jnp.dot(lower, eq.astype(BF), preferred_element_type=F32) + seen
        keep = (kk > thr) | ((kk == thr) & (rank <= need))
        causal = (k0 + _iota((tq, tq), 0)) <= qpos
        bias_ref[pl.ds(k0, tq), :] = jnp.where(keep & causal, 0.0, NEG)
        return seen + jnp.sum(eq, axis=0, keepdims=True)

    lax.fori_loop(0, n_ch, tie_body, jnp.zeros((1, tq), F32))

    qs = _stack_groups(q_ref[...], 6, N_HEADS)
    biased = lambda s, k0: s + _tile_lanes(bias_ref[pl.ds(k0, tq), :], N_HEADS)
    l, (acc,) = _flash_attend(i, tq, N_HEADS, (HEADS,), qs, k_ref, vt_ref, s_ref,
                              post=lambda s, c, k0: biased(s, k0), last=biased)
    o_ref[...] = _normalised(acc, l, HEADS, tq).T.astype(BF)


def _dsa(qk, vt, misc, T, tq):
    N = qk.shape[0]
    nq = T // tq
    out_spec, out_shape = _attn_out(N, T, tq)
    return pl.pallas_call(
        functools.partial(_dsa_kernel, tq=tq, n_keep=min(DSA_TOPK_MAX, T // 4)),
        grid=(N // T, nq),
        in_specs=_attn_specs(T, tq, SEG_ID['dsa_q'], SEG_ID['dsa_k'], 'dsa_v')
        + [pl.BlockSpec((tq, SEG), lambda b, i: (b * nq + i, SEG_ID['idx_q'])),
           pl.BlockSpec((T, SEG), lambda b, i: (b, SEG_ID['idx_k4'])),
           pl.BlockSpec((tq, LANES), lambda b, i: (b * nq + i, 0))],
        out_specs=out_spec, out_shape=out_shape,
        scratch_shapes=[pltpu.VMEM(((nq + nq % 2) * tq, tq), jnp.int32),
                        pltpu.VMEM(((nq + nq % 2) * tq, tq), BF), pltpu.VMEM((T, tq), F32),
                        _logit_scratch(N_HEADS, tq)],
        compiler_params=_params("parallel", "arbitrary"),
        name="dsa_attn",
    )(qk, qk, vt, qk, qk, misc)


def _mix_out_kernel(x_ref, oa_ref, ob_ref, oc_ref, od_ref, wg_ref, wb_ref, wo_ref, g_ref, b_ref,
                    y_ref, yb_ref):
    x = x_ref[...]
    xb = x.astype(BF)
    merged = jnp.zeros(x.shape, F32)
    for n, o_ref in enumerate((oa_ref, ob_ref, oc_ref, od_ref)):
        gate = _sigmoid(jnp.dot(xb, wg_ref[:, n * D_MODEL:(n + 1) * D_MODEL], preferred_element_type=F32))
        merged = merged + gate * jnp.dot(o_ref[...], wb_ref[n], preferred_element_type=F32)
    h = jnp.dot(merged.astype(BF), wo_ref[...], preferred_element_type=F32)
    y = _layer_norm(DEEPNORM_ALPHA * x + h, g_ref[...], b_ref[...])
    y_ref[...] = y
    yb_ref[...] = y.astype(BF)


def _mix_out(x2, branches, w_gates, w_branch, w_out, ln_g, ln_b, tm):
    N = x2.shape[0]
    row = lambda i: (i, 0)
    fixed2 = lambda i: (0, 0)
    return pl.pallas_call(
        _mix_out_kernel,
        grid=(N // tm,),
        in_specs=[pl.BlockSpec((tm, D_MODEL), row)] + [pl.BlockSpec((tm, SEG), row)] * N_BRANCHES
        + [pl.BlockSpec(w_gates.shape, fixed2),
           pl.BlockSpec(w_branch.shape, lambda i: (0, 0, 0)),
           pl.BlockSpec(w_out.shape, fixed2),
           pl.BlockSpec((1, D_MODEL), fixed2), pl.BlockSpec((1, D_MODEL), fixed2)],
        out_specs=[pl.BlockSpec((tm, D_MODEL), row), pl.BlockSpec((tm, D_MODEL), row)],
        out_shape=[jax.ShapeDtypeStruct((N, D_MODEL), F32), jax.ShapeDtypeStruct((N, D_MODEL), BF)],
        compiler_params=_params("parallel"),
        name="mix_out",
    )(x2, *branches, w_gates, w_branch.astype(BF), w_out.astype(BF),
      ln_g.astype(F32)[None, :], ln_b.astype(F32)[None, :])


def _split_bf16(a):
    hi = a.astype(BF)
    return hi, (a - hi.astype(F32)).astype(BF)


def _router_kernel(x_ref, whi_ref, wlo_ref, rb_ref, idx_ref, wsel_ref):
    xhi, xlo = _split_bf16(x_ref[...])
    whi = whi_ref[...]
    logits = _dot_nt(whi, xhi) + _dot_nt(whi, xlo) + _dot_nt(wlo_ref[...], xhi)
    tm = logits.shape[1]
    scores = _sigmoid(logits)
    biased = scores + _tile_lanes(rb_ref[...], tm // LANES)
    per_group = N_EXPERTS // N_GROUPS
    gs = []
    for g in range(N_GROUPS):
        bg = biased[g * per_group:(g + 1) * per_group]
        row = _iota(bg.shape, 0)
        m1 = jnp.max(bg, axis=0, keepdims=True)
        i1 = jnp.min(jnp.where(bg == m1, row, per_group), axis=0, keepdims=True)
        m2 = jnp.max(jnp.where(row == i1, -jnp.inf, bg), axis=0, keepdims=True)
        gs.append(m1 + m2)
    kept = []
    for g in range(N_GROUPS):
        rank = jnp.zeros((1, tm), F32)
        for o in range(N_GROUPS):
            if o != g:
                beats = (gs[o] >= gs[g]) if o < g else (gs[o] > gs[g])
                rank = rank + jnp.where(beats, 1.0, 0.0)
        kept.append(jnp.where(rank < TOPK_GROUPS, biased[g * per_group:(g + 1) * per_group], -jnp.inf))
    masked = jnp.concatenate(kept, axis=0)
    eid = _iota(masked.shape, 0)
    picks, weights = [], []
    for _ in range(TOP_K):
        mx = jnp.max(masked, axis=0, keepdims=True)
        pick = jnp.min(jnp.where(masked == mx, eid, N_EXPERTS), axis=0, keepdims=True)
        hit = eid == pick
        weights.append(jnp.sum(jnp.where(hit, scores, 0.0), axis=0, keepdims=True))
        masked = jnp.where(hit, -jnp.inf, masked)
        picks.append(pick)
    wsum = weights[0]
    for wk in weights[1:]:
        wsum = wsum + wk
    idx_ref[...] = jnp.concatenate(picks, axis=0)
    wsel_ref[...] = jnp.concatenate(weights, axis=0) / wsum * ROUTED_SCALE


def _router(x1, w_router, router_bias, tm):
    N = x1.shape[0]
    whi, wlo = _split_bf16(w_router.astype(F32).T)
    rb = jnp.broadcast_to(router_bias.astype(F32)[:, None], (N_EXPERTS, LANES))
    fixed = lambda i: (0, 0)
    col = lambda i: (0, i)
    return pl.pallas_call(
        _router_kernel,
        grid=(N // tm,),
        in_specs=[pl.BlockSpec((tm, D_MODEL), lambda i: (i, 0)), pl.BlockSpec(whi.shape, fixed),
                  pl.BlockSpec(wlo.shape, fixed), pl.BlockSpec(rb.shape, fixed)],
        out_specs=[pl.BlockSpec((TOP_K, tm), col), pl.BlockSpec((TOP_K, tm), col)],
        out_shape=[jax.ShapeDtypeStruct((TOP_K, N), jnp.int32), jax.ShapeDtypeStruct((TOP_K, N), F32)],
        compiler_params=_params("parallel"),
        name="router",
    )(x1, whi, wlo, rb)


def _expert_kernel(blk_ref, exp_ref, lo_ref, hi_ref, first_ref, newe_ref,
                   xs_ref, wg_ref, wu_ref, wd_ref, y_ref, wgb_ref, wub_ref, wdb_ref, *, bm):
    t = pl.program_id(0)
    lo, hi = lo_ref[t], hi_ref[t]

    @pl.when(newe_ref[t] == 1)
    def _():
        wgb_ref[...] = wg_ref[...].astype(BF)
        wub_ref[...] = wu_ref[...].astype(BF)
        wdb_ref[...] = wd_ref[...].astype(BF)

    @pl.when(first_ref[t] == 1)
    def _():
        y_ref[...] = jnp.zeros_like(y_ref)

    sub = bm // 2
    base = blk_ref[t] * bm

    def work(j0, n_sub):
        rows = slice(j0 * sub, (j0 + n_sub) * sub)
        xs = xs_ref[rows, :]
        g = jnp.dot(xs, wgb_ref[...], preferred_element_type=F32)
        u = jnp.dot(xs, wub_ref[...], preferred_element_type=F32)
        h = (g * _sigmoid(g) * u).astype(BF)
        y = jnp.dot(h, wdb_ref[...], preferred_element_type=F32)
        row = base + j0 * sub + _iota((n_sub * sub, 1), 0)
        mine = (row >= lo) & (row < hi)
        y_ref[rows, :] = jnp.where(mine, y.astype(y_ref.dtype), y_ref[rows, :])

    need = [(hi > lo) & (lo < base + (j + 1) * sub) & (hi > base + j * sub) for j in range(2)]
    pl.when(need[0] & need[1])(lambda: work(0, 2))
    pl.when(need[0] & jnp.logical_not(need[1]))(lambda: work(0, 1))
    pl.when(jnp.logical_not(need[0]) & need[1])(lambda: work(1, 1))


def _expert_kernel_onto(blk_ref, exp_ref, lo_ref, hi_ref, first_ref, newe_ref,
                        xs_ref, wg_ref, wu_ref, wd_ref, prev_ref, y_ref, *scratch, bm):
    del prev_ref
    _expert_kernel(blk_ref, exp_ref, lo_ref, hi_ref, first_ref, newe_ref,
                   xs_ref, wg_ref, wu_ref, wd_ref, y_ref, *scratch, bm=bm)


def _experts(xs_parts, plans, layer, w_gate, w_up, w_down, bm):
    rows_part = xs_parts[0].shape[0]
    A = rows_part * len(xs_parts)
    wspec = lambda shape: pl.BlockSpec((None, None) + shape, lambda t, blk, exp, *_: (layer, exp[t], 0, 0))
    ys = None
    for p, (xs, plan) in enumerate(zip(xs_parts, plans)):
        blk0 = p * rows_part // bm
        in_specs = [pl.BlockSpec((bm, D_MODEL), lambda t, blk, *_, blk0=blk0: (blk[t] - blk0, 0)),
                    wspec((D_MODEL, EXPERT_DIM)), wspec((D_MODEL, EXPERT_DIM)), wspec((EXPERT_DIM, D_MODEL))]
        args = (*plan, xs, w_gate, w_up, w_down)
        if ys is not None:
            in_specs.append(pl.BlockSpec(memory_space=pl.ANY))
            args += (ys,)
        ys = pl.pallas_call(
            functools.partial(_expert_kernel if ys is None else _expert_kernel_onto, bm=bm),
            grid_spec=pltpu.PrefetchScalarGridSpec(
                num_scalar_prefetch=len(plan),
                grid=(plan[0].shape[0],),
                in_specs=in_specs,
                out_specs=pl.BlockSpec((bm, D_MODEL), lambda t, blk, *_: (blk[t], 0)),
                scratch_shapes=[pltpu.VMEM((D_MODEL, EXPERT_DIM), BF), pltpu.VMEM((D_MODEL, EXPERT_DIM), BF),
                                pltpu.VMEM((EXPERT_DIM, D_MODEL), BF)],
            ),
            out_shape=jax.ShapeDtypeStruct((A, D_MODEL), BF),
            input_output_aliases={} if ys is None else {len(args) - 1: 0},
            compiler_params=_params("arbitrary"),
            name="experts",
        )(*args)
    return ys


def _dispatch_plan(eidx_t, bm, n_parts):
    N = eidx_t.shape[1]
    A, E = N * TOP_K, N_EXPERTS
    assert A % (bm * n_parts) == 0
    ids = jnp.arange(A, dtype=jnp.int32)
    se, sid = lax.sort((eidx_t.reshape(A), ids), num_keys=1, is_stable=True)
    _, pos = lax.sort((sid, ids), num_keys=1)
    start = jnp.searchsorted(se, jnp.arange(E, dtype=jnp.int32), side='left').astype(jnp.int32)
    changed = lambda a: jnp.concatenate([jnp.ones((1,), jnp.int32), (a[1:] != a[:-1]).astype(jnp.int32)])
    plans = []
    for p in range(n_parts):
        a0, a1 = p * (A // n_parts), (p + 1) * (A // n_parts)
        lo = jnp.sort(jnp.concatenate([jnp.arange(a0, a1, bm, dtype=jnp.int32), jnp.clip(start, a0, a1)]))
        hi = jnp.concatenate([lo[1:], jnp.full((1,), a1, jnp.int32)])
        blk = jnp.minimum(lo // bm, a1 // bm - 1)
        exp = se[jnp.minimum(lo, A - 1)]
        plans.append((blk, exp, lo, hi, changed(blk), changed(exp)))
    return sid % N, pos.reshape(TOP_K, N), plans


def _moe_out_kernel(x_ref, r_ref, rw_ref, wg_ref, wu_ref, wd_ref, g_ref, b_ref, y_ref):
    x = x_ref[...]
    tm = x.shape[0]
    xb = x.astype(BF)
    g = jnp.dot(xb, wg_ref[...], preferred_element_type=F32)
    u = jnp.dot(xb, wu_ref[...], preferred_element_type=F32)
    h = (g * _sigmoid(g) * u).astype(BF)
    shared = jnp.dot(h, wd_ref[...], preferred_element_type=F32)
    rw = jnp.concatenate([rw_ref[...], jnp.zeros((LANES - TOP_K, tm), F32)], axis=0).T
    routed = rw[:, 0:1] * r_ref[0].astype(F32)
    for k in range(1, TOP_K):
        routed = routed + rw[:, k:k + 1] * r_ref[k].astype(F32)
    y_ref[...] = _layer_norm(DEEPNORM_ALPHA * x + (routed + shared), g_ref[...], b_ref[...])


def _moe_out(x1, routed, route_w, w_sh_gate, w_sh_up, w_sh_down, ln_g, ln_b, tm):
    N = x1.shape[0]
    row = lambda i: (i, 0)
    fixed = lambda i: (0, 0)
    return pl.pallas_call(
        _moe_out_kernel,
        grid=(N // tm,),
        in_specs=[pl.BlockSpec((tm, D_MODEL), row), pl.BlockSpec((TOP_K, tm, D_MODEL), lambda i: (0, i, 0)),
                  pl.BlockSpec((TOP_K, tm), lambda i: (0, i)),
                  pl.BlockSpec(w_sh_gate.shape, fixed), pl.BlockSpec(w_sh_up.shape, fixed),
                  pl.BlockSpec(w_sh_down.shape, fixed),
                  pl.BlockSpec((1, D_MODEL), fixed), pl.BlockSpec((1, D_MODEL), fixed)],
        out_specs=pl.BlockSpec((tm, D_MODEL), row),
        out_shape=jax.ShapeDtypeStruct((N, D_MODEL), F32),
        compiler_params=_params("parallel"),
        name="moe_out",
    )(x1, routed, route_w, w_sh_gate.astype(BF), w_sh_up.astype(BF), w_sh_down.astype(BF),
      ln_g.astype(F32)[None, :], ln_b.astype(F32)[None, :])


EXPERT_PARTS = 4

def _tiles(N, T):
    return min(512, T), min(256, T), MOBA_BLOCK, min(1024, N * TOP_K)


def _mixer_layer(x2, T, w_in, b_forget, diff_lambda, diff_subln, w_branch, w_out, ln_g, ln_b, lambda_init):
    N = x2.shape[0]
    tm_proj, tm_row, tq, _ = _tiles(N, T)
    w, wvt, w_gates, bf = _in_proj_weights(w_in, b_forget)
    qk, vt, misc = _in_proj(x2, w, wvt, bf, _rope_tables(T), T, tm_proj)
    dl = diff_lambda.astype(F32)
    lam = jnp.exp(jnp.sum(dl[0] * dl[1])) - jnp.exp(jnp.sum(dl[2] * dl[3])) + lambda_init
    o_a = _moba(qk, vt, T, tq)
    o_b = _diff(qk, vt, lam, diff_subln, lambda_init, T, tq)
    o_c = _fox(qk, vt, misc, T, tq)
    o_d = _dsa(qk, vt, misc, T, tq)
    return _mix_out(x2, (o_a, o_b, o_c, o_d), w_gates, w_branch, w_out, ln_g, ln_b, tm_row)


def _moe_layer(x1, x1b, T, layer, w_router, router_bias, w_exp_gate, w_exp_up, w_exp_down,
               w_sh_gate, w_sh_up, w_sh_down, ln_g, ln_b):
    N = x1.shape[0]
    _, tm_row, _, bm = _tiles(N, T)
    eidx_t, wsel_t = _router(x1, w_router, router_bias, tm_row)
    row_tok, pos, plans = _dispatch_plan(eidx_t, bm, EXPERT_PARTS)
    rows_part = row_tok.shape[0] // EXPERT_PARTS
    xs_parts = [x1b[row_tok[p * rows_part:(p + 1) * rows_part]] for p in range(EXPERT_PARTS)]
    ys = _experts(xs_parts, plans, layer, w_exp_gate, w_exp_up, w_exp_down, bm)
    return _moe_out(x1, ys[pos], wsel_t, w_sh_gate, w_sh_up, w_sh_down, ln_g, ln_b, tm_row)


def kernel(x, w_in, b_forget, diff_lambda, diff_subln, w_branch, w_out, ln1_g, ln1_b, w_router, router_bias,
           w_exp_gate, w_exp_up, w_exp_down, w_sh_gate, w_sh_up, w_sh_down, ln2_g, ln2_b):
    B, T, D = x.shape
    x2 = x.reshape(B * T, D)
    for l in range(DEPTH):
        lambda_init = 0.8 - 0.6 * math.exp(-0.3 * l)
        x1, x1b = _mixer_layer(x2, T, w_in[l], b_forget[l], diff_lambda[l], diff_subln[l], w_branch[l],
                               w_out[l], ln1_g[l], ln1_b[l], lambda_init)
        x2 = _moe_layer(x1, x1b, T, l, w_router[l], router_bias[l], w_exp_gate, w_exp_up, w_exp_down,
                        w_sh_gate[l], w_sh_up[l], w_sh_down[l], ln2_g[l], ln2_b[l])
    return x2.reshape(B, T, D)
```

```python
import functools
import math

import jax
import jax.numpy as jnp
from jax import lax
from jax.experimental import pallas as pl
from jax.experimental.pallas import tpu as pltpu

F32 = jnp.float32
BF = jnp.bfloat16

D_MODEL = 1024
DEPTH = 2
HEAD_DIM = 64
N_HEADS = 4
DIFF_DIM = HEAD_DIM // 2
N_IDX_HEADS = 4
IDX_DIM = 64
BRANCH_WIDTH = N_HEADS * HEAD_DIM
N_BRANCHES = 4
MOBA_BLOCK = 256
MOBA_TOPK = 3
DSA_TOPK_MAX = 256
ROPE_THETA = 10000.0
N_EXPERTS = 256
TOP_K = 8
N_GROUPS = 8
TOPK_GROUPS = 4
EXPERT_DIM = 256
ROUTED_SCALE = 2.5
LN_EPS = 1e-5
DEEPNORM_ALPHA = (2 * DEPTH) ** 0.25

IN_SEGMENTS = (
    ('moba_q', BRANCH_WIDTH), ('moba_k', BRANCH_WIDTH), ('moba_v', BRANCH_WIDTH),
    ('diff_q', BRANCH_WIDTH), ('diff_k', BRANCH_WIDTH), ('diff_v', BRANCH_WIDTH),
    ('fox_q', BRANCH_WIDTH), ('fox_k', BRANCH_WIDTH), ('fox_v', BRANCH_WIDTH), ('fox_f', N_HEADS),
    ('dsa_q', BRANCH_WIDTH), ('dsa_k', BRANCH_WIDTH), ('dsa_v', BRANCH_WIDTH),
    ('idx_q', N_IDX_HEADS * IDX_DIM), ('idx_k', IDX_DIM), ('idx_w', N_IDX_HEADS),
    ('gates', N_BRANCHES * D_MODEL),
)

LANES = 128
SUBLANES = 8
SEG = BRANCH_WIDTH
NEG = -1e30
INT_MIN = -2 ** 31
VMEM_LIMIT = 48 * 1024 * 1024

LOG2E = math.log2(math.e)
_QSCALE = HEAD_DIM ** -0.5 * LOG2E
PROJ_SEGS = (
    ('moba_q', 64, _QSCALE), ('moba_k', 64, 1.0),
    ('diff_q', 32, DIFF_DIM ** -0.5 * LOG2E), ('diff_k', 32, 1.0),
    ('fox_q', 0, _QSCALE), ('fox_k', 0, 1.0),
    ('dsa_q', 64, _QSCALE), ('dsa_k', 64, 1.0),
    ('idx_q', 64, IDX_DIM ** -0.5), ('idx_k4', 64, 1.0),
)
SEG_ID = {name: i for i, (name, _, _) in enumerate(PROJ_SEGS)}
N_SEG = len(PROJ_SEGS)
V_SEGS = ('moba_v', 'diff_v', 'fox_v', 'dsa_v')
V_ID = {name: i for i, name in enumerate(V_SEGS)}
MISC_LOGF = 0
MISC_IDXW = 4


def _params(*sem):
    return pltpu.CompilerParams(dimension_semantics=sem, vmem_limit_bytes=VMEM_LIMIT)


def _iota(shape, dim):
    return lax.broadcasted_iota(jnp.int32, shape, dim)


def _dot_nt(a, b):
    return lax.dot_general(a, b, (((1,), (1,)), ((), ())), preferred_element_type=F32)


def _sigmoid(z):
    return 1.0 / (1.0 + jnp.exp(-z))


def _layer_norm(y, g, b):
    mu = jnp.mean(y, axis=-1, keepdims=True)
    yc = y - mu
    var = jnp.mean(yc * yc, axis=-1, keepdims=True)
    return yc * lax.rsqrt(var + LN_EPS) * g + b


def _swap_halves(a, half):
    w = a.shape[-1]
    first = (_iota(a.shape, 1) & (2 * half - 1)) < half
    return jnp.where(first, pltpu.roll(a, w - half, 1), pltpu.roll(a, half, 1))


def _in_proj_kernel(x_ref, w_ref, wvt_ref, c64_ref, s64_ref, c32_ref, s32_ref, bf_ref,
                    qk_ref, vt_ref, misc_ref):
    xb = x_ref[...].astype(BF)
    for s, (_, rot, scale) in enumerate(PROJ_SEGS):
        acc = jnp.dot(xb, w_ref[:, s * SEG:(s + 1) * SEG], preferred_element_type=F32)
        if rot == 64:
            acc = acc * c64_ref[...] + _swap_halves(acc, 32) * s64_ref[...]
        elif rot == 32:
            acc = acc * c32_ref[...] + _swap_halves(acc, 16) * s32_ref[...]
        if scale != 1.0:
            acc = acc * scale
        qk_ref[:, s * SEG:(s + 1) * SEG] = acc.astype(BF)
    vt_ref[...] = _dot_nt(wvt_ref[...], xb).astype(BF)
    m = jnp.dot(xb, w_ref[:, N_SEG * SEG:N_SEG * SEG + LANES], preferred_element_type=F32)
    z = m + bf_ref[...]
    logf = jnp.minimum(z, 0.0) - jnp.log1p(jnp.exp(-jnp.abs(z)))
    lane = _iota(m.shape, 1)
    misc_ref[...] = jnp.where(lane < MISC_IDXW, logf, m)


def _rope_tables(T):
    pos = jnp.arange(T).astype(F32)

    def tab(group, reps):
        half = group // 2
        inv_freq = ROPE_THETA ** (-jnp.arange(half, dtype=F32) / half)
        ang = pos[:, None] * inv_freq[None, :]
        cos, sin = jnp.cos(ang), jnp.sin(ang)
        return (jnp.tile(jnp.concatenate([cos, cos], -1), (1, reps)),
                jnp.tile(jnp.concatenate([-sin, sin], -1), (1, reps)))

    return tab(64, SEG // 64) + tab(32, SEG // 32)


def _in_proj_weights(w_in, b_forget):
    parts, off = {}, 0
    for name, width in IN_SEGMENTS:
        parts[name] = w_in[:, off:off + width]
        off += width
    parts['idx_k4'] = jnp.tile(parts['idx_k'], (1, N_IDX_HEADS))
    misc = jnp.concatenate([parts['fox_f'], parts['idx_w'],
                            jnp.zeros((D_MODEL, LANES - 2 * N_HEADS), w_in.dtype)], axis=1)
    w = jnp.concatenate([parts[name] for name, _, _ in PROJ_SEGS] + [misc], axis=1).astype(BF)
    wvt = jnp.concatenate([parts[name] for name in V_SEGS], axis=1).T.astype(BF)
    bf = jnp.zeros((1, LANES), F32).at[0, MISC_LOGF:MISC_LOGF + N_HEADS].set(b_forget.astype(F32))
    return w, wvt, parts['gates'].astype(BF), bf


def _in_proj(x2, w, wvt, bf, tables, T, tm):
    N = x2.shape[0]
    nt = T // tm
    tab_spec = pl.BlockSpec((tm, SEG), lambda i: (i % nt, 0))
    fixed = lambda i: (0, 0)
    return pl.pallas_call(
        _in_proj_kernel,
        grid=(N // tm,),
        in_specs=[pl.BlockSpec((tm, D_MODEL), lambda i: (i, 0)),
                  pl.BlockSpec(w.shape, fixed), pl.BlockSpec(wvt.shape, fixed),
                  tab_spec, tab_spec, tab_spec, tab_spec,
                  pl.BlockSpec((1, LANES), fixed)],
        out_specs=[pl.BlockSpec((tm, N_SEG * SEG), lambda i: (i, 0)),
                   pl.BlockSpec((len(V_SEGS) * SEG, tm), lambda i: (0, i)),
                   pl.BlockSpec((tm, LANES), lambda i: (i, 0))],
        out_shape=[jax.ShapeDtypeStruct((N, N_SEG * SEG), BF),
                   jax.ShapeDtypeStruct((len(V_SEGS) * SEG, N), BF),
                   jax.ShapeDtypeStruct((N, LANES), F32)],
        compiler_params=_params("parallel"),
        name="in_proj",
    )(x2, w, wvt, *tables, bf)


def _group_mask(shape, shift, g):
    return (_iota(shape, 1) >> shift) == g


def _stack_groups(q, shift, n_groups):
    zero = jnp.zeros_like(q)
    return jnp.concatenate([jnp.where(_group_mask(q.shape, shift, g), q, zero) for g in range(n_groups)],
                           axis=0)


def _cols(a, g, tq):
    return a[:, g * tq:(g + 1) * tq]


def _head_rows(a, h):
    return a[h * HEAD_DIM:(h + 1) * HEAD_DIM]


ONES_ROWS = 16


def _flash_init(n_groups, tq, n_sets):
    return (jnp.full((1, n_groups * tq), NEG, F32), jnp.zeros((1, n_groups * tq), F32),
            tuple(jnp.zeros((SEG, tq), F32) for _ in range(n_sets)))


def _flash_update(s, carry, vt, tq, head_sets):
    m, l, accs = carry
    m_new = jnp.maximum(m, jnp.max(s, axis=0, keepdims=True))
    alpha = jnp.exp2(m - m_new)
    pb = jnp.exp2(s - m_new).astype(BF)
    ones = jnp.ones((ONES_ROWS, vt.shape[1]), BF)
    new_accs, p_sum = [], {}
    for acc, groups in zip(accs, head_sets):
        parts = []
        for h, g in enumerate(groups):
            r = jnp.dot(jnp.concatenate([_head_rows(vt, h), ones], axis=0), _cols(pb, g, tq),
                        preferred_element_type=F32)
            parts.append(_cols(alpha, g, tq) * _head_rows(acc, h) + r[0:HEAD_DIM])
            p_sum[g] = r[HEAD_DIM:HEAD_DIM + 1]
        new_accs.append(jnp.concatenate(parts, axis=0))
    l = alpha * l + jnp.concatenate([p_sum[g] for g in sorted(p_sum)], axis=1)
    return m_new, l, tuple(new_accs)


def _causal(n_groups, tq):
    shape = (tq, n_groups * tq)
    return _iota(shape, 0) <= (_iota(shape, 1) & (tq - 1))


def _normalised(acc, l, groups, tq):
    return jnp.concatenate([_head_rows(acc, h) * (1.0 / _cols(l, g, tq)) for h, g in enumerate(groups)],
                           axis=0)


def _tile_lanes(a, n):
    return jnp.concatenate([a] * n, axis=1) if n > 1 else a


HEADS = tuple(range(N_HEADS))


def _flash_attend(i, tq, n_groups, head_sets, qs, k_ref, vt_ref, s_ref, post, last):
    def raw(k0):
        return _dot_nt(k_ref[pl.ds(k0, tq), :], qs)

    if s_ref is not None:
        s_ref[0] = raw(0)

    def body(c, carry):
        k0 = pl.multiple_of(c * tq, tq)
        if s_ref is None:
            s = post(raw(k0), c, k0)
        else:
            s = post(s_ref[c & 1], c, k0)
            s_ref[(c + 1) & 1] = raw(pl.multiple_of(k0 + tq, tq))
        return _flash_update(s, carry, vt_ref[:, pl.ds(k0, tq)], tq, head_sets)

    carry = lax.fori_loop(0, i, body, _flash_init(n_groups, tq, len(head_sets)))
    k0 = pl.multiple_of(i * tq, tq)
    s = last(raw(k0) if s_ref is None else s_ref[i & 1], k0)
    _, l, accs = _flash_update(s, carry, vt_ref[:, pl.ds(k0, tq)], tq, head_sets)
    return l, accs


def _logit_scratch(n_groups, tq):
    return pltpu.VMEM((2, tq, n_groups * tq), F32)


def _attn_specs(T, tq, q_seg, k_seg, v_name):
    nq = T // tq
    v_id = V_ID[v_name]
    return [pl.BlockSpec((tq, SEG), lambda b, i: (b * nq + i, q_seg)),
            pl.BlockSpec((T, SEG), lambda b, i: (b, k_seg)),
            pl.BlockSpec((SEG, T), lambda b, i: (v_id, b))]


def _attn_out(N, T, tq):
    nq = T // tq
    return (pl.BlockSpec((tq, SEG), lambda b, i: (b * nq + i, 0)),
            jax.ShapeDtypeStruct((N, SEG), BF))


def _moba_kernel(q_ref, k_ref, vt_ref, o_ref, km_ref, sel_ref, s_ref, *, tq, nb, n_sel):
    i = pl.program_id(1)

    @pl.when(i == 0)
    def _():
        km_ref[...] = jnp.zeros_like(km_ref)
        for n in range(nb):
            blk = k_ref[n * tq:(n + 1) * tq, :].astype(F32)
            km_ref[n:n + 1, :] = jnp.mean(blk, axis=0, keepdims=True)

    qs = _stack_groups(q_ref[...], 6, N_HEADS)
    km = km_ref[...]
    km_hi = km.astype(BF)
    km_lo = (km - km_hi.astype(F32)).astype(BF)
    nrow = sel_ref.shape[0]
    g = (_dot_nt(km_hi, qs) + _dot_nt(km_lo, qs))[0:nrow]
    blk_id = _iota(g.shape, 0)
    g = jnp.where(blk_id < i, g, -jnp.inf)
    sel = jnp.zeros(g.shape, F32)
    for _ in range(n_sel):
        gmax = jnp.max(g, axis=0, keepdims=True)
        first = jnp.min(jnp.where(g == gmax, blk_id, nrow), axis=0, keepdims=True)
        hit = blk_id == first
        sel = jnp.where(hit & (gmax > -jnp.inf), 1.0, sel)
        g = jnp.where(hit, -jnp.inf, g)
    sel_ref[...] = sel

    l, (acc,) = _flash_attend(
        i, tq, N_HEADS, (HEADS,), qs, k_ref, vt_ref, s_ref,
        post=lambda s, c, k0: jnp.where(sel_ref[pl.ds(c, 1), :] > 0.5, s, NEG),
        last=lambda s, k0: jnp.where(_causal(N_HEADS, tq), s, NEG))
    o_ref[...] = _normalised(acc, l, HEADS, tq).T.astype(BF)


def _moba(qk, vt, T, tq):
    N = qk.shape[0]
    nb = T // MOBA_BLOCK
    assert T % MOBA_BLOCK == 0 and tq == MOBA_BLOCK and nb <= LANES
    sel_rows = -(-nb // SUBLANES) * SUBLANES
    out_spec, out_shape = _attn_out(N, T, tq)
    return pl.pallas_call(
        functools.partial(_moba_kernel, tq=tq, nb=nb, n_sel=min(MOBA_TOPK, nb - 1)),
        grid=(N // T, T // tq),
        in_specs=_attn_specs(T, tq, SEG_ID['moba_q'], SEG_ID['moba_k'], 'moba_v'),
        out_specs=out_spec, out_shape=out_shape,
        scratch_shapes=[pltpu.VMEM((LANES, SEG), F32), pltpu.VMEM((sel_rows, N_HEADS * tq), F32),
                        _logit_scratch(N_HEADS, tq)],
        compiler_params=_params("parallel", "arbitrary"),
        name="moba_attn",
    )(qk, qk, vt)


DIFF_SETS = (tuple(2 * h for h in HEADS), tuple(2 * h + 1 for h in HEADS))


def _diff_kernel(lam_ref, q_ref, k_ref, vt_ref, g_ref, o_ref, *, tq, out_scale):
    i = pl.program_id(1)
    n_groups = 2 * N_HEADS
    qs = _stack_groups(q_ref[...], 5, n_groups)
    l, (a1, a2) = _flash_attend(
        i, tq, n_groups, DIFF_SETS, qs, k_ref, vt_ref, None,
        post=lambda s, c, k0: s,
        last=lambda s, k0: jnp.where(_causal(n_groups, tq), s, NEG))
    out = _normalised(a1, l, DIFF_SETS[0], tq) - lam_ref[0] * _normalised(a2, l, DIFF_SETS[1], tq)
    normed = []
    for h in HEADS:
        oh = _head_rows(out, h)
        ms = jnp.mean(oh * oh, axis=0, keepdims=True)
        normed.append(oh * lax.rsqrt(ms + LN_EPS))
    o_ref[...] = (jnp.concatenate(normed, axis=0).T * g_ref[...] * out_scale).astype(BF)


def _diff(qk, vt, lam, subln_g, lambda_init, T, tq):
    N = qk.shape[0]
    out_spec, out_shape = _attn_out(N, T, tq)
    g = jnp.tile(subln_g.astype(F32), N_HEADS)[None, :]
    return pl.pallas_call(
        functools.partial(_diff_kernel, tq=tq, out_scale=1.0 - lambda_init),
        grid=(N // T, T // tq),
        in_specs=[pl.BlockSpec(memory_space=pltpu.SMEM)]
        + _attn_specs(T, tq, SEG_ID['diff_q'], SEG_ID['diff_k'], 'diff_v')
        + [pl.BlockSpec((1, SEG), lambda b, i: (0, 0))],
        out_specs=out_spec, out_shape=out_shape,
        compiler_params=_params("parallel", "arbitrary"),
        name="diff_attn",
    )(lam.reshape(1).astype(F32), qk, qk, vt, g)


def _fox_kernel(q_ref, k_ref, vt_ref, misc_ref, o_ref, c_ref, s_ref, *, tq):
    i = pl.program_id(1)

    @pl.when(i == 0)
    def _():
        tri = jnp.where(_iota((tq, tq), 1) <= _iota((tq, tq), 0), 1.0, 0.0).astype(BF)
        carry = [jnp.zeros((1, LANES), F32) for _ in HEADS]
        for n in range(c_ref.shape[1] // tq):
            blk = misc_ref[n * tq:(n + 1) * tq, :] * LOG2E
            for h in HEADS:
                col = jnp.broadcast_to(blk[:, MISC_LOGF + h:MISC_LOGF + h + 1], blk.shape)
                hi = col.astype(BF)
                rest = col - hi.astype(F32)
                mid = rest.astype(BF)
                lo = (rest - mid.astype(F32)).astype(BF)
                cs = carry[h] + (jnp.dot(tri, hi, preferred_element_type=F32)
                                 + jnp.dot(tri, mid, preferred_element_type=F32)
                                 + jnp.dot(tri, lo, preferred_element_type=F32))
                c_ref[h, n * tq:(n + 1) * tq, :] = cs
                carry[h] = cs[tq - 1:tq, :]

    qs = _stack_groups(q_ref[...], 6, N_HEADS)

    def decayed(s, k0):
        return jnp.concatenate(
            [_cols(s, h, tq) - _tile_lanes(c_ref[h, pl.ds(k0, tq), :], tq // LANES) for h in HEADS], axis=1)

    l, (acc,) = _flash_attend(
        i, tq, N_HEADS, (HEADS,), qs, k_ref, vt_ref, s_ref,
        post=lambda s, c, k0: decayed(s, k0),
        last=lambda s, k0: jnp.where(_causal(N_HEADS, tq), decayed(s, k0), NEG))
    o_ref[...] = _normalised(acc, l, HEADS, tq).T.astype(BF)


def _fox(qk, vt, misc, T, tq):
    N = qk.shape[0]
    out_spec, out_shape = _attn_out(N, T, tq)
    return pl.pallas_call(
        functools.partial(_fox_kernel, tq=tq),
        grid=(N // T, T // tq),
        in_specs=_attn_specs(T, tq, SEG_ID['fox_q'], SEG_ID['fox_k'], 'fox_v')
        + [pl.BlockSpec((T, LANES), lambda b, i: (b, 0))],
        out_specs=out_spec, out_shape=out_shape,
        scratch_shapes=[pltpu.VMEM((N_HEADS, T, LANES), F32), _logit_scratch(N_HEADS, tq)],
        compiler_params=_params("parallel", "arbitrary"),
        name="fox_attn",
    )(qk, qk, vt, misc)


COUNT_ROWS = 64
F32_TINY = 2.0 ** -126

def _dsa_kernel(q_ref, k_ref, vt_ref, qi_ref, ki_ref, w_ref, o_ref, key_ref, top_ref, bias_ref, s_ref,
                *, tq, n_keep):
    i = pl.program_id(1)
    n_ch = i + 1
    qpos = i * tq + _iota((1, tq), 1)

    qis = _stack_groups(qi_ref[...], 6, N_IDX_HEADS)
    wt = w_ref[...].T * (N_IDX_HEADS ** -0.5)
    w_rows = [wt[MISC_IDXW + h:MISC_IDXW + h + 1] for h in range(N_IDX_HEADS)]

    def score_body(c, carry):
        k0 = pl.multiple_of(c * tq, tq)
        d = jnp.maximum(_dot_nt(ki_ref[pl.ds(k0, tq), :], qis), 0.0)
        score = w_rows[0] * _cols(d, 0, tq)
        for h in range(1, N_IDX_HEADS):
            score = score + w_rows[h] * _cols(d, h, tq)
        sc = jnp.where((k0 + _iota((tq, tq), 0)) <= qpos, score, -jnp.inf)
        sc = jnp.where(jnp.abs(sc) < F32_TINY, 0.0, sc)
        bits = pltpu.bitcast(sc, jnp.int32)
        key_ref[pl.ds(k0, tq), :] = jnp.where(bits < 0, bits ^ jnp.int32(0x7FFFFFFF), bits)
        top_ref[pl.ds(k0, tq), :] = pltpu.bitcast(bits & jnp.int32(-65536), F32).astype(BF)
        return carry

    lax.fori_loop(0, n_ch, score_body, 0)

    @pl.when(n_ch % 2 == 1)
    def _():
        pad = pl.ds(pl.multiple_of(n_ch * tq, tq), tq)
        key_ref[pad, :] = jnp.full((tq, tq), INT_MIN, jnp.int32)
        top_ref[pad, :] = jnp.full((tq, tq), jnp.nan, BF)

    def count(src_ref, pred, dtype):
        one, zero = jnp.ones((), dtype), jnp.zeros((), dtype)

        def body(c, acc):
            k0 = pl.multiple_of(c * 2 * tq, 2 * tq)
            hit = jnp.where(pred(src_ref[pl.ds(k0, 2 * tq), :]), one, zero)
            parts = [hit[j * COUNT_ROWS:(j + 1) * COUNT_ROWS] for j in range(2 * tq // COUNT_ROWS)]
            while len(parts) > 1:
                parts = [a + b for a, b in zip(parts[0::2], parts[1::2])]
            return acc + parts[0]
        acc = lax.fori_loop(0, (n_ch + 1) // 2, body, jnp.zeros((COUNT_ROWS, tq), dtype))
        return jnp.sum(acc.astype(F32), axis=0, keepdims=True)

    def top_body(it, thr):
        cand = thr + lax.shift_left(jnp.int32(1), 31 - it)
        fbits = jnp.where(cand < 0, cand ^ jnp.int32(0x7FFFFFFF), cand) & jnp.int32(-65536)
        subnormal = ((fbits & jnp.int32(0x7F800000)) == 0) & ((fbits & jnp.int32(0x007F0000)) != 0)
        fbits = jnp.where(subnormal, jnp.where(fbits < 0, 0, jnp.int32(0x00800000)), fbits)
        cand_f = pltpu.bitcast(fbits, F32).astype(BF)
        return jnp.where(count(top_ref, lambda tt: tt >= cand_f, BF) >= n_keep, cand, thr)

    def low_body(it, thr):
        cand = thr + lax.shift_left(jnp.int32(1), 31 - it)
        return jnp.where(count(key_ref, lambda kk: kk >= cand, F32) >= n_keep, cand, thr)

    thr = lax.fori_loop(0, 16, top_body, jnp.full((1, tq), INT_MIN, jnp.int32))
    thr = lax.fori_loop(16, 32, low_body, thr)

    need = n_keep - count(key_ref, lambda kk: kk > thr, F32)
    lower = jnp.where(_iota((tq, tq), 1) <= _iota((tq, tq), 0), 1.0, 0.0).astype(BF)

    def tie_body(c, seen):
        k0 = pl.multiple_of(c * tq, tq)
        kk = key_ref[pl.ds(k0, tq), :]
        eq = jnp.where(kk == thr, 1.0, 0.0)
        rank = jnp.dot(lower, eq.astype(BF), preferred_element_type=F32) + seen
        keep = (kk > thr) | ((kk == thr) & (rank <= need))
        causal = (k0 + _iota((tq, tq), 0)) <= qpos
        bias_ref[pl.ds(k0, tq), :] = jnp.where(keep & causal, 0.0, NEG)
        return seen + jnp.sum(eq, axis=0, keepdims=True)

    lax.fori_loop(0, n_ch, tie_body, jnp.zeros((1, tq), F32))

    qs = _stack_groups(q_ref[...], 6, N_HEADS)
    biased = lambda s, k0: s + _tile_lanes(bias_ref[pl.ds(k0, tq), :], N_HEADS)
    l, (acc,) = _flash_attend(i, tq, N_HEADS, (HEADS,), qs, k_ref, vt_ref, s_ref,
                              post=lambda s, c, k0: biased(s, k0), last=biased)
    o_ref[...] = _normalised(acc, l, HEADS, tq).T.astype(BF)


def _dsa(qk, vt, misc, T, tq):
    N = qk.shape[0]
    nq = T // tq
    out_spec, out_shape = _attn_out(N, T, tq)
    return pl.pallas_call(
        functools.partial(_dsa_kernel, tq=tq, n_keep=min(DSA_TOPK_MAX, T // 4)),
        grid=(N // T, nq),
        in_specs=_attn_specs(T, tq, SEG_ID['dsa_q'], SEG_ID['dsa_k'], 'dsa_v')
        + [pl.BlockSpec((tq, SEG), lambda b, i: (b * nq + i, SEG_ID['idx_q'])),
           pl.BlockSpec((T, SEG), lambda b, i: (b, SEG_ID['idx_k4'])),
           pl.BlockSpec((tq, LANES), lambda b, i: (b * nq + i, 0))],
        out_specs=out_spec, out_shape=out_shape,
        scratch_shapes=[pltpu.VMEM(((nq + nq % 2) * tq, tq), jnp.int32),
                        pltpu.VMEM(((nq + nq % 2) * tq, tq), BF), pltpu.VMEM((T, tq), F32),
                        _logit_scratch(N_HEADS, tq)],
        compiler_params=_params("parallel", "arbitrary"),
        name="dsa_attn",
    )(qk, qk, vt, qk, qk, misc)


def _mix_out_kernel(x_ref, oa_ref, ob_ref, oc_ref, od_ref, wg_ref, wb_ref, wo_ref, g_ref, b_ref,
                    y_ref, yb_ref):
    x = x_ref[...]
    xb = x.astype(BF)
    merged = jnp.zeros(x.shape, F32)
    for n, o_ref in enumerate((oa_ref, ob_ref, oc_ref, od_ref)):
        gate = _sigmoid(jnp.dot(xb, wg_ref[:, n * D_MODEL:(n + 1) * D_MODEL], preferred_element_type=F32))
        merged = merged + gate * jnp.dot(o_ref[...], wb_ref[n], preferred_element_type=F32)
    h = jnp.dot(merged.astype(BF), wo_ref[...], preferred_element_type=F32)
    y = _layer_norm(DEEPNORM_ALPHA * x + h, g_ref[...], b_ref[...])
    y_ref[...] = y
    yb_ref[...] = y.astype(BF)


def _mix_out(x2, branches, w_gates, w_branch, w_out, ln_g, ln_b, tm):
    N = x2.shape[0]
    row = lambda i: (i, 0)
    fixed2 = lambda i: (0, 0)
    return pl.pallas_call(
        _mix_out_kernel,
        grid=(N // tm,),
        in_specs=[pl.BlockSpec((tm, D_MODEL), row)] + [pl.BlockSpec((tm, SEG), row)] * N_BRANCHES
        + [pl.BlockSpec(w_gates.shape, fixed2),
           pl.BlockSpec(w_branch.shape, lambda i: (0, 0, 0)),
           pl.BlockSpec(w_out.shape, fixed2),
           pl.BlockSpec((1, D_MODEL), fixed2), pl.BlockSpec((1, D_MODEL), fixed2)],
        out_specs=[pl.BlockSpec((tm, D_MODEL), row), pl.BlockSpec((tm, D_MODEL), row)],
        out_shape=[jax.ShapeDtypeStruct((N, D_MODEL), F32), jax.ShapeDtypeStruct((N, D_MODEL), BF)],
        compiler_params=_params("parallel"),
        name="mix_out",
    )(x2, *branches, w_gates, w_branch.astype(BF), w_out.astype(BF),
      ln_g.astype(F32)[None, :], ln_b.astype(F32)[None, :])


def _split_bf16(a):
    hi = a.astype(BF)
    return hi, (a - hi.astype(F32)).astype(BF)


def _router_kernel(x_ref, whi_ref, wlo_ref, rb_ref, idx_ref, wsel_ref):
    xhi, xlo = _split_bf16(x_ref[...])
    whi = whi_ref[...]
    logits = _dot_nt(whi, xhi) + _dot_nt(whi, xlo) + _dot_nt(wlo_ref[...], xhi)
    tm = logits.shape[1]
    scores = _sigmoid(logits)
    biased = scores + _tile_lanes(rb_ref[...], tm // LANES)
    per_group = N_EXPERTS // N_GROUPS
    gs = []
    for g in range(N_GROUPS):
        bg = biased[g * per_group:(g + 1) * per_group]
        row = _iota(bg.shape, 0)
        m1 = jnp.max(bg, axis=0, keepdims=True)
        i1 = jnp.min(jnp.where(bg == m1, row, per_group), axis=0, keepdims=True)
        m2 = jnp.max(jnp.where(row == i1, -jnp.inf, bg), axis=0, keepdims=True)
        gs.append(m1 + m2)
    kept = []
    for g in range(N_GROUPS):
        rank = jnp.zeros((1, tm), F32)
        for o in range(N_GROUPS):
            if o != g:
                beats = (gs[o] >= gs[g]) if o < g else (gs[o] > gs[g])
                rank = rank + jnp.where(beats, 1.0, 0.0)
        kept.append(jnp.where(rank < TOPK_GROUPS, biased[g * per_group:(g + 1) * per_group], -jnp.inf))
    masked = jnp.concatenate(kept, axis=0)
    eid = _iota(masked.shape, 0)
    picks, weights = [], []
    for _ in range(TOP_K):
        mx = jnp.max(masked, axis=0, keepdims=True)
        pick = jnp.min(jnp.where(masked == mx, eid, N_EXPERTS), axis=0, keepdims=True)
        hit = eid == pick
        weights.append(jnp.sum(jnp.where(hit, scores, 0.0), axis=0, keepdims=True))
        masked = jnp.where(hit, -jnp.inf, masked)
        picks.append(pick)
    wsum = weights[0]
    for wk in weights[1:]:
        wsum = wsum + wk
    idx_ref[...] = jnp.concatenate(picks, axis=0)
    wsel_ref[...] = jnp.concatenate(weights, axis=0) / wsum * ROUTED_SCALE


def _router(x1, w_router, router_bias, tm):
    N = x1.shape[0]
    whi, wlo = _split_bf16(w_router.astype(F32).T)
    rb = jnp.broadcast_to(router_bias.astype(F32)[:, None], (N_EXPERTS, LANES))
    fixed = lambda i: (0, 0)
    col = lambda i: (0, i)
    return pl.pallas_call(
        _router_kernel,
        grid=(N // tm,),
        in_specs=[pl.BlockSpec((tm, D_MODEL), lambda i: (i, 0)), pl.BlockSpec(whi.shape, fixed),
                  pl.BlockSpec(wlo.shape, fixed), pl.BlockSpec(rb.shape, fixed)],
        out_specs=[pl.BlockSpec((TOP_K, tm), col), pl.BlockSpec((TOP_K, tm), col)],
        out_shape=[jax.ShapeDtypeStruct((TOP_K, N), jnp.int32), jax.ShapeDtypeStruct((TOP_K, N), F32)],
        compiler_params=_params("parallel"),
        name="router",
    )(x1, whi, wlo, rb)


def _expert_kernel(blk_ref, exp_ref, lo_ref, hi_ref, first_ref, newe_ref,
                   xs_ref, wg_ref, wu_ref, wd_ref, y_ref, wgb_ref, wub_ref, wdb_ref, *, bm):
    t = pl.program_id(0)
    lo, hi = lo_ref[t], hi_ref[t]

    @pl.when(newe_ref[t] == 1)
    def _():
        wgb_ref[...] = wg_ref[...].astype(BF)
        wub_ref[...] = wu_ref[...].astype(BF)
        wdb_ref[...] = wd_ref[...].astype(BF)

    @pl.when(first_ref[t] == 1)
    def _():
        y_ref[...] = jnp.zeros_like(y_ref)

    sub = bm // 2
    base = blk_ref[t] * bm

    def work(j0, n_sub):
        rows = slice(j0 * sub, (j0 + n_sub) * sub)
        xs = xs_ref[rows, :]
        g = jnp.dot(xs, wgb_ref[...], preferred_element_type=F32)
        u = jnp.dot(xs, wub_ref[...], preferred_element_type=F32)
        h = (g * _sigmoid(g) * u).astype(BF)
        y = jnp.dot(h, wdb_ref[...], preferred_element_type=F32)
        row = base + j0 * sub + _iota((n_sub * sub, 1), 0)
        mine = (row >= lo) & (row < hi)
        y_ref[rows, :] = jnp.where(mine, y.astype(y_ref.dtype), y_ref[rows, :])

    need = [(hi > lo) & (lo < base + (j + 1) * sub) & (hi > base + j * sub) for j in range(2)]
    pl.when(need[0] & need[1])(lambda: work(0, 2))
    pl.when(need[0] & jnp.logical_not(need[1]))(lambda: work(0, 1))
    pl.when(jnp.logical_not(need[0]) & need[1])(lambda: work(1, 1))


def _expert_kernel_onto(blk_ref, exp_ref, lo_ref, hi_ref, first_ref, newe_ref,
                        xs_ref, wg_ref, wu_ref, wd_ref, prev_ref, y_ref, *scratch, bm):
    del prev_ref
    _expert_kernel(blk_ref, exp_ref, lo_ref, hi_ref, first_ref, newe_ref,
                   xs_ref, wg_ref, wu_ref, wd_ref, y_ref, *scratch, bm=bm)


def _experts(xs_parts, plans, layer, w_gate, w_up, w_down, bm):
    rows_part = xs_parts[0].shape[0]
    A = rows_part * len(xs_parts)
    wspec = lambda shape: pl.BlockSpec((None, None) + shape, lambda t, blk, exp, *_: (layer, exp[t], 0, 0))
    ys = None
    for p, (xs, plan) in enumerate(zip(xs_parts, plans)):
        blk0 = p * rows_part // bm
        in_specs = [pl.BlockSpec((bm, D_MODEL), lambda t, blk, *_, blk0=blk0: (blk[t] - blk0, 0)),
                    wspec((D_MODEL, EXPERT_DIM)), wspec((D_MODEL, EXPERT_DIM)), wspec((EXPERT_DIM, D_MODEL))]
        args = (*plan, xs, w_gate, w_up, w_down)
        if ys is not None:
            in_specs.append(pl.BlockSpec(memory_space=pl.ANY))
            args += (ys,)
        ys = pl.pallas_call(
            functools.partial(_expert_kernel if ys is None else _expert_kernel_onto, bm=bm),
            grid_spec=pltpu.PrefetchScalarGridSpec(
                num_scalar_prefetch=len(plan),
                grid=(plan[0].shape[0],),
                in_specs=in_specs,
                out_specs=pl.BlockSpec((bm, D_MODEL), lambda t, blk, *_: (blk[t], 0)),
                scratch_shapes=[pltpu.VMEM((D_MODEL, EXPERT_DIM), BF), pltpu.VMEM((D_MODEL, EXPERT_DIM), BF),
                                pltpu.VMEM((EXPERT_DIM, D_MODEL), BF)],
            ),
            out_shape=jax.ShapeDtypeStruct((A, D_MODEL), BF),
            input_output_aliases={} if ys is None else {len(args) - 1: 0},
            compiler_params=_params("arbitrary"),
            name="experts",
        )(*args)
    return ys


def _dispatch_plan(eidx_t, bm, n_parts):
    N = eidx_t.shape[1]
    A, E = N * TOP_K, N_EXPERTS
    assert A % (bm * n_parts) == 0
    ids = jnp.arange(A, dtype=jnp.int32)
    id_bits = (A - 1).bit_length()
    assert id_bits + (E - 1).bit_length() <= 31
    packed = jnp.sort((eidx_t.reshape(A) << id_bits) | ids)
    se, sid = packed >> id_bits, packed & ((1 << id_bits) - 1)
    _, pos = lax.sort((sid, ids), num_keys=1)
    start = jnp.searchsorted(se, jnp.arange(E, dtype=jnp.int32), side='left').astype(jnp.int32)
    changed = lambda a: jnp.concatenate([jnp.ones((1,), jnp.int32), (a[1:] != a[:-1]).astype(jnp.int32)])
    plans = []
    for p in range(n_parts):
        a0, a1 = p * (A // n_parts), (p + 1) * (A // n_parts)
        lo = jnp.sort(jnp.concatenate([jnp.arange(a0, a1, bm, dtype=jnp.int32), jnp.clip(start, a0, a1)]))
        hi = jnp.concatenate([lo[1:], jnp.full((1,), a1, jnp.int32)])
        blk = jnp.minimum(lo // bm, a1 // bm - 1)
        exp = se[jnp.minimum(lo, A - 1)]
        plans.append((blk, exp, lo, hi, changed(blk), changed(exp)))
    return sid % N, pos.reshape(TOP_K, N), plans


def _moe_out_kernel(x_ref, r_ref, rw_ref, wg_ref, wu_ref, wd_ref, g_ref, b_ref, y_ref):
    x = x_ref[...]
    tm = x.shape[0]
    xb = x.astype(BF)
    g = jnp.dot(xb, wg_ref[...], preferred_element_type=F32)
    u = jnp.dot(xb, wu_ref[...], preferred_element_type=F32)
    h = (g * _sigmoid(g) * u).astype(BF)
    shared = jnp.dot(h, wd_ref[...], preferred_element_type=F32)
    rw = jnp.concatenate([rw_ref[...], jnp.zeros((LANES - TOP_K, tm), F32)], axis=0).T
    routed = rw[:, 0:1] * r_ref[0].astype(F32)
    for k in range(1, TOP_K):
        routed = routed + rw[:, k:k + 1] * r_ref[k].astype(F32)
    y_ref[...] = _layer_norm(DEEPNORM_ALPHA * x + (routed + shared), g_ref[...], b_ref[...])


def _moe_out_kernel_onto(x_ref, r_ref, rw_ref, wg_ref, wu_ref, wd_ref, g_ref, b_ref, prev_ref, y_ref):
    del prev_ref
    _moe_out_kernel(x_ref, r_ref, rw_ref, wg_ref, wu_ref, wd_ref, g_ref, b_ref, y_ref)


def _moe_out(x1, routed_parts, route_w, w_sh_gate, w_sh_up, w_sh_down, ln_g, ln_b, tm):
    N = x1.shape[0]
    n_tok = routed_parts[0].shape[1]
    fixed = lambda i: (0, 0)
    weights = (w_sh_gate.astype(BF), w_sh_up.astype(BF), w_sh_down.astype(BF),
               ln_g.astype(F32)[None, :], ln_b.astype(F32)[None, :])
    y = None
    for p, routed in enumerate(routed_parts):
        t0 = p * n_tok // tm
        row = lambda i, t0=t0: (t0 + i, 0)
        in_specs = [pl.BlockSpec((tm, D_MODEL), row),
                    pl.BlockSpec((TOP_K, tm, D_MODEL), lambda i: (0, i, 0)),
                    pl.BlockSpec((TOP_K, tm), lambda i, t0=t0: (0, t0 + i)),
                    pl.BlockSpec(w_sh_gate.shape, fixed), pl.BlockSpec(w_sh_up.shape, fixed),
                    pl.BlockSpec(w_sh_down.shape, fixed),
                    pl.BlockSpec((1, D_MODEL), fixed), pl.BlockSpec((1, D_MODEL), fixed)]
        args = (x1, routed, route_w) + weights
        if y is not None:
            in_specs.append(pl.BlockSpec(memory_space=pl.ANY))
            args += (y,)
        y = pl.pallas_call(
            _moe_out_kernel if y is None else _moe_out_kernel_onto,
            grid=(n_tok // tm,),
            in_specs=in_specs,
            out_specs=pl.BlockSpec((tm, D_MODEL), row),
            out_shape=jax.ShapeDtypeStruct((N, D_MODEL), F32),
            input_output_aliases={} if y is None else {len(args) - 1: 0},
            compiler_params=_params("parallel"),
            name="moe_out",
        )(*args)
    return y


EXPERT_PARTS = 2
COMBINE_PARTS = 2

def _tiles(N, T):
    return min(512, T), min(256, T), MOBA_BLOCK, min(1024, N * TOP_K)


def _mixer_layer(x2, T, w_in, b_forget, diff_lambda, diff_subln, w_branch, w_out, ln_g, ln_b, lambda_init):
    N = x2.shape[0]
    tm_proj, tm_row, tq, _ = _tiles(N, T)
    w, wvt, w_gates, bf = _in_proj_weights(w_in, b_forget)
    qk, vt, misc = _in_proj(x2, w, wvt, bf, _rope_tables(T), T, tm_proj)
    dl = diff_lambda.astype(F32)
    lam = jnp.exp(jnp.sum(dl[0] * dl[1])) - jnp.exp(jnp.sum(dl[2] * dl[3])) + lambda_init
    o_a = _moba(qk, vt, T, tq)
    o_b = _diff(qk, vt, lam, diff_subln, lambda_init, T, tq)
    o_c = _fox(qk, vt, misc, T, tq)
    o_d = _dsa(qk, vt, misc, T, tq)
    return _mix_out(x2, (o_a, o_b, o_c, o_d), w_gates, w_branch, w_out, ln_g, ln_b, tm_row)


def _moe_layer(x1, x1b, T, layer, w_router, router_bias, w_exp_gate, w_exp_up, w_exp_down,
               w_sh_gate, w_sh_up, w_sh_down, ln_g, ln_b):
    N = x1.shape[0]
    _, tm_row, _, bm = _tiles(N, T)
    eidx_t, wsel_t = _router(x1, w_router, router_bias, tm_row)
    row_tok, pos, plans = _dispatch_plan(eidx_t, bm, EXPERT_PARTS)
    rows_part = row_tok.shape[0] // EXPERT_PARTS
    xs_parts = [x1b[row_tok[p * rows_part:(p + 1) * rows_part]] for p in range(EXPERT_PARTS)]
    ys = _experts(xs_parts, plans, layer, w_exp_gate, w_exp_up, w_exp_down, bm)
    n_tok = N // COMBINE_PARTS
    routed_parts = [ys[pos[:, p * n_tok:(p + 1) * n_tok]] for p in range(COMBINE_PARTS)]
    return _moe_out(x1, routed_parts, wsel_t, w_sh_gate, w_sh_up, w_sh_down, ln_g, ln_b, tm_row)


def kernel(x, w_in, b_forget, diff_lambda, diff_subln, w_branch, w_out, ln1_g, ln1_b, w_router, router_bias,
           w_exp_gate, w_exp_up, w_exp_down, w_sh_gate, w_sh_up, w_sh_down, ln2_g, ln2_b):
    B, T, D = x.shape
    x2 = x.reshape(B * T, D)
    for l in range(DEPTH):
        lambda_init = 0.8 - 0.6 * math.exp(-0.3 * l)
        x1, x1b = _mixer_layer(x2, T, w_in[l], b_forget[l], diff_lambda[l], diff_subln[l], w_branch[l],
                               w_out[l], ln1_g[l], ln1_b[l], lambda_init)
        x2 = _moe_layer(x1, x1b, T, l, w_router[l], router_bias[l], w_exp_gate, w_exp_up, w_exp_down,
                        w_sh_gate[l], w_sh_up[l], w_sh_down[l], ln2_g[l], ln2_b[l])
    return x2.reshape(B, T, D)
```

```python
import functools
import math

import jax
import jax.numpy as jnp
from jax import lax
from jax.experimental import pallas as pl
from jax.experimental.pallas import tpu as pltpu

F32 = jnp.float32
BF = jnp.bfloat16

D_MODEL = 1024
DEPTH = 2
HEAD_DIM = 64
N_HEADS = 4
DIFF_DIM = HEAD_DIM // 2
N_IDX_HEADS = 4
IDX_DIM = 64
BRANCH_WIDTH = N_HEADS * HEAD_DIM
N_BRANCHES = 4
MOBA_BLOCK = 256
MOBA_TOPK = 3
DSA_TOPK_MAX = 256
ROPE_THETA = 10000.0
N_EXPERTS = 256
TOP_K = 8
N_GROUPS = 8
TOPK_GROUPS = 4
EXPERT_DIM = 256
ROUTED_SCALE = 2.5
LN_EPS = 1e-5
DEEPNORM_ALPHA = (2 * DEPTH) ** 0.25

IN_SEGMENTS = (
    ('moba_q', BRANCH_WIDTH), ('moba_k', BRANCH_WIDTH), ('moba_v', BRANCH_WIDTH),
    ('diff_q', BRANCH_WIDTH), ('diff_k', BRANCH_WIDTH), ('diff_v', BRANCH_WIDTH),
    ('fox_q', BRANCH_WIDTH), ('fox_k', BRANCH_WIDTH), ('fox_v', BRANCH_WIDTH), ('fox_f', N_HEADS),
    ('dsa_q', BRANCH_WIDTH), ('dsa_k', BRANCH_WIDTH), ('dsa_v', BRANCH_WIDTH),
    ('idx_q', N_IDX_HEADS * IDX_DIM), ('idx_k', IDX_DIM), ('idx_w', N_IDX_HEADS),
    ('gates', N_BRANCHES * D_MODEL),
)

LANES = 128
SUBLANES = 8
SEG = BRANCH_WIDTH
NEG = -1e30
INT_MIN = -2 ** 31
VMEM_LIMIT = 48 * 1024 * 1024

LOG2E = math.log2(math.e)
_QSCALE = HEAD_DIM ** -0.5 * LOG2E
PROJ_SEGS = (
    ('moba_q', 64, _QSCALE), ('moba_k', 64, 1.0),
    ('diff_q', 32, DIFF_DIM ** -0.5 * LOG2E), ('diff_k', 32, 1.0),
    ('fox_q', 0, _QSCALE), ('fox_k', 0, 1.0),
    ('dsa_q', 64, _QSCALE), ('dsa_k', 64, 1.0),
    ('idx_q', 64, IDX_DIM ** -0.5), ('idx_k4', 64, 1.0),
)
SEG_ID = {name: i for i, (name, _, _) in enumerate(PROJ_SEGS)}
N_SEG = len(PROJ_SEGS)
V_SEGS = ('moba_v', 'diff_v', 'fox_v', 'dsa_v')
V_ID = {name: i for i, name in enumerate(V_SEGS)}
MISC_LOGF = 0
MISC_IDXW = 4


def _params(*sem):
    return pltpu.CompilerParams(dimension_semantics=sem, vmem_limit_bytes=VMEM_LIMIT)


def _iota(shape, dim):
    return lax.broadcasted_iota(jnp.int32, shape, dim)


def _dot_nt(a, b):
    return lax.dot_general(a, b, (((1,), (1,)), ((), ())), preferred_element_type=F32)


def _sigmoid(z):
    return 1.0 / (1.0 + jnp.exp(-z))


def _layer_norm(y, g, b):
    mu = jnp.mean(y, axis=-1, keepdims=True)
    yc = y - mu
    var = jnp.mean(yc * yc, axis=-1, keepdims=True)
    return yc * lax.rsqrt(var + LN_EPS) * g + b


def _swap_halves(a, half):
    w = a.shape[-1]
    first = (_iota(a.shape, 1) & (2 * half - 1)) < half
    return jnp.where(first, pltpu.roll(a, w - half, 1), pltpu.roll(a, half, 1))


def _in_proj_kernel(x_ref, w_ref, wvt_ref, c64_ref, s64_ref, c32_ref, s32_ref, bf_ref,
                    qk_ref, vt_ref, misc_ref):
    xb = x_ref[...].astype(BF)
    for s, (_, rot, scale) in enumerate(PROJ_SEGS):
        acc = jnp.dot(xb, w_ref[:, s * SEG:(s + 1) * SEG], preferred_element_type=F32)
        if rot == 64:
            acc = acc * c64_ref[...] + _swap_halves(acc, 32) * s64_ref[...]
        elif rot == 32:
            acc = acc * c32_ref[...] + _swap_halves(acc, 16) * s32_ref[...]
        if scale != 1.0:
            acc = acc * scale
        qk_ref[:, s * SEG:(s + 1) * SEG] = acc.astype(BF)
    vt_ref[...] = _dot_nt(wvt_ref[...], xb).astype(BF)
    m = jnp.dot(xb, w_ref[:, N_SEG * SEG:N_SEG * SEG + LANES], preferred_element_type=F32)
    z = m + bf_ref[...]
    logf = jnp.minimum(z, 0.0) - jnp.log1p(jnp.exp(-jnp.abs(z)))
    lane = _iota(m.shape, 1)
    misc_ref[...] = jnp.where(lane < MISC_IDXW, logf, m)


def _rope_tables(T):
    pos = jnp.arange(T).astype(F32)

    def tab(group, reps):
        half = group // 2
        inv_freq = ROPE_THETA ** (-jnp.arange(half, dtype=F32) / half)
        ang = pos[:, None] * inv_freq[None, :]
        cos, sin = jnp.cos(ang), jnp.sin(ang)
        return (jnp.tile(jnp.concatenate([cos, cos], -1), (1, reps)),
                jnp.tile(jnp.concatenate([-sin, sin], -1), (1, reps)))

    return tab(64, SEG // 64) + tab(32, SEG // 32)


def _in_proj_weights(w_in, b_forget):
    parts, off = {}, 0
    for name, width in IN_SEGMENTS:
        parts[name] = w_in[:, off:off + width]
        off += width
    parts['idx_k4'] = jnp.tile(parts['idx_k'], (1, N_IDX_HEADS))
    misc = jnp.concatenate([parts['fox_f'], parts['idx_w'],
                            jnp.zeros((D_MODEL, LANES - 2 * N_HEADS), w_in.dtype)], axis=1)
    w = jnp.concatenate([parts[name] for name, _, _ in PROJ_SEGS] + [misc], axis=1).astype(BF)
    wvt = jnp.concatenate([parts[name] for name in V_SEGS], axis=1).T.astype(BF)
    bf = jnp.zeros((1, LANES), F32).at[0, MISC_LOGF:MISC_LOGF + N_HEADS].set(b_forget.astype(F32))
    return w, wvt, parts['gates'].astype(BF), bf


def _in_proj(x2, w, wvt, bf, tables, T, tm):
    N = x2.shape[0]
    nt = T // tm
    tab_spec = pl.BlockSpec((tm, SEG), lambda i: (i % nt, 0))
    fixed = lambda i: (0, 0)
    return pl.pallas_call(
        _in_proj_kernel,
        grid=(N // tm,),
        in_specs=[pl.BlockSpec((tm, D_MODEL), lambda i: (i, 0)),
                  pl.BlockSpec(w.shape, fixed), pl.BlockSpec(wvt.shape, fixed),
                  tab_spec, tab_spec, tab_spec, tab_spec,
                  pl.BlockSpec((1, LANES), fixed)],
        out_specs=[pl.BlockSpec((tm, N_SEG * SEG), lambda i: (i, 0)),
                   pl.BlockSpec((len(V_SEGS) * SEG, tm), lambda i: (0, i)),
                   pl.BlockSpec((tm, LANES), lambda i: (i, 0))],
        out_shape=[jax.ShapeDtypeStruct((N, N_SEG * SEG), BF),
                   jax.ShapeDtypeStruct((len(V_SEGS) * SEG, N), BF),
                   jax.ShapeDtypeStruct((N, LANES), F32)],
        compiler_params=_params("parallel"),
        name="in_proj",
    )(x2, w, wvt, *tables, bf)


def _group_mask(shape, shift, g):
    return (_iota(shape, 1) >> shift) == g


def _stack_groups(q, shift, n_groups):
    zero = jnp.zeros_like(q)
    return jnp.concatenate([jnp.where(_group_mask(q.shape, shift, g), q, zero) for g in range(n_groups)],
                           axis=0)


def _cols(a, g, tq):
    return a[:, g * tq:(g + 1) * tq]


def _head_rows(a, h):
    return a[h * HEAD_DIM:(h + 1) * HEAD_DIM]


ONES_ROWS = 16


def _flash_init(n_groups, tq, n_sets):
    return (jnp.full((1, n_groups * tq), NEG, F32), jnp.zeros((1, n_groups * tq), F32),
            tuple(jnp.zeros((SEG, tq), F32) for _ in range(n_sets)))


def _flash_update(s, carry, vt, tq, head_sets):
    m, l, accs = carry
    m_new = jnp.maximum(m, jnp.max(s, axis=0, keepdims=True))
    alpha = jnp.exp2(m - m_new)
    pb = jnp.exp2(s - m_new).astype(BF)
    ones = jnp.ones((ONES_ROWS, vt.shape[1]), BF)
    new_accs, p_sum = [], {}
    for acc, groups in zip(accs, head_sets):
        parts = []
        for h, g in enumerate(groups):
            r = jnp.dot(jnp.concatenate([_head_rows(vt, h), ones], axis=0), _cols(pb, g, tq),
                        preferred_element_type=F32)
            parts.append(_cols(alpha, g, tq) * _head_rows(acc, h) + r[0:HEAD_DIM])
            p_sum[g] = r[HEAD_DIM:HEAD_DIM + 1]
        new_accs.append(jnp.concatenate(parts, axis=0))
    l = alpha * l + jnp.concatenate([p_sum[g] for g in sorted(p_sum)], axis=1)
    return m_new, l, tuple(new_accs)


def _causal(n_groups, tq):
    shape = (tq, n_groups * tq)
    return _iota(shape, 0) <= (_iota(shape, 1) & (tq - 1))


def _normalised(acc, l, groups, tq):
    return jnp.concatenate([_head_rows(acc, h) * (1.0 / _cols(l, g, tq)) for h, g in enumerate(groups)],
                           axis=0)


def _tile_lanes(a, n):
    return jnp.concatenate([a] * n, axis=1) if n > 1 else a


HEADS = tuple(range(N_HEADS))


def _flash_attend(i, tq, n_groups, head_sets, qs, k_ref, vt_ref, s_ref, post, last):
    def raw(k0):
        return _dot_nt(k_ref[pl.ds(k0, tq), :], qs)

    if s_ref is not None:
        s_ref[0] = raw(0)

    def body(c, carry):
        k0 = pl.multiple_of(c * tq, tq)
        if s_ref is None:
            s = post(raw(k0), c, k0)
        else:
            s = post(s_ref[c & 1], c, k0)
            s_ref[(c + 1) & 1] = raw(pl.multiple_of(k0 + tq, tq))
        return _flash_update(s, carry, vt_ref[:, pl.ds(k0, tq)], tq, head_sets)

    carry = lax.fori_loop(0, i, body, _flash_init(n_groups, tq, len(head_sets)))
    k0 = pl.multiple_of(i * tq, tq)
    s = last(raw(k0) if s_ref is None else s_ref[i & 1], k0)
    _, l, accs = _flash_update(s, carry, vt_ref[:, pl.ds(k0, tq)], tq, head_sets)
    return l, accs


def _logit_scratch(n_groups, tq):
    return pltpu.VMEM((2, tq, n_groups * tq), F32)


def _attn_specs(T, tq, q_seg, k_seg, v_name):
    nq = T // tq
    v_id = V_ID[v_name]
    return [pl.BlockSpec((tq, SEG), lambda b, i: (b * nq + i, q_seg)),
            pl.BlockSpec((T, SEG), lambda b, i: (b, k_seg)),
            pl.BlockSpec((SEG, T), lambda b, i: (v_id, b))]


def _attn_out(N, T, tq):
    nq = T // tq
    return (pl.BlockSpec((tq, SEG), lambda b, i: (b * nq + i, 0)),
            jax.ShapeDtypeStruct((N, SEG), BF))


def _moba_kernel(q_ref, k_ref, vt_ref, o_ref, km_ref, sel_ref, s_ref, *, tq, nb, n_sel):
    i = pl.program_id(1)

    @pl.when(i == 0)
    def _():
        km_ref[...] = jnp.zeros_like(km_ref)
        for n in range(nb):
            blk = k_ref[n * tq:(n + 1) * tq, :].astype(F32)
            km_ref[n:n + 1, :] = jnp.mean(blk, axis=0, keepdims=True)

    qs = _stack_groups(q_ref[...], 6, N_HEADS)
    km = km_ref[...]
    km_hi = km.astype(BF)
    km_lo = (km - km_hi.astype(F32)).astype(BF)
    nrow = sel_ref.shape[0]
    g = (_dot_nt(km_hi, qs) + _dot_nt(km_lo, qs))[0:nrow]
    blk_id = _iota(g.shape, 0)
    g = jnp.where(blk_id < i, g, -jnp.inf)
    sel = jnp.zeros(g.shape, F32)
    for _ in range(n_sel):
        gmax = jnp.max(g, axis=0, keepdims=True)
        first = jnp.min(jnp.where(g == gmax, blk_id, nrow), axis=0, keepdims=True)
        hit = blk_id == first
        sel = jnp.where(hit & (gmax > -jnp.inf), 1.0, sel)
        g = jnp.where(hit, -jnp.inf, g)
    sel_ref[...] = sel

    l, (acc,) = _flash_attend(
        i, tq, N_HEADS, (HEADS,), qs, k_ref, vt_ref, s_ref,
        post=lambda s, c, k0: jnp.where(sel_ref[pl.ds(c, 1), :] > 0.5, s, NEG),
        last=lambda s, k0: jnp.where(_causal(N_HEADS, tq), s, NEG))
    o_ref[...] = _normalised(acc, l, HEADS, tq).T.astype(BF)


def _moba(qk, vt, T, tq):
    N = qk.shape[0]
    nb = T // MOBA_BLOCK
    assert T % MOBA_BLOCK == 0 and tq == MOBA_BLOCK and nb <= LANES
    sel_rows = -(-nb // SUBLANES) * SUBLANES
    out_spec, out_shape = _attn_out(N, T, tq)
    return pl.pallas_call(
        functools.partial(_moba_kernel, tq=tq, nb=nb, n_sel=min(MOBA_TOPK, nb - 1)),
        grid=(N // T, T // tq),
        in_specs=_attn_specs(T, tq, SEG_ID['moba_q'], SEG_ID['moba_k'], 'moba_v'),
        out_specs=out_spec, out_shape=out_shape,
        scratch_shapes=[pltpu.VMEM((LANES, SEG), F32), pltpu.VMEM((sel_rows, N_HEADS * tq), F32),
                        _logit_scratch(N_HEADS, tq)],
        compiler_params=_params("parallel", "arbitrary"),
        name="moba_attn",
    )(qk, qk, vt)


DIFF_SETS = (tuple(2 * h for h in HEADS), tuple(2 * h + 1 for h in HEADS))


def _diff_kernel(lam_ref, q_ref, k_ref, vt_ref, g_ref, o_ref, *, tq, out_scale):
    i = pl.program_id(1)
    n_groups = 2 * N_HEADS
    qs = _stack_groups(q_ref[...], 5, n_groups)
    l, (a1, a2) = _flash_attend(
        i, tq, n_groups, DIFF_SETS, qs, k_ref, vt_ref, None,
        post=lambda s, c, k0: s,
        last=lambda s, k0: jnp.where(_causal(n_groups, tq), s, NEG))
    out = _normalised(a1, l, DIFF_SETS[0], tq) - lam_ref[0] * _normalised(a2, l, DIFF_SETS[1], tq)
    normed = []
    for h in HEADS:
        oh = _head_rows(out, h)
        ms = jnp.mean(oh * oh, axis=0, keepdims=True)
        normed.append(oh * lax.rsqrt(ms + LN_EPS))
    o_ref[...] = (jnp.concatenate(normed, axis=0).T * g_ref[...] * out_scale).astype(BF)


def _diff(qk, vt, lam, subln_g, lambda_init, T, tq):
    N = qk.shape[0]
    out_spec, out_shape = _attn_out(N, T, tq)
    g = jnp.tile(subln_g.astype(F32), N_HEADS)[None, :]
    return pl.pallas_call(
        functools.partial(_diff_kernel, tq=tq, out_scale=1.0 - lambda_init),
        grid=(N // T, T // tq),
        in_specs=[pl.BlockSpec(memory_space=pltpu.SMEM)]
        + _attn_specs(T, tq, SEG_ID['diff_q'], SEG_ID['diff_k'], 'diff_v')
        + [pl.BlockSpec((1, SEG), lambda b, i: (0, 0))],
        out_specs=out_spec, out_shape=out_shape,
        compiler_params=_params("parallel", "arbitrary"),
        name="diff_attn",
    )(lam.reshape(1).astype(F32), qk, qk, vt, g)


def _fox_kernel(q_ref, k_ref, vt_ref, misc_ref, o_ref, c_ref, s_ref, *, tq):
    i = pl.program_id(1)

    @pl.when(i == 0)
    def _():
        tri = jnp.where(_iota((tq, tq), 1) <= _iota((tq, tq), 0), 1.0, 0.0).astype(BF)
        carry = [jnp.zeros((1, LANES), F32) for _ in HEADS]
        for n in range(c_ref.shape[1] // tq):
            blk = misc_ref[n * tq:(n + 1) * tq, :] * LOG2E
            for h in HEADS:
                col = jnp.broadcast_to(blk[:, MISC_LOGF + h:MISC_LOGF + h + 1], blk.shape)
                hi = col.astype(BF)
                rest = col - hi.astype(F32)
                mid = rest.astype(BF)
                lo = (rest - mid.astype(F32)).astype(BF)
                cs = carry[h] + (jnp.dot(tri, hi, preferred_element_type=F32)
                                 + jnp.dot(tri, mid, preferred_element_type=F32)
                                 + jnp.dot(tri, lo, preferred_element_type=F32))
                c_ref[h, n * tq:(n + 1) * tq, :] = cs
                carry[h] = cs[tq - 1:tq, :]

    qs = _stack_groups(q_ref[...], 6, N_HEADS)

    def decayed(s, k0):
        return jnp.concatenate(
            [_cols(s, h, tq) - _tile_lanes(c_ref[h, pl.ds(k0, tq), :], tq // LANES) for h in HEADS], axis=1)

    l, (acc,) = _flash_attend(
        i, tq, N_HEADS, (HEADS,), qs, k_ref, vt_ref, s_ref,
        post=lambda s, c, k0: decayed(s, k0),
        last=lambda s, k0: jnp.where(_causal(N_HEADS, tq), decayed(s, k0), NEG))
    o_ref[...] = _normalised(acc, l, HEADS, tq).T.astype(BF)


def _fox(qk, vt, misc, T, tq):
    N = qk.shape[0]
    out_spec, out_shape = _attn_out(N, T, tq)
    return pl.pallas_call(
        functools.partial(_fox_kernel, tq=tq),
        grid=(N // T, T // tq),
        in_specs=_attn_specs(T, tq, SEG_ID['fox_q'], SEG_ID['fox_k'], 'fox_v')
        + [pl.BlockSpec((T, LANES), lambda b, i: (b, 0))],
        out_specs=out_spec, out_shape=out_shape,
        scratch_shapes=[pltpu.VMEM((N_HEADS, T, LANES), F32), _logit_scratch(N_HEADS, tq)],
        compiler_params=_params("parallel", "arbitrary"),
        name="fox_attn",
    )(qk, qk, vt, misc)


COUNT_ROWS = 64
F32_TINY = 2.0 ** -126

def _dsa_kernel(q_ref, k_ref, vt_ref, qi_ref, ki_ref, w_ref, o_ref, key_ref, top_ref, bias_ref, s_ref,
                *, tq, n_keep):
    i = pl.program_id(1)
    n_ch = i + 1
    qpos = i * tq + _iota((1, tq), 1)

    qis = _stack_groups(qi_ref[...], 6, N_IDX_HEADS)
    wt = w_ref[...].T * (N_IDX_HEADS ** -0.5)
    w_rows = [wt[MISC_IDXW + h:MISC_IDXW + h + 1] for h in range(N_IDX_HEADS)]

    def score_body(c, carry):
        k0 = pl.multiple_of(c * tq, tq)
        d = jnp.maximum(_dot_nt(ki_ref[pl.ds(k0, tq), :], qis), 0.0)
        score = w_rows[0] * _cols(d, 0, tq)
        for h in range(1, N_IDX_HEADS):
            score = score + w_rows[h] * _cols(d, h, tq)
        sc = jnp.where((k0 + _iota((tq, tq), 0)) <= qpos, score, -jnp.inf)
        sc = jnp.where(jnp.abs(sc) < F32_TINY, 0.0, sc)
        bits = pltpu.bitcast(sc, jnp.int32)
        key_ref[pl.ds(k0, tq), :] = jnp.where(bits < 0, bits ^ jnp.int32(0x7FFFFFFF), bits)
        top_ref[pl.ds(k0, tq), :] = pltpu.bitcast(bits & jnp.int32(-65536), F32).astype(BF)
        return carry

    lax.fori_loop(0, n_ch, score_body, 0)

    @pl.when(n_ch % 2 == 1)
    def _():
        pad = pl.ds(pl.multiple_of(n_ch * tq, tq), tq)
        key_ref[pad, :] = jnp.full((tq, tq), INT_MIN, jnp.int32)
        top_ref[pad, :] = jnp.full((tq, tq), jnp.nan, BF)

    def count(src_ref, pred, dtype):
        one, zero = jnp.ones((), dtype), jnp.zeros((), dtype)

        def body(c, acc):
            k0 = pl.multiple_of(c * 2 * tq, 2 * tq)
            hit = jnp.where(pred(src_ref[pl.ds(k0, 2 * tq), :]), one, zero)
            parts = [hit[j * COUNT_ROWS:(j + 1) * COUNT_ROWS] for j in range(2 * tq // COUNT_ROWS)]
            while len(parts) > 1:
                parts = [a + b for a, b in zip(parts[0::2], parts[1::2])]
            return acc + parts[0]
        acc = lax.fori_loop(0, (n_ch + 1) // 2, body, jnp.zeros((COUNT_ROWS, tq), dtype))
        return jnp.sum(acc.astype(F32), axis=0, keepdims=True)

    def top_body(it, thr):
        cand = thr + lax.shift_left(jnp.int32(1), 31 - it)
        fbits = jnp.where(cand < 0, cand ^ jnp.int32(0x7FFFFFFF), cand) & jnp.int32(-65536)
        subnormal = ((fbits & jnp.int32(0x7F800000)) == 0) & ((fbits & jnp.int32(0x007F0000)) != 0)
        fbits = jnp.where(subnormal, jnp.where(fbits < 0, 0, jnp.int32(0x00800000)), fbits)
        cand_f = pltpu.bitcast(fbits, F32).astype(BF)
        return jnp.where(count(top_ref, lambda tt: tt >= cand_f, BF) >= n_keep, cand, thr)

    def low_body(it, thr):
        cand = thr + lax.shift_left(jnp.int32(1), 31 - it)
        return jnp.where(count(key_ref, lambda kk: kk >= cand, F32) >= n_keep, cand, thr)

    thr = lax.fori_loop(0, 16, top_body, jnp.full((1, tq), INT_MIN, jnp.int32))
    thr = lax.fori_loop(16, 32, low_body, thr)

    need = n_keep - count(key_ref, lambda kk: kk > thr, F32)
    lower = jnp.where(_iota((tq, tq), 1) <= _iota((tq, tq), 0), 1.0, 0.0).astype(BF)

    def tie_body(c, seen):
        k0 = pl.multiple_of(c * tq, tq)
        kk = key_ref[pl.ds(k0, tq), :]
        eq = jnp.where(kk == thr, 1.0, 0.0)
        rank = jnp.dot(lower, eq.astype(BF), preferred_element_type=F32) + seen
        keep = (kk > thr) | ((kk == thr) & (rank <= need))
        causal = (k0 + _iota((tq, tq), 0)) <= qpos
        bias_ref[pl.ds(k0, tq), :] = jnp.where(keep & causal, 0.0, NEG)
        return seen + jnp.sum(eq, axis=0, keepdims=True)

    lax.fori_loop(0, n_ch, tie_body, jnp.zeros((1, tq), F32))

    qs = _stack_groups(q_ref[...], 6, N_HEADS)
    biased = lambda s, k0: s + _tile_lanes(bias_ref[pl.ds(k0, tq), :], N_HEADS)
    l, (acc,) = _flash_attend(i, tq, N_HEADS, (HEADS,), qs, k_ref, vt_ref, s_ref,
                              post=lambda s, c, k0: biased(s, k0), last=biased)
    o_ref[...] = _normalised(acc, l, HEADS, tq).T.astype(BF)


def _dsa(qk, vt, misc, T, tq):
    N = qk.shape[0]
    nq = T // tq
    out_spec, out_shape = _attn_out(N, T, tq)
    return pl.pallas_call(
        functools.partial(_dsa_kernel, tq=tq, n_keep=min(DSA_TOPK_MAX, T // 4)),
        grid=(N // T, nq),
        in_specs=_attn_specs(T, tq, SEG_ID['dsa_q'], SEG_ID['dsa_k'], 'dsa_v')
        + [pl.BlockSpec((tq, SEG), lambda b, i: (b * nq + i, SEG_ID['idx_q'])),
           pl.BlockSpec((T, SEG), lambda b, i: (b, SEG_ID['idx_k4'])),
           pl.BlockSpec((tq, LANES), lambda b, i: (b * nq + i, 0))],
        out_specs=out_spec, out_shape=out_shape,
        scratch_shapes=[pltpu.VMEM(((nq + nq % 2) * tq, tq), jnp.int32),
                        pltpu.VMEM(((nq + nq % 2) * tq, tq), BF), pltpu.VMEM((T, tq), F32),
                        _logit_scratch(N_HEADS, tq)],
        compiler_params=_params("parallel", "arbitrary"),
        name="dsa_attn",
    )(qk, qk, vt, qk, qk, misc)


def _mix_out_kernel(x_ref, oa_ref, ob_ref, oc_ref, od_ref, wg_ref, wb_ref, wo_ref, g_ref, b_ref,
                    y_ref, yb_ref):
    x = x_ref[...]
    xb = x.astype(BF)
    merged = jnp.zeros(x.shape, F32)
    for n, o_ref in enumerate((oa_ref, ob_ref, oc_ref, od_ref)):
        gate = _sigmoid(jnp.dot(xb, wg_ref[:, n * D_MODEL:(n + 1) * D_MODEL], preferred_element_type=F32))
        merged = merged + gate * jnp.dot(o_ref[...], wb_ref[n], preferred_element_type=F32)
    h = jnp.dot(merged.astype(BF), wo_ref[...], preferred_element_type=F32)
    y = _layer_norm(DEEPNORM_ALPHA * x + h, g_ref[...], b_ref[...])
    y_ref[...] = y
    yb_ref[...] = y.astype(BF)


def _mix_out(x2, branches, w_gates, w_branch, w_out, ln_g, ln_b, tm):
    N = x2.shape[0]
    row = lambda i: (i, 0)
    fixed2 = lambda i: (0, 0)
    return pl.pallas_call(
        _mix_out_kernel,
        grid=(N // tm,),
        in_specs=[pl.BlockSpec((tm, D_MODEL), row)] + [pl.BlockSpec((tm, SEG), row)] * N_BRANCHES
        + [pl.BlockSpec(w_gates.shape, fixed2),
           pl.BlockSpec(w_branch.shape, lambda i: (0, 0, 0)),
           pl.BlockSpec(w_out.shape, fixed2),
           pl.BlockSpec((1, D_MODEL), fixed2), pl.BlockSpec((1, D_MODEL), fixed2)],
        out_specs=[pl.BlockSpec((tm, D_MODEL), row), pl.BlockSpec((tm, D_MODEL), row)],
        out_shape=[jax.ShapeDtypeStruct((N, D_MODEL), F32), jax.ShapeDtypeStruct((N, D_MODEL), BF)],
        compiler_params=_params("parallel"),
        name="mix_out",
    )(x2, *branches, w_gates, w_branch.astype(BF), w_out.astype(BF),
      ln_g.astype(F32)[None, :], ln_b.astype(F32)[None, :])


def _split_bf16(a):
    hi = a.astype(BF)
    return hi, (a - hi.astype(F32)).astype(BF)


def _router_kernel(x_ref, whi_ref, wlo_ref, rb_ref, idx_ref, wsel_ref):
    xhi, xlo = _split_bf16(x_ref[...])
    whi = whi_ref[...]
    logits = _dot_nt(whi, xhi) + _dot_nt(whi, xlo) + _dot_nt(wlo_ref[...], xhi)
    tm = logits.shape[1]
    scores = _sigmoid(logits)
    biased = scores + _tile_lanes(rb_ref[...], tm // LANES)
    per_group = N_EXPERTS // N_GROUPS
    gs = []
    for g in range(N_GROUPS):
        bg = biased[g * per_group:(g + 1) * per_group]
        row = _iota(bg.shape, 0)
        m1 = jnp.max(bg, axis=0, keepdims=True)
        i1 = jnp.min(jnp.where(bg == m1, row, per_group), axis=0, keepdims=True)
        m2 = jnp.max(jnp.where(row == i1, -jnp.inf, bg), axis=0, keepdims=True)
        gs.append(m1 + m2)
    kept = []
    for g in range(N_GROUPS):
        rank = jnp.zeros((1, tm), F32)
        for o in range(N_GROUPS):
            if o != g:
                beats = (gs[o] >= gs[g]) if o < g else (gs[o] > gs[g])
                rank = rank + jnp.where(beats, 1.0, 0.0)
        kept.append(jnp.where(rank < TOPK_GROUPS, biased[g * per_group:(g + 1) * per_group], -jnp.inf))
    masked = jnp.concatenate(kept, axis=0)
    eid = _iota(masked.shape, 0)
    picks, weights = [], []
    for _ in range(TOP_K):
        mx = jnp.max(masked, axis=0, keepdims=True)
        pick = jnp.min(jnp.where(masked == mx, eid, N_EXPERTS), axis=0, keepdims=True)
        hit = eid == pick
        weights.append(jnp.sum(jnp.where(hit, scores, 0.0), axis=0, keepdims=True))
        masked = jnp.where(hit, -jnp.inf, masked)
        picks.append(pick)
    wsum = weights[0]
    for wk in weights[1:]:
        wsum = wsum + wk
    idx_ref[...] = jnp.concatenate(picks, axis=0)
    wsel_ref[...] = jnp.concatenate(weights, axis=0) / wsum * ROUTED_SCALE


def _router(x1, w_router, router_bias, tm):
    N = x1.shape[0]
    whi, wlo = _split_bf16(w_router.astype(F32).T)
    rb = jnp.broadcast_to(router_bias.astype(F32)[:, None], (N_EXPERTS, LANES))
    fixed = lambda i: (0, 0)
    col = lambda i: (0, i)
    return pl.pallas_call(
        _router_kernel,
        grid=(N // tm,),
        in_specs=[pl.BlockSpec((tm, D_MODEL), lambda i: (i, 0)), pl.BlockSpec(whi.shape, fixed),
                  pl.BlockSpec(wlo.shape, fixed), pl.BlockSpec(rb.shape, fixed)],
        out_specs=[pl.BlockSpec((TOP_K, tm), col), pl.BlockSpec((TOP_K, tm), col)],
        out_shape=[jax.ShapeDtypeStruct((TOP_K, N), jnp.int32), jax.ShapeDtypeStruct((TOP_K, N), F32)],
        compiler_params=_params("parallel"),
        name="router",
    )(x1, whi, wlo, rb)


def _expert_kernel(blk_ref, exp_ref, lo_ref, hi_ref, first_ref, newe_ref,
                   xs_ref, wg_ref, wu_ref, wd_ref, y_ref, wgb_ref, wub_ref, wdb_ref, *, bm):
    t = pl.program_id(0)
    lo, hi = lo_ref[t], hi_ref[t]

    @pl.when(newe_ref[t] == 1)
    def _():
        wgb_ref[...] = wg_ref[...].astype(BF)
        wub_ref[...] = wu_ref[...].astype(BF)
        wdb_ref[...] = wd_ref[...].astype(BF)

    base = blk_ref[t] * bm
    whole = (lo <= base) & (hi >= base + bm)

    @pl.when((first_ref[t] == 1) & jnp.logical_not(whole))
    def _():
        y_ref[...] = jnp.zeros_like(y_ref)

    sub = bm // 2

    def work(j0, n_sub, masked):
        rows = slice(j0 * sub, (j0 + n_sub) * sub)
        xs = xs_ref[rows, :]
        g = jnp.dot(xs, wgb_ref[...], preferred_element_type=F32)
        u = jnp.dot(xs, wub_ref[...], preferred_element_type=F32)
        h = (g * _sigmoid(g) * u).astype(BF)
        y = jnp.dot(h, wdb_ref[...], preferred_element_type=F32).astype(y_ref.dtype)
        if masked:
            row = base + j0 * sub + _iota((n_sub * sub, 1), 0)
            y = jnp.where((row >= lo) & (row < hi), y, y_ref[rows, :])
        y_ref[rows, :] = y

    need = [(hi > lo) & (lo < base + (j + 1) * sub) & (hi > base + j * sub) for j in range(2)]
    shared = jnp.logical_not(whole)
    pl.when(whole)(lambda: work(0, 2, False))
    pl.when(shared & need[0] & need[1])(lambda: work(0, 2, True))
    pl.when(shared & need[0] & jnp.logical_not(need[1]))(lambda: work(0, 1, True))
    pl.when(shared & jnp.logical_not(need[0]) & need[1])(lambda: work(1, 1, True))


def _expert_kernel_onto(blk_ref, exp_ref, lo_ref, hi_ref, first_ref, newe_ref,
                        xs_ref, wg_ref, wu_ref, wd_ref, prev_ref, y_ref, *scratch, bm):
    del prev_ref
    _expert_kernel(blk_ref, exp_ref, lo_ref, hi_ref, first_ref, newe_ref,
                   xs_ref, wg_ref, wu_ref, wd_ref, y_ref, *scratch, bm=bm)


def _experts(xs_parts, plans, layer, w_gate, w_up, w_down, bm):
    rows_part = xs_parts[0].shape[0]
    A = rows_part * len(xs_parts)
    wspec = lambda shape: pl.BlockSpec((None, None) + shape, lambda t, blk, exp, *_: (layer, exp[t], 0, 0))
    ys = None
    for p, (xs, plan) in enumerate(zip(xs_parts, plans)):
        blk0 = p * rows_part // bm
        in_specs = [pl.BlockSpec((bm, D_MODEL), lambda t, blk, *_, blk0=blk0: (blk[t] - blk0, 0)),
                    wspec((D_MODEL, EXPERT_DIM)), wspec((D_MODEL, EXPERT_DIM)), wspec((EXPERT_DIM, D_MODEL))]
        args = (*plan, xs, w_gate, w_up, w_down)
        if ys is not None:
            in_specs.append(pl.BlockSpec(memory_space=pl.ANY))
            args += (ys,)
        ys = pl.pallas_call(
            functools.partial(_expert_kernel if ys is None else _expert_kernel_onto, bm=bm),
            grid_spec=pltpu.PrefetchScalarGridSpec(
                num_scalar_prefetch=len(plan),
                grid=(plan[0].shape[0],),
                in_specs=in_specs,
                out_specs=pl.BlockSpec((bm, D_MODEL), lambda t, blk, *_: (blk[t], 0)),
                scratch_shapes=[pltpu.VMEM((D_MODEL, EXPERT_DIM), BF), pltpu.VMEM((D_MODEL, EXPERT_DIM), BF),
                                pltpu.VMEM((EXPERT_DIM, D_MODEL), BF)],
            ),
            out_shape=jax.ShapeDtypeStruct((A, D_MODEL), BF),
            input_output_aliases={} if ys is None else {len(args) - 1: 0},
            compiler_params=_params("arbitrary"),
            name="experts",
        )(*args)
    return ys


def _dispatch_plan(eidx_t, bm, n_parts):
    N = eidx_t.shape[1]
    A, E = N * TOP_K, N_EXPERTS
    assert A % (bm * n_parts) == 0
    ids = jnp.arange(A, dtype=jnp.int32)
    id_bits = (A - 1).bit_length()
    assert id_bits + (E - 1).bit_length() <= 31
    packed = jnp.sort((eidx_t.reshape(A) << id_bits) | ids)
    se, sid = packed >> id_bits, packed & ((1 << id_bits) - 1)
    pos = jnp.zeros((A,), jnp.int32).at[sid].set(ids, unique_indices=True, mode='promise_in_bounds')
    start = jnp.searchsorted(se, jnp.arange(E, dtype=jnp.int32), side='left').astype(jnp.int32)
    changed = lambda a: jnp.concatenate([jnp.ones((1,), jnp.int32), (a[1:] != a[:-1]).astype(jnp.int32)])
    plans = []
    for p in range(n_parts):
        a0, a1 = p * (A // n_parts), (p + 1) * (A // n_parts)
        lo = jnp.sort(jnp.concatenate([jnp.arange(a0, a1, bm, dtype=jnp.int32), jnp.clip(start, a0, a1)]))
        hi = jnp.concatenate([lo[1:], jnp.full((1,), a1, jnp.int32)])
        blk = jnp.minimum(lo // bm, a1 // bm - 1)
        exp = se[jnp.minimum(lo, A - 1)]
        plans.append((blk, exp, lo, hi, changed(blk), changed(exp)))
    return sid % N, pos.reshape(TOP_K, N), plans


def _moe_out_kernel(x_ref, r_ref, rw_ref, wg_ref, wu_ref, wd_ref, g_ref, b_ref, y_ref):
    x = x_ref[...]
    tm = x.shape[0]
    xb = x.astype(BF)
    g = jnp.dot(xb, wg_ref[...], preferred_element_type=F32)
    u = jnp.dot(xb, wu_ref[...], preferred_element_type=F32)
    h = (g * _sigmoid(g) * u).astype(BF)
    shared = jnp.dot(h, wd_ref[...], preferred_element_type=F32)
    rw = jnp.concatenate([rw_ref[...], jnp.zeros((LANES - TOP_K, tm), F32)], axis=0).T
    routed = rw[:, 0:1] * r_ref[0].astype(F32)
    for k in range(1, TOP_K):
        routed = routed + rw[:, k:k + 1] * r_ref[k].astype(F32)
    y_ref[...] = _layer_norm(DEEPNORM_ALPHA * x + (routed + shared), g_ref[...], b_ref[...])


def _moe_out_kernel_onto(x_ref, r_ref, rw_ref, wg_ref, wu_ref, wd_ref, g_ref, b_ref, prev_ref, y_ref):
    del prev_ref
    _moe_out_kernel(x_ref, r_ref, rw_ref, wg_ref, wu_ref, wd_ref, g_ref, b_ref, y_ref)


def _moe_out(x1, routed_parts, route_w, w_sh_gate, w_sh_up, w_sh_down, ln_g, ln_b, tm):
    N = x1.shape[0]
    n_tok = routed_parts[0].shape[1]
    fixed = lambda i: (0, 0)
    weights = (w_sh_gate.astype(BF), w_sh_up.astype(BF), w_sh_down.astype(BF),
               ln_g.astype(F32)[None, :], ln_b.astype(F32)[None, :])
    y = None
    for p, routed in enumerate(routed_parts):
        t0 = p * n_tok // tm
        row = lambda i, t0=t0: (t0 + i, 0)
        in_specs = [pl.BlockSpec((tm, D_MODEL), row),
                    pl.BlockSpec((TOP_K, tm, D_MODEL), lambda i: (0, i, 0)),
                    pl.BlockSpec((TOP_K, tm), lambda i, t0=t0: (0, t0 + i)),
                    pl.BlockSpec(w_sh_gate.shape, fixed), pl.BlockSpec(w_sh_up.shape, fixed),
                    pl.BlockSpec(w_sh_down.shape, fixed),
                    pl.BlockSpec((1, D_MODEL), fixed), pl.BlockSpec((1, D_MODEL), fixed)]
        args = (x1, routed, route_w) + weights
        if y is not None:
            in_specs.append(pl.BlockSpec(memory_space=pl.ANY))
            args += (y,)
        y = pl.pallas_call(
            _moe_out_kernel if y is None else _moe_out_kernel_onto,
            grid=(n_tok // tm,),
            in_specs=in_specs,
            out_specs=pl.BlockSpec((tm, D_MODEL), row),
            out_shape=jax.ShapeDtypeStruct((N, D_MODEL), F32),
            input_output_aliases={} if y is None else {len(args) - 1: 0},
            compiler_params=_params("parallel"),
            name="moe_out",
        )(*args)
    return y


EXPERT_PARTS = 2
COMBINE_PARTS = 4

def _tiles(N, T):
    return min(512, T), min(256, T), MOBA_BLOCK, min(1024, N * TOP_K)


def _mixer_layer(x2, T, w_in, b_forget, diff_lambda, diff_subln, w_branch, w_out, ln_g, ln_b, lambda_init):
    N = x2.shape[0]
    tm_proj, tm_row, tq, _ = _tiles(N, T)
    w, wvt, w_gates, bf = _in_proj_weights(w_in, b_forget)
    qk, vt, misc = _in_proj(x2, w, wvt, bf, _rope_tables(T), T, tm_proj)
    dl = diff_lambda.astype(F32)
    lam = jnp.exp(jnp.sum(dl[0] * dl[1])) - jnp.exp(jnp.sum(dl[2] * dl[3])) + lambda_init
    o_a = _moba(qk, vt, T, tq)
    o_b = _diff(qk, vt, lam, diff_subln, lambda_init, T, tq)
    o_c = _fox(qk, vt, misc, T, tq)
    o_d = _dsa(qk, vt, misc, T, tq)
    return _mix_out(x2, (o_a, o_b, o_c, o_d), w_gates, w_branch, w_out, ln_g, ln_b, tm_row)


def _moe_layer(x1, x1b, T, layer, w_router, router_bias, w_exp_gate, w_exp_up, w_exp_down,
               w_sh_gate, w_sh_up, w_sh_down, ln_g, ln_b):
    N = x1.shape[0]
    _, tm_row, _, bm = _tiles(N, T)
    eidx_t, wsel_t = _router(x1, w_router, router_bias, tm_row)
    row_tok, pos, plans = _dispatch_plan(eidx_t, bm, EXPERT_PARTS)
    rows_part = row_tok.shape[0] // EXPERT_PARTS
    xs_parts = [x1b[row_tok[p * rows_part:(p + 1) * rows_part]] for p in range(EXPERT_PARTS)]
    ys = _experts(xs_parts, plans, layer, w_exp_gate, w_exp_up, w_exp_down, bm)
    n_tok = N // COMBINE_PARTS
    routed_parts = [ys[pos[:, p * n_tok:(p + 1) * n_tok]] for p in range(COMBINE_PARTS)]
    return _moe_out(x1, routed_parts, wsel_t, w_sh_gate, w_sh_up, w_sh_down, ln_g, ln_b, tm_row)


def kernel(x, w_in, b_forget, diff_lambda, diff_subln, w_branch, w_out, ln1_g, ln1_b, w_router, router_bias,
           w_exp_gate, w_exp_up, w_exp_down, w_sh_gate, w_sh_up, w_sh_down, ln2_g, ln2_b):
    B, T, D = x.shape
    x2 = x.reshape(B * T, D)
    for l in range(DEPTH):
        lambda_init = 0.8 - 0.6 * math.exp(-0.3 * l)
        x1, x1b = _mixer_layer(x2, T, w_in[l], b_forget[l], diff_lambda[l], diff_subln[l], w_branch[l],
                               w_out[l], ln1_g[l], ln1_b[l], lambda_init)
        x2 = _moe_layer(x1, x1b, T, l, w_router[l], router_bias[l], w_exp_gate, w_exp_up, w_exp_down,
                        w_sh_gate[l], w_sh_up[l], w_sh_down[l], ln2_g[l], ln2_b[l])
    return x2.reshape(B, T, D)
```

```python
import functools
import math

import jax
import jax.numpy as jnp
from jax import lax
from jax.experimental import pallas as pl
from jax.experimental.pallas import tpu as pltpu

F32 = jnp.float32
BF = jnp.bfloat16

D_MODEL = 1024
DEPTH = 2
HEAD_DIM = 64
N_HEADS = 4
DIFF_DIM = HEAD_DIM // 2
N_IDX_HEADS = 4
IDX_DIM = 64
BRANCH_WIDTH = N_HEADS * HEAD_DIM
N_BRANCHES = 4
MOBA_BLOCK = 256
MOBA_TOPK = 3
DSA_TOPK_MAX = 256
ROPE_THETA = 10000.0
N_EXPERTS = 256
TOP_K = 8
N_GROUPS = 8
TOPK_GROUPS = 4
EXPERT_DIM = 256
ROUTED_SCALE = 2.5
LN_EPS = 1e-5
DEEPNORM_ALPHA = (2 * DEPTH) ** 0.25

IN_SEGMENTS = (
    ('moba_q', BRANCH_WIDTH), ('moba_k', BRANCH_WIDTH), ('moba_v', BRANCH_WIDTH),
    ('diff_q', BRANCH_WIDTH), ('diff_k', BRANCH_WIDTH), ('diff_v', BRANCH_WIDTH),
    ('fox_q', BRANCH_WIDTH), ('fox_k', BRANCH_WIDTH), ('fox_v', BRANCH_WIDTH), ('fox_f', N_HEADS),
    ('dsa_q', BRANCH_WIDTH), ('dsa_k', BRANCH_WIDTH), ('dsa_v', BRANCH_WIDTH),
    ('idx_q', N_IDX_HEADS * IDX_DIM), ('idx_k', IDX_DIM), ('idx_w', N_IDX_HEADS),
    ('gates', N_BRANCHES * D_MODEL),
)

LANES = 128
SUBLANES = 8
SEG = BRANCH_WIDTH
NEG = -1e30
INT_MIN = -2 ** 31
VMEM_LIMIT = 48 * 1024 * 1024

LOG2E = math.log2(math.e)
_QSCALE = HEAD_DIM ** -0.5 * LOG2E
PROJ_SEGS = (
    ('moba_q', 64, _QSCALE), ('moba_k', 64, 1.0),
    ('diff_q', 32, DIFF_DIM ** -0.5 * LOG2E), ('diff_k', 32, 1.0),
    ('fox_q', 0, _QSCALE), ('fox_k', 0, 1.0),
    ('dsa_q', 64, _QSCALE), ('dsa_k', 64, 1.0),
    ('idx_q', 64, IDX_DIM ** -0.5), ('idx_k4', 64, 1.0),
)
SEG_ID = {name: i for i, (name, _, _) in enumerate(PROJ_SEGS)}
N_SEG = len(PROJ_SEGS)
V_SEGS = ('moba_v', 'diff_v', 'fox_v', 'dsa_v')
V_ID = {name: i for i, name in enumerate(V_SEGS)}
MISC_LOGF = 0
MISC_IDXW = 4


def _params(*sem):
    return pltpu.CompilerParams(dimension_semantics=sem, vmem_limit_bytes=VMEM_LIMIT)


def _iota(shape, dim):
    return lax.broadcasted_iota(jnp.int32, shape, dim)


def _dot_nt(a, b):
    return lax.dot_general(a, b, (((1,), (1,)), ((), ())), preferred_element_type=F32)


def _sigmoid(z):
    return 1.0 / (1.0 + jnp.exp(-z))


def _layer_norm(y, g, b):
    mu = jnp.mean(y, axis=-1, keepdims=True)
    yc = y - mu
    var = jnp.mean(yc * yc, axis=-1, keepdims=True)
    return yc * lax.rsqrt(var + LN_EPS) * g + b


def _swap_halves(a, half):
    w = a.shape[-1]
    first = (_iota(a.shape, 1) & (2 * half - 1)) < half
    return jnp.where(first, pltpu.roll(a, w - half, 1), pltpu.roll(a, half, 1))


def _in_proj_kernel(x_ref, w_ref, wvt_ref, c64_ref, s64_ref, c32_ref, s32_ref, bf_ref,
                    qk_ref, vt_ref, misc_ref):
    xb = x_ref[...].astype(BF)
    for s, (_, rot, scale) in enumerate(PROJ_SEGS):
        acc = jnp.dot(xb, w_ref[:, s * SEG:(s + 1) * SEG], preferred_element_type=F32)
        if rot == 64:
            acc = acc * c64_ref[...] + _swap_halves(acc, 32) * s64_ref[...]
        elif rot == 32:
            acc = acc * c32_ref[...] + _swap_halves(acc, 16) * s32_ref[...]
        if scale != 1.0:
            acc = acc * scale
        qk_ref[:, s * SEG:(s + 1) * SEG] = acc.astype(BF)
    vt_ref[...] = _dot_nt(wvt_ref[...], xb).astype(BF)
    m = jnp.dot(xb, w_ref[:, N_SEG * SEG:N_SEG * SEG + LANES], preferred_element_type=F32)
    z = m + bf_ref[...]
    logf = jnp.minimum(z, 0.0) - jnp.log1p(jnp.exp(-jnp.abs(z)))
    lane = _iota(m.shape, 1)
    misc_ref[...] = jnp.where(lane < MISC_IDXW, logf, m)


def _rope_tables(T):
    pos = jnp.arange(T).astype(F32)

    def tab(group, reps):
        half = group // 2
        inv_freq = ROPE_THETA ** (-jnp.arange(half, dtype=F32) / half)
        ang = pos[:, None] * inv_freq[None, :]
        cos, sin = jnp.cos(ang), jnp.sin(ang)
        return (jnp.tile(jnp.concatenate([cos, cos], -1), (1, reps)),
                jnp.tile(jnp.concatenate([-sin, sin], -1), (1, reps)))

    return tab(64, SEG // 64) + tab(32, SEG // 32)


def _in_proj_weights(w_in, b_forget):
    parts, off = {}, 0
    for name, width in IN_SEGMENTS:
        parts[name] = w_in[:, off:off + width]
        off += width
    parts['idx_k4'] = jnp.tile(parts['idx_k'], (1, N_IDX_HEADS))
    misc = jnp.concatenate([parts['fox_f'], parts['idx_w'],
                            jnp.zeros((D_MODEL, LANES - 2 * N_HEADS), w_in.dtype)], axis=1)
    w = jnp.concatenate([parts[name] for name, _, _ in PROJ_SEGS] + [misc], axis=1).astype(BF)
    wvt = jnp.concatenate([parts[name] for name in V_SEGS], axis=1).T.astype(BF)
    bf = jnp.zeros((1, LANES), F32).at[0, MISC_LOGF:MISC_LOGF + N_HEADS].set(b_forget.astype(F32))
    return w, wvt, parts['gates'].astype(BF), bf


def _in_proj(x2, w, wvt, bf, tables, T, tm):
    N = x2.shape[0]
    nt = T // tm
    tab_spec = pl.BlockSpec((tm, SEG), lambda i: (i % nt, 0))
    fixed = lambda i: (0, 0)
    return pl.pallas_call(
        _in_proj_kernel,
        grid=(N // tm,),
        in_specs=[pl.BlockSpec((tm, D_MODEL), lambda i: (i, 0)),
                  pl.BlockSpec(w.shape, fixed), pl.BlockSpec(wvt.shape, fixed),
                  tab_spec, tab_spec, tab_spec, tab_spec,
                  pl.BlockSpec((1, LANES), fixed)],
        out_specs=[pl.BlockSpec((tm, N_SEG * SEG), lambda i: (i, 0)),
                   pl.BlockSpec((len(V_SEGS) * SEG, tm), lambda i: (0, i)),
                   pl.BlockSpec((tm, LANES), lambda i: (i, 0))],
        out_shape=[jax.ShapeDtypeStruct((N, N_SEG * SEG), BF),
                   jax.ShapeDtypeStruct((len(V_SEGS) * SEG, N), BF),
                   jax.ShapeDtypeStruct((N, LANES), F32)],
        compiler_params=_params("parallel"),
        name="in_proj",
    )(x2, w, wvt, *tables, bf)


def _group_mask(shape, shift, g):
    return (_iota(shape, 1) >> shift) == g


def _stack_groups(q, shift, n_groups):
    zero = jnp.zeros_like(q)
    return jnp.concatenate([jnp.where(_group_mask(q.shape, shift, g), q, zero) for g in range(n_groups)],
                           axis=0)


def _cols(a, g, tq):
    return a[:, g * tq:(g + 1) * tq]


def _head_rows(a, h):
    return a[h * HEAD_DIM:(h + 1) * HEAD_DIM]


ONES_ROWS = 16


def _flash_init(n_groups, tq, n_sets):
    return (jnp.full((1, n_groups * tq), NEG, F32), jnp.zeros((1, n_groups * tq), F32),
            tuple(jnp.zeros((SEG, tq), F32) for _ in range(n_sets)))


def _flash_update(s, carry, vt, tq, head_sets):
    m, l, accs = carry
    m_new = jnp.maximum(m, jnp.max(s, axis=0, keepdims=True))
    alpha = jnp.exp2(m - m_new)
    pb = jnp.exp2(s - m_new).astype(BF)
    ones = jnp.ones((ONES_ROWS, vt.shape[1]), BF)
    new_accs, p_sum = [], {}
    for acc, groups in zip(accs, head_sets):
        parts = []
        for h, g in enumerate(groups):
            r = jnp.dot(jnp.concatenate([_head_rows(vt, h), ones], axis=0), _cols(pb, g, tq),
                        preferred_element_type=F32)
            parts.append(_cols(alpha, g, tq) * _head_rows(acc, h) + r[0:HEAD_DIM])
            p_sum[g] = r[HEAD_DIM:HEAD_DIM + 1]
        new_accs.append(jnp.concatenate(parts, axis=0))
    l = alpha * l + jnp.concatenate([p_sum[g] for g in sorted(p_sum)], axis=1)
    return m_new, l, tuple(new_accs)


def _causal(n_groups, tq):
    shape = (tq, n_groups * tq)
    return _iota(shape, 0) <= (_iota(shape, 1) & (tq - 1))


def _normalised(acc, l, groups, tq):
    return jnp.concatenate([_head_rows(acc, h) * (1.0 / _cols(l, g, tq)) for h, g in enumerate(groups)],
                           axis=0)


def _tile_lanes(a, n):
    return jnp.concatenate([a] * n, axis=1) if n > 1 else a


HEADS = tuple(range(N_HEADS))


def _flash_attend(i, tq, n_groups, head_sets, qs, k_ref, vt_ref, s_ref, post, last):
    def raw(k0):
        return _dot_nt(k_ref[pl.ds(k0, tq), :], qs)

    if s_ref is not None:
        s_ref[0] = raw(0)

    def body(c, carry):
        k0 = pl.multiple_of(c * tq, tq)
        if s_ref is None:
            s = post(raw(k0), c, k0)
        else:
            s = post(s_ref[c & 1], c, k0)
            s_ref[(c + 1) & 1] = raw(pl.multiple_of(k0 + tq, tq))
        return _flash_update(s, carry, vt_ref[:, pl.ds(k0, tq)], tq, head_sets)

    carry = lax.fori_loop(0, i, body, _flash_init(n_groups, tq, len(head_sets)))
    k0 = pl.multiple_of(i * tq, tq)
    s = last(raw(k0) if s_ref is None else s_ref[i & 1], k0)
    _, l, accs = _flash_update(s, carry, vt_ref[:, pl.ds(k0, tq)], tq, head_sets)
    return l, accs


def _logit_scratch(n_groups, tq):
    return pltpu.VMEM((2, tq, n_groups * tq), F32)


def _attn_specs(T, tq, q_seg, k_seg, v_name):
    nq = T // tq
    v_id = V_ID[v_name]
    return [pl.BlockSpec((tq, SEG), lambda b, i: (b * nq + i, q_seg)),
            pl.BlockSpec((T, SEG), lambda b, i: (b, k_seg)),
            pl.BlockSpec((SEG, T), lambda b, i: (v_id, b))]


def _attn_out(N, T, tq):
    nq = T // tq
    return (pl.BlockSpec((tq, SEG), lambda b, i: (b * nq + i, 0)),
            jax.ShapeDtypeStruct((N, SEG), BF))


def _moba_kernel(q_ref, k_ref, vt_ref, o_ref, km_ref, sel_ref, s_ref, *, tq, nb, n_sel):
    i = pl.program_id(1)

    @pl.when(i == 0)
    def _():
        km_ref[...] = jnp.zeros_like(km_ref)
        for n in range(nb):
            blk = k_ref[n * tq:(n + 1) * tq, :].astype(F32)
            km_ref[n:n + 1, :] = jnp.mean(blk, axis=0, keepdims=True)

    qs = _stack_groups(q_ref[...], 6, N_HEADS)
    km = km_ref[...]
    km_hi = km.astype(BF)
    km_lo = (km - km_hi.astype(F32)).astype(BF)
    nrow = sel_ref.shape[0]
    g = (_dot_nt(km_hi, qs) + _dot_nt(km_lo, qs))[0:nrow]
    blk_id = _iota(g.shape, 0)
    g = jnp.where(blk_id < i, g, -jnp.inf)
    sel = jnp.zeros(g.shape, F32)
    for _ in range(n_sel):
        gmax = jnp.max(g, axis=0, keepdims=True)
        first = jnp.min(jnp.where(g == gmax, blk_id, nrow), axis=0, keepdims=True)
        hit = blk_id == first
        sel = jnp.where(hit & (gmax > -jnp.inf), 1.0, sel)
        g = jnp.where(hit, -jnp.inf, g)
    sel_ref[...] = sel

    l, (acc,) = _flash_attend(
        i, tq, N_HEADS, (HEADS,), qs, k_ref, vt_ref, s_ref,
        post=lambda s, c, k0: jnp.where(sel_ref[pl.ds(c, 1), :] > 0.5, s, NEG),
        last=lambda s, k0: jnp.where(_causal(N_HEADS, tq), s, NEG))
    o_ref[...] = _normalised(acc, l, HEADS, tq).T.astype(BF)


def _moba(qk, vt, T, tq):
    N = qk.shape[0]
    nb = T // MOBA_BLOCK
    assert T % MOBA_BLOCK == 0 and tq == MOBA_BLOCK and nb <= LANES
    sel_rows = -(-nb // SUBLANES) * SUBLANES
    out_spec, out_shape = _attn_out(N, T, tq)
    return pl.pallas_call(
        functools.partial(_moba_kernel, tq=tq, nb=nb, n_sel=min(MOBA_TOPK, nb - 1)),
        grid=(N // T, T // tq),
        in_specs=_attn_specs(T, tq, SEG_ID['moba_q'], SEG_ID['moba_k'], 'moba_v'),
        out_specs=out_spec, out_shape=out_shape,
        scratch_shapes=[pltpu.VMEM((LANES, SEG), F32), pltpu.VMEM((sel_rows, N_HEADS * tq), F32),
                        _logit_scratch(N_HEADS, tq)],
        compiler_params=_params("parallel", "arbitrary"),
        name="moba_attn",
    )(qk, qk, vt)


DIFF_SETS = (tuple(2 * h for h in HEADS), tuple(2 * h + 1 for h in HEADS))


def _diff_kernel(lam_ref, q_ref, k_ref, vt_ref, g_ref, o_ref, *, tq, out_scale):
    i = pl.program_id(1)
    n_groups = 2 * N_HEADS
    qs = _stack_groups(q_ref[...], 5, n_groups)
    l, (a1, a2) = _flash_attend(
        i, tq, n_groups, DIFF_SETS, qs, k_ref, vt_ref, None,
        post=lambda s, c, k0: s,
        last=lambda s, k0: jnp.where(_causal(n_groups, tq), s, NEG))
    out = _normalised(a1, l, DIFF_SETS[0], tq) - lam_ref[0] * _normalised(a2, l, DIFF_SETS[1], tq)
    normed = []
    for h in HEADS:
        oh = _head_rows(out, h)
        ms = jnp.mean(oh * oh, axis=0, keepdims=True)
        normed.append(oh * lax.rsqrt(ms + LN_EPS))
    o_ref[...] = (jnp.concatenate(normed, axis=0).T * g_ref[...] * out_scale).astype(BF)


def _diff(qk, vt, lam, subln_g, lambda_init, T, tq):
    N = qk.shape[0]
    out_spec, out_shape = _attn_out(N, T, tq)
    g = jnp.tile(subln_g.astype(F32), N_HEADS)[None, :]
    return pl.pallas_call(
        functools.partial(_diff_kernel, tq=tq, out_scale=1.0 - lambda_init),
        grid=(N // T, T // tq),
        in_specs=[pl.BlockSpec(memory_space=pltpu.SMEM)]
        + _attn_specs(T, tq, SEG_ID['diff_q'], SEG_ID['diff_k'], 'diff_v')
        + [pl.BlockSpec((1, SEG), lambda b, i: (0, 0))],
        out_specs=out_spec, out_shape=out_shape,
        compiler_params=_params("parallel", "arbitrary"),
        name="diff_attn",
    )(lam.reshape(1).astype(F32), qk, qk, vt, g)


def _fox_kernel(q_ref, k_ref, vt_ref, misc_ref, o_ref, c_ref, s_ref, *, tq):
    i = pl.program_id(1)

    @pl.when(i == 0)
    def _():
        tri = jnp.where(_iota((tq, tq), 1) <= _iota((tq, tq), 0), 1.0, 0.0).astype(BF)
        carry = [jnp.zeros((1, LANES), F32) for _ in HEADS]
        for n in range(c_ref.shape[1] // tq):
            blk = misc_ref[n * tq:(n + 1) * tq, :] * LOG2E
            for h in HEADS:
                col = jnp.broadcast_to(blk[:, MISC_LOGF + h:MISC_LOGF + h + 1], blk.shape)
                hi = col.astype(BF)
                rest = col - hi.astype(F32)
                mid = rest.astype(BF)
                lo = (rest - mid.astype(F32)).astype(BF)
                cs = carry[h] + (jnp.dot(tri, hi, preferred_element_type=F32)
                                 + jnp.dot(tri, mid, preferred_element_type=F32)
                                 + jnp.dot(tri, lo, preferred_element_type=F32))
                c_ref[h, n * tq:(n + 1) * tq, :] = cs
                carry[h] = cs[tq - 1:tq, :]

    qs = _stack_groups(q_ref[...], 6, N_HEADS)

    def decayed(s, k0):
        return jnp.concatenate(
            [_cols(s, h, tq) - _tile_lanes(c_ref[h, pl.ds(k0, tq), :], tq // LANES) for h in HEADS], axis=1)

    l, (acc,) = _flash_attend(
        i, tq, N_HEADS, (HEADS,), qs, k_ref, vt_ref, s_ref,
        post=lambda s, c, k0: decayed(s, k0),
        last=lambda s, k0: jnp.where(_causal(N_HEADS, tq), decayed(s, k0), NEG))
    o_ref[...] = _normalised(acc, l, HEADS, tq).T.astype(BF)


def _fox(qk, vt, misc, T, tq):
    N = qk.shape[0]
    out_spec, out_shape = _attn_out(N, T, tq)
    return pl.pallas_call(
        functools.partial(_fox_kernel, tq=tq),
        grid=(N // T, T // tq),
        in_specs=_attn_specs(T, tq, SEG_ID['fox_q'], SEG_ID['fox_k'], 'fox_v')
        + [pl.BlockSpec((T, LANES), lambda b, i: (b, 0))],
        out_specs=out_spec, out_shape=out_shape,
        scratch_shapes=[pltpu.VMEM((N_HEADS, T, LANES), F32), _logit_scratch(N_HEADS, tq)],
        compiler_params=_params("parallel", "arbitrary"),
        name="fox_attn",
    )(qk, qk, vt, misc)


COUNT_ROWS = 64
F32_TINY = 2.0 ** -126

def _dsa_kernel(q_ref, k_ref, vt_ref, qi_ref, ki_ref, w_ref, o_ref, key_ref, top_ref, bias_ref, s_ref,
                *, tq, n_keep):
    i = pl.program_id(1)
    n_ch = i + 1
    qpos = i * tq + _iota((1, tq), 1)

    qis = _stack_groups(qi_ref[...], 6, N_IDX_HEADS)
    wt = w_ref[...].T * (N_IDX_HEADS ** -0.5)
    w_rows = [wt[MISC_IDXW + h:MISC_IDXW + h + 1] for h in range(N_IDX_HEADS)]

    def score_body(c, carry):
        k0 = pl.multiple_of(c * tq, tq)
        d = jnp.maximum(_dot_nt(ki_ref[pl.ds(k0, tq), :], qis), 0.0)
        score = w_rows[0] * _cols(d, 0, tq)
        for h in range(1, N_IDX_HEADS):
            score = score + w_rows[h] * _cols(d, h, tq)
        sc = jnp.where((k0 + _iota((tq, tq), 0)) <= qpos, score, -jnp.inf)
        sc = jnp.where(jnp.abs(sc) < F32_TINY, 0.0, sc)
        bits = pltpu.bitcast(sc, jnp.int32)
        key_ref[pl.ds(k0, tq), :] = jnp.where(bits < 0, bits ^ jnp.int32(0x7FFFFFFF), bits)
        top_ref[pl.ds(k0, tq), :] = pltpu.bitcast(bits & jnp.int32(-65536), F32).astype(BF)
        return carry

    lax.fori_loop(0, n_ch, score_body, 0)

    @pl.when(n_ch % 2 == 1)
    def _():
        pad = pl.ds(pl.multiple_of(n_ch * tq, tq), tq)
        key_ref[pad, :] = jnp.full((tq, tq), INT_MIN, jnp.int32)
        top_ref[pad, :] = jnp.full((tq, tq), jnp.nan, BF)

    def count(src_ref, pred, dtype):
        one, zero = jnp.ones((), dtype), jnp.zeros((), dtype)

        def body(c, acc):
            k0 = pl.multiple_of(c * 2 * tq, 2 * tq)
            hit = jnp.where(pred(src_ref[pl.ds(k0, 2 * tq), :]), one, zero)
            parts = [hit[j * COUNT_ROWS:(j + 1) * COUNT_ROWS] for j in range(2 * tq // COUNT_ROWS)]
            while len(parts) > 1:
                parts = [a + b for a, b in zip(parts[0::2], parts[1::2])]
            return acc + parts[0]
        acc = lax.fori_loop(0, (n_ch + 1) // 2, body, jnp.zeros((COUNT_ROWS, tq), dtype))
        return jnp.sum(acc.astype(F32), axis=0, keepdims=True)

    def top_body(it, thr):
        cand = thr + lax.shift_left(jnp.int32(1), 31 - it)
        fbits = jnp.where(cand < 0, cand ^ jnp.int32(0x7FFFFFFF), cand) & jnp.int32(-65536)
        subnormal = ((fbits & jnp.int32(0x7F800000)) == 0) & ((fbits & jnp.int32(0x007F0000)) != 0)
        fbits = jnp.where(subnormal, jnp.where(fbits < 0, 0, jnp.int32(0x00800000)), fbits)
        cand_f = pltpu.bitcast(fbits, F32).astype(BF)
        return jnp.where(count(top_ref, lambda tt: tt >= cand_f, BF) >= n_keep, cand, thr)

    def low_body(it, thr):
        cand = thr + lax.shift_left(jnp.int32(1), 31 - it)
        return jnp.where(count(key_ref, lambda kk: kk >= cand, F32) >= n_keep, cand, thr)

    thr = lax.fori_loop(0, 16, top_body, jnp.full((1, tq), INT_MIN, jnp.int32))
    thr = lax.fori_loop(16, 32, low_body, thr)

    surplus = jnp.max(count(key_ref, lambda kk: kk >= thr, F32)) > n_keep

    @pl.when(jnp.logical_not(surplus))
    def _():
        def plain_body(c, carry):
            k0 = pl.multiple_of(c * tq, tq)
            keep = (key_ref[pl.ds(k0, tq), :] >= thr) & ((k0 + _iota((tq, tq), 0)) <= qpos)
            bias_ref[pl.ds(k0, tq), :] = jnp.where(keep, 0.0, NEG)
            return carry

        lax.fori_loop(0, n_ch, plain_body, 0)

    @pl.when(surplus)
    def _():
        need = n_keep - count(key_ref, lambda kk: kk > thr, F32)
        lower = jnp.where(_iota((tq, tq), 1) <= _iota((tq, tq), 0), 1.0, 0.0).astype(BF)

        def tie_body(c, seen):
            k0 = pl.multiple_of(c * tq, tq)
            kk = key_ref[pl.ds(k0, tq), :]
            eq = jnp.where(kk == thr, 1.0, 0.0)
            rank = jnp.dot(lower, eq.astype(BF), preferred_element_type=F32) + seen
            keep = (kk > thr) | ((kk == thr) & (rank <= need))
            causal = (k0 + _iota((tq, tq), 0)) <= qpos
            bias_ref[pl.ds(k0, tq), :] = jnp.where(keep & causal, 0.0, NEG)
            return seen + jnp.sum(eq, axis=0, keepdims=True)

        lax.fori_loop(0, n_ch, tie_body, jnp.zeros((1, tq), F32))

    qs = _stack_groups(q_ref[...], 6, N_HEADS)
    biased = lambda s, k0: s + _tile_lanes(bias_ref[pl.ds(k0, tq), :], N_HEADS)
    l, (acc,) = _flash_attend(i, tq, N_HEADS, (HEADS,), qs, k_ref, vt_ref, s_ref,
                              post=lambda s, c, k0: biased(s, k0), last=biased)
    o_ref[...] = _normalised(acc, l, HEADS, tq).T.astype(BF)


def _dsa(qk, vt, misc, T, tq):
    N = qk.shape[0]
    nq = T // tq
    out_spec, out_shape = _attn_out(N, T, tq)
    return pl.pallas_call(
        functools.partial(_dsa_kernel, tq=tq, n_keep=min(DSA_TOPK_MAX, T // 4)),
        grid=(N // T, nq),
        in_specs=_attn_specs(T, tq, SEG_ID['dsa_q'], SEG_ID['dsa_k'], 'dsa_v')
        + [pl.BlockSpec((tq, SEG), lambda b, i: (b * nq + i, SEG_ID['idx_q'])),
           pl.BlockSpec((T, SEG), lambda b, i: (b, SEG_ID['idx_k4'])),
           pl.BlockSpec((tq, LANES), lambda b, i: (b * nq + i, 0))],
        out_specs=out_spec, out_shape=out_shape,
        scratch_shapes=[pltpu.VMEM(((nq + nq % 2) * tq, tq), jnp.int32),
                        pltpu.VMEM(((nq + nq % 2) * tq, tq), BF), pltpu.VMEM((T, tq), F32),
                        _logit_scratch(N_HEADS, tq)],
        compiler_params=_params("parallel", "arbitrary"),
        name="dsa_attn",
    )(qk, qk, vt, qk, qk, misc)


def _mix_out_kernel(x_ref, oa_ref, ob_ref, oc_ref, od_ref, wg_ref, wb_ref, wo_ref, g_ref, b_ref,
                    y_ref, yb_ref):
    x = x_ref[...]
    xb = x.astype(BF)
    merged = jnp.zeros(x.shape, F32)
    for n, o_ref in enumerate((oa_ref, ob_ref, oc_ref, od_ref)):
        gate = _sigmoid(jnp.dot(xb, wg_ref[:, n * D_MODEL:(n + 1) * D_MODEL], preferred_element_type=F32))
        merged = merged + gate * jnp.dot(o_ref[...], wb_ref[n], preferred_element_type=F32)
    h = jnp.dot(merged.astype(BF), wo_ref[...], preferred_element_type=F32)
    y = _layer_norm(DEEPNORM_ALPHA * x + h, g_ref[...], b_ref[...])
    y_ref[...] = y
    yb_ref[...] = y.astype(BF)


def _mix_out(x2, branches, w_gates, w_branch, w_out, ln_g, ln_b, tm):
    N = x2.shape[0]
    row = lambda i: (i, 0)
    fixed2 = lambda i: (0, 0)
    return pl.pallas_call(
        _mix_out_kernel,
        grid=(N // tm,),
        in_specs=[pl.BlockSpec((tm, D_MODEL), row)] + [pl.BlockSpec((tm, SEG), row)] * N_BRANCHES
        + [pl.BlockSpec(w_gates.shape, fixed2),
           pl.BlockSpec(w_branch.shape, lambda i: (0, 0, 0)),
           pl.BlockSpec(w_out.shape, fixed2),
           pl.BlockSpec((1, D_MODEL), fixed2), pl.BlockSpec((1, D_MODEL), fixed2)],
        out_specs=[pl.BlockSpec((tm, D_MODEL), row), pl.BlockSpec((tm, D_MODEL), row)],
        out_shape=[jax.ShapeDtypeStruct((N, D_MODEL), F32), jax.ShapeDtypeStruct((N, D_MODEL), BF)],
        compiler_params=_params("parallel"),
        name="mix_out",
    )(x2, *branches, w_gates, w_branch.astype(BF), w_out.astype(BF),
      ln_g.astype(F32)[None, :], ln_b.astype(F32)[None, :])


def _split_bf16(a):
    hi = a.astype(BF)
    return hi, (a - hi.astype(F32)).astype(BF)


def _router_kernel(x_ref, whi_ref, wlo_ref, rb_ref, idx_ref, wsel_ref):
    xhi, xlo = _split_bf16(x_ref[...])
    whi = whi_ref[...]
    logits = _dot_nt(whi, xhi) + _dot_nt(whi, xlo) + _dot_nt(wlo_ref[...], xhi)
    tm = logits.shape[1]
    scores = _sigmoid(logits)
    biased = scores + _tile_lanes(rb_ref[...], tm // LANES)
    per_group = N_EXPERTS // N_GROUPS
    gs = []
    for g in range(N_GROUPS):
        bg = biased[g * per_group:(g + 1) * per_group]
        row = _iota(bg.shape, 0)
        m1 = jnp.max(bg, axis=0, keepdims=True)
        i1 = jnp.min(jnp.where(bg == m1, row, per_group), axis=0, keepdims=True)
        m2 = jnp.max(jnp.where(row == i1, -jnp.inf, bg), axis=0, keepdims=True)
        gs.append(m1 + m2)
    kept = []
    for g in range(N_GROUPS):
        rank = jnp.zeros((1, tm), F32)
        for o in range(N_GROUPS):
            if o != g:
                beats = (gs[o] >= gs[g]) if o < g else (gs[o] > gs[g])
                rank = rank + jnp.where(beats, 1.0, 0.0)
        kept.append(jnp.where(rank < TOPK_GROUPS, biased[g * per_group:(g + 1) * per_group], -jnp.inf))
    masked = jnp.concatenate(kept, axis=0)
    eid = _iota(masked.shape, 0)
    picks, weights = [], []
    for _ in range(TOP_K):
        mx = jnp.max(masked, axis=0, keepdims=True)
        pick = jnp.min(jnp.where(masked == mx, eid, N_EXPERTS), axis=0, keepdims=True)
        hit = eid == pick
        weights.append(jnp.sum(jnp.where(hit, scores, 0.0), axis=0, keepdims=True))
        masked = jnp.where(hit, -jnp.inf, masked)
        picks.append(pick)
    wsum = weights[0]
    for wk in weights[1:]:
        wsum = wsum + wk
    idx_ref[...] = jnp.concatenate(picks, axis=0)
    wsel_ref[...] = jnp.concatenate(weights, axis=0) / wsum * ROUTED_SCALE


def _router(x1, w_router, router_bias, tm):
    N = x1.shape[0]
    whi, wlo = _split_bf16(w_router.astype(F32).T)
    rb = jnp.broadcast_to(router_bias.astype(F32)[:, None], (N_EXPERTS, LANES))
    fixed = lambda i: (0, 0)
    col = lambda i: (0, i)
    return pl.pallas_call(
        _router_kernel,
        grid=(N // tm,),
        in_specs=[pl.BlockSpec((tm, D_MODEL), lambda i: (i, 0)), pl.BlockSpec(whi.shape, fixed),
                  pl.BlockSpec(wlo.shape, fixed), pl.BlockSpec(rb.shape, fixed)],
        out_specs=[pl.BlockSpec((TOP_K, tm), col), pl.BlockSpec((TOP_K, tm), col)],
        out_shape=[jax.ShapeDtypeStruct((TOP_K, N), jnp.int32), jax.ShapeDtypeStruct((TOP_K, N), F32)],
        compiler_params=_params("parallel"),
        name="router",
    )(x1, whi, wlo, rb)


def _expert_kernel(blk_ref, exp_ref, lo_ref, hi_ref, first_ref, newe_ref,
                   xs_ref, wg_ref, wu_ref, wd_ref, y_ref, wgb_ref, wub_ref, wdb_ref, *, bm):
    t = pl.program_id(0)
    lo, hi = lo_ref[t], hi_ref[t]

    @pl.when(newe_ref[t] == 1)
    def _():
        wgb_ref[...] = wg_ref[...].astype(BF)
        wub_ref[...] = wu_ref[...].astype(BF)
        wdb_ref[...] = wd_ref[...].astype(BF)

    base = blk_ref[t] * bm
    whole = (lo <= base) & (hi >= base + bm)

    @pl.when((first_ref[t] == 1) & jnp.logical_not(whole))
    def _():
        y_ref[...] = jnp.zeros_like(y_ref)

    sub = bm // 2

    def work(j0, n_sub, masked):
        rows = slice(j0 * sub, (j0 + n_sub) * sub)
        xs = xs_ref[rows, :]
        g = jnp.dot(xs, wgb_ref[...], preferred_element_type=F32)
        u = jnp.dot(xs, wub_ref[...], preferred_element_type=F32)
        h = (g * _sigmoid(g) * u).astype(BF)
        y = jnp.dot(h, wdb_ref[...], preferred_element_type=F32).astype(y_ref.dtype)
        if masked:
            row = base + j0 * sub + _iota((n_sub * sub, 1), 0)
            y = jnp.where((row >= lo) & (row < hi), y, y_ref[rows, :])
        y_ref[rows, :] = y

    need = [(hi > lo) & (lo < base + (j + 1) * sub) & (hi > base + j * sub) for j in range(2)]
    shared = jnp.logical_not(whole)
    pl.when(whole)(lambda: work(0, 2, False))
    pl.when(shared & need[0] & need[1])(lambda: work(0, 2, True))
    pl.when(shared & need[0] & jnp.logical_not(need[1]))(lambda: work(0, 1, True))
    pl.when(shared & jnp.logical_not(need[0]) & need[1])(lambda: work(1, 1, True))


def _expert_kernel_onto(blk_ref, exp_ref, lo_ref, hi_ref, first_ref, newe_ref,
                        xs_ref, wg_ref, wu_ref, wd_ref, prev_ref, y_ref, *scratch, bm):
    del prev_ref
    _expert_kernel(blk_ref, exp_ref, lo_ref, hi_ref, first_ref, newe_ref,
                   xs_ref, wg_ref, wu_ref, wd_ref, y_ref, *scratch, bm=bm)


def _experts(xs_parts, plans, layer, w_gate, w_up, w_down, bm):
    rows_part = xs_parts[0].shape[0]
    A = rows_part * len(xs_parts)
    wspec = lambda shape: pl.BlockSpec((None, None) + shape, lambda t, blk, exp, *_: (layer, exp[t], 0, 0))
    ys = None
    for p, (xs, plan) in enumerate(zip(xs_parts, plans)):
        blk0 = p * rows_part // bm
        in_specs = [pl.BlockSpec((bm, D_MODEL), lambda t, blk, *_, blk0=blk0: (blk[t] - blk0, 0)),
                    wspec((D_MODEL, EXPERT_DIM)), wspec((D_MODEL, EXPERT_DIM)), wspec((EXPERT_DIM, D_MODEL))]
        args = (*plan, xs, w_gate, w_up, w_down)
        if ys is not None:
            in_specs.append(pl.BlockSpec(memory_space=pl.ANY))
            args += (ys,)
        ys = pl.pallas_call(
            functools.partial(_expert_kernel if ys is None else _expert_kernel_onto, bm=bm),
            grid_spec=pltpu.PrefetchScalarGridSpec(
                num_scalar_prefetch=len(plan),
                grid=(plan[0].shape[0],),
                in_specs=in_specs,
                out_specs=pl.BlockSpec((bm, D_MODEL), lambda t, blk, *_: (blk[t], 0)),
                scratch_shapes=[pltpu.VMEM((D_MODEL, EXPERT_DIM), BF), pltpu.VMEM((D_MODEL, EXPERT_DIM), BF),
                                pltpu.VMEM((EXPERT_DIM, D_MODEL), BF)],
            ),
            out_shape=jax.ShapeDtypeStruct((A, D_MODEL), BF),
            input_output_aliases={} if ys is None else {len(args) - 1: 0},
            compiler_params=_params("arbitrary"),
            name="experts",
        )(*args)
    return ys


def _dispatch_plan(eidx_t, bm, n_parts):
    N = eidx_t.shape[1]
    A, E = N * TOP_K, N_EXPERTS
    assert A % (bm * n_parts) == 0
    ids = jnp.arange(A, dtype=jnp.int32)
    id_bits = (A - 1).bit_length()
    assert id_bits + (E - 1).bit_length() <= 31
    packed = jnp.sort((eidx_t.reshape(A) << id_bits) | ids)
    se, sid = packed >> id_bits, packed & ((1 << id_bits) - 1)
    _, pos = lax.sort((sid, ids), num_keys=1)
    start = jnp.searchsorted(se, jnp.arange(E, dtype=jnp.int32), side='left').astype(jnp.int32)
    changed = lambda a: jnp.concatenate([jnp.ones((1,), jnp.int32), (a[1:] != a[:-1]).astype(jnp.int32)])
    plans = []
    for p in range(n_parts):
        a0, a1 = p * (A // n_parts), (p + 1) * (A // n_parts)
        lo = jnp.sort(jnp.concatenate([jnp.arange(a0, a1, bm, dtype=jnp.int32), jnp.clip(start, a0, a1)]))
        hi = jnp.concatenate([lo[1:], jnp.full((1,), a1, jnp.int32)])
        blk = jnp.minimum(lo // bm, a1 // bm - 1)
        exp = se[jnp.minimum(lo, A - 1)]
        plans.append((blk, exp, lo, hi, changed(blk), changed(exp)))
    return sid % N, pos.reshape(TOP_K, N), plans


def _moe_out_kernel(x_ref, r_ref, rw_ref, wg_ref, wu_ref, wd_ref, g_ref, b_ref, y_ref):
    x = x_ref[...]
    tm = x.shape[0]
    xb = x.astype(BF)
    g = jnp.dot(xb, wg_ref[...], preferred_element_type=F32)
    u = jnp.dot(xb, wu_ref[...], preferred_element_type=F32)
    h = (g * _sigmoid(g) * u).astype(BF)
    shared = jnp.dot(h, wd_ref[...], preferred_element_type=F32)
    rw = jnp.concatenate([rw_ref[...], jnp.zeros((LANES - TOP_K, tm), F32)], axis=0).T
    routed = rw[:, 0:1] * r_ref[0].astype(F32)
    for k in range(1, TOP_K):
        routed = routed + rw[:, k:k + 1] * r_ref[k].astype(F32)
    y_ref[...] = _layer_norm(DEEPNORM_ALPHA * x + (routed + shared), g_ref[...], b_ref[...])


def _moe_out_kernel_onto(x_ref, r_ref, rw_ref, wg_ref, wu_ref, wd_ref, g_ref, b_ref, prev_ref, y_ref):
    del prev_ref
    _moe_out_kernel(x_ref, r_ref, rw_ref, wg_ref, wu_ref, wd_ref, g_ref, b_ref, y_ref)


def _moe_out(x1, routed_parts, route_w, w_sh_gate, w_sh_up, w_sh_down, ln_g, ln_b, tm):
    N = x1.shape[0]
    n_tok = routed_parts[0].shape[1]
    fixed = lambda i: (0, 0)
    weights = (w_sh_gate.astype(BF), w_sh_up.astype(BF), w_sh_down.astype(BF),
               ln_g.astype(F32)[None, :], ln_b.astype(F32)[None, :])
    y = None
    for p, routed in enumerate(routed_parts):
        t0 = p * n_tok // tm
        row = lambda i, t0=t0: (t0 + i, 0)
        in_specs = [pl.BlockSpec((tm, D_MODEL), row),
                    pl.BlockSpec((TOP_K, tm, D_MODEL), lambda i: (0, i, 0)),
                    pl.BlockSpec((TOP_K, tm), lambda i, t0=t0: (0, t0 + i)),
                    pl.BlockSpec(w_sh_gate.shape, fixed), pl.BlockSpec(w_sh_up.shape, fixed),
                    pl.BlockSpec(w_sh_down.shape, fixed),
                    pl.BlockSpec((1, D_MODEL), fixed), pl.BlockSpec((1, D_MODEL), fixed)]
        args = (x1, routed, route_w) + weights
        if y is not None:
            in_specs.append(pl.BlockSpec(memory_space=pl.ANY))
            args += (y,)
        y = pl.pallas_call(
            _moe_out_kernel if y is None else _moe_out_kernel_onto,
            grid=(n_tok // tm,),
            in_specs=in_specs,
            out_specs=pl.BlockSpec((tm, D_MODEL), row),
            out_shape=jax.ShapeDtypeStruct((N, D_MODEL), F32),
            input_output_aliases={} if y is None else {len(args) - 1: 0},
            compiler_params=_params("parallel"),
            name="moe_out",
        )(*args)
    return y


EXPERT_PARTS = 2
COMBINE_PARTS = 4

def _tiles(N, T):
    return min(512, T), min(256, T), MOBA_BLOCK, min(1024, N * TOP_K)


def _mixer_layer(x2, T, w_in, b_forget, diff_lambda, diff_subln, w_branch, w_out, ln_g, ln_b, lambda_init):
    N = x2.shape[0]
    tm_proj, tm_row, tq, _ = _tiles(N, T)
    w, wvt, w_gates, bf = _in_proj_weights(w_in, b_forget)
    qk, vt, misc = _in_proj(x2, w, wvt, bf, _rope_tables(T), T, tm_proj)
    dl = diff_lambda.astype(F32)
    lam = jnp.exp(jnp.sum(dl[0] * dl[1])) - jnp.exp(jnp.sum(dl[2] * dl[3])) + lambda_init
    o_a = _moba(qk, vt, T, tq)
    o_b = _diff(qk, vt, lam, diff_subln, lambda_init, T, tq)
    o_c = _fox(qk, vt, misc, T, tq)
    o_d = _dsa(qk, vt, misc, T, tq)
    return _mix_out(x2, (o_a, o_b, o_c, o_d), w_gates, w_branch, w_out, ln_g, ln_b, tm_row)


def _moe_layer(x1, x1b, T, layer, w_router, router_bias, w_exp_gate, w_exp_up, w_exp_down,
               w_sh_gate, w_sh_up, w_sh_down, ln_g, ln_b):
    N = x1.shape[0]
    _, tm_row, _, bm = _tiles(N, T)
    eidx_t, wsel_t = _router(x1, w_router, router_bias, tm_row)
    row_tok, pos, plans = _dispatch_plan(eidx_t, bm, EXPERT_PARTS)
    rows_part = row_tok.shape[0] // EXPERT_PARTS
    xs_parts = [x1b[row_tok[p * rows_part:(p + 1) * rows_part]] for p in range(EXPERT_PARTS)]
    ys = _experts(xs_parts, plans, layer, w_exp_gate, w_exp_up, w_exp_down, bm)
    n_tok = N // COMBINE_PARTS
    routed_parts = [ys[pos[:, p * n_tok:(p + 1) * n_tok]] for p in range(COMBINE_PARTS)]
    return _moe_out(x1, routed_parts, wsel_t, w_sh_gate, w_sh_up, w_sh_down, ln_g, ln_b, tm_row)


def kernel(x, w_in, b_forget, diff_lambda, diff_subln, w_branch, w_out, ln1_g, ln1_b, w_router, router_bias,
           w_exp_gate, w_exp_up, w_exp_down, w_sh_gate, w_sh_up, w_sh_down, ln2_g, ln2_b):
    B, T, D = x.shape
    x2 = x.reshape(B * T, D)
    for l in range(DEPTH):
        lambda_init = 0.8 - 0.6 * math.exp(-0.3 * l)
        x1, x1b = _mixer_layer(x2, T, w_in[l], b_forget[l], diff_lambda[l], diff_subln[l], w_branch[l],
                               w_out[l], ln1_g[l], ln1_b[l], lambda_init)
        x2 = _moe_layer(x1, x1b, T, l, w_router[l], router_bias[l], w_exp_gate, w_exp_up, w_exp_down,
                        w_sh_gate[l], w_sh_up[l], w_sh_down[l], ln2_g[l], ln2_b[l])
    return x2.reshape(B, T, D)
```

```python
import functools
import math

import jax
import jax.numpy as jnp
from jax import lax
from jax.experimental import pallas as pl
from jax.experimental.pallas import tpu as pltpu

F32 = jnp.float32
BF = jnp.bfloat16

D_MODEL = 1024
DEPTH = 2
HEAD_DIM = 64
N_HEADS = 4
DIFF_DIM = HEAD_DIM // 2
N_IDX_HEADS = 4
IDX_DIM = 64
BRANCH_WIDTH = N_HEADS * HEAD_DIM
N_BRANCHES = 4
MOBA_BLOCK = 256
MOBA_TOPK = 3
DSA_TOPK_MAX = 256
ROPE_THETA = 10000.0
N_EXPERTS = 256
TOP_K = 8
N_GROUPS = 8
TOPK_GROUPS = 4
EXPERT_DIM = 256
ROUTED_SCALE = 2.5
LN_EPS = 1e-5
DEEPNORM_ALPHA = (2 * DEPTH) ** 0.25

IN_SEGMENTS = (
    ('moba_q', BRANCH_WIDTH), ('moba_k', BRANCH_WIDTH), ('moba_v', BRANCH_WIDTH),
    ('diff_q', BRANCH_WIDTH), ('diff_k', BRANCH_WIDTH), ('diff_v', BRANCH_WIDTH),
    ('fox_q', BRANCH_WIDTH), ('fox_k', BRANCH_WIDTH), ('fox_v', BRANCH_WIDTH), ('fox_f', N_HEADS),
    ('dsa_q', BRANCH_WIDTH), ('dsa_k', BRANCH_WIDTH), ('dsa_v', BRANCH_WIDTH),
    ('idx_q', N_IDX_HEADS * IDX_DIM), ('idx_k', IDX_DIM), ('idx_w', N_IDX_HEADS),
    ('gates', N_BRANCHES * D_MODEL),
)

LANES = 128
SUBLANES = 8
SEG = BRANCH_WIDTH
NEG = -1e30
INT_MIN = -2 ** 31
VMEM_LIMIT = 56 * 1024 * 1024

LOG2E = math.log2(math.e)
_QSCALE = HEAD_DIM ** -0.5 * LOG2E
PROJ_SEGS = (
    ('moba_q', 64, _QSCALE), ('moba_k', 64, 1.0),
    ('diff_q', 32, DIFF_DIM ** -0.5 * LOG2E), ('diff_k', 32, 1.0),
    ('fox_q', 0, _QSCALE), ('fox_k', 0, 1.0),
    ('dsa_q', 64, _QSCALE), ('dsa_k', 64, 1.0),
    ('idx_q', 64, IDX_DIM ** -0.5), ('idx_k4', 64, 1.0),
)
SEG_ID = {name: i for i, (name, _, _) in enumerate(PROJ_SEGS)}
N_SEG = len(PROJ_SEGS)
V_SEGS = ('moba_v', 'diff_v', 'fox_v', 'dsa_v')
V_ID = {name: i for i, name in enumerate(V_SEGS)}
MISC_LOGF = 0
MISC_IDXW = 4


def _params(*sem):
    return pltpu.CompilerParams(dimension_semantics=sem, vmem_limit_bytes=VMEM_LIMIT)


def _iota(shape, dim):
    return lax.broadcasted_iota(jnp.int32, shape, dim)


def _dot_nt(a, b):
    return lax.dot_general(a, b, (((1,), (1,)), ((), ())), preferred_element_type=F32)


def _sigmoid(z):
    return 1.0 / (1.0 + jnp.exp(-z))


def _layer_norm(y, g, b):
    mu = jnp.mean(y, axis=-1, keepdims=True)
    yc = y - mu
    var = jnp.mean(yc * yc, axis=-1, keepdims=True)
    return yc * lax.rsqrt(var + LN_EPS) * g + b


def _swap_halves(a, half):
    w = a.shape[-1]
    first = (_iota(a.shape, 1) & (2 * half - 1)) < half
    return jnp.where(first, pltpu.roll(a, w - half, 1), pltpu.roll(a, half, 1))


def _in_proj_kernel(x_ref, w_ref, wvt_ref, c64_ref, s64_ref, c32_ref, s32_ref, bf_ref,
                    qk_ref, vt_ref, misc_ref):
    xb = x_ref[...].astype(BF)
    for s, (_, rot, scale) in enumerate(PROJ_SEGS):
        acc = jnp.dot(xb, w_ref[:, s * SEG:(s + 1) * SEG], preferred_element_type=F32)
        if rot == 64:
            acc = acc * c64_ref[...] + _swap_halves(acc, 32) * s64_ref[...]
        elif rot == 32:
            acc = acc * c32_ref[...] + _swap_halves(acc, 16) * s32_ref[...]
        if scale != 1.0:
            acc = acc * scale
        qk_ref[:, s * SEG:(s + 1) * SEG] = acc.astype(BF)
    vt_ref[...] = _dot_nt(wvt_ref[...], xb).astype(BF)
    m = jnp.dot(xb, w_ref[:, N_SEG * SEG:N_SEG * SEG + LANES], preferred_element_type=F32)
    z = m + bf_ref[...]
    logf = jnp.minimum(z, 0.0) - jnp.log1p(jnp.exp(-jnp.abs(z)))
    lane = _iota(m.shape, 1)
    misc_ref[...] = jnp.where(lane < MISC_IDXW, logf, m)


def _rope_tables(T):
    pos = jnp.arange(T).astype(F32)

    def tab(group, reps):
        half = group // 2
        inv_freq = ROPE_THETA ** (-jnp.arange(half, dtype=F32) / half)
        ang = pos[:, None] * inv_freq[None, :]
        cos, sin = jnp.cos(ang), jnp.sin(ang)
        return (jnp.tile(jnp.concatenate([cos, cos], -1), (1, reps)),
                jnp.tile(jnp.concatenate([-sin, sin], -1), (1, reps)))

    return tab(64, SEG // 64) + tab(32, SEG // 32)


def _in_proj_weights(w_in, b_forget):
    parts, off = {}, 0
    for name, width in IN_SEGMENTS:
        parts[name] = w_in[:, off:off + width]
        off += width
    parts['idx_k4'] = jnp.tile(parts['idx_k'], (1, N_IDX_HEADS))
    misc = jnp.concatenate([parts['fox_f'], parts['idx_w'],
                            jnp.zeros((D_MODEL, LANES - 2 * N_HEADS), w_in.dtype)], axis=1)
    w = jnp.concatenate([parts[name] for name, _, _ in PROJ_SEGS] + [misc], axis=1).astype(BF)
    wvt = jnp.concatenate([parts[name] for name in V_SEGS], axis=1).T.astype(BF)
    bf = jnp.zeros((1, LANES), F32).at[0, MISC_LOGF:MISC_LOGF + N_HEADS].set(b_forget.astype(F32))
    return w, wvt, parts['gates'].astype(BF), bf


def _in_proj(x2, w, wvt, bf, tables, T, tm):
    N = x2.shape[0]
    nt = T // tm
    tab_spec = pl.BlockSpec((tm, SEG), lambda i: (i % nt, 0))
    fixed = lambda i: (0, 0)
    return pl.pallas_call(
        _in_proj_kernel,
        grid=(N // tm,),
        in_specs=[pl.BlockSpec((tm, D_MODEL), lambda i: (i, 0)),
                  pl.BlockSpec(w.shape, fixed), pl.BlockSpec(wvt.shape, fixed),
                  tab_spec, tab_spec, tab_spec, tab_spec,
                  pl.BlockSpec((1, LANES), fixed)],
        out_specs=[pl.BlockSpec((tm, N_SEG * SEG), lambda i: (i, 0)),
                   pl.BlockSpec((len(V_SEGS) * SEG, tm), lambda i: (0, i)),
                   pl.BlockSpec((tm, LANES), lambda i: (i, 0))],
        out_shape=[jax.ShapeDtypeStruct((N, N_SEG * SEG), BF),
                   jax.ShapeDtypeStruct((len(V_SEGS) * SEG, N), BF),
                   jax.ShapeDtypeStruct((N, LANES), F32)],
        compiler_params=_params("parallel"),
        name="in_proj",
    )(x2, w, wvt, *tables, bf)


def _group_mask(shape, shift, g):
    return (_iota(shape, 1) >> shift) == g


def _stack_groups(q, shift, n_groups):
    zero = jnp.zeros_like(q)
    return jnp.concatenate([jnp.where(_group_mask(q.shape, shift, g), q, zero) for g in range(n_groups)],
                           axis=0)


def _cols(a, g, tq):
    return a[:, g * tq:(g + 1) * tq]


def _head_rows(a, h):
    return a[h * HEAD_DIM:(h + 1) * HEAD_DIM]


ONES_ROWS = 16


def _flash_init(n_groups, tq, n_sets):
    return (jnp.full((1, n_groups * tq), NEG, F32), jnp.zeros((1, n_groups * tq), F32),
            tuple(jnp.zeros((SEG, tq), F32) for _ in range(n_sets)))


def _flash_update(s, carry, vt, tq, head_sets):
    m, l, accs = carry
    m_new = jnp.maximum(m, jnp.max(s, axis=0, keepdims=True))
    alpha = jnp.exp2(m - m_new)
    pb = jnp.exp2(s - m_new).astype(BF)
    ones = jnp.ones((ONES_ROWS, vt.shape[1]), BF)
    new_accs, p_sum = [], {}
    for acc, groups in zip(accs, head_sets):
        parts = []
        for h, g in enumerate(groups):
            r = jnp.dot(jnp.concatenate([_head_rows(vt, h), ones], axis=0), _cols(pb, g, tq),
                        preferred_element_type=F32)
            parts.append(_cols(alpha, g, tq) * _head_rows(acc, h) + r[0:HEAD_DIM])
            p_sum[g] = r[HEAD_DIM:HEAD_DIM + 1]
        new_accs.append(jnp.concatenate(parts, axis=0))
    l = alpha * l + jnp.concatenate([p_sum[g] for g in sorted(p_sum)], axis=1)
    return m_new, l, tuple(new_accs)


def _causal(n_groups, tq):
    shape = (tq, n_groups * tq)
    return _iota(shape, 0) <= (_iota(shape, 1) & (tq - 1))


def _normalised(acc, l, groups, tq):
    return jnp.concatenate([_head_rows(acc, h) * (1.0 / _cols(l, g, tq)) for h, g in enumerate(groups)],
                           axis=0)


def _tile_lanes(a, n):
    return jnp.concatenate([a] * n, axis=1) if n > 1 else a


HEADS = tuple(range(N_HEADS))


def _flash_attend(i, tq, n_groups, head_sets, qs, k_ref, vt_ref, s_ref, post, last):
    def raw(k0):
        return _dot_nt(k_ref[pl.ds(k0, tq), :], qs)

    if s_ref is not None:
        s_ref[0] = raw(0)

    def body(c, carry):
        k0 = pl.multiple_of(c * tq, tq)
        if s_ref is None:
            s = post(raw(k0), c, k0)
        else:
            s = post(s_ref[c & 1], c, k0)
            s_ref[(c + 1) & 1] = raw(pl.multiple_of(k0 + tq, tq))
        return _flash_update(s, carry, vt_ref[:, pl.ds(k0, tq)], tq, head_sets)

    carry = lax.fori_loop(0, i, body, _flash_init(n_groups, tq, len(head_sets)))
    k0 = pl.multiple_of(i * tq, tq)
    s = last(raw(k0) if s_ref is None else s_ref[i & 1], k0)
    _, l, accs = _flash_update(s, carry, vt_ref[:, pl.ds(k0, tq)], tq, head_sets)
    return l, accs


def _logit_scratch(n_groups, tq):
    return pltpu.VMEM((2, tq, n_groups * tq), F32)


def _attn_specs(T, tq, q_seg, k_seg, v_name):
    nq = T // tq
    v_id = V_ID[v_name]
    return [pl.BlockSpec((tq, SEG), lambda b, i: (b * nq + i, q_seg)),
            pl.BlockSpec((T, SEG), lambda b, i: (b, k_seg)),
            pl.BlockSpec((SEG, T), lambda b, i: (v_id, b))]


def _attn_out(N, T, tq):
    nq = T // tq
    return (pl.BlockSpec((tq, SEG), lambda b, i: (b * nq + i, 0)),
            jax.ShapeDtypeStruct((N, SEG), BF))


def _moba_kernel(q_ref, k_ref, vt_ref, o_ref, km_ref, sel_ref, s_ref, *, tq, nb, n_sel):
    i = pl.program_id(1)

    @pl.when(i == 0)
    def _():
        km_ref[...] = jnp.zeros_like(km_ref)
        for n in range(nb):
            blk = k_ref[n * tq:(n + 1) * tq, :].astype(F32)
            km_ref[n:n + 1, :] = jnp.mean(blk, axis=0, keepdims=True)

    qs = _stack_groups(q_ref[...], 6, N_HEADS)
    km = km_ref[...]
    km_hi = km.astype(BF)
    km_lo = (km - km_hi.astype(F32)).astype(BF)
    nrow = sel_ref.shape[0]
    g = (_dot_nt(km_hi, qs) + _dot_nt(km_lo, qs))[0:nrow]
    blk_id = _iota(g.shape, 0)
    g = jnp.where(blk_id < i, g, -jnp.inf)
    sel = jnp.zeros(g.shape, F32)
    for _ in range(n_sel):
        gmax = jnp.max(g, axis=0, keepdims=True)
        first = jnp.min(jnp.where(g == gmax, blk_id, nrow), axis=0, keepdims=True)
        hit = blk_id == first
        sel = jnp.where(hit & (gmax > -jnp.inf), 1.0, sel)
        g = jnp.where(hit, -jnp.inf, g)
    sel_ref[...] = sel

    l, (acc,) = _flash_attend(
        i, tq, N_HEADS, (HEADS,), qs, k_ref, vt_ref, s_ref,
        post=lambda s, c, k0: jnp.where(sel_ref[pl.ds(c, 1), :] > 0.5, s, NEG),
        last=lambda s, k0: jnp.where(_causal(N_HEADS, tq), s, NEG))
    o_ref[...] = _normalised(acc, l, HEADS, tq).T.astype(BF)


def _moba(qk, vt, T, tq):
    N = qk.shape[0]
    nb = T // MOBA_BLOCK
    assert T % MOBA_BLOCK == 0 and tq == MOBA_BLOCK and nb <= LANES
    sel_rows = -(-nb // SUBLANES) * SUBLANES
    out_spec, out_shape = _attn_out(N, T, tq)
    return pl.pallas_call(
        functools.partial(_moba_kernel, tq=tq, nb=nb, n_sel=min(MOBA_TOPK, nb - 1)),
        grid=(N // T, T // tq),
        in_specs=_attn_specs(T, tq, SEG_ID['moba_q'], SEG_ID['moba_k'], 'moba_v'),
        out_specs=out_spec, out_shape=out_shape,
        scratch_shapes=[pltpu.VMEM((LANES, SEG), F32), pltpu.VMEM((sel_rows, N_HEADS * tq), F32),
                        _logit_scratch(N_HEADS, tq)],
        compiler_params=_params("parallel", "arbitrary"),
        name="moba_attn",
    )(qk, qk, vt)


DIFF_SETS = (tuple(2 * h for h in HEADS), tuple(2 * h + 1 for h in HEADS))


def _diff_kernel(lam_ref, q_ref, k_ref, vt_ref, g_ref, o_ref, *, tq, out_scale):
    i = pl.program_id(1)
    n_groups = 2 * N_HEADS
    qs = _stack_groups(q_ref[...], 5, n_groups)
    l, (a1, a2) = _flash_attend(
        i, tq, n_groups, DIFF_SETS, qs, k_ref, vt_ref, None,
        post=lambda s, c, k0: s,
        last=lambda s, k0: jnp.where(_causal(n_groups, tq), s, NEG))
    out = _normalised(a1, l, DIFF_SETS[0], tq) - lam_ref[0] * _normalised(a2, l, DIFF_SETS[1], tq)
    normed = []
    for h in HEADS:
        oh = _head_rows(out, h)
        ms = jnp.mean(oh * oh, axis=0, keepdims=True)
        normed.append(oh * lax.rsqrt(ms + LN_EPS))
    o_ref[...] = (jnp.concatenate(normed, axis=0).T * g_ref[...] * out_scale).astype(BF)


def _diff(qk, vt, lam, subln_g, lambda_init, T, tq):
    N = qk.shape[0]
    out_spec, out_shape = _attn_out(N, T, tq)
    g = jnp.tile(subln_g.astype(F32), N_HEADS)[None, :]
    return pl.pallas_call(
        functools.partial(_diff_kernel, tq=tq, out_scale=1.0 - lambda_init),
        grid=(N // T, T // tq),
        in_specs=[pl.BlockSpec(memory_space=pltpu.SMEM)]
        + _attn_specs(T, tq, SEG_ID['diff_q'], SEG_ID['diff_k'], 'diff_v')
        + [pl.BlockSpec((1, SEG), lambda b, i: (0, 0))],
        out_specs=out_spec, out_shape=out_shape,
        compiler_params=_params("parallel", "arbitrary"),
        name="diff_attn",
    )(lam.reshape(1).astype(F32), qk, qk, vt, g)


def _fox_kernel(q_ref, k_ref, vt_ref, misc_ref, o_ref, c_ref, s_ref, *, tq):
    i = pl.program_id(1)

    @pl.when(i == 0)
    def _():
        tri = jnp.where(_iota((tq, tq), 1) <= _iota((tq, tq), 0), 1.0, 0.0).astype(BF)
        carry = [jnp.zeros((1, LANES), F32) for _ in HEADS]
        for n in range(c_ref.shape[1] // tq):
            blk = misc_ref[n * tq:(n + 1) * tq, :] * LOG2E
            for h in HEADS:
                col = jnp.broadcast_to(blk[:, MISC_LOGF + h:MISC_LOGF + h + 1], blk.shape)
                hi = col.astype(BF)
                rest = col - hi.astype(F32)
                mid = rest.astype(BF)
                lo = (rest - mid.astype(F32)).astype(BF)
                cs = carry[h] + (jnp.dot(tri, hi, preferred_element_type=F32)
                                 + jnp.dot(tri, mid, preferred_element_type=F32)
                                 + jnp.dot(tri, lo, preferred_element_type=F32))
                c_ref[h, n * tq:(n + 1) * tq, :] = cs
                carry[h] = cs[tq - 1:tq, :]

    qs = _stack_groups(q_ref[...], 6, N_HEADS)

    def decayed(s, k0):
        return jnp.concatenate(
            [_cols(s, h, tq) - _tile_lanes(c_ref[h, pl.ds(k0, tq), :], tq // LANES) for h in HEADS], axis=1)

    l, (acc,) = _flash_attend(
        i, tq, N_HEADS, (HEADS,), qs, k_ref, vt_ref, s_ref,
        post=lambda s, c, k0: decayed(s, k0),
        last=lambda s, k0: jnp.where(_causal(N_HEADS, tq), decayed(s, k0), NEG))
    o_ref[...] = _normalised(acc, l, HEADS, tq).T.astype(BF)


def _fox(qk, vt, misc, T, tq):
    N = qk.shape[0]
    out_spec, out_shape = _attn_out(N, T, tq)
    return pl.pallas_call(
        functools.partial(_fox_kernel, tq=tq),
        grid=(N // T, T // tq),
        in_specs=_attn_specs(T, tq, SEG_ID['fox_q'], SEG_ID['fox_k'], 'fox_v')
        + [pl.BlockSpec((T, LANES), lambda b, i: (b, 0))],
        out_specs=out_spec, out_shape=out_shape,
        scratch_shapes=[pltpu.VMEM((N_HEADS, T, LANES), F32), _logit_scratch(N_HEADS, tq)],
        compiler_params=_params("parallel", "arbitrary"),
        name="fox_attn",
    )(qk, qk, vt, misc)


COUNT_ROWS = 64
F32_TINY = 2.0 ** -126

def _dsa_kernel(q_ref, k_ref, vt_ref, qi_ref, ki_ref, w_ref, o_ref, key_ref, top_ref, bias_ref, s_ref,
                *, tq, n_keep):
    i = pl.program_id(1)
    n_ch = i + 1
    qpos = i * tq + _iota((1, tq), 1)

    qis = _stack_groups(qi_ref[...], 6, N_IDX_HEADS)
    wt = w_ref[...].T * (N_IDX_HEADS ** -0.5)
    w_rows = [wt[MISC_IDXW + h:MISC_IDXW + h + 1] for h in range(N_IDX_HEADS)]

    def score_body(c, carry):
        k0 = pl.multiple_of(c * tq, tq)
        d = jnp.maximum(_dot_nt(ki_ref[pl.ds(k0, tq), :], qis), 0.0)
        score = w_rows[0] * _cols(d, 0, tq)
        for h in range(1, N_IDX_HEADS):
            score = score + w_rows[h] * _cols(d, h, tq)
        sc = jnp.where((k0 + _iota((tq, tq), 0)) <= qpos, score, -jnp.inf)
        sc = jnp.where(jnp.abs(sc) < F32_TINY, 0.0, sc)
        bits = pltpu.bitcast(sc, jnp.int32)
        key_ref[pl.ds(k0, tq), :] = jnp.where(bits < 0, bits ^ jnp.int32(0x7FFFFFFF), bits)
        top_ref[pl.ds(k0, tq), :] = pltpu.bitcast(bits & jnp.int32(-65536), F32).astype(BF)
        return carry

    lax.fori_loop(0, n_ch, score_body, 0)

    @pl.when(n_ch % 2 == 1)
    def _():
        pad = pl.ds(pl.multiple_of(n_ch * tq, tq), tq)
        key_ref[pad, :] = jnp.full((tq, tq), INT_MIN, jnp.int32)
        top_ref[pad, :] = jnp.full((tq, tq), jnp.nan, BF)

    def count(src_ref, pred, dtype):
        one, zero = jnp.ones((), dtype), jnp.zeros((), dtype)

        def body(c, acc):
            k0 = pl.multiple_of(c * 2 * tq, 2 * tq)
            hit = jnp.where(pred(src_ref[pl.ds(k0, 2 * tq), :]), one, zero)
            parts = [hit[j * COUNT_ROWS:(j + 1) * COUNT_ROWS] for j in range(2 * tq // COUNT_ROWS)]
            while len(parts) > 1:
                parts = [a + b for a, b in zip(parts[0::2], parts[1::2])]
            return acc + parts[0]
        acc = lax.fori_loop(0, (n_ch + 1) // 2, body, jnp.zeros((COUNT_ROWS, tq), dtype))
        return jnp.sum(acc.astype(F32), axis=0, keepdims=True)

    def top_body(it, thr):
        cand = thr + lax.shift_left(jnp.int32(1), 31 - it)
        fbits = jnp.where(cand < 0, cand ^ jnp.int32(0x7FFFFFFF), cand) & jnp.int32(-65536)
        subnormal = ((fbits & jnp.int32(0x7F800000)) == 0) & ((fbits & jnp.int32(0x007F0000)) != 0)
        fbits = jnp.where(subnormal, jnp.where(fbits < 0, 0, jnp.int32(0x00800000)), fbits)
        cand_f = pltpu.bitcast(fbits, F32).astype(BF)
        return jnp.where(count(top_ref, lambda tt: tt >= cand_f, BF) >= n_keep, cand, thr)

    def low_body(it, thr):
        cand = thr + lax.shift_left(jnp.int32(1), 31 - it)
        return jnp.where(count(key_ref, lambda kk: kk >= cand, F32) >= n_keep, cand, thr)

    thr = lax.fori_loop(0, 16, top_body, jnp.full((1, tq), INT_MIN, jnp.int32))
    thr = lax.fori_loop(16, 32, low_body, thr)

    need = n_keep - count(key_ref, lambda kk: kk > thr, F32)
    lower = jnp.where(_iota((tq, tq), 1) <= _iota((tq, tq), 0), 1.0, 0.0).astype(BF)

    def tie_body(c, seen):
        k0 = pl.multiple_of(c * tq, tq)
        kk = key_ref[pl.ds(k0, tq), :]
        eq = jnp.where(kk == thr, 1.0, 0.0)
        rank = jnp.dot(lower, eq.astype(BF), preferred_element_type=F32) + seen
        keep = (kk > thr) | ((kk == thr) & (rank <= need))
        causal = (k0 + _iota((tq, tq), 0)) <= qpos
        bias_ref[pl.ds(k0, tq), :] = jnp.where(keep & causal, 0.0, NEG)
        return seen + jnp.sum(eq, axis=0, keepdims=True)

    lax.fori_loop(0, n_ch, tie_body, jnp.zeros((1, tq), F32))

    qs = _stack_groups(q_ref[...], 6, N_HEADS)
    biased = lambda s, k0: s + _tile_lanes(bias_ref[pl.ds(k0, tq), :], N_HEADS)
    l, (acc,) = _flash_attend(i, tq, N_HEADS, (HEADS,), qs, k_ref, vt_ref, s_ref,
                              post=lambda s, c, k0: biased(s, k0), last=biased)
    o_ref[...] = _normalised(acc, l, HEADS, tq).T.astype(BF)


def _dsa(qk, vt, misc, T, tq):
    N = qk.shape[0]
    nq = T // tq
    out_spec, out_shape = _attn_out(N, T, tq)
    return pl.pallas_call(
        functools.partial(_dsa_kernel, tq=tq, n_keep=min(DSA_TOPK_MAX, T // 4)),
        grid=(N // T, nq),
        in_specs=_attn_specs(T, tq, SEG_ID['dsa_q'], SEG_ID['dsa_k'], 'dsa_v')
        + [pl.BlockSpec((tq, SEG), lambda b, i: (b * nq + i, SEG_ID['idx_q'])),
           pl.BlockSpec((T, SEG), lambda b, i: (b, SEG_ID['idx_k4'])),
           pl.BlockSpec((tq, LANES), lambda b, i: (b * nq + i, 0))],
        out_specs=out_spec, out_shape=out_shape,
        scratch_shapes=[pltpu.VMEM(((nq + nq % 2) * tq, tq), jnp.int32),
                        pltpu.VMEM(((nq + nq % 2) * tq, tq), BF), pltpu.VMEM((T, tq), F32),
                        _logit_scratch(N_HEADS, tq)],
        compiler_params=_params("parallel", "arbitrary"),
        name="dsa_attn",
    )(qk, qk, vt, qk, qk, misc)


def _mix_out_kernel(x_ref, oa_ref, ob_ref, oc_ref, od_ref, wg_ref, wb_ref, wo_ref, g_ref, b_ref,
                    y_ref, yb_ref):
    x = x_ref[...]
    xb = x.astype(BF)
    merged = jnp.zeros(x.shape, F32)
    for n, o_ref in enumerate((oa_ref, ob_ref, oc_ref, od_ref)):
        gate = _sigmoid(jnp.dot(xb, wg_ref[:, n * D_MODEL:(n + 1) * D_MODEL], preferred_element_type=F32))
        merged = merged + gate * jnp.dot(o_ref[...], wb_ref[n], preferred_element_type=F32)
    h = jnp.dot(merged.astype(BF), wo_ref[...], preferred_element_type=F32)
    y = _layer_norm(DEEPNORM_ALPHA * x + h, g_ref[...], b_ref[...])
    y_ref[...] = y
    yb_ref[...] = y.astype(BF)


def _mix_out(x2, branches, w_gates, w_branch, w_out, ln_g, ln_b, tm):
    N = x2.shape[0]
    row = lambda i: (i, 0)
    fixed2 = lambda i: (0, 0)
    return pl.pallas_call(
        _mix_out_kernel,
        grid=(N // tm,),
        in_specs=[pl.BlockSpec((tm, D_MODEL), row)] + [pl.BlockSpec((tm, SEG), row)] * N_BRANCHES
        + [pl.BlockSpec(w_gates.shape, fixed2),
           pl.BlockSpec(w_branch.shape, lambda i: (0, 0, 0)),
           pl.BlockSpec(w_out.shape, fixed2),
           pl.BlockSpec((1, D_MODEL), fixed2), pl.BlockSpec((1, D_MODEL), fixed2)],
        out_specs=[pl.BlockSpec((tm, D_MODEL), row), pl.BlockSpec((tm, D_MODEL), row)],
        out_shape=[jax.ShapeDtypeStruct((N, D_MODEL), F32), jax.ShapeDtypeStruct((N, D_MODEL), BF)],
        compiler_params=_params("parallel"),
        name="mix_out",
    )(x2, *branches, w_gates, w_branch.astype(BF), w_out.astype(BF),
      ln_g.astype(F32)[None, :], ln_b.astype(F32)[None, :])


def _split_bf16(a):
    hi = a.astype(BF)
    return hi, (a - hi.astype(F32)).astype(BF)


def _router_kernel(x_ref, whi_ref, wlo_ref, rb_ref, idx_ref, wsel_ref):
    xhi, xlo = _split_bf16(x_ref[...])
    whi = whi_ref[...]
    logits = _dot_nt(whi, xhi) + _dot_nt(whi, xlo) + _dot_nt(wlo_ref[...], xhi)
    tm = logits.shape[1]
    scores = _sigmoid(logits)
    biased = scores + _tile_lanes(rb_ref[...], tm // LANES)
    per_group = N_EXPERTS // N_GROUPS
    gs = []
    for g in range(N_GROUPS):
        bg = biased[g * per_group:(g + 1) * per_group]
        row = _iota(bg.shape, 0)
        m1 = jnp.max(bg, axis=0, keepdims=True)
        i1 = jnp.min(jnp.where(bg == m1, row, per_group), axis=0, keepdims=True)
        m2 = jnp.max(jnp.where(row == i1, -jnp.inf, bg), axis=0, keepdims=True)
        gs.append(m1 + m2)
    kept = []
    for g in range(N_GROUPS):
        rank = jnp.zeros((1, tm), F32)
        for o in range(N_GROUPS):
            if o != g:
                beats = (gs[o] >= gs[g]) if o < g else (gs[o] > gs[g])
                rank = rank + jnp.where(beats, 1.0, 0.0)
        kept.append(jnp.where(rank < TOPK_GROUPS, biased[g * per_group:(g + 1) * per_group], -jnp.inf))
    masked = jnp.concatenate(kept, axis=0)
    eid = _iota(masked.shape, 0)
    picks, weights = [], []
    for _ in range(TOP_K):
        mx = jnp.max(masked, axis=0, keepdims=True)
        pick = jnp.min(jnp.where(masked == mx, eid, N_EXPERTS), axis=0, keepdims=True)
        hit = eid == pick
        weights.append(jnp.sum(jnp.where(hit, scores, 0.0), axis=0, keepdims=True))
        masked = jnp.where(hit, -jnp.inf, masked)
        picks.append(pick)
    wsum = weights[0]
    for wk in weights[1:]:
        wsum = wsum + wk
    idx_ref[...] = jnp.concatenate(picks, axis=0)
    wsel_ref[...] = jnp.concatenate(weights, axis=0) / wsum * ROUTED_SCALE


def _router(x1, w_router, router_bias, tm):
    N = x1.shape[0]
    whi, wlo = _split_bf16(w_router.astype(F32).T)
    rb = jnp.broadcast_to(router_bias.astype(F32)[:, None], (N_EXPERTS, LANES))
    fixed = lambda i: (0, 0)
    col = lambda i: (0, i)
    return pl.pallas_call(
        _router_kernel,
        grid=(N // tm,),
        in_specs=[pl.BlockSpec((tm, D_MODEL), lambda i: (i, 0)), pl.BlockSpec(whi.shape, fixed),
                  pl.BlockSpec(wlo.shape, fixed), pl.BlockSpec(rb.shape, fixed)],
        out_specs=[pl.BlockSpec((TOP_K, tm), col), pl.BlockSpec((TOP_K, tm), col)],
        out_shape=[jax.ShapeDtypeStruct((TOP_K, N), jnp.int32), jax.ShapeDtypeStruct((TOP_K, N), F32)],
        compiler_params=_params("parallel"),
        name="router",
    )(x1, whi, wlo, rb)


def _expert_kernel(blk_ref, exp_ref, lo_ref, hi_ref, first_ref, newe_ref,
                   xs_ref, wg_ref, wu_ref, wd_ref, y_ref, wgb_ref, wub_ref, wdb_ref, *, bm):
    t = pl.program_id(0)
    lo, hi = lo_ref[t], hi_ref[t]

    @pl.when(newe_ref[t] == 1)
    def _():
        wgb_ref[...] = wg_ref[...].astype(BF)
        wub_ref[...] = wu_ref[...].astype(BF)
        wdb_ref[...] = wd_ref[...].astype(BF)

    @pl.when(first_ref[t] == 1)
    def _():
        y_ref[...] = jnp.zeros_like(y_ref)

    sub = bm // 2
    base = blk_ref[t] * bm

    def work(j0, n_sub):
        rows = slice(j0 * sub, (j0 + n_sub) * sub)
        xs = xs_ref[rows, :]
        g = jnp.dot(xs, wgb_ref[...], preferred_element_type=F32)
        u = jnp.dot(xs, wub_ref[...], preferred_element_type=F32)
        h = (g * _sigmoid(g) * u).astype(BF)
        y = jnp.dot(h, wdb_ref[...], preferred_element_type=F32)
        row = base + j0 * sub + _iota((n_sub * sub, 1), 0)
        mine = (row >= lo) & (row < hi)
        y_ref[rows, :] = jnp.where(mine, y.astype(y_ref.dtype), y_ref[rows, :])

    need = [(hi > lo) & (lo < base + (j + 1) * sub) & (hi > base + j * sub) for j in range(2)]
    pl.when(need[0] & need[1])(lambda: work(0, 2))
    pl.when(need[0] & jnp.logical_not(need[1]))(lambda: work(0, 1))
    pl.when(jnp.logical_not(need[0]) & need[1])(lambda: work(1, 1))


def _expert_kernel_onto(blk_ref, exp_ref, lo_ref, hi_ref, first_ref, newe_ref,
                        xs_ref, wg_ref, wu_ref, wd_ref, prev_ref, y_ref, *scratch, bm):
    del prev_ref
    _expert_kernel(blk_ref, exp_ref, lo_ref, hi_ref, first_ref, newe_ref,
                   xs_ref, wg_ref, wu_ref, wd_ref, y_ref, *scratch, bm=bm)


def _experts(xs_parts, plans, layer, w_gate, w_up, w_down, bm):
    rows_part = xs_parts[0].shape[0]
    A = rows_part * len(xs_parts)
    wspec = lambda shape: pl.BlockSpec((None, None) + shape, lambda t, blk, exp, *_: (layer, exp[t], 0, 0))
    ys = None
    for p, (xs, plan) in enumerate(zip(xs_parts, plans)):
        blk0 = p * rows_part // bm
        in_specs = [pl.BlockSpec((bm, D_MODEL), lambda t, blk, *_, blk0=blk0: (blk[t] - blk0, 0)),
                    wspec((D_MODEL, EXPERT_DIM)), wspec((D_MODEL, EXPERT_DIM)), wspec((EXPERT_DIM, D_MODEL))]
        args = (*plan, xs, w_gate, w_up, w_down)
        if ys is not None:
            in_specs.append(pl.BlockSpec(memory_space=pl.ANY))
            args += (ys,)
        ys = pl.pallas_call(
            functools.partial(_expert_kernel if ys is None else _expert_kernel_onto, bm=bm),
            grid_spec=pltpu.PrefetchScalarGridSpec(
                num_scalar_prefetch=len(plan),
                grid=(plan[0].shape[0],),
                in_specs=in_specs,
                out_specs=pl.BlockSpec((bm, D_MODEL), lambda t, blk, *_: (blk[t], 0)),
                scratch_shapes=[pltpu.VMEM((D_MODEL, EXPERT_DIM), BF), pltpu.VMEM((D_MODEL, EXPERT_DIM), BF),
                                pltpu.VMEM((EXPERT_DIM, D_MODEL), BF)],
            ),
            out_shape=jax.ShapeDtypeStruct((A, D_MODEL), BF),
            input_output_aliases={} if ys is None else {len(args) - 1: 0},
            compiler_params=_params("arbitrary"),
            name="experts",
        )(*args)
    return ys


def _dispatch_plan(eidx_t, bm, n_parts):
    N = eidx_t.shape[1]
    A, E = N * TOP_K, N_EXPERTS
    assert A % (bm * n_parts) == 0
    ids = jnp.arange(A, dtype=jnp.int32)
    id_bits = (A - 1).bit_length()
    assert id_bits + (E - 1).bit_length() <= 31
    packed = jnp.sort((eidx_t.reshape(A) << id_bits) | ids)
    se, sid = packed >> id_bits, packed & ((1 << id_bits) - 1)
    _, pos = lax.sort((sid, ids), num_keys=1)
    start = jnp.searchsorted(se, jnp.arange(E, dtype=jnp.int32), side='left').astype(jnp.int32)
    changed = lambda a: jnp.concatenate([jnp.ones((1,), jnp.int32), (a[1:] != a[:-1]).astype(jnp.int32)])
    plans = []
    for p in range(n_parts):
        a0, a1 = p * (A // n_parts), (p + 1) * (A // n_parts)
        lo = jnp.sort(jnp.concatenate([jnp.arange(a0, a1, bm, dtype=jnp.int32), jnp.clip(start, a0, a1)]))
        hi = jnp.concatenate([lo[1:], jnp.full((1,), a1, jnp.int32)])
        blk = jnp.minimum(lo // bm, a1 // bm - 1)
        exp = se[jnp.minimum(lo, A - 1)]
        plans.append((blk, exp, lo, hi, changed(blk), changed(exp)))
    return sid % N, pos.reshape(TOP_K, N), plans


def _moe_out_kernel(x_ref, r_ref, rw_ref, wg_ref, wu_ref, wd_ref, g_ref, b_ref, y_ref):
    x = x_ref[...]
    tm = x.shape[0]
    xb = x.astype(BF)
    g = jnp.dot(xb, wg_ref[...], preferred_element_type=F32)
    u = jnp.dot(xb, wu_ref[...], preferred_element_type=F32)
    h = (g * _sigmoid(g) * u).astype(BF)
    shared = jnp.dot(h, wd_ref[...], preferred_element_type=F32)
    rw = jnp.concatenate([rw_ref[...], jnp.zeros((LANES - TOP_K, tm), F32)], axis=0).T
    routed = rw[:, 0:1] * r_ref[0].astype(F32)
    for k in range(1, TOP_K):
        routed = routed + rw[:, k:k + 1] * r_ref[k].astype(F32)
    y_ref[...] = _layer_norm(DEEPNORM_ALPHA * x + (routed + shared), g_ref[...], b_ref[...])


def _moe_out_kernel_onto(x_ref, r_ref, rw_ref, wg_ref, wu_ref, wd_ref, g_ref, b_ref, prev_ref, y_ref):
    del prev_ref
    _moe_out_kernel(x_ref, r_ref, rw_ref, wg_ref, wu_ref, wd_ref, g_ref, b_ref, y_ref)


def _moe_out(x1, routed_parts, route_w, w_sh_gate, w_sh_up, w_sh_down, ln_g, ln_b, tm):
    N = x1.shape[0]
    n_tok = routed_parts[0].shape[1]
    fixed = lambda i: (0, 0)
    weights = (w_sh_gate.astype(BF), w_sh_up.astype(BF), w_sh_down.astype(BF),
               ln_g.astype(F32)[None, :], ln_b.astype(F32)[None, :])
    y = None
    for p, routed in enumerate(routed_parts):
        t0 = p * n_tok // tm
        row = lambda i, t0=t0: (t0 + i, 0)
        in_specs = [pl.BlockSpec((tm, D_MODEL), row),
                    pl.BlockSpec((TOP_K, tm, D_MODEL), lambda i: (0, i, 0)),
                    pl.BlockSpec((TOP_K, tm), lambda i, t0=t0: (0, t0 + i)),
                    pl.BlockSpec(w_sh_gate.shape, fixed), pl.BlockSpec(w_sh_up.shape, fixed),
                    pl.BlockSpec(w_sh_down.shape, fixed),
                    pl.BlockSpec((1, D_MODEL), fixed), pl.BlockSpec((1, D_MODEL), fixed)]
        args = (x1, routed, route_w) + weights
        if y is not None:
            in_specs.append(pl.BlockSpec(memory_space=pl.ANY))
            args += (y,)
        y = pl.pallas_call(
            _moe_out_kernel if y is None else _moe_out_kernel_onto,
            grid=(n_tok // tm,),
            in_specs=in_specs,
            out_specs=pl.BlockSpec((tm, D_MODEL), row),
            out_shape=jax.ShapeDtypeStruct((N, D_MODEL), F32),
            input_output_aliases={} if y is None else {len(args) - 1: 0},
            compiler_params=_params("parallel"),
            name="moe_out",
        )(*args)
    return y


EXPERT_PARTS = 2
COMBINE_PARTS = 2

def _tiles(N, T):
    return min(512, T), min(256, T), MOBA_BLOCK, min(1024, N * TOP_K)


def _mixer_layer(x2, T, w_in, b_forget, diff_lambda, diff_subln, w_branch, w_out, ln_g, ln_b, lambda_init):
    N = x2.shape[0]
    tm_proj, _, tq, _ = _tiles(N, T)
    w, wvt, w_gates, bf = _in_proj_weights(w_in, b_forget)
    qk, vt, misc = _in_proj(x2, w, wvt, bf, _rope_tables(T), T, tm_proj)
    dl = diff_lambda.astype(F32)
    lam = jnp.exp(jnp.sum(dl[0] * dl[1])) - jnp.exp(jnp.sum(dl[2] * dl[3])) + lambda_init
    o_a = _moba(qk, vt, T, tq)
    o_b = _diff(qk, vt, lam, diff_subln, lambda_init, T, tq)
    o_c = _fox(qk, vt, misc, T, tq)
    o_d = _dsa(qk, vt, misc, T, tq)
    return _mix_out(x2, (o_a, o_b, o_c, o_d), w_gates, w_branch, w_out, ln_g, ln_b, tm_proj)


def _moe_layer(x1, x1b, T, layer, w_router, router_bias, w_exp_gate, w_exp_up, w_exp_down,
               w_sh_gate, w_sh_up, w_sh_down, ln_g, ln_b):
    N = x1.shape[0]
    _, tm_row, _, bm = _tiles(N, T)
    eidx_t, wsel_t = _router(x1, w_router, router_bias, tm_row)
    row_tok, pos, plans = _dispatch_plan(eidx_t, bm, EXPERT_PARTS)
    rows_part = row_tok.shape[0] // EXPERT_PARTS
    xs_parts = [x1b[row_tok[p * rows_part:(p + 1) * rows_part]] for p in range(EXPERT_PARTS)]
    ys = _experts(xs_parts, plans, layer, w_exp_gate, w_exp_up, w_exp_down, bm)
    n_tok = N // COMBINE_PARTS
    routed_parts = [ys[pos[:, p * n_tok:(p + 1) * n_tok]] for p in range(COMBINE_PARTS)]
    return _moe_out(x1, routed_parts, wsel_t, w_sh_gate, w_sh_up, w_sh_down, ln_g, ln_b, tm_row)


def kernel(x, w_in, b_forget, diff_lambda, diff_subln, w_branch, w_out, ln1_g, ln1_b, w_router, router_bias,
           w_exp_gate, w_exp_up, w_exp_down, w_sh_gate, w_sh_up, w_sh_down, ln2_g, ln2_b):
    B, T, D = x.shape
    x2 = x.reshape(B * T, D)
    for l in range(DEPTH):
        lambda_init = 0.8 - 0.6 * math.exp(-0.3 * l)
        x1, x1b = _mixer_layer(x2, T, w_in[l], b_forget[l], diff_lambda[l], diff_subln[l], w_branch[l],
                               w_out[l], ln1_g[l], ln1_b[l], lambda_init)
        x2 = _moe_layer(x1, x1b, T, l, w_router[l], router_bias[l], w_exp_gate, w_exp_up, w_exp_down,
                        w_sh_gate[l], w_sh_up[l], w_sh_down[l], ln2_g[l], ln2_b[l])
    return x2.reshape(B, T, D)
```

```python
import functools
import math

import jax
import jax.numpy as jnp
from jax import lax
from jax.experimental import pallas as pl
from jax.experimental.pallas import tpu as pltpu

F32 = jnp.float32
BF = jnp.bfloat16

D_MODEL = 1024
DEPTH = 2
HEAD_DIM = 64
N_HEADS = 4
DIFF_DIM = HEAD_DIM // 2
N_IDX_HEADS = 4
IDX_DIM = 64
BRANCH_WIDTH = N_HEADS * HEAD_DIM
N_BRANCHES = 4
MOBA_BLOCK = 256
MOBA_TOPK = 3
DSA_TOPK_MAX = 256
ROPE_THETA = 10000.0
N_EXPERTS = 256
TOP_K = 8
N_GROUPS = 8
TOPK_GROUPS = 4
EXPERT_DIM = 256
ROUTED_SCALE = 2.5
LN_EPS = 1e-5
DEEPNORM_ALPHA = (2 * DEPTH) ** 0.25

IN_SEGMENTS = (
    ('moba_q', BRANCH_WIDTH), ('moba_k', BRANCH_WIDTH), ('moba_v', BRANCH_WIDTH),
    ('diff_q', BRANCH_WIDTH), ('diff_k', BRANCH_WIDTH), ('diff_v', BRANCH_WIDTH),
    ('fox_q', BRANCH_WIDTH), ('fox_k', BRANCH_WIDTH), ('fox_v', BRANCH_WIDTH), ('fox_f', N_HEADS),
    ('dsa_q', BRANCH_WIDTH), ('dsa_k', BRANCH_WIDTH), ('dsa_v', BRANCH_WIDTH),
    ('idx_q', N_IDX_HEADS * IDX_DIM), ('idx_k', IDX_DIM), ('idx_w', N_IDX_HEADS),
    ('gates', N_BRANCHES * D_MODEL),
)

LANES = 128
SUBLANES = 8
SEG = BRANCH_WIDTH
NEG = -1e30
INT_MIN = -2 ** 31
VMEM_LIMIT = 48 * 1024 * 1024

LOG2E = math.log2(math.e)
_QSCALE = HEAD_DIM ** -0.5 * LOG2E
PROJ_SEGS = (
    ('moba_q', 64, _QSCALE), ('moba_k', 64, 1.0),
    ('diff_q', 32, DIFF_DIM ** -0.5 * LOG2E), ('diff_k', 32, 1.0),
    ('fox_q', 0, _QSCALE), ('fox_k', 0, 1.0),
    ('dsa_q', 64, _QSCALE), ('dsa_k', 64, 1.0),
    ('idx_q', 64, IDX_DIM ** -0.5), ('idx_k4', 64, 1.0),
)
SEG_ID = {name: i for i, (name, _, _) in enumerate(PROJ_SEGS)}
N_SEG = len(PROJ_SEGS)
V_SEGS = ('moba_v', 'diff_v', 'fox_v', 'dsa_v')
V_ID = {name: i for i, name in enumerate(V_SEGS)}
MISC_LOGF = 0
MISC_IDXW = 4


def _params(*sem):
    return pltpu.CompilerParams(dimension_semantics=sem, vmem_limit_bytes=VMEM_LIMIT)


def _iota(shape, dim):
    return lax.broadcasted_iota(jnp.int32, shape, dim)


def _dot_nt(a, b):
    return lax.dot_general(a, b, (((1,), (1,)), ((), ())), preferred_element_type=F32)


def _sigmoid(z):
    return 1.0 / (1.0 + jnp.exp(-z))


def _layer_norm(y, g, b):
    mu = jnp.mean(y, axis=-1, keepdims=True)
    yc = y - mu
    var = jnp.mean(yc * yc, axis=-1, keepdims=True)
    return yc * lax.rsqrt(var + LN_EPS) * g + b


def _swap_halves(a, half):
    w = a.shape[-1]
    first = (_iota(a.shape, 1) & (2 * half - 1)) < half
    return jnp.where(first, pltpu.roll(a, w - half, 1), pltpu.roll(a, half, 1))


def _in_proj_kernel(x_ref, w_ref, wvt_ref, c64_ref, s64_ref, c32_ref, s32_ref, bf_ref,
                    qk_ref, vt_ref, misc_ref):
    xb = x_ref[...].astype(BF)
    for s, (_, rot, scale) in enumerate(PROJ_SEGS):
        acc = jnp.dot(xb, w_ref[:, s * SEG:(s + 1) * SEG], preferred_element_type=F32)
        if rot == 64:
            acc = acc * c64_ref[...] + _swap_halves(acc, 32) * s64_ref[...]
        elif rot == 32:
            acc = acc * c32_ref[...] + _swap_halves(acc, 16) * s32_ref[...]
        if scale != 1.0:
            acc = acc * scale
        qk_ref[:, s * SEG:(s + 1) * SEG] = acc.astype(BF)
    vt_ref[...] = _dot_nt(wvt_ref[...], xb).astype(BF)
    m = jnp.dot(xb, w_ref[:, N_SEG * SEG:N_SEG * SEG + LANES], preferred_element_type=F32)
    z = m + bf_ref[...]
    logf = jnp.minimum(z, 0.0) - jnp.log1p(jnp.exp(-jnp.abs(z)))
    lane = _iota(m.shape, 1)
    misc_ref[...] = jnp.where(lane < MISC_IDXW, logf, m)


def _rope_tables(T):
    pos = jnp.arange(T).astype(F32)

    def tab(group, reps):
        half = group // 2
        inv_freq = ROPE_THETA ** (-jnp.arange(half, dtype=F32) / half)
        ang = pos[:, None] * inv_freq[None, :]
        cos, sin = jnp.cos(ang), jnp.sin(ang)
        return (jnp.tile(jnp.concatenate([cos, cos], -1), (1, reps)),
                jnp.tile(jnp.concatenate([-sin, sin], -1), (1, reps)))

    return tab(64, SEG // 64) + tab(32, SEG // 32)


def _in_proj_weights(w_in, b_forget):
    parts, off = {}, 0
    for name, width in IN_SEGMENTS:
        parts[name] = w_in[:, off:off + width]
        off += width
    parts['idx_k4'] = jnp.tile(parts['idx_k'], (1, N_IDX_HEADS))
    misc = jnp.concatenate([parts['fox_f'], parts['idx_w'],
                            jnp.zeros((D_MODEL, LANES - 2 * N_HEADS), w_in.dtype)], axis=1)
    w = jnp.concatenate([parts[name] for name, _, _ in PROJ_SEGS] + [misc], axis=1).astype(BF)
    wvt = jnp.concatenate([parts[name] for name in V_SEGS], axis=1).T.astype(BF)
    bf = jnp.zeros((1, LANES), F32).at[0, MISC_LOGF:MISC_LOGF + N_HEADS].set(b_forget.astype(F32))
    return w, wvt, parts['gates'].astype(BF), bf


def _in_proj(x2, w, wvt, bf, tables, T, tm):
    N = x2.shape[0]
    nt = T // tm
    tab_spec = pl.BlockSpec((tm, SEG), lambda i: (i % nt, 0))
    fixed = lambda i: (0, 0)
    return pl.pallas_call(
        _in_proj_kernel,
        grid=(N // tm,),
        in_specs=[pl.BlockSpec((tm, D_MODEL), lambda i: (i, 0)),
                  pl.BlockSpec(w.shape, fixed), pl.BlockSpec(wvt.shape, fixed),
                  tab_spec, tab_spec, tab_spec, tab_spec,
                  pl.BlockSpec((1, LANES), fixed)],
        out_specs=[pl.BlockSpec((tm, N_SEG * SEG), lambda i: (i, 0)),
                   pl.BlockSpec((len(V_SEGS) * SEG, tm), lambda i: (0, i)),
                   pl.BlockSpec((tm, LANES), lambda i: (i, 0))],
        out_shape=[jax.ShapeDtypeStruct((N, N_SEG * SEG), BF),
                   jax.ShapeDtypeStruct((len(V_SEGS) * SEG, N), BF),
                   jax.ShapeDtypeStruct((N, LANES), F32)],
        compiler_params=_params("parallel"),
        name="in_proj",
    )(x2, w, wvt, *tables, bf)


def _group_mask(shape, shift, g):
    return (_iota(shape, 1) >> shift) == g


def _stack_groups(q, shift, n_groups):
    zero = jnp.zeros_like(q)
    return jnp.concatenate([jnp.where(_group_mask(q.shape, shift, g), q, zero) for g in range(n_groups)],
                           axis=0)


def _cols(a, g, tq):
    return a[:, g * tq:(g + 1) * tq]


def _head_rows(a, h):
    return a[h * HEAD_DIM:(h + 1) * HEAD_DIM]


ONES_ROWS = 16


def _flash_init(n_groups, tq, n_sets):
    return (jnp.full((1, n_groups * tq), NEG, F32), jnp.zeros((1, n_groups * tq), F32),
            tuple(jnp.zeros((SEG, tq), F32) for _ in range(n_sets)))


def _flash_update(s, carry, vt, tq, head_sets):
    m, l, accs = carry
    m_new = jnp.maximum(m, jnp.max(s, axis=0, keepdims=True))
    alpha = jnp.exp2(m - m_new)
    pb = jnp.exp2(s - m_new).astype(BF)
    ones = jnp.ones((ONES_ROWS, vt.shape[1]), BF)
    new_accs, p_sum = [], {}
    for acc, groups in zip(accs, head_sets):
        parts = []
        for h, g in enumerate(groups):
            r = jnp.dot(jnp.concatenate([_head_rows(vt, h), ones], axis=0), _cols(pb, g, tq),
                        preferred_element_type=F32)
            parts.append(_cols(alpha, g, tq) * _head_rows(acc, h) + r[0:HEAD_DIM])
            p_sum[g] = r[HEAD_DIM:HEAD_DIM + 1]
        new_accs.append(jnp.concatenate(parts, axis=0))
    l = alpha * l + jnp.concatenate([p_sum[g] for g in sorted(p_sum)], axis=1)
    return m_new, l, tuple(new_accs)


def _causal(n_groups, tq):
    shape = (tq, n_groups * tq)
    return _iota(shape, 0) <= (_iota(shape, 1) & (tq - 1))


def _normalised(acc, l, groups, tq):
    return jnp.concatenate([_head_rows(acc, h) * (1.0 / _cols(l, g, tq)) for h, g in enumerate(groups)],
                           axis=0)


def _tile_lanes(a, n):
    return jnp.concatenate([a] * n, axis=1) if n > 1 else a


HEADS = tuple(range(N_HEADS))


def _flash_attend(i, tq, n_groups, head_sets, qs, k_ref, vt_ref, s_ref, post, last):
    def raw(k0):
        return _dot_nt(k_ref[pl.ds(k0, tq), :], qs)

    if s_ref is not None:
        s_ref[0] = raw(0)

    def body(c, carry):
        k0 = pl.multiple_of(c * tq, tq)
        if s_ref is None:
            s = post(raw(k0), c, k0)
        else:
            s = post(s_ref[c & 1], c, k0)
            s_ref[(c + 1) & 1] = raw(pl.multiple_of(k0 + tq, tq))
        return _flash_update(s, carry, vt_ref[:, pl.ds(k0, tq)], tq, head_sets)

    carry = lax.fori_loop(0, i, body, _flash_init(n_groups, tq, len(head_sets)))
    k0 = pl.multiple_of(i * tq, tq)
    s = last(raw(k0) if s_ref is None else s_ref[i & 1], k0)
    _, l, accs = _flash_update(s, carry, vt_ref[:, pl.ds(k0, tq)], tq, head_sets)
    return l, accs


def _logit_scratch(n_groups, tq):
    return pltpu.VMEM((2, tq, n_groups * tq), F32)


def _attn_specs(T, tq, q_seg, k_seg, v_name):
    nq = T // tq
    v_id = V_ID[v_name]
    return [pl.BlockSpec((tq, SEG), lambda b, i: (b * nq + i, q_seg)),
            pl.BlockSpec((T, SEG), lambda b, i: (b, k_seg)),
            pl.BlockSpec((SEG, T), lambda b, i: (v_id, b))]


def _attn_out(N, T, tq):
    nq = T // tq
    return (pl.BlockSpec((tq, SEG), lambda b, i: (b * nq + i, 0)),
            jax.ShapeDtypeStruct((N, SEG), BF))


def _moba_kernel(q_ref, k_ref, vt_ref, o_ref, km_ref, sel_ref, s_ref, *, tq, nb, n_sel):
    i = pl.program_id(1)

    @pl.when(i == 0)
    def _():
        km_ref[...] = jnp.zeros_like(km_ref)
        for n in range(nb):
            blk = k_ref[n * tq:(n + 1) * tq, :].astype(F32)
            km_ref[n:n + 1, :] = jnp.mean(blk, axis=0, keepdims=True)

    qs = _stack_groups(q_ref[...], 6, N_HEADS)
    km = km_ref[...]
    km_hi = km.astype(BF)
    km_lo = (km - km_hi.astype(F32)).astype(BF)
    nrow = sel_ref.shape[0]
    g = (_dot_nt(km_hi, qs) + _dot_nt(km_lo, qs))[0:nrow]
    blk_id = _iota(g.shape, 0)
    g = jnp.where(blk_id < i, g, -jnp.inf)
    sel = jnp.zeros(g.shape, F32)
    for _ in range(n_sel):
        gmax = jnp.max(g, axis=0, keepdims=True)
        first = jnp.min(jnp.where(g == gmax, blk_id, nrow), axis=0, keepdims=True)
        hit = blk_id == first
        sel = jnp.where(hit & (gmax > -jnp.inf), 1.0, sel)
        g = jnp.where(hit, -jnp.inf, g)
    sel_ref[...] = sel

    l, (acc,) = _flash_attend(
        i, tq, N_HEADS, (HEADS,), qs, k_ref, vt_ref, s_ref,
        post=lambda s, c, k0: jnp.where(sel_ref[pl.ds(c, 1), :] > 0.5, s, NEG),
        last=lambda s, k0: jnp.where(_causal(N_HEADS, tq), s, NEG))
    o_ref[...] = _normalised(acc, l, HEADS, tq).T.astype(BF)


def _moba(qk, vt, T, tq):
    N = qk.shape[0]
    nb = T // MOBA_BLOCK
    assert T % MOBA_BLOCK == 0 and tq == MOBA_BLOCK and nb <= LANES
    sel_rows = -(-nb // SUBLANES) * SUBLANES
    out_spec, out_shape = _attn_out(N, T, tq)
    return pl.pallas_call(
        functools.partial(_moba_kernel, tq=tq, nb=nb, n_sel=min(MOBA_TOPK, nb - 1)),
        grid=(N // T, T // tq),
        in_specs=_attn_specs(T, tq, SEG_ID['moba_q'], SEG_ID['moba_k'], 'moba_v'),
        out_specs=out_spec, out_shape=out_shape,
        scratch_shapes=[pltpu.VMEM((LANES, SEG), F32), pltpu.VMEM((sel_rows, N_HEADS * tq), F32),
                        _logit_scratch(N_HEADS, tq)],
        compiler_params=_params("parallel", "arbitrary"),
        name="moba_attn",
    )(qk, qk, vt)


DIFF_SETS = (tuple(2 * h for h in HEADS), tuple(2 * h + 1 for h in HEADS))


def _diff_kernel(lam_ref, q_ref, k_ref, vt_ref, g_ref, o_ref, *, tq, out_scale):
    i = pl.program_id(1)
    n_groups = 2 * N_HEADS
    qs = _stack_groups(q_ref[...], 5, n_groups)
    l, (a1, a2) = _flash_attend(
        i, tq, n_groups, DIFF_SETS, qs, k_ref, vt_ref, None,
        post=lambda s, c, k0: s,
        last=lambda s, k0: jnp.where(_causal(n_groups, tq), s, NEG))
    out = _normalised(a1, l, DIFF_SETS[0], tq) - lam_ref[0] * _normalised(a2, l, DIFF_SETS[1], tq)
    normed = []
    for h in HEADS:
        oh = _head_rows(out, h)
        ms = jnp.mean(oh * oh, axis=0, keepdims=True)
        normed.append(oh * lax.rsqrt(ms + LN_EPS))
    o_ref[...] = (jnp.concatenate(normed, axis=0).T * g_ref[...] * out_scale).astype(BF)


def _diff(qk, vt, lam, subln_g, lambda_init, T, tq):
    N = qk.shape[0]
    out_spec, out_shape = _attn_out(N, T, tq)
    g = jnp.tile(subln_g.astype(F32), N_HEADS)[None, :]
    return pl.pallas_call(
        functools.partial(_diff_kernel, tq=tq, out_scale=1.0 - lambda_init),
        grid=(N // T, T // tq),
        in_specs=[pl.BlockSpec(memory_space=pltpu.SMEM)]
        + _attn_specs(T, tq, SEG_ID['diff_q'], SEG_ID['diff_k'], 'diff_v')
        + [pl.BlockSpec((1, SEG), lambda b, i: (0, 0))],
        out_specs=out_spec, out_shape=out_shape,
        compiler_params=_params("parallel", "arbitrary"),
        name="diff_attn",
    )(lam.reshape(1).astype(F32), qk, qk, vt, g)


def _fox_kernel(q_ref, k_ref, vt_ref, misc_ref, o_ref, c_ref, s_ref, *, tq):
    i = pl.program_id(1)

    @pl.when(i == 0)
    def _():
        tri = jnp.where(_iota((tq, tq), 1) <= _iota((tq, tq), 0), 1.0, 0.0).astype(BF)
        carry = [jnp.zeros((1, LANES), F32) for _ in HEADS]
        for n in range(c_ref.shape[1] // tq):
            blk = misc_ref[n * tq:(n + 1) * tq, :] * LOG2E
            for h in HEADS:
                col = jnp.broadcast_to(blk[:, MISC_LOGF + h:MISC_LOGF + h + 1], blk.shape)
                hi = col.astype(BF)
                rest = col - hi.astype(F32)
                mid = rest.astype(BF)
                lo = (rest - mid.astype(F32)).astype(BF)
                cs = carry[h] + (jnp.dot(tri, hi, preferred_element_type=F32)
                                 + jnp.dot(tri, mid, preferred_element_type=F32)
                                 + jnp.dot(tri, lo, preferred_element_type=F32))
                c_ref[h, n * tq:(n + 1) * tq, :] = cs
                carry[h] = cs[tq - 1:tq, :]

    qs = _stack_groups(q_ref[...], 6, N_HEADS)

    def decayed(s, k0):
        return jnp.concatenate(
            [_cols(s, h, tq) - _tile_lanes(c_ref[h, pl.ds(k0, tq), :], tq // LANES) for h in HEADS], axis=1)

    l, (acc,) = _flash_attend(
        i, tq, N_HEADS, (HEADS,), qs, k_ref, vt_ref, s_ref,
        post=lambda s, c, k0: decayed(s, k0),
        last=lambda s, k0: jnp.where(_causal(N_HEADS, tq), decayed(s, k0), NEG))
    o_ref[...] = _normalised(acc, l, HEADS, tq).T.astype(BF)


def _fox(qk, vt, misc, T, tq):
    N = qk.shape[0]
    out_spec, out_shape = _attn_out(N, T, tq)
    return pl.pallas_call(
        functools.partial(_fox_kernel, tq=tq),
        grid=(N // T, T // tq),
        in_specs=_attn_specs(T, tq, SEG_ID['fox_q'], SEG_ID['fox_k'], 'fox_v')
        + [pl.BlockSpec((T, LANES), lambda b, i: (b, 0))],
        out_specs=out_spec, out_shape=out_shape,
        scratch_shapes=[pltpu.VMEM((N_HEADS, T, LANES), F32), _logit_scratch(N_HEADS, tq)],
        compiler_params=_params("parallel", "arbitrary"),
        name="fox_attn",
    )(qk, qk, vt, misc)


COUNT_ROWS = 64
F32_TINY = 2.0 ** -126

def _dsa_kernel(q_ref, k_ref, vt_ref, qi_ref, ki_ref, w_ref, o_ref, key_ref, top_ref, bias_ref, s_ref,
                *, tq, n_keep):
    i = pl.program_id(1)
    n_ch = i + 1
    qpos = i * tq + _iota((1, tq), 1)

    qis = _stack_groups(qi_ref[...], 6, N_IDX_HEADS)
    wt = w_ref[...].T * (N_IDX_HEADS ** -0.5)
    w_rows = [wt[MISC_IDXW + h:MISC_IDXW + h + 1] for h in range(N_IDX_HEADS)]

    def score_body(c, carry):
        k0 = pl.multiple_of(c * tq, tq)
        d = jnp.maximum(_dot_nt(ki_ref[pl.ds(k0, tq), :], qis), 0.0)
        score = w_rows[0] * _cols(d, 0, tq)
        for h in range(1, N_IDX_HEADS):
            score = score + w_rows[h] * _cols(d, h, tq)
        sc = jnp.where((k0 + _iota((tq, tq), 0)) <= qpos, score, -jnp.inf)
        sc = jnp.where(jnp.abs(sc) < F32_TINY, 0.0, sc)
        bits = pltpu.bitcast(sc, jnp.int32)
        key_ref[pl.ds(k0, tq), :] = jnp.where(bits < 0, bits ^ jnp.int32(0x7FFFFFFF), bits)
        top_ref[pl.ds(k0, tq), :] = pltpu.bitcast(bits & jnp.int32(-65536), F32).astype(BF)
        return carry

    lax.fori_loop(0, n_ch, score_body, 0)

    @pl.when(n_ch % 2 == 1)
    def _():
        pad = pl.ds(pl.multiple_of(n_ch * tq, tq), tq)
        key_ref[pad, :] = jnp.full((tq, tq), INT_MIN, jnp.int32)
        top_ref[pad, :] = jnp.full((tq, tq), jnp.nan, BF)

    def count(src_ref, pred, dtype):
        one, zero = jnp.ones((), dtype), jnp.zeros((), dtype)

        def body(c, acc):
            k0 = pl.multiple_of(c * 2 * tq, 2 * tq)
            hit = jnp.where(pred(src_ref[pl.ds(k0, 2 * tq), :]), one, zero)
            parts = [hit[j * COUNT_ROWS:(j + 1) * COUNT_ROWS] for j in range(2 * tq // COUNT_ROWS)]
            while len(parts) > 1:
                parts = [a + b for a, b in zip(parts[0::2], parts[1::2])]
            return acc + parts[0]
        acc = lax.fori_loop(0, (n_ch + 1) // 2, body, jnp.zeros((COUNT_ROWS, tq), dtype))
        return jnp.sum(acc.astype(F32), axis=0, keepdims=True)

    def top_body(it, thr):
        cand = thr + lax.shift_left(jnp.int32(1), 31 - it)
        fbits = jnp.where(cand < 0, cand ^ jnp.int32(0x7FFFFFFF), cand) & jnp.int32(-65536)
        subnormal = ((fbits & jnp.int32(0x7F800000)) == 0) & ((fbits & jnp.int32(0x007F0000)) != 0)
        fbits = jnp.where(subnormal, jnp.where(fbits < 0, 0, jnp.int32(0x00800000)), fbits)
        cand_f = pltpu.bitcast(fbits, F32).astype(BF)
        return jnp.where(count(top_ref, lambda tt: tt >= cand_f, BF) >= n_keep, cand, thr)

    def low_body(it, thr):
        cand = thr + lax.shift_left(jnp.int32(1), 31 - it)
        return jnp.where(count(key_ref, lambda kk: kk >= cand, F32) >= n_keep, cand, thr)

    thr = lax.fori_loop(0, 16, top_body, jnp.full((1, tq), INT_MIN, jnp.int32))
    thr = lax.fori_loop(16, 32, low_body, thr)

    need = n_keep - count(key_ref, lambda kk: kk > thr, F32)
    lower = jnp.where(_iota((tq, tq), 1) <= _iota((tq, tq), 0), 1.0, 0.0).astype(BF)

    def tie_body(c, seen):
        k0 = pl.multiple_of(c * tq, tq)
        kk = key_ref[pl.ds(k0, tq), :]
        eq = jnp.where(kk == thr, 1.0, 0.0)
        rank = jnp.dot(lower, eq.astype(BF), preferred_element_type=F32) + seen
        keep = (kk > thr) | ((kk == thr) & (rank <= need))
        causal = (k0 + _iota((tq, tq), 0)) <= qpos
        bias_ref[pl.ds(k0, tq), :] = jnp.where(keep & causal, 0.0, NEG)
        return seen + jnp.sum(eq, axis=0, keepdims=True)

    lax.fori_loop(0, n_ch, tie_body, jnp.zeros((1, tq), F32))

    qs = _stack_groups(q_ref[...], 6, N_HEADS)
    biased = lambda s, k0: s + _tile_lanes(bias_ref[pl.ds(k0, tq), :], N_HEADS)
    l, (acc,) = _flash_attend(i, tq, N_HEADS, (HEADS,), qs, k_ref, vt_ref, s_ref,
                              post=lambda s, c, k0: biased(s, k0), last=biased)
    o_ref[...] = _normalised(acc, l, HEADS, tq).T.astype(BF)


def _dsa(qk, vt, misc, T, tq):
    N = qk.shape[0]
    nq = T // tq
    out_spec, out_shape = _attn_out(N, T, tq)
    return pl.pallas_call(
        functools.partial(_dsa_kernel, tq=tq, n_keep=min(DSA_TOPK_MAX, T // 4)),
        grid=(N // T, nq),
        in_specs=_attn_specs(T, tq, SEG_ID['dsa_q'], SEG_ID['dsa_k'], 'dsa_v')
        + [pl.BlockSpec((tq, SEG), lambda b, i: (b * nq + i, SEG_ID['idx_q'])),
           pl.BlockSpec((T, SEG), lambda b, i: (b, SEG_ID['idx_k4'])),
           pl.BlockSpec((tq, LANES), lambda b, i: (b * nq + i, 0))],
        out_specs=out_spec, out_shape=out_shape,
        scratch_shapes=[pltpu.VMEM(((nq + nq % 2) * tq, tq), jnp.int32),
                        pltpu.VMEM(((nq + nq % 2) * tq, tq), BF), pltpu.VMEM((T, tq), F32),
                        _logit_scratch(N_HEADS, tq)],
        compiler_params=_params("parallel", "arbitrary"),
        name="dsa_attn",
    )(qk, qk, vt, qk, qk, misc)


def _mix_out_kernel(x_ref, oa_ref, ob_ref, oc_ref, od_ref, wg_ref, wb_ref, wo_ref, g_ref, b_ref,
                    y_ref, yb_ref):
    x = x_ref[...]
    xb = x.astype(BF)
    merged = jnp.zeros(x.shape, F32)
    for n, o_ref in enumerate((oa_ref, ob_ref, oc_ref, od_ref)):
        gate = _sigmoid(jnp.dot(xb, wg_ref[:, n * D_MODEL:(n + 1) * D_MODEL], preferred_element_type=F32))
        merged = merged + gate * jnp.dot(o_ref[...], wb_ref[n], preferred_element_type=F32)
    h = jnp.dot(merged.astype(BF), wo_ref[...], preferred_element_type=F32)
    y = _layer_norm(DEEPNORM_ALPHA * x + h, g_ref[...], b_ref[...])
    y_ref[...] = y
    yb_ref[...] = y.astype(BF)


def _mix_out(x2, branches, w_gates, w_branch, w_out, ln_g, ln_b, tm):
    N = x2.shape[0]
    row = lambda i: (i, 0)
    fixed2 = lambda i: (0, 0)
    return pl.pallas_call(
        _mix_out_kernel,
        grid=(N // tm,),
        in_specs=[pl.BlockSpec((tm, D_MODEL), row)] + [pl.BlockSpec((tm, SEG), row)] * N_BRANCHES
        + [pl.BlockSpec(w_gates.shape, fixed2),
           pl.BlockSpec(w_branch.shape, lambda i: (0, 0, 0)),
           pl.BlockSpec(w_out.shape, fixed2),
           pl.BlockSpec((1, D_MODEL), fixed2), pl.BlockSpec((1, D_MODEL), fixed2)],
        out_specs=[pl.BlockSpec((tm, D_MODEL), row), pl.BlockSpec((tm, D_MODEL), row)],
        out_shape=[jax.ShapeDtypeStruct((N, D_MODEL), F32), jax.ShapeDtypeStruct((N, D_MODEL), BF)],
        compiler_params=_params("parallel"),
        name="mix_out",
    )(x2, *branches, w_gates, w_branch.astype(BF), w_out.astype(BF),
      ln_g.astype(F32)[None, :], ln_b.astype(F32)[None, :])


def _split_bf16(a):
    hi = a.astype(BF)
    return hi, (a - hi.astype(F32)).astype(BF)


def _router_kernel(x_ref, whi_ref, wlo_ref, rb_ref, idx_ref, wsel_ref):
    xhi, xlo = _split_bf16(x_ref[...])
    whi = whi_ref[...]
    logits = _dot_nt(whi, xhi) + _dot_nt(whi, xlo) + _dot_nt(wlo_ref[...], xhi)
    tm = logits.shape[1]
    scores = _sigmoid(logits)
    biased = scores + _tile_lanes(rb_ref[...], tm // LANES)
    per_group = N_EXPERTS // N_GROUPS
    gs = []
    for g in range(N_GROUPS):
        bg = biased[g * per_group:(g + 1) * per_group]
        row = _iota(bg.shape, 0)
        m1 = jnp.max(bg, axis=0, keepdims=True)
        i1 = jnp.min(jnp.where(bg == m1, row, per_group), axis=0, keepdims=True)
        m2 = jnp.max(jnp.where(row == i1, -jnp.inf, bg), axis=0, keepdims=True)
        gs.append(m1 + m2)
    kept = []
    for g in range(N_GROUPS):
        rank = jnp.zeros((1, tm), F32)
        for o in range(N_GROUPS):
            if o != g:
                beats = (gs[o] >= gs[g]) if o < g else (gs[o] > gs[g])
                rank = rank + jnp.where(beats, 1.0, 0.0)
        kept.append(jnp.where(rank < TOPK_GROUPS, biased[g * per_group:(g + 1) * per_group], -jnp.inf))
    masked = jnp.concatenate(kept, axis=0)
    eid = _iota(masked.shape, 0)
    picks, weights = [], []
    for _ in range(TOP_K):
        mx = jnp.max(masked, axis=0, keepdims=True)
        pick = jnp.min(jnp.where(masked == mx, eid, N_EXPERTS), axis=0, keepdims=True)
        hit = eid == pick
        weights.append(jnp.sum(jnp.where(hit, scores, 0.0), axis=0, keepdims=True))
        masked = jnp.where(hit, -jnp.inf, masked)
        picks.append(pick)
    wsum = weights[0]
    for wk in weights[1:]:
        wsum = wsum + wk
    idx_ref[...] = jnp.concatenate(picks, axis=0)
    wsel_ref[...] = jnp.concatenate(weights, axis=0) / wsum * ROUTED_SCALE


def _router(x1, w_router, router_bias, tm):
    N = x1.shape[0]
    whi, wlo = _split_bf16(w_router.astype(F32).T)
    rb = jnp.broadcast_to(router_bias.astype(F32)[:, None], (N_EXPERTS, LANES))
    fixed = lambda i: (0, 0)
    col = lambda i: (0, i)
    return pl.pallas_call(
        _router_kernel,
        grid=(N // tm,),
        in_specs=[pl.BlockSpec((tm, D_MODEL), lambda i: (i, 0)), pl.BlockSpec(whi.shape, fixed),
                  pl.BlockSpec(wlo.shape, fixed), pl.BlockSpec(rb.shape, fixed)],
        out_specs=[pl.BlockSpec((TOP_K, tm), col), pl.BlockSpec((TOP_K, tm), col)],
        out_shape=[jax.ShapeDtypeStruct((TOP_K, N), jnp.int32), jax.ShapeDtypeStruct((TOP_K, N), F32)],
        compiler_params=_params("parallel"),
        name="router",
    )(x1, whi, wlo, rb)


def _expert_kernel(blk_ref, exp_ref, lo_ref, hi_ref, first_ref, newe_ref,
                   xs_ref, wg_ref, wu_ref, wd_ref, y_ref, wgb_ref, wub_ref, wdb_ref, *, bm):
    t = pl.program_id(0)
    lo, hi = lo_ref[t], hi_ref[t]

    @pl.when(newe_ref[t] == 1)
    def _():
        wgb_ref[...] = wg_ref[...].astype(BF)
        wub_ref[...] = wu_ref[...].astype(BF)
        wdb_ref[...] = wd_ref[...].astype(BF)

    @pl.when(first_ref[t] == 1)
    def _():
        y_ref[...] = jnp.zeros_like(y_ref)

    sub = bm // 2
    base = blk_ref[t] * bm

    def work(j0, n_sub):
        rows = slice(j0 * sub, (j0 + n_sub) * sub)
        xs = xs_ref[rows, :]
        g = jnp.dot(xs, wgb_ref[...], preferred_element_type=F32)
        u = jnp.dot(xs, wub_ref[...], preferred_element_type=F32)
        h = (g * _sigmoid(g) * u).astype(BF)
        y = jnp.dot(h, wdb_ref[...], preferred_element_type=F32)
        row = base + j0 * sub + _iota((n_sub * sub, 1), 0)
        mine = (row >= lo) & (row < hi)
        y_ref[rows, :] = jnp.where(mine, y.astype(y_ref.dtype), y_ref[rows, :])

    need = [(hi > lo) & (lo < base + (j + 1) * sub) & (hi > base + j * sub) for j in range(2)]
    pl.when(need[0] & need[1])(lambda: work(0, 2))
    pl.when(need[0] & jnp.logical_not(need[1]))(lambda: work(0, 1))
    pl.when(jnp.logical_not(need[0]) & need[1])(lambda: work(1, 1))


def _expert_kernel_onto(blk_ref, exp_ref, lo_ref, hi_ref, first_ref, newe_ref,
                        xs_ref, wg_ref, wu_ref, wd_ref, prev_ref, y_ref, *scratch, bm):
    del prev_ref
    _expert_kernel(blk_ref, exp_ref, lo_ref, hi_ref, first_ref, newe_ref,
                   xs_ref, wg_ref, wu_ref, wd_ref, y_ref, *scratch, bm=bm)


def _experts(xs_parts, plans, layer, w_gate, w_up, w_down, bm):
    A = sum(xs.shape[0] for xs in xs_parts)
    wspec = lambda shape: pl.BlockSpec((None, None) + shape, lambda t, blk, exp, *_: (layer, exp[t], 0, 0))
    ys, row0 = None, 0
    for xs, plan in zip(xs_parts, plans):
        blk0 = row0 // bm
        row0 += xs.shape[0]
        in_specs = [pl.BlockSpec((bm, D_MODEL), lambda t, blk, *_, blk0=blk0: (blk[t] - blk0, 0)),
                    wspec((D_MODEL, EXPERT_DIM)), wspec((D_MODEL, EXPERT_DIM)), wspec((EXPERT_DIM, D_MODEL))]
        args = (*plan, xs, w_gate, w_up, w_down)
        if ys is not None:
            in_specs.append(pl.BlockSpec(memory_space=pl.ANY))
            args += (ys,)
        ys = pl.pallas_call(
            functools.partial(_expert_kernel if ys is None else _expert_kernel_onto, bm=bm),
            grid_spec=pltpu.PrefetchScalarGridSpec(
                num_scalar_prefetch=len(plan),
                grid=(plan[0].shape[0],),
                in_specs=in_specs,
                out_specs=pl.BlockSpec((bm, D_MODEL), lambda t, blk, *_: (blk[t], 0)),
                scratch_shapes=[pltpu.VMEM((D_MODEL, EXPERT_DIM), BF), pltpu.VMEM((D_MODEL, EXPERT_DIM), BF),
                                pltpu.VMEM((EXPERT_DIM, D_MODEL), BF)],
            ),
            out_shape=jax.ShapeDtypeStruct((A, D_MODEL), BF),
            input_output_aliases={} if ys is None else {len(args) - 1: 0},
            compiler_params=_params("arbitrary"),
            name="experts",
        )(*args)
    return ys


def _part_bounds(A, bm, shares):
    total, acc, bounds = sum(shares), 0, [0]
    for share in shares:
        acc += share
        bounds.append(A * acc // total)
    assert all(b % bm == 0 for b in bounds) and len(set(bounds)) == len(bounds)
    return bounds


def _dispatch_plan(eidx_t, bm, bounds):
    N = eidx_t.shape[1]
    A, E = N * TOP_K, N_EXPERTS
    ids = jnp.arange(A, dtype=jnp.int32)
    id_bits = (A - 1).bit_length()
    assert id_bits + (E - 1).bit_length() <= 31
    packed = jnp.sort((eidx_t.reshape(A) << id_bits) | ids)
    se, sid = packed >> id_bits, packed & ((1 << id_bits) - 1)
    _, pos = lax.sort((sid, ids), num_keys=1)
    start = jnp.searchsorted(se, jnp.arange(E, dtype=jnp.int32), side='left').astype(jnp.int32)
    changed = lambda a: jnp.concatenate([jnp.ones((1,), jnp.int32), (a[1:] != a[:-1]).astype(jnp.int32)])
    plans = []
    for a0, a1 in zip(bounds[:-1], bounds[1:]):
        lo = jnp.sort(jnp.concatenate([jnp.arange(a0, a1, bm, dtype=jnp.int32), jnp.clip(start, a0, a1)]))
        hi = jnp.concatenate([lo[1:], jnp.full((1,), a1, jnp.int32)])
        blk = jnp.minimum(lo // bm, a1 // bm - 1)
        exp = se[jnp.minimum(lo, A - 1)]
        plans.append((blk, exp, lo, hi, changed(blk), changed(exp)))
    return sid % N, pos.reshape(TOP_K, N), plans


def _moe_out_kernel(x_ref, r_ref, rw_ref, wg_ref, wu_ref, wd_ref, g_ref, b_ref, y_ref):
    x = x_ref[...]
    tm = x.shape[0]
    xb = x.astype(BF)
    g = jnp.dot(xb, wg_ref[...], preferred_element_type=F32)
    u = jnp.dot(xb, wu_ref[...], preferred_element_type=F32)
    h = (g * _sigmoid(g) * u).astype(BF)
    shared = jnp.dot(h, wd_ref[...], preferred_element_type=F32)
    rw = jnp.concatenate([rw_ref[...], jnp.zeros((LANES - TOP_K, tm), F32)], axis=0).T
    routed = rw[:, 0:1] * r_ref[0].astype(F32)
    for k in range(1, TOP_K):
        routed = routed + rw[:, k:k + 1] * r_ref[k].astype(F32)
    y_ref[...] = _layer_norm(DEEPNORM_ALPHA * x + (routed + shared), g_ref[...], b_ref[...])


def _moe_out_kernel_onto(x_ref, r_ref, rw_ref, wg_ref, wu_ref, wd_ref, g_ref, b_ref, prev_ref, y_ref):
    del prev_ref
    _moe_out_kernel(x_ref, r_ref, rw_ref, wg_ref, wu_ref, wd_ref, g_ref, b_ref, y_ref)


def _moe_out(x1, routed_parts, route_w, w_sh_gate, w_sh_up, w_sh_down, ln_g, ln_b, tm):
    N = x1.shape[0]
    n_tok = routed_parts[0].shape[1]
    fixed = lambda i: (0, 0)
    weights = (w_sh_gate.astype(BF), w_sh_up.astype(BF), w_sh_down.astype(BF),
               ln_g.astype(F32)[None, :], ln_b.astype(F32)[None, :])
    y = None
    for p, routed in enumerate(routed_parts):
        t0 = p * n_tok // tm
        row = lambda i, t0=t0: (t0 + i, 0)
        in_specs = [pl.BlockSpec((tm, D_MODEL), row),
                    pl.BlockSpec((TOP_K, tm, D_MODEL), lambda i: (0, i, 0)),
                    pl.BlockSpec((TOP_K, tm), lambda i, t0=t0: (0, t0 + i)),
                    pl.BlockSpec(w_sh_gate.shape, fixed), pl.BlockSpec(w_sh_up.shape, fixed),
                    pl.BlockSpec(w_sh_down.shape, fixed),
                    pl.BlockSpec((1, D_MODEL), fixed), pl.BlockSpec((1, D_MODEL), fixed)]
        args = (x1, routed, route_w) + weights
        if y is not None:
            in_specs.append(pl.BlockSpec(memory_space=pl.ANY))
            args += (y,)
        y = pl.pallas_call(
            _moe_out_kernel if y is None else _moe_out_kernel_onto,
            grid=(n_tok // tm,),
            in_specs=in_specs,
            out_specs=pl.BlockSpec((tm, D_MODEL), row),
            out_shape=jax.ShapeDtypeStruct((N, D_MODEL), F32),
            input_output_aliases={} if y is None else {len(args) - 1: 0},
            compiler_params=_params("parallel"),
            name="moe_out",
        )(*args)
    return y


EXPERT_PART_SHARES = (1, 3)
COMBINE_PARTS = 2

def _tiles(N, T):
    return min(512, T), min(256, T), MOBA_BLOCK, min(1024, N * TOP_K)


def _mixer_layer(x2, T, w_in, b_forget, diff_lambda, diff_subln, w_branch, w_out, ln_g, ln_b, lambda_init):
    N = x2.shape[0]
    tm_proj, tm_row, tq, _ = _tiles(N, T)
    w, wvt, w_gates, bf = _in_proj_weights(w_in, b_forget)
    qk, vt, misc = _in_proj(x2, w, wvt, bf, _rope_tables(T), T, tm_proj)
    dl = diff_lambda.astype(F32)
    lam = jnp.exp(jnp.sum(dl[0] * dl[1])) - jnp.exp(jnp.sum(dl[2] * dl[3])) + lambda_init
    o_a = _moba(qk, vt, T, tq)
    o_b = _diff(qk, vt, lam, diff_subln, lambda_init, T, tq)
    o_c = _fox(qk, vt, misc, T, tq)
    o_d = _dsa(qk, vt, misc, T, tq)
    return _mix_out(x2, (o_a, o_b, o_c, o_d), w_gates, w_branch, w_out, ln_g, ln_b, tm_row)


def _moe_layer(x1, x1b, T, layer, w_router, router_bias, w_exp_gate, w_exp_up, w_exp_down,
               w_sh_gate, w_sh_up, w_sh_down, ln_g, ln_b):
    N = x1.shape[0]
    _, tm_row, _, bm = _tiles(N, T)
    eidx_t, wsel_t = _router(x1, w_router, router_bias, tm_row)
    bounds = _part_bounds(N * TOP_K, bm, EXPERT_PART_SHARES)
    row_tok, pos, plans = _dispatch_plan(eidx_t, bm, bounds)
    xs_parts = [x1b[row_tok[a0:a1]] for a0, a1 in zip(bounds[:-1], bounds[1:])]
    ys = _experts(xs_parts, plans, layer, w_exp_gate, w_exp_up, w_exp_down, bm)
    n_tok = N // COMBINE_PARTS
    routed_parts = [ys[pos[:, p * n_tok:(p + 1) * n_tok]] for p in range(COMBINE_PARTS)]
    return _moe_out(x1, routed_parts, wsel_t, w_sh_gate, w_sh_up, w_sh_down, ln_g, ln_b, tm_row)


def kernel(x, w_in, b_forget, diff_lambda, diff_subln, w_branch, w_out, ln1_g, ln1_b, w_router, router_bias,
           w_exp_gate, w_exp_up, w_exp_down, w_sh_gate, w_sh_up, w_sh_down, ln2_g, ln2_b):
    B, T, D = x.shape
    x2 = x.reshape(B * T, D)
    for l in range(DEPTH):
        lambda_init = 0.8 - 0.6 * math.exp(-0.3 * l)
        x1, x1b = _mixer_layer(x2, T, w_in[l], b_forget[l], diff_lambda[l], diff_subln[l], w_branch[l],
                               w_out[l], ln1_g[l], ln1_b[l], lambda_init)
        x2 = _moe_layer(x1, x1b, T, l, w_router[l], router_bias[l], w_exp_gate, w_exp_up, w_exp_down,
                        w_sh_gate[l], w_sh_up[l], w_sh_down[l], ln2_g[l], ln2_b[l])
    return x2.reshape(B, T, D)
```

```python
import functools
import math

import jax
import jax.numpy as jnp
from jax import lax
from jax.experimental import pallas as pl
from jax.experimental.pallas import tpu as pltpu

F32 = jnp.float32
BF = jnp.bfloat16

D_MODEL = 1024
DEPTH = 2
HEAD_DIM = 64
N_HEADS = 4
DIFF_DIM = HEAD_DIM // 2
N_IDX_HEADS = 4
IDX_DIM = 64
BRANCH_WIDTH = N_HEADS * HEAD_DIM
N_BRANCHES = 4
MOBA_BLOCK = 256
MOBA_TOPK = 3
DSA_TOPK_MAX = 256
ROPE_THETA = 10000.0
N_EXPERTS = 256
TOP_K = 8
N_GROUPS = 8
TOPK_GROUPS = 4
EXPERT_DIM = 256
ROUTED_SCALE = 2.5
LN_EPS = 1e-5
DEEPNORM_ALPHA = (2 * DEPTH) ** 0.25

IN_SEGMENTS = (
    ('moba_q', BRANCH_WIDTH), ('moba_k', BRANCH_WIDTH), ('moba_v', BRANCH_WIDTH),
    ('diff_q', BRANCH_WIDTH), ('diff_k', BRANCH_WIDTH), ('diff_v', BRANCH_WIDTH),
    ('fox_q', BRANCH_WIDTH), ('fox_k', BRANCH_WIDTH), ('fox_v', BRANCH_WIDTH), ('fox_f', N_HEADS),
    ('dsa_q', BRANCH_WIDTH), ('dsa_k', BRANCH_WIDTH), ('dsa_v', BRANCH_WIDTH),
    ('idx_q', N_IDX_HEADS * IDX_DIM), ('idx_k', IDX_DIM), ('idx_w', N_IDX_HEADS),
    ('gates', N_BRANCHES * D_MODEL),
)

LANES = 128
SUBLANES = 8
SEG = BRANCH_WIDTH
NEG = -1e30
INT_MIN = -2 ** 31
VMEM_LIMIT = 48 * 1024 * 1024

LOG2E = math.log2(math.e)
_QSCALE = HEAD_DIM ** -0.5 * LOG2E
PROJ_SEGS = (
    ('moba_q', 64, _QSCALE), ('moba_k', 64, 1.0),
    ('diff_q', 32, DIFF_DIM ** -0.5 * LOG2E), ('diff_k', 32, 1.0),
    ('fox_q', 0, _QSCALE), ('fox_k', 0, 1.0),
    ('dsa_q', 64, _QSCALE), ('dsa_k', 64, 1.0),
    ('idx_q', 64, IDX_DIM ** -0.5), ('idx_k4', 64, 1.0),
)
SEG_ID = {name: i for i, (name, _, _) in enumerate(PROJ_SEGS)}
N_SEG = len(PROJ_SEGS)
V_SEGS = ('moba_v', 'diff_v', 'fox_v', 'dsa_v')
V_ID = {name: i for i, name in enumerate(V_SEGS)}
MISC_LOGF = 0
MISC_IDXW = 4


def _params(*sem):
    return pltpu.CompilerParams(dimension_semantics=sem, vmem_limit_bytes=VMEM_LIMIT)


def _iota(shape, dim):
    return lax.broadcasted_iota(jnp.int32, shape, dim)


def _dot_nt(a, b):
    return lax.dot_general(a, b, (((1,), (1,)), ((), ())), preferred_element_type=F32)


def _sigmoid(z):
    return 1.0 / (1.0 + jnp.exp(-z))


def _layer_norm(y, g, b):
    mu = jnp.mean(y, axis=-1, keepdims=True)
    yc = y - mu
    var = jnp.mean(yc * yc, axis=-1, keepdims=True)
    return yc * lax.rsqrt(var + LN_EPS) * g + b


def _swap_halves(a, half):
    w = a.shape[-1]
    first = (_iota(a.shape, 1) & (2 * half - 1)) < half
    return jnp.where(first, pltpu.roll(a, w - half, 1), pltpu.roll(a, half, 1))


def _in_proj_kernel(x_ref, w_ref, wvt_ref, c64_ref, s64_ref, c32_ref, s32_ref, bf_ref,
                    qk_ref, vt_ref, misc_ref):
    xb = x_ref[...].astype(BF)
    for s, (_, rot, scale) in enumerate(PROJ_SEGS):
        acc = jnp.dot(xb, w_ref[:, s * SEG:(s + 1) * SEG], preferred_element_type=F32)
        if rot == 64:
            acc = acc * c64_ref[...] + _swap_halves(acc, 32) * s64_ref[...]
        elif rot == 32:
            acc = acc * c32_ref[...] + _swap_halves(acc, 16) * s32_ref[...]
        if scale != 1.0:
            acc = acc * scale
        qk_ref[:, s * SEG:(s + 1) * SEG] = acc.astype(BF)
    vt_ref[...] = _dot_nt(wvt_ref[...], xb).astype(BF)
    m = jnp.dot(xb, w_ref[:, N_SEG * SEG:N_SEG * SEG + LANES], preferred_element_type=F32)
    z = m + bf_ref[...]
    logf = jnp.minimum(z, 0.0) - jnp.log1p(jnp.exp(-jnp.abs(z)))
    lane = _iota(m.shape, 1)
    misc_ref[...] = jnp.where(lane < MISC_IDXW, logf, m)


def _rope_tables(T):
    pos = jnp.arange(T).astype(F32)

    def tab(group, reps):
        half = group // 2
        inv_freq = ROPE_THETA ** (-jnp.arange(half, dtype=F32) / half)
        ang = pos[:, None] * inv_freq[None, :]
        cos, sin = jnp.cos(ang), jnp.sin(ang)
        return (jnp.tile(jnp.concatenate([cos, cos], -1), (1, reps)),
                jnp.tile(jnp.concatenate([-sin, sin], -1), (1, reps)))

    return tab(64, SEG // 64) + tab(32, SEG // 32)


def _in_proj_weights(w_in, b_forget):
    parts, off = {}, 0
    for name, width in IN_SEGMENTS:
        parts[name] = w_in[:, off:off + width]
        off += width
    parts['idx_k4'] = jnp.tile(parts['idx_k'], (1, N_IDX_HEADS))
    misc = jnp.concatenate([parts['fox_f'], parts['idx_w'],
                            jnp.zeros((D_MODEL, LANES - 2 * N_HEADS), w_in.dtype)], axis=1)
    w = jnp.concatenate([parts[name] for name, _, _ in PROJ_SEGS] + [misc], axis=1).astype(BF)
    wvt = jnp.concatenate([parts[name] for name in V_SEGS], axis=1).T.astype(BF)
    bf = jnp.zeros((1, LANES), F32).at[0, MISC_LOGF:MISC_LOGF + N_HEADS].set(b_forget.astype(F32))
    return w, wvt, parts['gates'].astype(BF), bf


def _in_proj(x2, w, wvt, bf, tables, T, tm):
    N = x2.shape[0]
    nt = T // tm
    tab_spec = pl.BlockSpec((tm, SEG), lambda i: (i % nt, 0))
    fixed = lambda i: (0, 0)
    return pl.pallas_call(
        _in_proj_kernel,
        grid=(N // tm,),
        in_specs=[pl.BlockSpec((tm, D_MODEL), lambda i: (i, 0)),
                  pl.BlockSpec(w.shape, fixed), pl.BlockSpec(wvt.shape, fixed),
                  tab_spec, tab_spec, tab_spec, tab_spec,
                  pl.BlockSpec((1, LANES), fixed)],
        out_specs=[pl.BlockSpec((tm, N_SEG * SEG), lambda i: (i, 0)),
                   pl.BlockSpec((len(V_SEGS) * SEG, tm), lambda i: (0, i)),
                   pl.BlockSpec((tm, LANES), lambda i: (i, 0))],
        out_shape=[jax.ShapeDtypeStruct((N, N_SEG * SEG), BF),
                   jax.ShapeDtypeStruct((len(V_SEGS) * SEG, N), BF),
                   jax.ShapeDtypeStruct((N, LANES), F32)],
        compiler_params=_params("parallel"),
        name="in_proj",
    )(x2, w, wvt, *tables, bf)


def _group_mask(shape, shift, g):
    return (_iota(shape, 1) >> shift) == g


def _stack_groups(q, shift, n_groups):
    zero = jnp.zeros_like(q)
    return jnp.concatenate([jnp.where(_group_mask(q.shape, shift, g), q, zero) for g in range(n_groups)],
                           axis=0)


def _cols(a, g, tq):
    return a[:, g * tq:(g + 1) * tq]


def _head_rows(a, h):
    return a[h * HEAD_DIM:(h + 1) * HEAD_DIM]


ONES_ROWS = 16


def _flash_init(n_groups, tq, n_sets):
    return (jnp.full((1, n_groups * tq), NEG, F32), jnp.zeros((1, n_groups * tq), F32),
            tuple(jnp.zeros((SEG, tq), F32) for _ in range(n_sets)))


def _flash_update(s, carry, vt, tq, head_sets):
    m, l, accs = carry
    m_new = jnp.maximum(m, jnp.max(s, axis=0, keepdims=True))
    alpha = jnp.exp2(m - m_new)
    pb = jnp.exp2(s - m_new).astype(BF)
    ones = jnp.ones((ONES_ROWS, vt.shape[1]), BF)
    new_accs, p_sum = [], {}
    for acc, groups in zip(accs, head_sets):
        parts = []
        for h, g in enumerate(groups):
            r = jnp.dot(jnp.concatenate([_head_rows(vt, h), ones], axis=0), _cols(pb, g, tq),
                        preferred_element_type=F32)
            parts.append(_cols(alpha, g, tq) * _head_rows(acc, h) + r[0:HEAD_DIM])
            p_sum[g] = r[HEAD_DIM:HEAD_DIM + 1]
        new_accs.append(jnp.concatenate(parts, axis=0))
    l = alpha * l + jnp.concatenate([p_sum[g] for g in sorted(p_sum)], axis=1)
    return m_new, l, tuple(new_accs)


def _causal(n_groups, tq):
    shape = (tq, n_groups * tq)
    return _iota(shape, 0) <= (_iota(shape, 1) & (tq - 1))


def _normalised(acc, l, groups, tq):
    return jnp.concatenate([_head_rows(acc, h) * (1.0 / _cols(l, g, tq)) for h, g in enumerate(groups)],
                           axis=0)


def _tile_lanes(a, n):
    return jnp.concatenate([a] * n, axis=1) if n > 1 else a


HEADS = tuple(range(N_HEADS))


def _flash_attend(i, tq, n_groups, head_sets, qs, k_ref, vt_ref, s_ref, post, last):
    def raw(k0):
        return _dot_nt(k_ref[pl.ds(k0, tq), :], qs)

    if s_ref is not None:
        s_ref[0] = raw(0)

    def body(c, carry):
        k0 = pl.multiple_of(c * tq, tq)
        if s_ref is None:
            s = post(raw(k0), c, k0)
        else:
            s = post(s_ref[c & 1], c, k0)
            s_ref[(c + 1) & 1] = raw(pl.multiple_of(k0 + tq, tq))
        return _flash_update(s, carry, vt_ref[:, pl.ds(k0, tq)], tq, head_sets)

    carry = lax.fori_loop(0, i, body, _flash_init(n_groups, tq, len(head_sets)))
    k0 = pl.multiple_of(i * tq, tq)
    s = last(raw(k0) if s_ref is None else s_ref[i & 1], k0)
    _, l, accs = _flash_update(s, carry, vt_ref[:, pl.ds(k0, tq)], tq, head_sets)
    return l, accs


def _logit_scratch(n_groups, tq):
    return pltpu.VMEM((2, tq, n_groups * tq), F32)


def _attn_specs(T, tq, q_seg, k_seg, v_name):
    nq = T // tq
    v_id = V_ID[v_name]
    return [pl.BlockSpec((tq, SEG), lambda b, i: (b * nq + i, q_seg)),
            pl.BlockSpec((T, SEG), lambda b, i: (b, k_seg)),
            pl.BlockSpec((SEG, T), lambda b, i: (v_id, b))]


def _attn_out(N, T, tq):
    nq = T // tq
    return (pl.BlockSpec((tq, SEG), lambda b, i: (b * nq + i, 0)),
            jax.ShapeDtypeStruct((N, SEG), BF))


def _moba_kernel(q_ref, k_ref, vt_ref, o_ref, km_ref, sel_ref, s_ref, *, tq, nb, n_sel):
    i = pl.program_id(1)

    @pl.when(i == 0)
    def _():
        km_ref[...] = jnp.zeros_like(km_ref)
        for n in range(nb):
            blk = k_ref[n * tq:(n + 1) * tq, :].astype(F32)
            km_ref[n:n + 1, :] = jnp.mean(blk, axis=0, keepdims=True)

    qs = _stack_groups(q_ref[...], 6, N_HEADS)
    km = km_ref[...]
    km_hi = km.astype(BF)
    km_lo = (km - km_hi.astype(F32)).astype(BF)
    nrow = sel_ref.shape[0]
    g = (_dot_nt(km_hi, qs) + _dot_nt(km_lo, qs))[0:nrow]
    blk_id = _iota(g.shape, 0)
    g = jnp.where(blk_id < i, g, -jnp.inf)
    sel = jnp.zeros(g.shape, F32)
    for _ in range(n_sel):
        gmax = jnp.max(g, axis=0, keepdims=True)
        first = jnp.min(jnp.where(g == gmax, blk_id, nrow), axis=0, keepdims=True)
        hit = blk_id == first
        sel = jnp.where(hit & (gmax > -jnp.inf), 1.0, sel)
        g = jnp.where(hit, -jnp.inf, g)
    sel_ref[...] = sel

    l, (acc,) = _flash_attend(
        i, tq, N_HEADS, (HEADS,), qs, k_ref, vt_ref, s_ref,
        post=lambda s, c, k0: jnp.where(sel_ref[pl.ds(c, 1), :] > 0.5, s, NEG),
        last=lambda s, k0: jnp.where(_causal(N_HEADS, tq), s, NEG))
    o_ref[...] = _normalised(acc, l, HEADS, tq).T.astype(BF)


def _moba(qk, vt, T, tq):
    N = qk.shape[0]
    nb = T // MOBA_BLOCK
    assert T % MOBA_BLOCK == 0 and tq == MOBA_BLOCK and nb <= LANES
    sel_rows = -(-nb // SUBLANES) * SUBLANES
    out_spec, out_shape = _attn_out(N, T, tq)
    return pl.pallas_call(
        functools.partial(_moba_kernel, tq=tq, nb=nb, n_sel=min(MOBA_TOPK, nb - 1)),
        grid=(N // T, T // tq),
        in_specs=_attn_specs(T, tq, SEG_ID['moba_q'], SEG_ID['moba_k'], 'moba_v'),
        out_specs=out_spec, out_shape=out_shape,
        scratch_shapes=[pltpu.VMEM((LANES, SEG), F32), pltpu.VMEM((sel_rows, N_HEADS * tq), F32),
                        _logit_scratch(N_HEADS, tq)],
        compiler_params=_params("parallel", "arbitrary"),
        name="moba_attn",
    )(qk, qk, vt)


DIFF_SETS = (tuple(2 * h for h in HEADS), tuple(2 * h + 1 for h in HEADS))


def _diff_kernel(lam_ref, q_ref, k_ref, vt_ref, g_ref, o_ref, *, tq, out_scale):
    i = pl.program_id(1)
    n_groups = 2 * N_HEADS
    qs = _stack_groups(q_ref[...], 5, n_groups)
    l, (a1, a2) = _flash_attend(
        i, tq, n_groups, DIFF_SETS, qs, k_ref, vt_ref, None,
        post=lambda s, c, k0: s,
        last=lambda s, k0: jnp.where(_causal(n_groups, tq), s, NEG))
    out = _normalised(a1, l, DIFF_SETS[0], tq) - lam_ref[0] * _normalised(a2, l, DIFF_SETS[1], tq)
    normed = []
    for h in HEADS:
        oh = _head_rows(out, h)
        ms = jnp.mean(oh * oh, axis=0, keepdims=True)
        normed.append(oh * lax.rsqrt(ms + LN_EPS))
    o_ref[...] = (jnp.concatenate(normed, axis=0).T * g_ref[...] * out_scale).astype(BF)


def _diff(qk, vt, lam, subln_g, lambda_init, T, tq):
    N = qk.shape[0]
    out_spec, out_shape = _attn_out(N, T, tq)
    g = jnp.tile(subln_g.astype(F32), N_HEADS)[None, :]
    return pl.pallas_call(
        functools.partial(_diff_kernel, tq=tq, out_scale=1.0 - lambda_init),
        grid=(N // T, T // tq),
        in_specs=[pl.BlockSpec(memory_space=pltpu.SMEM)]
        + _attn_specs(T, tq, SEG_ID['diff_q'], SEG_ID['diff_k'], 'diff_v')
        + [pl.BlockSpec((1, SEG), lambda b, i: (0, 0))],
        out_specs=out_spec, out_shape=out_shape,
        compiler_params=_params("parallel", "arbitrary"),
        name="diff_attn",
    )(lam.reshape(1).astype(F32), qk, qk, vt, g)


def _fox_kernel(q_ref, k_ref, vt_ref, misc_ref, o_ref, c_ref, s_ref, *, tq):
    i = pl.program_id(1)

    @pl.when(i == 0)
    def _():
        tri = jnp.where(_iota((tq, tq), 1) <= _iota((tq, tq), 0), 1.0, 0.0).astype(BF)
        carry = [jnp.zeros((1, LANES), F32) for _ in HEADS]
        for n in range(c_ref.shape[1] // tq):
            blk = misc_ref[n * tq:(n + 1) * tq, :] * LOG2E
            for h in HEADS:
                col = jnp.broadcast_to(blk[:, MISC_LOGF + h:MISC_LOGF + h + 1], blk.shape)
                hi = col.astype(BF)
                rest = col - hi.astype(F32)
                mid = rest.astype(BF)
                lo = (rest - mid.astype(F32)).astype(BF)
                cs = carry[h] + (jnp.dot(tri, hi, preferred_element_type=F32)
                                 + jnp.dot(tri, mid, preferred_element_type=F32)
                                 + jnp.dot(tri, lo, preferred_element_type=F32))
                c_ref[h, n * tq:(n + 1) * tq, :] = cs
                carry[h] = cs[tq - 1:tq, :]

    qs = _stack_groups(q_ref[...], 6, N_HEADS)

    def decayed(s, k0):
        return jnp.concatenate(
            [_cols(s, h, tq) - _tile_lanes(c_ref[h, pl.ds(k0, tq), :], tq // LANES) for h in HEADS], axis=1)

    l, (acc,) = _flash_attend(
        i, tq, N_HEADS, (HEADS,), qs, k_ref, vt_ref, s_ref,
        post=lambda s, c, k0: decayed(s, k0),
        last=lambda s, k0: jnp.where(_causal(N_HEADS, tq), decayed(s, k0), NEG))
    o_ref[...] = _normalised(acc, l, HEADS, tq).T.astype(BF)


def _fox(qk, vt, misc, T, tq):
    N = qk.shape[0]
    out_spec, out_shape = _attn_out(N, T, tq)
    return pl.pallas_call(
        functools.partial(_fox_kernel, tq=tq),
        grid=(N // T, T // tq),
        in_specs=_attn_specs(T, tq, SEG_ID['fox_q'], SEG_ID['fox_k'], 'fox_v')
        + [pl.BlockSpec((T, LANES), lambda b, i: (b, 0))],
        out_specs=out_spec, out_shape=out_shape,
        scratch_shapes=[pltpu.VMEM((N_HEADS, T, LANES), F32), _logit_scratch(N_HEADS, tq)],
        compiler_params=_params("parallel", "arbitrary"),
        name="fox_attn",
    )(qk, qk, vt, misc)


COUNT_ROWS = 64
F32_TINY = 2.0 ** -126

def _dsa_kernel(q_ref, k_ref, vt_ref, qi_ref, ki_ref, w_ref, o_ref, key_ref, top_ref, bias_ref, s_ref,
                *, tq, n_keep):
    i = pl.program_id(1)
    n_ch = i + 1
    qpos = i * tq + _iota((1, tq), 1)

    qis = _stack_groups(qi_ref[...], 6, N_IDX_HEADS)
    wt = w_ref[...].T * (N_IDX_HEADS ** -0.5)
    w_rows = [wt[MISC_IDXW + h:MISC_IDXW + h + 1] for h in range(N_IDX_HEADS)]

    def score_body(c, carry):
        k0 = pl.multiple_of(c * tq, tq)
        d = jnp.maximum(_dot_nt(ki_ref[pl.ds(k0, tq), :], qis), 0.0)
        score = w_rows[0] * _cols(d, 0, tq)
        for h in range(1, N_IDX_HEADS):
            score = score + w_rows[h] * _cols(d, h, tq)
        sc = jnp.where((k0 + _iota((tq, tq), 0)) <= qpos, score, -jnp.inf)
        sc = jnp.where(jnp.abs(sc) < F32_TINY, 0.0, sc)
        bits = pltpu.bitcast(sc, jnp.int32)
        key_ref[pl.ds(k0, tq), :] = jnp.where(bits < 0, bits ^ jnp.int32(0x7FFFFFFF), bits)
        top_ref[pl.ds(k0, tq), :] = pltpu.bitcast(bits & jnp.int32(-65536), F32).astype(BF)
        return carry

    lax.fori_loop(0, n_ch, score_body, 0)

    @pl.when(n_ch % 2 == 1)
    def _():
        pad = pl.ds(pl.multiple_of(n_ch * tq, tq), tq)
        key_ref[pad, :] = jnp.full((tq, tq), INT_MIN, jnp.int32)
        top_ref[pad, :] = jnp.full((tq, tq), jnp.nan, BF)

    def count(src_ref, pred, dtype):
        one, zero = jnp.ones((), dtype), jnp.zeros((), dtype)

        def body(c, acc):
            k0 = pl.multiple_of(c * 2 * tq, 2 * tq)
            hit = jnp.where(pred(src_ref[pl.ds(k0, 2 * tq), :]), one, zero)
            parts = [hit[j * COUNT_ROWS:(j + 1) * COUNT_ROWS] for j in range(2 * tq // COUNT_ROWS)]
            while len(parts) > 1:
                parts = [a + b for a, b in zip(parts[0::2], parts[1::2])]
            return acc + parts[0]
        acc = lax.fori_loop(0, (n_ch + 1) // 2, body, jnp.zeros((COUNT_ROWS, tq), dtype))
        return jnp.sum(acc.astype(F32), axis=0, keepdims=True)

    def top_body(it, thr):
        cand = thr + lax.shift_left(jnp.int32(1), 31 - it)
        fbits = jnp.where(cand < 0, cand ^ jnp.int32(0x7FFFFFFF), cand) & jnp.int32(-65536)
        subnormal = ((fbits & jnp.int32(0x7F800000)) == 0) & ((fbits & jnp.int32(0x007F0000)) != 0)
        fbits = jnp.where(subnormal, jnp.where(fbits < 0, 0, jnp.int32(0x00800000)), fbits)
        cand_f = pltpu.bitcast(fbits, F32).astype(BF)
        return jnp.where(count(top_ref, lambda tt: tt >= cand_f, BF) >= n_keep, cand, thr)

    def low_body(it, thr):
        cand = thr + lax.shift_left(jnp.int32(1), 31 - it)
        return jnp.where(count(key_ref, lambda kk: kk >= cand, F32) >= n_keep, cand, thr)

    thr = lax.fori_loop(0, 16, top_body, jnp.full((1, tq), INT_MIN, jnp.int32))
    thr = lax.fori_loop(16, 32, low_body, thr)

    need = n_keep - count(key_ref, lambda kk: kk > thr, F32)
    lower = jnp.where(_iota((tq, tq), 1) <= _iota((tq, tq), 0), 1.0, 0.0).astype(BF)

    def tie_body(c, seen):
        k0 = pl.multiple_of(c * tq, tq)
        kk = key_ref[pl.ds(k0, tq), :]
        eq = jnp.where(kk == thr, 1.0, 0.0)
        rank = jnp.dot(lower, eq.astype(BF), preferred_element_type=F32) + seen
        keep = (kk > thr) | ((kk == thr) & (rank <= need))
        causal = (k0 + _iota((tq, tq), 0)) <= qpos
        bias_ref[pl.ds(k0, tq), :] = jnp.where(keep & causal, 0.0, NEG)
        return seen + jnp.sum(eq, axis=0, keepdims=True)

    lax.fori_loop(0, n_ch, tie_body, jnp.zeros((1, tq), F32))

    qs = _stack_groups(q_ref[...], 6, N_HEADS)
    biased = lambda s, k0: s + _tile_lanes(bias_ref[pl.ds(k0, tq), :], N_HEADS)
    l, (acc,) = _flash_attend(i, tq, N_HEADS, (HEADS,), qs, k_ref, vt_ref, s_ref,
                              post=lambda s, c, k0: biased(s, k0), last=biased)
    o_ref[...] = _normalised(acc, l, HEADS, tq).T.astype(BF)


def _dsa(qk, vt, misc, T, tq):
    N = qk.shape[0]
    nq = T // tq
    out_spec, out_shape = _attn_out(N, T, tq)
    return pl.pallas_call(
        functools.partial(_dsa_kernel, tq=tq, n_keep=min(DSA_TOPK_MAX, T // 4)),
        grid=(N // T, nq),
        in_specs=_attn_specs(T, tq, SEG_ID['dsa_q'], SEG_ID['dsa_k'], 'dsa_v')
        + [pl.BlockSpec((tq, SEG), lambda b, i: (b * nq + i, SEG_ID['idx_q'])),
           pl.BlockSpec((T, SEG), lambda b, i: (b, SEG_ID['idx_k4'])),
           pl.BlockSpec((tq, LANES), lambda b, i: (b * nq + i, 0))],
        out_specs=out_spec, out_shape=out_shape,
        scratch_shapes=[pltpu.VMEM(((nq + nq % 2) * tq, tq), jnp.int32),
                        pltpu.VMEM(((nq + nq % 2) * tq, tq), BF), pltpu.VMEM((T, tq), F32),
                        _logit_scratch(N_HEADS, tq)],
        compiler_params=_params("parallel", "arbitrary"),
        name="dsa_attn",
    )(qk, qk, vt, qk, qk, misc)


def _mix_out_kernel(x_ref, oa_ref, ob_ref, oc_ref, od_ref, wg_ref, wb_ref, wo_ref, g_ref, b_ref,
                    y_ref, yb_ref):
    x = x_ref[...]
    xb = x.astype(BF)
    merged = jnp.zeros(x.shape, F32)
    for n, o_ref in enumerate((oa_ref, ob_ref, oc_ref, od_ref)):
        gate = _sigmoid(jnp.dot(xb, wg_ref[:, n * D_MODEL:(n + 1) * D_MODEL], preferred_element_type=F32))
        merged = merged + gate * jnp.dot(o_ref[...], wb_ref[n], preferred_element_type=F32)
    h = jnp.dot(merged.astype(BF), wo_ref[...], preferred_element_type=F32)
    y = _layer_norm(DEEPNORM_ALPHA * x + h, g_ref[...], b_ref[...])
    y_ref[...] = y
    yb_ref[...] = y.astype(BF)


def _mix_out(x2, branches, w_gates, w_branch, w_out, ln_g, ln_b, tm):
    N = x2.shape[0]
    row = lambda i: (i, 0)
    fixed2 = lambda i: (0, 0)
    return pl.pallas_call(
        _mix_out_kernel,
        grid=(N // tm,),
        in_specs=[pl.BlockSpec((tm, D_MODEL), row)] + [pl.BlockSpec((tm, SEG), row)] * N_BRANCHES
        + [pl.BlockSpec(w_gates.shape, fixed2),
           pl.BlockSpec(w_branch.shape, lambda i: (0, 0, 0)),
           pl.BlockSpec(w_out.shape, fixed2),
           pl.BlockSpec((1, D_MODEL), fixed2), pl.BlockSpec((1, D_MODEL), fixed2)],
        out_specs=[pl.BlockSpec((tm, D_MODEL), row), pl.BlockSpec((tm, D_MODEL), row)],
        out_shape=[jax.ShapeDtypeStruct((N, D_MODEL), F32), jax.ShapeDtypeStruct((N, D_MODEL), BF)],
        compiler_params=_params("parallel"),
        name="mix_out",
    )(x2, *branches, w_gates, w_branch.astype(BF), w_out.astype(BF),
      ln_g.astype(F32)[None, :], ln_b.astype(F32)[None, :])


def _split_bf16(a):
    hi = a.astype(BF)
    return hi, (a - hi.astype(F32)).astype(BF)


def _router_kernel(x_ref, whi_ref, wlo_ref, rb_ref, idx_ref, wsel_ref):
    xhi, xlo = _split_bf16(x_ref[...])
    whi = whi_ref[...]
    logits = _dot_nt(whi, xhi) + _dot_nt(whi, xlo) + _dot_nt(wlo_ref[...], xhi)
    tm = logits.shape[1]
    scores = _sigmoid(logits)
    biased = scores + _tile_lanes(rb_ref[...], tm // LANES)
    per_group = N_EXPERTS // N_GROUPS
    gs = []
    for g in range(N_GROUPS):
        bg = biased[g * per_group:(g + 1) * per_group]
        row = _iota(bg.shape, 0)
        m1 = jnp.max(bg, axis=0, keepdims=True)
        i1 = jnp.min(jnp.where(bg == m1, row, per_group), axis=0, keepdims=True)
        m2 = jnp.max(jnp.where(row == i1, -jnp.inf, bg), axis=0, keepdims=True)
        gs.append(m1 + m2)
    kept = []
    for g in range(N_GROUPS):
        rank = jnp.zeros((1, tm), F32)
        for o in range(N_GROUPS):
            if o != g:
                beats = (gs[o] >= gs[g]) if o < g else (gs[o] > gs[g])
                rank = rank + jnp.where(beats, 1.0, 0.0)
        kept.append(jnp.where(rank < TOPK_GROUPS, biased[g * per_group:(g + 1) * per_group], -jnp.inf))
    masked = jnp.concatenate(kept, axis=0)
    eid = _iota(masked.shape, 0)
    picks, weights = [], []
    for _ in range(TOP_K):
        mx = jnp.max(masked, axis=0, keepdims=True)
        pick = jnp.min(jnp.where(masked == mx, eid, N_EXPERTS), axis=0, keepdims=True)
        hit = eid == pick
        weights.append(jnp.sum(jnp.where(hit, scores, 0.0), axis=0, keepdims=True))
        masked = jnp.where(hit, -jnp.inf, masked)
        picks.append(pick)
    wsum = weights[0]
    for wk in weights[1:]:
        wsum = wsum + wk
    idx_ref[...] = jnp.concatenate(picks, axis=0)
    wsel_ref[...] = jnp.concatenate(weights, axis=0) / wsum * ROUTED_SCALE


def _router(x1, w_router, router_bias, tm):
    N = x1.shape[0]
    whi, wlo = _split_bf16(w_router.astype(F32).T)
    rb = jnp.broadcast_to(router_bias.astype(F32)[:, None], (N_EXPERTS, LANES))
    fixed = lambda i: (0, 0)
    col = lambda i: (0, i)
    return pl.pallas_call(
        _router_kernel,
        grid=(N // tm,),
        in_specs=[pl.BlockSpec((tm, D_MODEL), lambda i: (i, 0)), pl.BlockSpec(whi.shape, fixed),
                  pl.BlockSpec(wlo.shape, fixed), pl.BlockSpec(rb.shape, fixed)],
        out_specs=[pl.BlockSpec((TOP_K, tm), col), pl.BlockSpec((TOP_K, tm), col)],
        out_shape=[jax.ShapeDtypeStruct((TOP_K, N), jnp.int32), jax.ShapeDtypeStruct((TOP_K, N), F32)],
        compiler_params=_params("parallel"),
        name="router",
    )(x1, whi, wlo, rb)


def _expert_kernel(blk_ref, exp_ref, lo_ref, hi_ref, first_ref, newe_ref,
                   xs_ref, wg_ref, wu_ref, wd_ref, y_ref, wgb_ref, wub_ref, wdb_ref, *, bm):
    t = pl.program_id(0)
    lo, hi = lo_ref[t], hi_ref[t]

    @pl.when(newe_ref[t] == 1)
    def _():
        wgb_ref[...] = wg_ref[...].astype(BF)
        wub_ref[...] = wu_ref[...].astype(BF)
        wdb_ref[...] = wd_ref[...].astype(BF)

    @pl.when(first_ref[t] == 1)
    def _():
        y_ref[...] = jnp.zeros_like(y_ref)

    sub = bm // 2
    base = blk_ref[t] * bm

    def work(j0, n_sub):
        rows = slice(j0 * sub, (j0 + n_sub) * sub)
        xs = xs_ref[rows, :]
        g = jnp.dot(xs, wgb_ref[...], preferred_element_type=F32)
        u = jnp.dot(xs, wub_ref[...], preferred_element_type=F32)
        h = (g * _sigmoid(g) * u).astype(BF)
        y = jnp.dot(h, wdb_ref[...], preferred_element_type=F32)
        row = base + j0 * sub + _iota((n_sub * sub, 1), 0)
        mine = (row >= lo) & (row < hi)
        y_ref[rows, :] = jnp.where(mine, y.astype(y_ref.dtype), y_ref[rows, :])

    need = [(hi > lo) & (lo < base + (j + 1) * sub) & (hi > base + j * sub) for j in range(2)]
    pl.when(need[0] & need[1])(lambda: work(0, 2))
    pl.when(need[0] & jnp.logical_not(need[1]))(lambda: work(0, 1))
    pl.when(jnp.logical_not(need[0]) & need[1])(lambda: work(1, 1))


def _expert_kernel_onto(blk_ref, exp_ref, lo_ref, hi_ref, first_ref, newe_ref,
                        xs_ref, wg_ref, wu_ref, wd_ref, prev_ref, y_ref, *scratch, bm):
    del prev_ref
    _expert_kernel(blk_ref, exp_ref, lo_ref, hi_ref, first_ref, newe_ref,
                   xs_ref, wg_ref, wu_ref, wd_ref, y_ref, *scratch, bm=bm)


def _experts(xs_parts, plans, layer, w_gate, w_up, w_down, bm):
    rows_part = xs_parts[0].shape[0]
    A = rows_part * len(xs_parts)
    wspec = lambda shape: pl.BlockSpec((None, None) + shape, lambda t, blk, exp, *_: (layer, exp[t], 0, 0))
    ys = None
    for p, (xs, plan) in enumerate(zip(xs_parts, plans)):
        blk0 = p * rows_part // bm
        in_specs = [pl.BlockSpec((bm, D_MODEL), lambda t, blk, *_, blk0=blk0: (blk[t] - blk0, 0)),
                    wspec((D_MODEL, EXPERT_DIM)), wspec((D_MODEL, EXPERT_DIM)), wspec((EXPERT_DIM, D_MODEL))]
        args = (*plan, xs, w_gate, w_up, w_down)
        if ys is not None:
            in_specs.append(pl.BlockSpec(memory_space=pl.ANY))
            args += (ys,)
        ys = pl.pallas_call(
            functools.partial(_expert_kernel if ys is None else _expert_kernel_onto, bm=bm),
            grid_spec=pltpu.PrefetchScalarGridSpec(
                num_scalar_prefetch=len(plan),
                grid=(plan[0].shape[0],),
                in_specs=in_specs,
                out_specs=pl.BlockSpec((bm, D_MODEL), lambda t, blk, *_: (blk[t], 0)),
                scratch_shapes=[pltpu.VMEM((D_MODEL, EXPERT_DIM), BF), pltpu.VMEM((D_MODEL, EXPERT_DIM), BF),
                                pltpu.VMEM((EXPERT_DIM, D_MODEL), BF)],
            ),
            out_shape=jax.ShapeDtypeStruct((A, D_MODEL), BF),
            input_output_aliases={} if ys is None else {len(args) - 1: 0},
            compiler_params=_params("arbitrary"),
            name="experts",
        )(*args)
    return ys


def _dispatch_plan(eidx_t, bm, n_parts):
    N = eidx_t.shape[1]
    A, E = N * TOP_K, N_EXPERTS
    assert A % (bm * n_parts) == 0
    ids = jnp.arange(A, dtype=jnp.int32)
    id_bits = (A - 1).bit_length()
    assert id_bits + (E - 1).bit_length() <= 31
    packed = jnp.sort((eidx_t.reshape(A) << id_bits) | ids)
    se, sid = packed >> id_bits, packed & ((1 << id_bits) - 1)
    _, pos = lax.sort((sid, ids), num_keys=1)
    start = jnp.searchsorted(se, jnp.arange(E, dtype=jnp.int32), side='left').astype(jnp.int32)
    changed = lambda a: jnp.concatenate([jnp.ones((1,), jnp.int32), (a[1:] != a[:-1]).astype(jnp.int32)])
    plans = []
    for p in range(n_parts):
        a0, a1 = p * (A // n_parts), (p + 1) * (A // n_parts)
        lo = jnp.sort(jnp.concatenate([jnp.arange(a0, a1, bm, dtype=jnp.int32), jnp.clip(start, a0, a1)]))
        hi = jnp.concatenate([lo[1:], jnp.full((1,), a1, jnp.int32)])
        blk = jnp.minimum(lo // bm, a1 // bm - 1)
        exp = se[jnp.minimum(lo, A - 1)]
        plans.append((blk, exp, lo, hi, changed(blk), changed(exp)))
    return sid % N, pos.reshape(TOP_K, N), plans


def _moe_out_kernel(x_ref, r_ref, rw_ref, wg_ref, wu_ref, wd_ref, g_ref, b_ref, y_ref):
    x = x_ref[...]
    tm = x.shape[0]
    xb = x.astype(BF)
    g = jnp.dot(xb, wg_ref[...], preferred_element_type=F32)
    u = jnp.dot(xb, wu_ref[...], preferred_element_type=F32)
    h = (g * _sigmoid(g) * u).astype(BF)
    shared = jnp.dot(h, wd_ref[...], preferred_element_type=F32)
    rw = jnp.concatenate([rw_ref[...], jnp.zeros((LANES - TOP_K, tm), F32)], axis=0).T
    routed = rw[:, 0:1] * r_ref[0].astype(F32)
    for k in range(1, TOP_K):
        routed = routed + rw[:, k:k + 1] * r_ref[k].astype(F32)
    y_ref[...] = _layer_norm(DEEPNORM_ALPHA * x + (routed + shared), g_ref[...], b_ref[...])


def _moe_out_kernel_onto(x_ref, r_ref, rw_ref, wg_ref, wu_ref, wd_ref, g_ref, b_ref, prev_ref, y_ref):
    del prev_ref
    _moe_out_kernel(x_ref, r_ref, rw_ref, wg_ref, wu_ref, wd_ref, g_ref, b_ref, y_ref)


def _moe_out(x1, routed_parts, route_w, w_sh_gate, w_sh_up, w_sh_down, ln_g, ln_b, tm):
    N = x1.shape[0]
    n_tok = routed_parts[0].shape[1]
    fixed = lambda i: (0, 0)
    weights = (w_sh_gate.astype(BF), w_sh_up.astype(BF), w_sh_down.astype(BF),
               ln_g.astype(F32)[None, :], ln_b.astype(F32)[None, :])
    y = None
    for p, routed in enumerate(routed_parts):
        t0 = p * n_tok // tm
        row = lambda i, t0=t0: (t0 + i, 0)
        in_specs = [pl.BlockSpec((tm, D_MODEL), row),
                    pl.BlockSpec((TOP_K, tm, D_MODEL), lambda i: (0, i, 0)),
                    pl.BlockSpec((TOP_K, tm), lambda i, t0=t0: (0, t0 + i)),
                    pl.BlockSpec(w_sh_gate.shape, fixed), pl.BlockSpec(w_sh_up.shape, fixed),
                    pl.BlockSpec(w_sh_down.shape, fixed),
                    pl.BlockSpec((1, D_MODEL), fixed), pl.BlockSpec((1, D_MODEL), fixed)]
        args = (x1, routed, route_w) + weights
        if y is not None:
            in_specs.append(pl.BlockSpec(memory_space=pl.ANY))
            args += (y,)
        y = pl.pallas_call(
            _moe_out_kernel if y is None else _moe_out_kernel_onto,
            grid=(n_tok // tm,),
            in_specs=in_specs,
            out_specs=pl.BlockSpec((tm, D_MODEL), row),
            out_shape=jax.ShapeDtypeStruct((N, D_MODEL), F32),
            input_output_aliases={} if y is None else {len(args) - 1: 0},
            compiler_params=_params("parallel"),
            name="moe_out",
        )(*args)
    return y


EXPERT_PARTS = 2
COMBINE_PARTS = 2

def _tiles(N, T):
    return min(512, T), min(256, T), MOBA_BLOCK, min(1024, N * TOP_K)


def _mixer_layer(x2, T, w_in, b_forget, diff_lambda, diff_subln, w_branch, w_out, ln_g, ln_b, lambda_init):
    N = x2.shape[0]
    tm_proj, tm_row, tq, _ = _tiles(N, T)
    w, wvt, w_gates, bf = _in_proj_weights(w_in, b_forget)
    qk, vt, misc = _in_proj(x2, w, wvt, bf, _rope_tables(T), T, tm_proj)
    dl = diff_lambda.astype(F32)
    lam = jnp.exp(jnp.sum(dl[0] * dl[1])) - jnp.exp(jnp.sum(dl[2] * dl[3])) + lambda_init
    o_a = _moba(qk, vt, T, tq)
    o_b = _diff(qk, vt, lam, diff_subln, lambda_init, T, tq)
    o_c = _fox(qk, vt, misc, T, tq)
    o_d = _dsa(qk, vt, misc, T, tq)
    return _mix_out(x2, (o_a, o_b, o_c, o_d), w_gates, w_branch, w_out, ln_g, ln_b, tm_row)


def _moe_layer(x1, x1b, T, layer, w_router, router_bias, w_exp_gate, w_exp_up, w_exp_down,
               w_sh_gate, w_sh_up, w_sh_down, ln_g, ln_b):
    N = x1.shape[0]
    tm_row, _, _, bm = _tiles(N, T)
    eidx_t, wsel_t = _router(x1, w_router, router_bias, tm_row)
    row_tok, pos, plans = _dispatch_plan(eidx_t, bm, EXPERT_PARTS)
    rows_part = row_tok.shape[0] // EXPERT_PARTS
    xs_parts = [x1b[row_tok[p * rows_part:(p + 1) * rows_part]] for p in range(EXPERT_PARTS)]
    ys = _experts(xs_parts, plans, layer, w_exp_gate, w_exp_up, w_exp_down, bm)
    n_tok = N // COMBINE_PARTS
    routed_parts = [ys[pos[:, p * n_tok:(p + 1) * n_tok]] for p in range(COMBINE_PARTS)]
    return _moe_out(x1, routed_parts, wsel_t, w_sh_gate, w_sh_up, w_sh_down, ln_g, ln_b, tm_row)


def kernel(x, w_in, b_forget, diff_lambda, diff_subln, w_branch, w_out, ln1_g, ln1_b, w_router, router_bias,
           w_exp_gate, w_exp_up, w_exp_down, w_sh_gate, w_sh_up, w_sh_down, ln2_g, ln2_b):
    B, T, D = x.shape
    x2 = x.reshape(B * T, D)
    for l in range(DEPTH):
        lambda_init = 0.8 - 0.6 * math.exp(-0.3 * l)
        x1, x1b = _mixer_layer(x2, T, w_in[l], b_forget[l], diff_lambda[l], diff_subln[l], w_branch[l],
                               w_out[l], ln1_g[l], ln1_b[l], lambda_init)
        x2 = _moe_layer(x1, x1b, T, l, w_router[l], router_bias[l], w_exp_gate, w_exp_up, w_exp_down,
                        w_sh_gate[l], w_sh_up[l], w_sh_down[l], ln2_g[l], ln2_b[l])
    return x2.reshape(B, T, D)
```

```python
import functools
import math

import jax
import jax.numpy as jnp
from jax import lax
from jax.experimental import pallas as pl
from jax.experimental.pallas import tpu as pltpu

F32 = jnp.float32
BF = jnp.bfloat16

D_MODEL = 1024
DEPTH = 2
HEAD_DIM = 64
N_HEADS = 4
DIFF_DIM = HEAD_DIM // 2
N_IDX_HEADS = 4
IDX_DIM = 64
BRANCH_WIDTH = N_HEADS * HEAD_DIM
N_BRANCHES = 4
MOBA_BLOCK = 256
MOBA_TOPK = 3
DSA_TOPK_MAX = 256
ROPE_THETA = 10000.0
N_EXPERTS = 256
TOP_K = 8
N_GROUPS = 8
TOPK_GROUPS = 4
EXPERT_DIM = 256
ROUTED_SCALE = 2.5
LN_EPS = 1e-5
DEEPNORM_ALPHA = (2 * DEPTH) ** 0.25

IN_SEGMENTS = (
    ('moba_q', BRANCH_WIDTH), ('moba_k', BRANCH_WIDTH), ('moba_v', BRANCH_WIDTH),
    ('diff_q', BRANCH_WIDTH), ('diff_k', BRANCH_WIDTH), ('diff_v', BRANCH_WIDTH),
    ('fox_q', BRANCH_WIDTH), ('fox_k', BRANCH_WIDTH), ('fox_v', BRANCH_WIDTH), ('fox_f', N_HEADS),
    ('dsa_q', BRANCH_WIDTH), ('dsa_k', BRANCH_WIDTH), ('dsa_v', BRANCH_WIDTH),
    ('idx_q', N_IDX_HEADS * IDX_DIM), ('idx_k', IDX_DIM), ('idx_w', N_IDX_HEADS),
    ('gates', N_BRANCHES * D_MODEL),
)

LANES = 128
SUBLANES = 8
SEG = BRANCH_WIDTH
NEG = -1e30
INT_MIN = -2 ** 31
VMEM_LIMIT = 48 * 1024 * 1024

LOG2E = math.log2(math.e)
_QSCALE = HEAD_DIM ** -0.5 * LOG2E
PROJ_SEGS = (
    ('moba_q', 64, _QSCALE), ('moba_k', 64, 1.0),
    ('diff_q', 32, DIFF_DIM ** -0.5 * LOG2E), ('diff_k', 32, 1.0),
    ('fox_q', 0, _QSCALE), ('fox_k', 0, 1.0),
    ('dsa_q', 64, _QSCALE), ('dsa_k', 64, 1.0),
    ('idx_q', 64, IDX_DIM ** -0.5), ('idx_k4', 64, 1.0),
)
SEG_ID = {name: i for i, (name, _, _) in enumerate(PROJ_SEGS)}
N_SEG = len(PROJ_SEGS)
V_SEGS = ('moba_v', 'diff_v', 'fox_v', 'dsa_v')
V_ID = {name: i for i, name in enumerate(V_SEGS)}
MISC_LOGF = 0
MISC_IDXW = 4


def _params(*sem):
    return pltpu.CompilerParams(dimension_semantics=sem, vmem_limit_bytes=VMEM_LIMIT)


def _iota(shape, dim):
    return lax.broadcasted_iota(jnp.int32, shape, dim)


def _dot_nt(a, b):
    return lax.dot_general(a, b, (((1,), (1,)), ((), ())), preferred_element_type=F32)


def _sigmoid(z):
    return 1.0 / (1.0 + jnp.exp(-z))


def _layer_norm(y, g, b):
    mu = jnp.mean(y, axis=-1, keepdims=True)
    yc = y - mu
    var = jnp.mean(yc * yc, axis=-1, keepdims=True)
    return yc * lax.rsqrt(var + LN_EPS) * g + b


def _swap_halves(a, half):
    w = a.shape[-1]
    first = (_iota(a.shape, 1) & (2 * half - 1)) < half
    return jnp.where(first, pltpu.roll(a, w - half, 1), pltpu.roll(a, half, 1))


def _in_proj_kernel(x_ref, w_ref, wvt_ref, c64_ref, s64_ref, c32_ref, s32_ref, bf_ref,
                    qk_ref, vt_ref, misc_ref):
    xb = x_ref[...].astype(BF)
    for s, (_, rot, scale) in enumerate(PROJ_SEGS):
        acc = jnp.dot(xb, w_ref[:, s * SEG:(s + 1) * SEG], preferred_element_type=F32)
        if rot == 64:
            acc = acc * c64_ref[...] + _swap_halves(acc, 32) * s64_ref[...]
        elif rot == 32:
            acc = acc * c32_ref[...] + _swap_halves(acc, 16) * s32_ref[...]
        if scale != 1.0:
            acc = acc * scale
        qk_ref[:, s * SEG:(s + 1) * SEG] = acc.astype(BF)
    vt_ref[...] = _dot_nt(wvt_ref[...], xb).astype(BF)
    m = jnp.dot(xb, w_ref[:, N_SEG * SEG:N_SEG * SEG + LANES], preferred_element_type=F32)
    z = m + bf_ref[...]
    logf = jnp.minimum(z, 0.0) - jnp.log1p(jnp.exp(-jnp.abs(z)))
    lane = _iota(m.shape, 1)
    misc_ref[...] = jnp.where(lane < MISC_IDXW, logf, m)


def _rope_tables(T):
    pos = jnp.arange(T).astype(F32)

    def tab(group, reps):
        half = group // 2
        inv_freq = ROPE_THETA ** (-jnp.arange(half, dtype=F32) / half)
        ang = pos[:, None] * inv_freq[None, :]
        cos, sin = jnp.cos(ang), jnp.sin(ang)
        return (jnp.tile(jnp.concatenate([cos, cos], -1), (1, reps)),
                jnp.tile(jnp.concatenate([-sin, sin], -1), (1, reps)))

    return tab(64, SEG // 64) + tab(32, SEG // 32)


def _in_proj_weights(w_in, b_forget):
    parts, off = {}, 0
    for name, width in IN_SEGMENTS:
        parts[name] = w_in[:, off:off + width]
        off += width
    parts['idx_k4'] = jnp.tile(parts['idx_k'], (1, N_IDX_HEADS))
    misc = jnp.concatenate([parts['fox_f'], parts['idx_w'],
                            jnp.zeros((D_MODEL, LANES - 2 * N_HEADS), w_in.dtype)], axis=1)
    w = jnp.concatenate([parts[name] for name, _, _ in PROJ_SEGS] + [misc], axis=1).astype(BF)
    wvt = jnp.concatenate([parts[name] for name in V_SEGS], axis=1).T.astype(BF)
    bf = jnp.zeros((1, LANES), F32).at[0, MISC_LOGF:MISC_LOGF + N_HEADS].set(b_forget.astype(F32))
    return w, wvt, parts['gates'].astype(BF), bf


def _in_proj(x2, w, wvt, bf, tables, T, tm):
    N = x2.shape[0]
    nt = T // tm
    tab_spec = pl.BlockSpec((tm, SEG), lambda i: (i % nt, 0))
    fixed = lambda i: (0, 0)
    return pl.pallas_call(
        _in_proj_kernel,
        grid=(N // tm,),
        in_specs=[pl.BlockSpec((tm, D_MODEL), lambda i: (i, 0)),
                  pl.BlockSpec(w.shape, fixed), pl.BlockSpec(wvt.shape, fixed),
                  tab_spec, tab_spec, tab_spec, tab_spec,
                  pl.BlockSpec((1, LANES), fixed)],
        out_specs=[pl.BlockSpec((tm, N_SEG * SEG), lambda i: (i, 0)),
                   pl.BlockSpec((len(V_SEGS) * SEG, tm), lambda i: (0, i)),
                   pl.BlockSpec((tm, LANES), lambda i: (i, 0))],
        out_shape=[jax.ShapeDtypeStruct((N, N_SEG * SEG), BF),
                   jax.ShapeDtypeStruct((len(V_SEGS) * SEG, N), BF),
                   jax.ShapeDtypeStruct((N, LANES), F32)],
        compiler_params=_params("parallel"),
        name="in_proj",
    )(x2, w, wvt, *tables, bf)


def _group_mask(shape, shift, g):
    return (_iota(shape, 1) >> shift) == g


def _stack_groups(q, shift, n_groups):
    zero = jnp.zeros_like(q)
    return jnp.concatenate([jnp.where(_group_mask(q.shape, shift, g), q, zero) for g in range(n_groups)],
                           axis=0)


def _cols(a, g, tq):
    return a[:, g * tq:(g + 1) * tq]


def _head_rows(a, h):
    return a[h * HEAD_DIM:(h + 1) * HEAD_DIM]


ONES_ROWS = 16


def _flash_init(n_groups, tq, n_sets):
    return (jnp.full((1, n_groups * tq), NEG, F32), jnp.zeros((1, n_groups * tq), F32),
            tuple(jnp.zeros((SEG, tq), F32) for _ in range(n_sets)))


def _flash_update(s, carry, vt, tq, head_sets):
    m, l, accs = carry
    m_new = jnp.maximum(m, jnp.max(s, axis=0, keepdims=True))
    alpha = jnp.exp2(m - m_new)
    pb = jnp.exp2(s - m_new).astype(BF)
    ones = jnp.ones((ONES_ROWS, vt.shape[1]), BF)
    new_accs, p_sum = [], {}
    for acc, groups in zip(accs, head_sets):
        parts = []
        for h, g in enumerate(groups):
            r = jnp.dot(jnp.concatenate([_head_rows(vt, h), ones], axis=0), _cols(pb, g, tq),
                        preferred_element_type=F32)
            parts.append(_cols(alpha, g, tq) * _head_rows(acc, h) + r[0:HEAD_DIM])
            p_sum[g] = r[HEAD_DIM:HEAD_DIM + 1]
        new_accs.append(jnp.concatenate(parts, axis=0))
    l = alpha * l + jnp.concatenate([p_sum[g] for g in sorted(p_sum)], axis=1)
    return m_new, l, tuple(new_accs)


def _causal(n_groups, tq):
    shape = (tq, n_groups * tq)
    return _iota(shape, 0) <= (_iota(shape, 1) & (tq - 1))


def _normalised(acc, l, groups, tq):
    return jnp.concatenate([_head_rows(acc, h) * (1.0 / _cols(l, g, tq)) for h, g in enumerate(groups)],
                           axis=0)


def _tile_lanes(a, n):
    return jnp.concatenate([a] * n, axis=1) if n > 1 else a


HEADS = tuple(range(N_HEADS))


def _flash_attend(i, tq, n_groups, head_sets, qs, k_ref, vt_ref, s_ref, post, last):
    def raw(k0):
        return _dot_nt(k_ref[pl.ds(k0, tq), :], qs)

    if s_ref is not None:
        s_ref[0] = raw(0)

    def body(c, carry):
        k0 = pl.multiple_of(c * tq, tq)
        if s_ref is None:
            s = post(raw(k0), c, k0)
        else:
            s = post(s_ref[c & 1], c, k0)
            s_ref[(c + 1) & 1] = raw(pl.multiple_of(k0 + tq, tq))
        return _flash_update(s, carry, vt_ref[:, pl.ds(k0, tq)], tq, head_sets)

    carry = lax.fori_loop(0, i, body, _flash_init(n_groups, tq, len(head_sets)))
    k0 = pl.multiple_of(i * tq, tq)
    s = last(raw(k0) if s_ref is None else s_ref[i & 1], k0)
    _, l, accs = _flash_update(s, carry, vt_ref[:, pl.ds(k0, tq)], tq, head_sets)
    return l, accs


def _logit_scratch(n_groups, tq):
    return pltpu.VMEM((2, tq, n_groups * tq), F32)


def _attn_specs(T, tq, q_seg, k_seg, v_name):
    nq = T // tq
    v_id = V_ID[v_name]
    return [pl.BlockSpec((tq, SEG), lambda b, i: (b * nq + i, q_seg)),
            pl.BlockSpec((T, SEG), lambda b, i: (b, k_seg)),
            pl.BlockSpec((SEG, T), lambda b, i: (v_id, b))]


def _attn_out(N, T, tq):
    nq = T // tq
    return (pl.BlockSpec((tq, SEG), lambda b, i: (b * nq + i, 0)),
            jax.ShapeDtypeStruct((N, SEG), BF))


def _moba_kernel(q_ref, k_ref, vt_ref, o_ref, km_ref, sel_ref, s_ref, *, tq, nb, n_sel):
    i = pl.program_id(1)

    @pl.when(i == 0)
    def _():
        km_ref[...] = jnp.zeros_like(km_ref)
        for n in range(nb):
            blk = k_ref[n * tq:(n + 1) * tq, :].astype(F32)
            km_ref[n:n + 1, :] = jnp.mean(blk, axis=0, keepdims=True)

    qs = _stack_groups(q_ref[...], 6, N_HEADS)
    km = km_ref[...]
    km_hi = km.astype(BF)
    km_lo = (km - km_hi.astype(F32)).astype(BF)
    nrow = sel_ref.shape[0]
    g = (_dot_nt(km_hi, qs) + _dot_nt(km_lo, qs))[0:nrow]
    blk_id = _iota(g.shape, 0)
    g = jnp.where(blk_id < i, g, -jnp.inf)
    sel = jnp.zeros(g.shape, F32)
    for _ in range(n_sel):
        gmax = jnp.max(g, axis=0, keepdims=True)
        first = jnp.min(jnp.where(g == gmax, blk_id, nrow), axis=0, keepdims=True)
        hit = blk_id == first
        sel = jnp.where(hit & (gmax > -jnp.inf), 1.0, sel)
        g = jnp.where(hit, -jnp.inf, g)
    sel_ref[...] = sel

    l, (acc,) = _flash_attend(
        i, tq, N_HEADS, (HEADS,), qs, k_ref, vt_ref, s_ref,
        post=lambda s, c, k0: jnp.where(sel_ref[pl.ds(c, 1), :] > 0.5, s, NEG),
        last=lambda s, k0: jnp.where(_causal(N_HEADS, tq), s, NEG))
    o_ref[...] = _normalised(acc, l, HEADS, tq).T.astype(BF)


def _moba(qk, vt, T, tq):
    N = qk.shape[0]
    nb = T // MOBA_BLOCK
    assert T % MOBA_BLOCK == 0 and tq == MOBA_BLOCK and nb <= LANES
    sel_rows = -(-nb // SUBLANES) * SUBLANES
    out_spec, out_shape = _attn_out(N, T, tq)
    return pl.pallas_call(
        functools.partial(_moba_kernel, tq=tq, nb=nb, n_sel=min(MOBA_TOPK, nb - 1)),
        grid=(N // T, T // tq),
        in_specs=_attn_specs(T, tq, SEG_ID['moba_q'], SEG_ID['moba_k'], 'moba_v'),
        out_specs=out_spec, out_shape=out_shape,
        scratch_shapes=[pltpu.VMEM((LANES, SEG), F32), pltpu.VMEM((sel_rows, N_HEADS * tq), F32),
                        _logit_scratch(N_HEADS, tq)],
        compiler_params=_params("parallel", "arbitrary"),
        name="moba_attn",
    )(qk, qk, vt)


DIFF_SETS = (tuple(2 * h for h in HEADS), tuple(2 * h + 1 for h in HEADS))


def _diff_kernel(lam_ref, q_ref, k_ref, vt_ref, g_ref, o_ref, *, tq, out_scale):
    i = pl.program_id(1)
    q = q_ref[...]
    zero = jnp.zeros_like(q)
    qs = [jnp.concatenate([jnp.where(_group_mask(q.shape, 5, g), q, zero) for g in groups], axis=0)
          for groups in DIFF_SETS]

    def step(k0, carries, mask):
        kch, vt = k_ref[pl.ds(k0, tq), :], vt_ref[:, pl.ds(k0, tq)]
        out = []
        for qs_c, carry in zip(qs, carries):
            s = _dot_nt(kch, qs_c)
            if mask is not None:
                s = jnp.where(mask, s, NEG)
            out.append(_flash_update(s, carry, vt, tq, (HEADS,)))
        return tuple(out)

    init = (_flash_init(N_HEADS, tq, 1), _flash_init(N_HEADS, tq, 1))
    carries = lax.fori_loop(0, i, lambda c, cs: step(pl.multiple_of(c * tq, tq), cs, None), init)
    (_, l1, (a1,)), (_, l2, (a2,)) = step(pl.multiple_of(i * tq, tq), carries, _causal(N_HEADS, tq))
    out = _normalised(a1, l1, HEADS, tq) - lam_ref[0] * _normalised(a2, l2, HEADS, tq)
    normed = []
    for h in HEADS:
        oh = _head_rows(out, h)
        ms = jnp.mean(oh * oh, axis=0, keepdims=True)
        normed.append(oh * lax.rsqrt(ms + LN_EPS))
    o_ref[...] = (jnp.concatenate(normed, axis=0).T * g_ref[...] * out_scale).astype(BF)


def _diff(qk, vt, lam, subln_g, lambda_init, T, tq):
    N = qk.shape[0]
    out_spec, out_shape = _attn_out(N, T, tq)
    g = jnp.tile(subln_g.astype(F32), N_HEADS)[None, :]
    return pl.pallas_call(
        functools.partial(_diff_kernel, tq=tq, out_scale=1.0 - lambda_init),
        grid=(N // T, T // tq),
        in_specs=[pl.BlockSpec(memory_space=pltpu.SMEM)]
        + _attn_specs(T, tq, SEG_ID['diff_q'], SEG_ID['diff_k'], 'diff_v')
        + [pl.BlockSpec((1, SEG), lambda b, i: (0, 0))],
        out_specs=out_spec, out_shape=out_shape,
        compiler_params=_params("parallel", "arbitrary"),
        name="diff_attn",
    )(lam.reshape(1).astype(F32), qk, qk, vt, g)


def _fox_kernel(q_ref, k_ref, vt_ref, misc_ref, o_ref, c_ref, s_ref, *, tq):
    i = pl.program_id(1)

    @pl.when(i == 0)
    def _():
        tri = jnp.where(_iota((tq, tq), 1) <= _iota((tq, tq), 0), 1.0, 0.0).astype(BF)
        carry = [jnp.zeros((1, LANES), F32) for _ in HEADS]
        for n in range(c_ref.shape[1] // tq):
            blk = misc_ref[n * tq:(n + 1) * tq, :] * LOG2E
            for h in HEADS:
                col = jnp.broadcast_to(blk[:, MISC_LOGF + h:MISC_LOGF + h + 1], blk.shape)
                hi = col.astype(BF)
                rest = col - hi.astype(F32)
                mid = rest.astype(BF)
                lo = (rest - mid.astype(F32)).astype(BF)
                cs = carry[h] + (jnp.dot(tri, hi, preferred_element_type=F32)
                                 + jnp.dot(tri, mid, preferred_element_type=F32)
                                 + jnp.dot(tri, lo, preferred_element_type=F32))
                c_ref[h, n * tq:(n + 1) * tq, :] = cs
                carry[h] = cs[tq - 1:tq, :]

    qs = _stack_groups(q_ref[...], 6, N_HEADS)

    def decayed(s, k0):
        return jnp.concatenate(
            [_cols(s, h, tq) - _tile_lanes(c_ref[h, pl.ds(k0, tq), :], tq // LANES) for h in HEADS], axis=1)

    l, (acc,) = _flash_attend(
        i, tq, N_HEADS, (HEADS,), qs, k_ref, vt_ref, s_ref,
        post=lambda s, c, k0: decayed(s, k0),
        last=lambda s, k0: jnp.where(_causal(N_HEADS, tq), decayed(s, k0), NEG))
    o_ref[...] = _normalised(acc, l, HEADS, tq).T.astype(BF)


def _fox(qk, vt, misc, T, tq):
    N = qk.shape[0]
    out_spec, out_shape = _attn_out(N, T, tq)
    return pl.pallas_call(
        functools.partial(_fox_kernel, tq=tq),
        grid=(N // T, T // tq),
        in_specs=_attn_specs(T, tq, SEG_ID['fox_q'], SEG_ID['fox_k'], 'fox_v')
        + [pl.BlockSpec((T, LANES), lambda b, i: (b, 0))],
        out_specs=out_spec, out_shape=out_shape,
        scratch_shapes=[pltpu.VMEM((N_HEADS, T, LANES), F32), _logit_scratch(N_HEADS, tq)],
        compiler_params=_params("parallel", "arbitrary"),
        name="fox_attn",
    )(qk, qk, vt, misc)


COUNT_ROWS = 64
F32_TINY = 2.0 ** -126

def _dsa_kernel(q_ref, k_ref, vt_ref, qi_ref, ki_ref, w_ref, o_ref, key_ref, top_ref, bias_ref, s_ref,
                *, tq, n_keep):
    i = pl.program_id(1)
    n_ch = i + 1
    qpos = i * tq + _iota((1, tq), 1)

    qis = _stack_groups(qi_ref[...], 6, N_IDX_HEADS)
    wt = w_ref[...].T * (N_IDX_HEADS ** -0.5)
    w_rows = [wt[MISC_IDXW + h:MISC_IDXW + h + 1] for h in range(N_IDX_HEADS)]

    def score_body(c, carry):
        k0 = pl.multiple_of(c * tq, tq)
        d = jnp.maximum(_dot_nt(ki_ref[pl.ds(k0, tq), :], qis), 0.0)
        score = w_rows[0] * _cols(d, 0, tq)
        for h in range(1, N_IDX_HEADS):
            score = score + w_rows[h] * _cols(d, h, tq)
        sc = jnp.where((k0 + _iota((tq, tq), 0)) <= qpos, score, -jnp.inf)
        sc = jnp.where(jnp.abs(sc) < F32_TINY, 0.0, sc)
        bits = pltpu.bitcast(sc, jnp.int32)
        key_ref[pl.ds(k0, tq), :] = jnp.where(bits < 0, bits ^ jnp.int32(0x7FFFFFFF), bits)
        top_ref[pl.ds(k0, tq), :] = pltpu.bitcast(bits & jnp.int32(-65536), F32).astype(BF)
        return carry

    lax.fori_loop(0, n_ch, score_body, 0)

    @pl.when(n_ch % 2 == 1)
    def _():
        pad = pl.ds(pl.multiple_of(n_ch * tq, tq), tq)
        key_ref[pad, :] = jnp.full((tq, tq), INT_MIN, jnp.int32)
        top_ref[pad, :] = jnp.full((tq, tq), jnp.nan, BF)

    def count(src_ref, pred, dtype):
        one, zero = jnp.ones((), dtype), jnp.zeros((), dtype)

        def body(c, acc):
            k0 = pl.multiple_of(c * 2 * tq, 2 * tq)
            hit = jnp.where(pred(src_ref[pl.ds(k0, 2 * tq), :]), one, zero)
            parts = [hit[j * COUNT_ROWS:(j + 1) * COUNT_ROWS] for j in range(2 * tq // COUNT_ROWS)]
            while len(parts) > 1:
                parts = [a + b for a, b in zip(parts[0::2], parts[1::2])]
            return acc + parts[0]
        acc = lax.fori_loop(0, (n_ch + 1) // 2, body, jnp.zeros((COUNT_ROWS, tq), dtype))
        return jnp.sum(acc.astype(F32), axis=0, keepdims=True)

    def top_body(it, thr):
        cand = thr + lax.shift_left(jnp.int32(1), 31 - it)
        fbits = jnp.where(cand < 0, cand ^ jnp.int32(0x7FFFFFFF), cand) & jnp.int32(-65536)
        subnormal = ((fbits & jnp.int32(0x7F800000)) == 0) & ((fbits & jnp.int32(0x007F0000)) != 0)
        fbits = jnp.where(subnormal, jnp.where(fbits < 0, 0, jnp.int32(0x00800000)), fbits)
        cand_f = pltpu.bitcast(fbits, F32).astype(BF)
        return jnp.where(count(top_ref, lambda tt: tt >= cand_f, BF) >= n_keep, cand, thr)

    def low_body(it, thr):
        cand = thr + lax.shift_left(jnp.int32(1), 31 - it)
        return jnp.where(count(key_ref, lambda kk: kk >= cand, F32) >= n_keep, cand, thr)

    thr = lax.fori_loop(0, 16, top_body, jnp.full((1, tq), INT_MIN, jnp.int32))
    thr = lax.fori_loop(16, 32, low_body, thr)

    need = n_keep - count(key_ref, lambda kk: kk > thr, F32)
    lower = jnp.where(_iota((tq, tq), 1) <= _iota((tq, tq), 0), 1.0, 0.0).astype(BF)

    def tie_body(c, seen):
        k0 = pl.multiple_of(c * tq, tq)
        kk = key_ref[pl.ds(k0, tq), :]
        eq = jnp.where(kk == thr, 1.0, 0.0)
        rank = jnp.dot(lower, eq.astype(BF), preferred_element_type=F32) + seen
        keep = (kk > thr) | ((kk == thr) & (rank <= need))
        causal = (k0 + _iota((tq, tq), 0)) <= qpos
        bias_ref[pl.ds(k0, tq), :] = jnp.where(keep & causal, 0.0, NEG)
        return seen + jnp.sum(eq, axis=0, keepdims=True)

    lax.fori_loop(0, n_ch, tie_body, jnp.zeros((1, tq), F32))

    qs = _stack_groups(q_ref[...], 6, N_HEADS)
    biased = lambda s, k0: s + _tile_lanes(bias_ref[pl.ds(k0, tq), :], N_HEADS)
    l, (acc,) = _flash_attend(i, tq, N_HEADS, (HEADS,), qs, k_ref, vt_ref, s_ref,
                              post=lambda s, c, k0: biased(s, k0), last=biased)
    o_ref[...] = _normalised(acc, l, HEADS, tq).T.astype(BF)


def _dsa(qk, vt, misc, T, tq):
    N = qk.shape[0]
    nq = T // tq
    out_spec, out_shape = _attn_out(N, T, tq)
    return pl.pallas_call(
        functools.partial(_dsa_kernel, tq=tq, n_keep=min(DSA_TOPK_MAX, T // 4)),
        grid=(N // T, nq),
        in_specs=_attn_specs(T, tq, SEG_ID['dsa_q'], SEG_ID['dsa_k'], 'dsa_v')
        + [pl.BlockSpec((tq, SEG), lambda b, i: (b * nq + i, SEG_ID['idx_q'])),
           pl.BlockSpec((T, SEG), lambda b, i: (b, SEG_ID['idx_k4'])),
           pl.BlockSpec((tq, LANES), lambda b, i: (b * nq + i, 0))],
        out_specs=out_spec, out_shape=out_shape,
        scratch_shapes=[pltpu.VMEM(((nq + nq % 2) * tq, tq), jnp.int32),
                        pltpu.VMEM(((nq + nq % 2) * tq, tq), BF), pltpu.VMEM((T, tq), F32),
                        _logit_scratch(N_HEADS, tq)],
        compiler_params=_params("parallel", "arbitrary"),
        name="dsa_attn",
    )(qk, qk, vt, qk, qk, misc)


def _mix_out_kernel(x_ref, oa_ref, ob_ref, oc_ref, od_ref, wg_ref, wb_ref, wo_ref, g_ref, b_ref,
                    y_ref, yb_ref):
    x = x_ref[...]
    xb = x.astype(BF)
    merged = jnp.zeros(x.shape, F32)
    for n, o_ref in enumerate((oa_ref, ob_ref, oc_ref, od_ref)):
        gate = _sigmoid(jnp.dot(xb, wg_ref[:, n * D_MODEL:(n + 1) * D_MODEL], preferred_element_type=F32))
        merged = merged + gate * jnp.dot(o_ref[...], wb_ref[n], preferred_element_type=F32)
    h = jnp.dot(merged.astype(BF), wo_ref[...], preferred_element_type=F32)
    y = _layer_norm(DEEPNORM_ALPHA * x + h, g_ref[...], b_ref[...])
    y_ref[...] = y
    yb_ref[...] = y.astype(BF)


def _mix_out(x2, branches, w_gates, w_branch, w_out, ln_g, ln_b, tm):
    N = x2.shape[0]
    row = lambda i: (i, 0)
    fixed2 = lambda i: (0, 0)
    return pl.pallas_call(
        _mix_out_kernel,
        grid=(N // tm,),
        in_specs=[pl.BlockSpec((tm, D_MODEL), row)] + [pl.BlockSpec((tm, SEG), row)] * N_BRANCHES
        + [pl.BlockSpec(w_gates.shape, fixed2),
           pl.BlockSpec(w_branch.shape, lambda i: (0, 0, 0)),
           pl.BlockSpec(w_out.shape, fixed2),
           pl.BlockSpec((1, D_MODEL), fixed2), pl.BlockSpec((1, D_MODEL), fixed2)],
        out_specs=[pl.BlockSpec((tm, D_MODEL), row), pl.BlockSpec((tm, D_MODEL), row)],
        out_shape=[jax.ShapeDtypeStruct((N, D_MODEL), F32), jax.ShapeDtypeStruct((N, D_MODEL), BF)],
        compiler_params=_params("parallel"),
        name="mix_out",
    )(x2, *branches, w_gates, w_branch.astype(BF), w_out.astype(BF),
      ln_g.astype(F32)[None, :], ln_b.astype(F32)[None, :])


def _split_bf16(a):
    hi = a.astype(BF)
    return hi, (a - hi.astype(F32)).astype(BF)


def _router_kernel(x_ref, whi_ref, wlo_ref, rb_ref, idx_ref, wsel_ref):
    xhi, xlo = _split_bf16(x_ref[...])
    whi = whi_ref[...]
    logits = _dot_nt(whi, xhi) + _dot_nt(whi, xlo) + _dot_nt(wlo_ref[...], xhi)
    tm = logits.shape[1]
    scores = _sigmoid(logits)
    biased = scores + _tile_lanes(rb_ref[...], tm // LANES)
    per_group = N_EXPERTS // N_GROUPS
    gs = []
    for g in range(N_GROUPS):
        bg = biased[g * per_group:(g + 1) * per_group]
        row = _iota(bg.shape, 0)
        m1 = jnp.max(bg, axis=0, keepdims=True)
        i1 = jnp.min(jnp.where(bg == m1, row, per_group), axis=0, keepdims=True)
        m2 = jnp.max(jnp.where(row == i1, -jnp.inf, bg), axis=0, keepdims=True)
        gs.append(m1 + m2)
    kept = []
    for g in range(N_GROUPS):
        rank = jnp.zeros((1, tm), F32)
        for o in range(N_GROUPS):
            if o != g:
                beats = (gs[o] >= gs[g]) if o < g else (gs[o] > gs[g])
                rank = rank + jnp.where(beats, 1.0, 0.0)
        kept.append(jnp.where(rank < TOPK_GROUPS, biased[g * per_group:(g + 1) * per_group], -jnp.inf))
    masked = jnp.concatenate(kept, axis=0)
    eid = _iota(masked.shape, 0)
    picks, weights = [], []
    for _ in range(TOP_K):
        mx = jnp.max(masked, axis=0, keepdims=True)
        pick = jnp.min(jnp.where(masked == mx, eid, N_EXPERTS), axis=0, keepdims=True)
        hit = eid == pick
        weights.append(jnp.sum(jnp.where(hit, scores, 0.0), axis=0, keepdims=True))
        masked = jnp.where(hit, -jnp.inf, masked)
        picks.append(pick)
    wsum = weights[0]
    for wk in weights[1:]:
        wsum = wsum + wk
    idx_ref[...] = jnp.concatenate(picks, axis=0)
    wsel_ref[...] = jnp.concatenate(weights, axis=0) / wsum * ROUTED_SCALE


def _router(x1, w_router, router_bias, tm):
    N = x1.shape[0]
    whi, wlo = _split_bf16(w_router.astype(F32).T)
    rb = jnp.broadcast_to(router_bias.astype(F32)[:, None], (N_EXPERTS, LANES))
    fixed = lambda i: (0, 0)
    col = lambda i: (0, i)
    return pl.pallas_call(
        _router_kernel,
        grid=(N // tm,),
        in_specs=[pl.BlockSpec((tm, D_MODEL), lambda i: (i, 0)), pl.BlockSpec(whi.shape, fixed),
                  pl.BlockSpec(wlo.shape, fixed), pl.BlockSpec(rb.shape, fixed)],
        out_specs=[pl.BlockSpec((TOP_K, tm), col), pl.BlockSpec((TOP_K, tm), col)],
        out_shape=[jax.ShapeDtypeStruct((TOP_K, N), jnp.int32), jax.ShapeDtypeStruct((TOP_K, N), F32)],
        compiler_params=_params("parallel"),
        name="router",
    )(x1, whi, wlo, rb)


def _expert_kernel(blk_ref, exp_ref, lo_ref, hi_ref, first_ref, newe_ref,
                   xs_ref, wg_ref, wu_ref, wd_ref, y_ref, wgb_ref, wub_ref, wdb_ref, *, bm):
    t = pl.program_id(0)
    lo, hi = lo_ref[t], hi_ref[t]

    @pl.when(newe_ref[t] == 1)
    def _():
        wgb_ref[...] = wg_ref[...].astype(BF)
        wub_ref[...] = wu_ref[...].astype(BF)
        wdb_ref[...] = wd_ref[...].astype(BF)

    @pl.when(first_ref[t] == 1)
    def _():
        y_ref[...] = jnp.zeros_like(y_ref)

    sub = bm // 2
    base = blk_ref[t] * bm

    def work(j0, n_sub):
        rows = slice(j0 * sub, (j0 + n_sub) * sub)
        xs = xs_ref[rows, :]
        g = jnp.dot(xs, wgb_ref[...], preferred_element_type=F32)
        u = jnp.dot(xs, wub_ref[...], preferred_element_type=F32)
        h = (g * _sigmoid(g) * u).astype(BF)
        y = jnp.dot(h, wdb_ref[...], preferred_element_type=F32)
        row = base + j0 * sub + _iota((n_sub * sub, 1), 0)
        mine = (row >= lo) & (row < hi)
        y_ref[rows, :] = jnp.where(mine, y.astype(y_ref.dtype), y_ref[rows, :])

    need = [(hi > lo) & (lo < base + (j + 1) * sub) & (hi > base + j * sub) for j in range(2)]
    pl.when(need[0] & need[1])(lambda: work(0, 2))
    pl.when(need[0] & jnp.logical_not(need[1]))(lambda: work(0, 1))
    pl.when(jnp.logical_not(need[0]) & need[1])(lambda: work(1, 1))


def _expert_kernel_onto(blk_ref, exp_ref, lo_ref, hi_ref, first_ref, newe_ref,
                        xs_ref, wg_ref, wu_ref, wd_ref, prev_ref, y_ref, *scratch, bm):
    del prev_ref
    _expert_kernel(blk_ref, exp_ref, lo_ref, hi_ref, first_ref, newe_ref,
                   xs_ref, wg_ref, wu_ref, wd_ref, y_ref, *scratch, bm=bm)


def _experts(xs_parts, plans, layer, w_gate, w_up, w_down, bm):
    rows_part = xs_parts[0].shape[0]
    A = rows_part * len(xs_parts)
    wspec = lambda shape: pl.BlockSpec((None, None) + shape, lambda t, blk, exp, *_: (layer, exp[t], 0, 0))
    ys = None
    for p, (xs, plan) in enumerate(zip(xs_parts, plans)):
        blk0 = p * rows_part // bm
        in_specs = [pl.BlockSpec((bm, D_MODEL), lambda t, blk, *_, blk0=blk0: (blk[t] - blk0, 0)),
                    wspec((D_MODEL, EXPERT_DIM)), wspec((D_MODEL, EXPERT_DIM)), wspec((EXPERT_DIM, D_MODEL))]
        args = (*plan, xs, w_gate, w_up, w_down)
        if ys is not None:
            in_specs.append(pl.BlockSpec(memory_space=pl.ANY))
            args += (ys,)
        ys = pl.pallas_call(
            functools.partial(_expert_kernel if ys is None else _expert_kernel_onto, bm=bm),
            grid_spec=pltpu.PrefetchScalarGridSpec(
                num_scalar_prefetch=len(plan),
                grid=(plan[0].shape[0],),
                in_specs=in_specs,
                out_specs=pl.BlockSpec((bm, D_MODEL), lambda t, blk, *_: (blk[t], 0)),
                scratch_shapes=[pltpu.VMEM((D_MODEL, EXPERT_DIM), BF), pltpu.VMEM((D_MODEL, EXPERT_DIM), BF),
                                pltpu.VMEM((EXPERT_DIM, D_MODEL), BF)],
            ),
            out_shape=jax.ShapeDtypeStruct((A, D_MODEL), BF),
            input_output_aliases={} if ys is None else {len(args) - 1: 0},
            compiler_params=_params("arbitrary"),
            name="experts",
        )(*args)
    return ys


def _dispatch_plan(eidx_t, bm, n_parts):
    N = eidx_t.shape[1]
    A, E = N * TOP_K, N_EXPERTS
    assert A % (bm * n_parts) == 0
    ids = jnp.arange(A, dtype=jnp.int32)
    id_bits = (A - 1).bit_length()
    assert id_bits + (E - 1).bit_length() <= 31
    packed = jnp.sort((eidx_t.reshape(A) << id_bits) | ids)
    se, sid = packed >> id_bits, packed & ((1 << id_bits) - 1)
    _, pos = lax.sort((sid, ids), num_keys=1)
    start = jnp.searchsorted(se, jnp.arange(E, dtype=jnp.int32), side='left').astype(jnp.int32)
    changed = lambda a: jnp.concatenate([jnp.ones((1,), jnp.int32), (a[1:] != a[:-1]).astype(jnp.int32)])
    plans = []
    for p in range(n_parts):
        a0, a1 = p * (A // n_parts), (p + 1) * (A // n_parts)
        lo = jnp.sort(jnp.concatenate([jnp.arange(a0, a1, bm, dtype=jnp.int32), jnp.clip(start, a0, a1)]))
        hi = jnp.concatenate([lo[1:], jnp.full((1,), a1, jnp.int32)])
        blk = jnp.minimum(lo // bm, a1 // bm - 1)
        exp = se[jnp.minimum(lo, A - 1)]
        plans.append((blk, exp, lo, hi, changed(blk), changed(exp)))
    return sid % N, pos.reshape(TOP_K, N), plans


def _moe_out_kernel(x_ref, r_ref, rw_ref, wg_ref, wu_ref, wd_ref, g_ref, b_ref, y_ref):
    x = x_ref[...]
    tm = x.shape[0]
    xb = x.astype(BF)
    g = jnp.dot(xb, wg_ref[...], preferred_element_type=F32)
    u = jnp.dot(xb, wu_ref[...], preferred_element_type=F32)
    h = (g * _sigmoid(g) * u).astype(BF)
    shared = jnp.dot(h, wd_ref[...], preferred_element_type=F32)
    rw = jnp.concatenate([rw_ref[...], jnp.zeros((LANES - TOP_K, tm), F32)], axis=0).T
    routed = rw[:, 0:1] * r_ref[0].astype(F32)
    for k in range(1, TOP_K):
        routed = routed + rw[:, k:k + 1] * r_ref[k].astype(F32)
    y_ref[...] = _layer_norm(DEEPNORM_ALPHA * x + (routed + shared), g_ref[...], b_ref[...])


def _moe_out_kernel_onto(x_ref, r_ref, rw_ref, wg_ref, wu_ref, wd_ref, g_ref, b_ref, prev_ref, y_ref):
    del prev_ref
    _moe_out_kernel(x_ref, r_ref, rw_ref, wg_ref, wu_ref, wd_ref, g_ref, b_ref, y_ref)


def _moe_out(x1, routed_parts, route_w, w_sh_gate, w_sh_up, w_sh_down, ln_g, ln_b, tm):
    N = x1.shape[0]
    n_tok = routed_parts[0].shape[1]
    fixed = lambda i: (0, 0)
    weights = (w_sh_gate.astype(BF), w_sh_up.astype(BF), w_sh_down.astype(BF),
               ln_g.astype(F32)[None, :], ln_b.astype(F32)[None, :])
    y = None
    for p, routed in enumerate(routed_parts):
        t0 = p * n_tok // tm
        row = lambda i, t0=t0: (t0 + i, 0)
        in_specs = [pl.BlockSpec((tm, D_MODEL), row),
                    pl.BlockSpec((TOP_K, tm, D_MODEL), lambda i: (0, i, 0)),
                    pl.BlockSpec((TOP_K, tm), lambda i, t0=t0: (0, t0 + i)),
                    pl.BlockSpec(w_sh_gate.shape, fixed), pl.BlockSpec(w_sh_up.shape, fixed),
                    pl.BlockSpec(w_sh_down.shape, fixed),
                    pl.BlockSpec((1, D_MODEL), fixed), pl.BlockSpec((1, D_MODEL), fixed)]
        args = (x1, routed, route_w) + weights
        if y is not None:
            in_specs.append(pl.BlockSpec(memory_space=pl.ANY))
            args += (y,)
        y = pl.pallas_call(
            _moe_out_kernel if y is None else _moe_out_kernel_onto,
            grid=(n_tok // tm,),
            in_specs=in_specs,
            out_specs=pl.BlockSpec((tm, D_MODEL), row),
            out_shape=jax.ShapeDtypeStruct((N, D_MODEL), F32),
            input_output_aliases={} if y is None else {len(args) - 1: 0},
            compiler_params=_params("parallel"),
            name="moe_out",
        )(*args)
    return y


EXPERT_PARTS = 2
COMBINE_PARTS = 2

def _tiles(N, T):
    return min(512, T), min(256, T), MOBA_BLOCK, min(1024, N * TOP_K)


def _mixer_layer(x2, T, w_in, b_forget, diff_lambda, diff_subln, w_branch, w_out, ln_g, ln_b, lambda_init):
    N = x2.shape[0]
    tm_proj, tm_row, tq, _ = _tiles(N, T)
    w, wvt, w_gates, bf = _in_proj_weights(w_in, b_forget)
    qk, vt, misc = _in_proj(x2, w, wvt, bf, _rope_tables(T), T, tm_proj)
    dl = diff_lambda.astype(F32)
    lam = jnp.exp(jnp.sum(dl[0] * dl[1])) - jnp.exp(jnp.sum(dl[2] * dl[3])) + lambda_init
    o_a = _moba(qk, vt, T, tq)
    o_b = _diff(qk, vt, lam, diff_subln, lambda_init, T, tq)
    o_c = _fox(qk, vt, misc, T, tq)
    o_d = _dsa(qk, vt, misc, T, tq)
    return _mix_out(x2, (o_a, o_b, o_c, o_d), w_gates, w_branch, w_out, ln_g, ln_b, tm_row)


def _moe_layer(x1, x1b, T, layer, w_router, router_bias, w_exp_gate, w_exp_up, w_exp_down,
               w_sh_gate, w_sh_up, w_sh_down, ln_g, ln_b):
    N = x1.shape[0]
    tm_row, _, _, bm = _tiles(N, T)
    eidx_t, wsel_t = _router(x1, w_router, router_bias, tm_row)
    row_tok, pos, plans = _dispatch_plan(eidx_t, bm, EXPERT_PARTS)
    rows_part = row_tok.shape[0] // EXPERT_PARTS
    xs_parts = [x1b[row_tok[p * rows_part:(p + 1) * rows_part]] for p in range(EXPERT_PARTS)]
    ys = _experts(xs_parts, plans, layer, w_exp_gate, w_exp_up, w_exp_down, bm)
    n_tok = N // COMBINE_PARTS
    routed_parts = [ys[pos[:, p * n_tok:(p + 1) * n_tok]] for p in range(COMBINE_PARTS)]
    return _moe_out(x1, routed_parts, wsel_t, w_sh_gate, w_sh_up, w_sh_down, ln_g, ln_b, tm_row)


def kernel(x, w_in, b_forget, diff_lambda, diff_subln, w_branch, w_out, ln1_g, ln1_b, w_router, router_bias,
           w_exp_gate, w_exp_up, w_exp_down, w_sh_gate, w_sh_up, w_sh_down, ln2_g, ln2_b):
    B, T, D = x.shape
    x2 = x.reshape(B * T, D)
    for l in range(DEPTH):
        lambda_init = 0.8 - 0.6 * math.exp(-0.3 * l)
        x1, x1b = _mixer_layer(x2, T, w_in[l], b_forget[l], diff_lambda[l], diff_subln[l], w_branch[l],
                               w_out[l], ln1_g[l], ln1_b[l], lambda_init)
        x2 = _moe_layer(x1, x1b, T, l, w_router[l], router_bias[l], w_exp_gate, w_exp_up, w_exp_down,
                        w_sh_gate[l], w_sh_up[l], w_sh_down[l], ln2_g[l], ln2_b[l])
    return x2.reshape(B, T, D)
```
